```python
import jax, jax.numpy as jnp
from jax import lax
import numpy as np

D_MODEL = 1024
BATCH = 8
SEQ = 2048
DEPTH = 2

CTX_LEN = 256
GRID_W = 64
N_MOD = 9
D_FF = 2816
EPS = 1e-6
A_HEADS = 4
A_HEAD_DIM = 128
A_WIDTH = A_HEADS * A_HEAD_DIM
MLSTM_CHUNK = 128
CONV_W = 3
B_GROUPS = 4
B_GROUP_DIM = 128
B_WIDTH = B_GROUPS * B_GROUP_DIM
SGU_CHUNK = 128
EVEN_IN = 4 * A_WIDTH + 4 * A_HEADS + 2 * B_WIDTH
EVEN_MIX = A_WIDTH + B_WIDTH
C_HEADS = 16
C_KV_HEADS = 4
C_GROUP = C_HEADS // C_KV_HEADS
C_HEAD_DIM = 64
WINDOW = 128
ATTN_BLOCK = 128
ODD_QKV = (C_HEADS + 2 * C_KV_HEADS) * C_HEAD_DIM
ROPE_BASE = 10000.0
N_EVEN = (DEPTH + 1) // 2
N_ODD = DEPTH // 2

kernel_name = 'hybrid_mlstm_sgu_swa_dit_block'

f32 = jnp.float32


def _rms_norm(t):
    tf = t.astype(f32)
    return (tf * lax.rsqrt(jnp.mean(tf * tf, axis=-1, keepdims=True) + EPS)).astype(t.dtype)


def _modulation(s, w, b):
    m = s @ w + b
    return m.reshape(s.shape[0], N_MOD, 1, D_MODEL)


def _modulate(h, shift, scale):
    return _rms_norm(h) * (1.0 + scale) + shift


def _swiglu(h, w_in, w_out):
    g, u = jnp.split(h @ w_in, 2, axis=-1)
    return (jax.nn.silu(g) * u) @ w_out


def _axial_rope_tables(T):
    rows = T // GRID_W
    row, col = jnp.meshgrid(jnp.arange(rows), jnp.arange(GRID_W), indexing='ij')
    n_freq = C_HEAD_DIM // 4
    inv = ROPE_BASE ** (-jnp.arange(n_freq, dtype=f32) / n_freq)
    ang = jnp.concatenate([row.reshape(-1, 1).astype(f32) * inv,
                           col.reshape(-1, 1).astype(f32) * inv], axis=-1)
    return jnp.cos(ang), jnp.sin(ang)


def _apply_rope(t, cos, sin):
    x1, x2 = t[..., 0::2], t[..., 1::2]
    cs = cos[None, :, None, :].astype(t.dtype)
    sn = sin[None, :, None, :].astype(t.dtype)
    return jnp.stack([x1 * cs - x2 * sn, x1 * sn + x2 * cs], axis=-1).reshape(t.shape)


def _centred_depthwise_conv(x, w):
    C = x.shape[-1]
    return lax.conv_general_dilated(x, w[:, None, :].astype(x.dtype), window_strides=(1,),
                                    padding=[(CONV_W // 2, CONV_W // 2)],
                                    dimension_numbers=('NWC', 'WIO', 'NWC'),
                                    feature_group_count=C)


def _zero_state(B_):
    return (jnp.zeros((B_, A_HEADS, A_HEAD_DIM, A_HEAD_DIM), f32),
            jnp.zeros((B_, A_HEADS, A_HEAD_DIM), f32),
            jnp.zeros((B_, A_HEADS), f32))


def _mlstm_chunkwise(q, k, v, ig, lf, state):
    B_, H, T, d = q.shape
    L = MLSTM_CHUNK
    N = T // L
    q = q.reshape(B_, H, N, L, d)
    k = k.reshape(B_, H, N, L, d)
    v = v.reshape(B_, H, N, L, d)
    ig = ig.reshape(B_, H, N, L)
    b = jnp.cumsum(lf.reshape(B_, H, N, L), axis=-1)
    g = b[..., -1]
    a = g[..., None] - b + ig
    m_loc = jnp.max(a, axis=-1)
    w = jnp.exp(a - m_loc[..., None])
    C_loc = jnp.einsum('bhnl,bhnld,bhnle->bhnde', w, k, v)
    n_loc = jnp.einsum('bhnl,bhnld->bhnd', w, k)

    def step(carry, xs):
        C, n, m = carry
        g_j, m_loc_j, C_loc_j, n_loc_j = xs
        m_new = jnp.maximum(g_j + m, m_loc_j)
        dec = jnp.exp(g_j + m - m_new)
        add = jnp.exp(m_loc_j - m_new)
        C_new = dec[..., None, None] * C + add[..., None, None] * C_loc_j
        n_new = dec[..., None] * n + add[..., None] * n_loc_j
        return (C_new, n_new, m_new), (C, n, m)

    xs = (jnp.moveaxis(g, 2, 0), jnp.moveaxis(m_loc, 2, 0),
          jnp.moveaxis(C_loc, 2, 0), jnp.moveaxis(n_loc, 2, 0))
    final, (C_prev, n_prev, m_prev) = lax.scan(step, state, xs)
    C_prev = jnp.moveaxis(C_prev, 0, 2)
    n_prev = jnp.moveaxis(n_prev, 0, 2)
    m_prev = jnp.moveaxis(m_prev, 0, 2)

    e = b + m_prev[..., None]
    tril = jnp.tril(jnp.ones((L, L), dtype=bool))
    Dm = jnp.where(tril, b[..., :, None] - b[..., None, :] + ig[..., None, :], -jnp.inf)
    m_t = jnp.maximum(e, jnp.max(Dm, axis=-1))
    S = jnp.einsum('bhntd,bhnsd->bhnts', q, k) * jnp.exp(Dm - m_t[..., None])
    inter = jnp.exp(e - m_t)
    num = (jnp.einsum('bhnts,bhnse->bhnte', S, v)
           + inter[..., None] * jnp.einsum('bhntd,bhnde->bhnte', q, C_prev))
    den = jnp.sum(S, axis=-1) + inter * jnp.einsum('bhntd,bhnd->bhnt', q, n_prev)
    h = num / jnp.maximum(jnp.abs(den), jnp.exp(-m_t))[..., None]
    return h.reshape(B_, H, T, d), final


def _flip(ts):
    return tuple(jnp.flip(t, axis=2) for t in ts)


def _even_stream_inputs(n, w_in, conv_w, gate_b):
    B_, T, _ = n.shape
    p = n @ w_in
    qk, v, o, gates, uv = jnp.split(
        p, [2 * A_WIDTH, 3 * A_WIDTH, 4 * A_WIDTH, 4 * A_WIDTH + 4 * A_HEADS], axis=-1)
    q, k = jnp.split(jax.nn.silu(_centred_depthwise_conv(qk, conv_w)), 2, axis=-1)

    def heads(t):
        return t.reshape(B_, T, A_HEADS, A_HEAD_DIM).transpose(0, 2, 1, 3).astype(f32)

    q, k, v = heads(q), heads(k) * (A_HEAD_DIM ** -0.5), heads(v)
    gt = (gates.astype(f32) + gate_b.reshape(-1).astype(f32)).reshape(B_, T, 4, A_HEADS).transpose(2, 0, 3, 1)
    fwd = (gt[0], jax.nn.log_sigmoid(gt[1]))
    bwd = (gt[2], jax.nn.log_sigmoid(gt[3]))
    return (q, k, v), fwd, bwd, o, uv


def _spatial_gating(uv, norm_g, ws, sb):
    u, v = jnp.split(jax.nn.gelu(uv), 2, axis=-1)
    v = _rms_norm(v) * norm_g
    B_, T, _ = v.shape
    vb = v.reshape(B_, T // SGU_CHUNK, SGU_CHUNK, B_GROUPS, B_GROUP_DIM)
    mixed = jnp.einsum('gpq,bnqgc->bnpgc', ws, vb) + sb.T[:, :, None]
    return u * mixed.reshape(B_, T, B_WIDTH)


def _even_output(hsum, o, uv, mnorm, sgu_g, ws, sb, w_out):
    B_, H, T, d = hsum.shape
    hn = hsum * lax.rsqrt(jnp.mean(hsum * hsum, axis=-1, keepdims=True) + EPS) * mnorm[None, :, None, :].astype(f32)
    h_a = jax.nn.sigmoid(o) * hn.transpose(0, 2, 1, 3).reshape(B_, T, A_WIDTH).astype(o.dtype)
    h_b = _spatial_gating(uv, sgu_g, ws, sb)
    return jnp.concatenate([h_a, h_b], axis=-1) @ w_out


def _even_mixer(nx, nc, w_in, w_out, conv_w, gate_b, mnorm, sgu_g, ws, sb, need_ctx):
    qkv_c, fw_c, bw_c, o_c, uv_c = _even_stream_inputs(nc, w_in, conv_w, gate_b)
    qkv_x, fw_x, bw_x, o_x, uv_x = _even_stream_inputs(nx, w_in, conv_w, gate_b)
    zero = _zero_state(nx.shape[0])
    hc_f, st_f = _mlstm_chunkwise(*qkv_c, *fw_c, zero)
    hc_b, st_b = _mlstm_chunkwise(*_flip(qkv_c), *_flip(bw_c), zero)
    hx_f, _ = _mlstm_chunkwise(*qkv_x, *fw_x, st_f)
    hx_b, _ = _mlstm_chunkwise(*_flip(qkv_x), *_flip(bw_x), st_b)
    yx = _even_output(hx_f + jnp.flip(hx_b, axis=2), o_x, uv_x, mnorm, sgu_g, ws, sb, w_out)
    yc = None
    if need_ctx:
        yc = _even_output(hc_f + jnp.flip(hc_b, axis=2), o_c, uv_c, mnorm, sgu_g, ws, sb, w_out)
    return yx, yc


def _softmax_with_sink(sink, *scores):
    ref = scores[0]
    s0 = jnp.broadcast_to(sink[None, :, :, None, None].astype(f32), ref.shape[:-1] + (1,))
    p = jax.nn.softmax(jnp.concatenate([s0] + [s.astype(f32) for s in scores], axis=-1), axis=-1)
    parts = []
    off = 1
    for s in scores:
        parts.append(p[..., off:off + s.shape[-1]])
        off += s.shape[-1]
    return parts


def _window_attention(q, k, v, kc, vc, sink):
    B_, T, _, _ = q.shape
    nb = T // ATTN_BLOCK
    scale = C_HEAD_DIM ** -0.5
    qb = (q * scale).reshape(B_, nb, ATTN_BLOCK, C_KV_HEADS, C_GROUP, C_HEAD_DIM).transpose(1, 0, 2, 3, 4, 5)
    pad = ((0, 0), (ATTN_BLOCK, ATTN_BLOCK), (0, 0), (0, 0))
    kp, vp = jnp.pad(k, pad), jnp.pad(v, pad)
    offs = jnp.arange(ATTN_BLOCK)
    band_offs = jnp.arange(3 * ATTN_BLOCK) - ATTN_BLOCK

    def block(args):
        qj, j = args
        start = j * ATTN_BLOCK
        kj = lax.dynamic_slice_in_dim(kp, start, 3 * ATTN_BLOCK, axis=1)
        vj = lax.dynamic_slice_in_dim(vp, start, 3 * ATTN_BLOCK, axis=1)
        qpos = start + offs
        kpos = start + band_offs
        valid = (jnp.abs(qpos[:, None] - kpos[None, :]) <= WINDOW) & (kpos[None, :] >= 0) & (kpos[None, :] < T)
        s_band = jnp.where(valid, jnp.einsum('bqhgd,bkhd->bhgqk', qj, kj).astype(f32), -jnp.inf)
        s_ctx = jnp.einsum('bqhgd,bkhd->bhgqk', qj, kc)
        p_ctx, p_band = _softmax_with_sink(sink, s_ctx, s_band)
        return (jnp.einsum('bhgqk,bkhd->bqhgd', p_ctx.astype(vc.dtype), vc)
                + jnp.einsum('bhgqk,bkhd->bqhgd', p_band.astype(vj.dtype), vj))

    out = lax.map(block, (qb, jnp.arange(nb)))
    return out.transpose(1, 0, 2, 3, 4, 5).reshape(B_, T, C_HEADS * C_HEAD_DIM)


def _context_attention(qc, kc, vc, sink):
    B_, Tc = qc.shape[:2]
    q = (qc * (C_HEAD_DIM ** -0.5)).reshape(B_, Tc, C_KV_HEADS, C_GROUP, C_HEAD_DIM)
    (p,) = _softmax_with_sink(sink, jnp.einsum('bqhgd,bkhd->bhgqk', q, kc))
    return jnp.einsum('bhgqk,bkhd->bqhgd', p.astype(vc.dtype), vc).reshape(B_, Tc, C_HEADS * C_HEAD_DIM)


def _odd_mixer(nx, nc, w_qkv, w_out, sink, cos, sin, need_ctx):
    B_, T, _ = nx.shape
    Tc = nc.shape[1]
    qdim = C_HEADS * C_HEAD_DIM
    kvdim = C_KV_HEADS * C_HEAD_DIM
    q, k, v = jnp.split(nx @ w_qkv, [qdim, qdim + kvdim], axis=-1)
    q = _apply_rope(q.reshape(B_, T, C_HEADS, C_HEAD_DIM), cos, sin)
    k = _apply_rope(k.reshape(B_, T, C_KV_HEADS, C_HEAD_DIM), cos, sin)
    v = v.reshape(B_, T, C_KV_HEADS, C_HEAD_DIM)
    kc, vc = jnp.split(nc @ w_qkv[:, qdim:], 2, axis=-1)
    kc = kc.reshape(B_, Tc, C_KV_HEADS, C_HEAD_DIM)
    vc = vc.reshape(B_, Tc, C_KV_HEADS, C_HEAD_DIM)
    sink = sink.reshape(C_KV_HEADS, C_GROUP)
    yx = _window_attention(q, k, v, kc, vc, sink) @ w_out
    yc = None
    if need_ctx:
        qc = (nc @ w_qkv[:, :qdim]).reshape(B_, Tc, C_HEADS, C_HEAD_DIM)
        yc = _context_attention(qc, kc, vc, sink) @ w_out
    return yx, yc


def setup_inputs(seed: int = 0) -> dict:
    key = jax.random.key(seed)
    ks = jax.random.split(key, 20)
    D = D_MODEL

    def nrm(k, shape, scale):
        return jax.random.normal(k, shape, jnp.float32) * scale

    is_forget = jnp.array([0.0, 1.0, 0.0, 1.0], jnp.float32)[:, None]
    forget_lin = jnp.linspace(3.0, 6.0, A_HEADS, dtype=jnp.float32)[None, :]
    return {
        'x': nrm(ks[0], (BATCH, SEQ, D), 1.0),
        'c': nrm(ks[1], (BATCH, D), 1.0),
        'ctx': nrm(ks[2], (BATCH, CTX_LEN, D), 1.0),
        'c_ctx': nrm(ks[3], (D,), 1.0),
        'ada_w': nrm(ks[4], (DEPTH, D, N_MOD * D), 0.5 * D ** -0.5),
        'ada_b': nrm(ks[5], (DEPTH, N_MOD * D), 0.02),
        'ffn_w_in': nrm(ks[6], (DEPTH, 2, D, 2 * D_FF), D ** -0.5),
        'ffn_w_out': nrm(ks[7], (DEPTH, 2, D_FF, D), D_FF ** -0.5),
        'even_w_in': nrm(ks[8], (N_EVEN, D, EVEN_IN), D ** -0.5),
        'even_w_out': nrm(ks[9], (N_EVEN, EVEN_MIX, D), EVEN_MIX ** -0.5),
        'mlstm_conv': nrm(ks[10], (N_EVEN, CONV_W, 2 * A_WIDTH), CONV_W ** -0.5),
        'mlstm_gate_b': is_forget * forget_lin + nrm(ks[11], (N_EVEN, 4, A_HEADS), 0.1),
        'mlstm_norm': 1.0 + nrm(ks[12], (N_EVEN, A_HEADS, A_HEAD_DIM), 0.05),
        'sgu_norm': 1.0 + nrm(ks[13], (N_EVEN, B_WIDTH), 0.05),
        'sgu_ws': nrm(ks[14], (N_EVEN, B_GROUPS, SGU_CHUNK, SGU_CHUNK), SGU_CHUNK ** -0.5),
        'sgu_b': 1.0 + nrm(ks[15], (N_EVEN, B_GROUPS, SGU_CHUNK), 0.1),
        'odd_w_qkv': nrm(ks[16], (N_ODD, D, ODD_QKV), D ** -0.5),
        'odd_w_out': nrm(ks[17], (N_ODD, C_HEADS * C_HEAD_DIM, D), (C_HEADS * C_HEAD_DIM) ** -0.5),
        'attn_sink': nrm(ks[18], (N_ODD, C_HEADS), 0.5),
        'final_norm': 1.0 + nrm(ks[19], (D,), 0.05),
    }


def reference(x, c, ctx, c_ctx, ada_w, ada_b, ffn_w_in, ffn_w_out, even_w_in, even_w_out,
              mlstm_conv, mlstm_gate_b, mlstm_norm, sgu_norm, sgu_ws, sgu_b,
              odd_w_qkv, odd_w_out, attn_sink, final_norm):
    cos, sin = _axial_rope_tables(x.shape[1])
    sc = jax.nn.silu(c)
    scc = jax.nn.silu(c_ctx)[None]
    h, hc = x, ctx
    for layer in range(DEPTH):
        last = layer == DEPTH - 1
        mod = _modulation(sc, ada_w[layer], ada_b[layer])
        modc = _modulation(scc, ada_w[layer], ada_b[layer])
        h = h + 0.5 * mod[:, 2] * _swiglu(_modulate(h, mod[:, 0], mod[:, 1]),
                                          ffn_w_in[layer, 0], ffn_w_out[layer, 0])
        hc = hc + 0.5 * modc[:, 2] * _swiglu(_modulate(hc, modc[:, 0], modc[:, 1]),
                                             ffn_w_in[layer, 0], ffn_w_out[layer, 0])
        nx = _modulate(h, mod[:, 3], mod[:, 4])
        nc = _modulate(hc, modc[:, 3], modc[:, 4])
        if layer % 2 == 0:
            e = layer // 2
            yx, yc = _even_mixer(nx, nc, even_w_in[e], even_w_out[e], mlstm_conv[e], mlstm_gate_b[e],
                                 mlstm_norm[e], sgu_norm[e], sgu_ws[e], sgu_b[e], not last)
        else:
            o = layer // 2
            yx, yc = _odd_mixer(nx, nc, odd_w_qkv[o], odd_w_out[o], attn_sink[o], cos, sin, not last)
        h = h + mod[:, 5] * yx
        h = h + 0.5 * mod[:, 8] * _swiglu(_modulate(h, mod[:, 6], mod[:, 7]),
                                          ffn_w_in[layer, 1], ffn_w_out[layer, 1])
        if not last:
            hc = hc + modc[:, 5] * yc
            hc = hc + 0.5 * modc[:, 8] * _swiglu(_modulate(hc, modc[:, 6], modc[:, 7]),
                                                 ffn_w_in[layer, 1], ffn_w_out[layer, 1])
    return _rms_norm(h) * final_norm
```

```python
import functools

import jax
import jax.numpy as jnp
from jax import lax
from jax.experimental import pallas as pl
from jax.experimental.pallas import tpu as pltpu

f32 = jnp.float32
bf16 = jnp.bfloat16

D = 1024
B = 8
SEQ = 2048
CTX = 256
TOK = CTX + SEQ
GRID_W = 64
N_MOD = 9
D_FF = 2816
EPS = 1e-6
HEADS_A = 4
CHUNK = 128
N_CHUNK = TOK // CHUNK
N_CTX_CHUNK = CTX // CHUNK
W_A = 512
EVEN_COLS = 3200
HEADS_C = 16
KV_HEADS = 4
GROUP = HEADS_C // KV_HEADS
DH = 64
QKV = (HEADS_C + 2 * KV_HEADS) * DH
N_BLK = SEQ // CHUNK
ROPE_BASE = 10000.0

R_CTX = B * CTX
R_LAT = B * SEQ
R_ALL = R_CTX + R_LAT

LANES = 128
TM_FFN = 512
TM_PROJ = 256
FC = 256
N_FC = D_FF // FC
MIB = 1024 * 1024


def _dot(a, b):
    return jnp.dot(a, b, preferred_element_type=f32)


def _dot_nt(a, b):
    return lax.dot_general(a, b, (((1,), (1,)), ((), ())), preferred_element_type=f32)


def _dot_tn(a, b):
    return lax.dot_general(a, b, (((0,), (0,)), ((), ())), preferred_element_type=f32)


def _sigmoid(x):
    return 1.0 / (1.0 + jnp.exp(-x))


def _split3(x):
    hi = x.astype(bf16)
    r1 = x - hi.astype(f32)
    mid = r1.astype(bf16)
    lo = (r1 - mid.astype(f32)).astype(bf16)
    return hi, mid, lo


def _modulated(h, shift, scale):
    ms = jnp.mean(h * h, axis=-1, keepdims=True)
    return h * lax.rsqrt(ms + EPS) * (1.0 + scale) + shift


def _mod_kernel(c_ref, w_ref, b_ref, o_ref):
    x = c_ref[...]
    s = x * _sigmoid(x)
    w = w_ref[...]
    s_hi = s.astype(bf16)
    s_lo = (s - s_hi.astype(f32)).astype(bf16)
    w_hi = w.astype(bf16)
    w_lo = (w - w_hi.astype(f32)).astype(bf16)
    o_ref[...] = _dot(s_hi, w_hi) + _dot(s_hi, w_lo) + _dot(s_lo, w_hi) + b_ref[...]


def _modulation(cs, ada_w, ada_b):
    depth = ada_w.shape[0]
    rows = cs.shape[0]
    n_col = N_MOD * D
    tn = 1024
    return pl.pallas_call(
        _mod_kernel,
        grid=(depth, n_col // tn),
        in_specs=[
            pl.BlockSpec((rows, D), lambda l, j: (0, 0)),
            pl.BlockSpec((None, D, tn), lambda l, j: (l, 0, j)),
            pl.BlockSpec((None, 1, tn), lambda l, j: (l, 0, j)),
        ],
        out_specs=pl.BlockSpec((None, rows, tn), lambda l, j: (l, 0, j)),
        out_shape=jax.ShapeDtypeStruct((depth, rows, n_col), f32),
        compiler_params=pltpu.CompilerParams(
            dimension_semantics=("parallel", "parallel"), vmem_limit_bytes=32 * MIB),
        name="modulation",
    )(cs, ada_w, ada_b.reshape(depth, 1, n_col))


def _who_flat(tile, tm):
    n_ctx = R_CTX // tm
    per_b = SEQ // tm
    return jnp.where(tile < n_ctx, 0, 1 + jnp.maximum(tile - n_ctx, 0) // per_b)


def _batch_block(tile):
    per_b = SEQ // TM_PROJ
    lat = jnp.maximum(tile - B, 0)
    blocks_b = TOK // TM_PROJ
    return jnp.where(tile < B, blocks_b * tile, blocks_b * (lat // per_b) + 1 + lat % per_b)


def _ffn_kernel(*refs, mi, final):
    if final:
        h_ref, mod_ref, wg_ref, wu_ref, wo_ref, fn_ref, o_ref, n_scr, acc_scr = refs
    else:
        h_ref, mod_ref, wg_ref, wu_ref, wo_ref, o_ref, n_scr, acc_scr = refs
    h = h_ref[...]
    n_scr[...] = _modulated(h, mod_ref[mi:mi + 1, :], mod_ref[mi + 1:mi + 2, :]).astype(bf16)
    for j in range(N_FC):
        n = n_scr[...]
        g = _dot(n, wg_ref[j])
        u = _dot(n, wu_ref[j])
        a = (g * _sigmoid(g) * u).astype(bf16)
        y = _dot(a, wo_ref[j])
        if j == 0:
            acc_scr[...] = y
        else:
            acc_scr[...] += y
    out = h_ref[...] + (0.5 * mod_ref[mi + 2:mi + 3, :]) * acc_scr[...]
    if final:
        ms = jnp.mean(out * out, axis=-1, keepdims=True)
        out = out * lax.rsqrt(ms + EPS) * fn_ref[...]
    o_ref[...] = out


def _ffn(h, mods, wg, wu, wo, *, mi, latent_only=False, final_norm=None):
    tm = TM_FFN
    tile0 = R_CTX // tm if latent_only else 0
    rows_out = h.shape[0]
    const3 = lambda i: (0, 0, 0)
    in_specs = [
        pl.BlockSpec((tm, D), lambda i: (i, 0)),
        pl.BlockSpec((None, N_MOD, D), lambda i: (_who_flat(i + tile0, tm), 0, 0)),
        pl.BlockSpec((N_FC, D, FC), const3, pipeline_mode=pl.Buffered(1)),
        pl.BlockSpec((N_FC, D, FC), const3, pipeline_mode=pl.Buffered(1)),
        pl.BlockSpec((N_FC, FC, D), const3, pipeline_mode=pl.Buffered(1)),
    ]
    args = [h, mods, wg, wu, wo]
    if final_norm is not None:
        in_specs.append(pl.BlockSpec((1, D), lambda i: (0, 0)))
        args.append(final_norm.reshape(1, D))
    return pl.pallas_call(
        functools.partial(_ffn_kernel, mi=mi, final=final_norm is not None),
        grid=(rows_out // tm,),
        in_specs=in_specs,
        out_specs=pl.BlockSpec((tm, D), lambda i: (i, 0)),
        out_shape=jax.ShapeDtypeStruct((rows_out, D), f32),
        scratch_shapes=[pltpu.VMEM((tm, D), bf16), pltpu.VMEM((tm, D), f32)],
        compiler_params=pltpu.CompilerParams(
            dimension_semantics=("parallel",), vmem_limit_bytes=48 * MIB),
        name="ffn_final" if final_norm is not None else "ffn",
    )(*args)


def _even_in_kernel(h_ref, mod_ref, w_ref, qk_ref, vo_ref, uv_ref, g_ref):
    n = _modulated(h_ref[...], mod_ref[3:4, :], mod_ref[4:5, :]).astype(bf16)
    qk_ref[...] = _dot(n, w_ref[:, 0:1024])
    vo_ref[...] = _dot(n, w_ref[:, 1024:2048]).astype(bf16)
    uv_ref[...] = _dot(n, w_ref[:, 2048:3072])
    g_ref[...] = _dot(n, w_ref[:, 3072:3200])


def _even_in(h, mods, w):
    tm = TM_PROJ
    out_map = lambda i: (_batch_block(i), 0)
    return pl.pallas_call(
        _even_in_kernel,
        grid=(R_ALL // tm,),
        in_specs=[
            pl.BlockSpec((tm, D), lambda i: (i, 0)),
            pl.BlockSpec((None, N_MOD, D), lambda i: (_who_flat(i, tm), 0, 0)),
            pl.BlockSpec((D, EVEN_COLS), lambda i: (0, 0), pipeline_mode=pl.Buffered(1)),
        ],
        out_specs=[
            pl.BlockSpec((tm, 1024), out_map),
            pl.BlockSpec((tm, 1024), out_map),
            pl.BlockSpec((tm, 1024), out_map),
            pl.BlockSpec((tm, LANES), out_map),
        ],
        out_shape=[
            jax.ShapeDtypeStruct((B * TOK, 1024), f32),
            jax.ShapeDtypeStruct((B * TOK, 1024), bf16),
            jax.ShapeDtypeStruct((B * TOK, 1024), f32),
            jax.ShapeDtypeStruct((B * TOK, LANES), f32),
        ],
        compiler_params=pltpu.CompilerParams(
            dimension_semantics=("parallel",), vmem_limit_bytes=40 * MIB),
        name="even_in",
    )(h, mods, w)


def _mlstm_kernel(q_ref, k_ref, v_ref, o_ref, g_ref, cq_ref, ck_ref, gb_ref, mn_ref, out_ref,
                  qs, ks, xc, xr, hf, hb, cst, nst, mst):
    head = pl.program_id(1)
    rowi = lax.broadcasted_iota(jnp.int32, (CHUNK, CHUNK), 0)
    coli = lax.broadcasted_iota(jnp.int32, (CHUNK, CHUNK), 1)
    lower = coli <= rowi
    upper = coli >= rowi
    tri = jnp.where(lower, 1.0, 0.0).astype(bf16)
    lane_shift = (LANES - head) % LANES

    for c in range(N_CHUNK):
        lo = c * CHUNK
        seq_start = lo in (0, CTX)
        seq_end = lo + CHUNK in (CTX, TOK)
        for src, dst, cw, scale in ((q_ref, qs, cq_ref, None), (k_ref, ks, ck_ref, CHUNK ** -0.5)):
            x = src[lo:lo + CHUNK, :]
            first = jnp.zeros((1, LANES), f32) if seq_start else src[lo - 1:lo, :]
            last = jnp.zeros((1, LANES), f32) if seq_end else src[lo + CHUNK:lo + CHUNK + 1, :]
            prev = jnp.where(rowi == 0, first, pltpu.roll(x, 1, 0))
            nxt = jnp.where(rowi == CHUNK - 1, last, pltpu.roll(x, CHUNK - 1, 0))
            y = cw[0:1, :] * prev + cw[1:2, :] * x + cw[2:3, :] * nxt
            y = y * _sigmoid(y)
            if scale is not None:
                y = y * scale
            dst[lo:lo + CHUNK, :] = y
        gt = pltpu.roll(g_ref[lo:lo + CHUNK, :] + gb_ref[...], lane_shift, 1)
        lf = jnp.minimum(gt, 0.0) - jnp.log1p(jnp.exp(-jnp.abs(gt)))
        hi, mid, lw = _split3(lf)
        pre = _dot(tri, hi) + _dot(tri, mid) + _dot(tri, lw)
        suf = pre[CHUNK - 1:CHUNK, :] - pre + lf
        x = jnp.where(coli == 4, pre, jnp.where(coli == 12, suf, gt))
        xc[lo:lo + CHUNK, :] = x
        xr[16 * c:16 * c + 16, :] = x.T[0:16, :]

    cst[...] = jnp.zeros(cst.shape, f32)
    nst[...] = jnp.zeros(nst.shape, f32)
    mst[...] = jnp.zeros(mst.shape, f32)

    def step(i, carry):
        for d in (0, 1):
            c = i if d == 0 else jnp.where(i < N_CTX_CHUNK, N_CTX_CHUNK - 1 - i, N_CHUNK + N_CTX_CHUNK - 1 - i)
            lo = pl.multiple_of(c * CHUNK, CHUNK)
            q = qs[pl.ds(lo, CHUNK), :]
            k = ks[pl.ds(lo, CHUNK), :]
            v = v_ref[pl.ds(lo, CHUNK), :]
            qb = q.astype(bf16)
            x_c = xc[pl.ds(lo, CHUNK), :]
            x_r = xr[pl.ds(pl.multiple_of(c * 16, 16), 16), :]
            ig_c = x_c[:, 8 * d:8 * d + 1]
            b_c = x_c[:, 8 * d + 4:8 * d + 5]
            ig_r = x_r[8 * d:8 * d + 1, :]
            b_r = x_r[8 * d + 4:8 * d + 5, :]
            c_prev = cst[d]
            n_prev = nst[d]
            m_prev = mst[d]
            if d == 0:
                g = b_c[CHUNK - 1:CHUNK, :]
                causal = lower
            else:
                g = b_c[0:1, :]
                causal = upper
            dm = jnp.where(causal, b_c - b_r + ig_r, -jnp.inf)
            e = b_c + m_prev
            m_t = jnp.maximum(e, jnp.max(dm, axis=1, keepdims=True))
            s = _dot_nt(qb, k.astype(bf16)) * jnp.exp(dm - m_t)
            inter = jnp.exp(e - m_t)
            num = _dot(s.astype(bf16), v) + inter * _dot(qb, c_prev.astype(bf16))
            den = jnp.sum(s, axis=1, keepdims=True) + inter * jnp.sum(q * n_prev, axis=1, keepdims=True)
            hout = num / jnp.maximum(jnp.abs(den), jnp.exp(-m_t))
            if d == 0:
                hf[pl.ds(lo, CHUNK), :] = hout
            else:
                hb[pl.ds(lo, CHUNK), :] = hout
            a_c = g - b_c + ig_c
            m_loc = jnp.max(a_c, axis=0, keepdims=True)
            kw = k * jnp.exp(a_c - m_loc)
            c_loc = _dot_tn(kw.astype(bf16), v)
            n_loc = jnp.sum(kw, axis=0, keepdims=True)
            m_new = jnp.maximum(g + m_prev, m_loc)
            dec = jnp.exp(g + m_prev - m_new)
            add = jnp.exp(m_loc - m_new)
            cst[d] = dec * c_prev + add * c_loc
            nst[d] = dec * n_prev + add * n_loc
            mst[d] = m_new
        return carry

    lax.fori_loop(0, N_CHUNK, step, 0)

    for c in range(N_CHUNK):
        lo = c * CHUNK
        hs = hf[lo:lo + CHUNK, :] + hb[lo:lo + CHUNK, :]
        hn = hs * lax.rsqrt(jnp.mean(hs * hs, axis=-1, keepdims=True) + EPS) * mn_ref[...]
        og = o_ref[lo:lo + CHUNK, :].astype(f32)
        out_ref[lo:lo + CHUNK, :] = (_sigmoid(og) * hn).astype(bf16)


def _mlstm(qk, vo, gates, conv_w, gate_b, mnorm):
    blk = lambda col0: pl.BlockSpec((None, TOK, LANES), lambda b, h: (b, 0, col0 + h))
    return pl.pallas_call(
        _mlstm_kernel,
        grid=(B, HEADS_A),
        in_specs=[
            blk(0), blk(HEADS_A),
            blk(0), blk(HEADS_A),
            pl.BlockSpec((None, TOK, LANES), lambda b, h: (b, 0, 0)),
            pl.BlockSpec((3, LANES), lambda b, h: (0, h)),
            pl.BlockSpec((3, LANES), lambda b, h: (0, HEADS_A + h)),
            pl.BlockSpec((1, LANES), lambda b, h: (0, 0)),
            pl.BlockSpec((None, 1, LANES), lambda b, h: (h, 0, 0)),
        ],
        out_specs=pl.BlockSpec((None, TOK, LANES), lambda b, h: (b, 0, h)),
        out_shape=jax.ShapeDtypeStruct((B, TOK, W_A), bf16),
        scratch_shapes=[
            pltpu.VMEM((TOK, LANES), f32),
            pltpu.VMEM((TOK, LANES), f32),
            pltpu.VMEM((TOK, LANES), f32),
            pltpu.VMEM((16 * N_CHUNK, LANES), f32),
            pltpu.VMEM((TOK, LANES), f32),
            pltpu.VMEM((TOK, LANES), f32),
            pltpu.VMEM((2, CHUNK, LANES), f32),
            pltpu.VMEM((2, 1, LANES), f32),
            pltpu.VMEM((2, 1, 1), f32),
        ],
        compiler_params=pltpu.CompilerParams(
            dimension_semantics=("parallel", "parallel"), vmem_limit_bytes=40 * MIB),
        name="mlstm",
    )(qk.reshape(B, TOK, 1024), qk.reshape(B, TOK, 1024), vo.reshape(B, TOK, 1024),
      vo.reshape(B, TOK, 1024), gates.reshape(B, TOK, LANES), conv_w, conv_w, gate_b, mnorm)


def _gelu_tanh(x):
    return x * (0.5 * (1.0 + jnp.tanh(0.7978845608028654 * (x + 0.044715 * (x * x * x)))))


def _even_out_kernel(h_ref, mod_ref, ha_ref, uv_ref, sg_ref, ws_ref, sb_ref, wo_ref, o_ref, hb_scr):
    u = _gelu_tanh(uv_ref[:, 0:W_A])
    v = _gelu_tanh(uv_ref[:, W_A:2 * W_A])
    vn = (v * lax.rsqrt(jnp.mean(v * v, axis=-1, keepdims=True) + EPS) * sg_ref[...]).astype(bf16)
    for n in range(TM_PROJ // CHUNK):
        r = slice(n * CHUNK, (n + 1) * CHUNK)
        for g in range(W_A // LANES):
            cs = slice(g * LANES, (g + 1) * LANES)
            mixed = _dot(ws_ref[g], vn[r, cs]) + sb_ref[:, cs]
            hb_scr[r, cs] = (u[r, cs] * mixed).astype(bf16)
    y = _dot(ha_ref[...], wo_ref[0:W_A, :]) + _dot(hb_scr[...], wo_ref[W_A:2 * W_A, :])
    o_ref[...] = h_ref[...] + mod_ref[5:6, :] * y


def _even_out(h, mods, ha, uv, sgu_norm, ws, sbx, w_out):
    tm = TM_PROJ
    in_map = lambda i: (_batch_block(i), 0)
    const2 = lambda i: (0, 0)
    return pl.pallas_call(
        _even_out_kernel,
        grid=(R_ALL // tm,),
        in_specs=[
            pl.BlockSpec((tm, D), lambda i: (i, 0)),
            pl.BlockSpec((None, N_MOD, D), lambda i: (_who_flat(i, tm), 0, 0)),
            pl.BlockSpec((tm, W_A), in_map),
            pl.BlockSpec((tm, 2 * W_A), in_map),
            pl.BlockSpec((1, W_A), const2),
            pl.BlockSpec((W_A // LANES, CHUNK, CHUNK), lambda i: (0, 0, 0)),
            pl.BlockSpec((CHUNK, W_A), const2),
            pl.BlockSpec((2 * W_A, D), const2),
        ],
        out_specs=pl.BlockSpec((tm, D), lambda i: (i, 0)),
        out_shape=jax.ShapeDtypeStruct((R_ALL, D), f32),
        scratch_shapes=[pltpu.VMEM((tm, W_A), bf16)],
        compiler_params=pltpu.CompilerParams(
            dimension_semantics=("parallel",), vmem_limit_bytes=32 * MIB),
        name="even_out",
    )(h, mods, ha, uv, sgu_norm, ws, sbx, w_out)


def _odd_in_kernel(h_ref, mod_ref, w_ref, cos_ref, sin_ref, o_ref):
    n = _modulated(h_ref[...], mod_ref[3:4, :], mod_ref[4:5, :]).astype(bf16)
    even_lane = lax.broadcasted_iota(jnp.int32, (TM_PROJ, LANES), 1) % 2 == 0
    cos = cos_ref[...]
    sin = sin_ref[...]
    n_rot = (HEADS_C + KV_HEADS) * DH // LANES
    for c in range(n_rot):
        t = _dot(n, w_ref[:, c * LANES:(c + 1) * LANES])
        swapped = jnp.where(even_lane, pltpu.roll(t, LANES - 1, 1), pltpu.roll(t, 1, 1))
        r = t * cos + swapped * sin
        if c < HEADS_C * DH // LANES:
            r = r * (DH ** -0.5)
        o_ref[:, c * LANES:(c + 1) * LANES] = r.astype(bf16)
    v0 = n_rot * LANES
    o_ref[:, v0:QKV] = _dot(n, w_ref[:, v0:QKV]).astype(bf16)


def _odd_in(h, mods, w, cos_t, sin_t):
    tm = TM_PROJ
    per_b = SEQ // tm
    rope_map = lambda i: (jnp.where(i < B, 0, 1 + jnp.maximum(i - B, 0) % per_b), 0)
    return pl.pallas_call(
        _odd_in_kernel,
        grid=(R_ALL // tm,),
        in_specs=[
            pl.BlockSpec((tm, D), lambda i: (i, 0)),
            pl.BlockSpec((None, N_MOD, D), lambda i: (_who_flat(i, tm), 0, 0)),
            pl.BlockSpec((D, QKV), lambda i: (0, 0), pipeline_mode=pl.Buffered(1)),
            pl.BlockSpec((tm, LANES), rope_map),
            pl.BlockSpec((tm, LANES), rope_map),
        ],
        out_specs=pl.BlockSpec((tm, QKV), lambda i: (_batch_block(i), 0)),
        out_shape=jax.ShapeDtypeStruct((B * TOK, QKV), bf16),
        compiler_params=pltpu.CompilerParams(
            dimension_semantics=("parallel",), vmem_limit_bytes=32 * MIB),
        name="odd_in",
    )(h, mods, w, cos_t, sin_t)


def _attn_kernel(sink_ref, q_ref, kc_ref, kp_ref, k0_ref, kn_ref, vc_ref, vp_ref, v0_ref, vn_ref, o_ref):
    blk = pl.program_id(1)
    rows = GROUP * CHUNK
    row = lax.broadcasted_iota(jnp.int32, (rows, CHUNK), 0) % CHUNK
    col = lax.broadcasted_iota(jnp.int32, (rows, CHUNK), 1)
    far = 1 << 20
    prev_ok = col >= row + jnp.where(blk > 0, 0, far)
    next_ok = col <= row - jnp.where(blk < N_BLK - 1, 0, far)
    grp = lax.broadcasted_iota(jnp.int32, (rows, 1), 0) // CHUNK
    neg = -1e30
    for j in range(KV_HEADS):
        ks = slice(j * DH, (j + 1) * DH)
        q = jnp.concatenate(
            [q_ref[:, (j * GROUP + g) * DH:(j * GROUP + g + 1) * DH] for g in range(GROUP)], axis=0)
        sink = jnp.full((rows, 1), sink_ref[j * GROUP], f32)
        for g in range(1, GROUP):
            sink = jnp.where(grp == g, sink_ref[j * GROUP + g], sink)
        s_c = _dot_nt(q, kc_ref[:, ks])
        s_p = jnp.where(prev_ok, _dot_nt(q, kp_ref[:, ks]), neg)
        s_0 = _dot_nt(q, k0_ref[:, ks])
        s_n = jnp.where(next_ok, _dot_nt(q, kn_ref[:, ks]), neg)
        m = jnp.maximum(
            jnp.maximum(jnp.max(s_c, axis=1, keepdims=True), jnp.max(s_p, axis=1, keepdims=True)),
            jnp.maximum(jnp.max(s_0, axis=1, keepdims=True), jnp.max(s_n, axis=1, keepdims=True)))
        m = jnp.maximum(m, sink)
        p_c = jnp.exp(s_c - m)
        p_p = jnp.exp(s_p - m)
        p_0 = jnp.exp(s_0 - m)
        p_n = jnp.exp(s_n - m)
        den = (jnp.exp(sink - m) + jnp.sum(p_c, axis=1, keepdims=True) + jnp.sum(p_p, axis=1, keepdims=True)
               + jnp.sum(p_0, axis=1, keepdims=True) + jnp.sum(p_n, axis=1, keepdims=True))
        acc = (_dot(p_c.astype(bf16), vc_ref[:, ks]) + _dot(p_p.astype(bf16), vp_ref[:, ks])
               + _dot(p_0.astype(bf16), v0_ref[:, ks]) + _dot(p_n.astype(bf16), vn_ref[:, ks]))
        out = acc / den
        for g in range(GROUP):
            o_ref[:, (j * GROUP + g) * DH:(j * GROUP + g + 1) * DH] = (
                out[g * CHUNK:(g + 1) * CHUNK, :].astype(bf16))


def _attention(qkv, sink):
    qkv3 = qkv.reshape(B, TOK, QKV)
    kv_w = KV_HEADS * DH
    k_col = HEADS_C * DH // kv_w
    v_col = k_col + 1
    lat0 = CTX // CHUNK
    q_spec = pl.BlockSpec((None, CHUNK, HEADS_C * DH), lambda b, i: (b, lat0 + i, 0))

    def band(col, off):
        def index(b, i):
            return (b, lat0 + jnp.clip(i + off, 0, N_BLK - 1), col)
        return pl.BlockSpec((None, CHUNK, kv_w), index)

    ctx = lambda col: pl.BlockSpec((None, CTX, kv_w), lambda b, i: (b, 0, col))
    return pl.pallas_call(
        _attn_kernel,
        grid=(B, N_BLK),
        in_specs=[
            pl.BlockSpec(memory_space=pltpu.SMEM),
            q_spec,
            ctx(k_col), band(k_col, -1), band(k_col, 0), band(k_col, 1),
            ctx(v_col), band(v_col, -1), band(v_col, 0), band(v_col, 1),
        ],
        out_specs=pl.BlockSpec((CHUNK, HEADS_C * DH), lambda b, i: (b * N_BLK + i, 0)),
        out_shape=jax.ShapeDtypeStruct((R_LAT, HEADS_C * DH), bf16),
        compiler_params=pltpu.CompilerParams(
            dimension_semantics=("parallel", "parallel"), vmem_limit_bytes=32 * MIB),
        name="window_attention",
    )(sink, qkv3, qkv3, qkv3, qkv3, qkv3, qkv3, qkv3, qkv3, qkv3)


def _odd_out_kernel(h_ref, mod_ref, a_ref, w_ref, o_ref):
    o_ref[...] = h_ref[...] + mod_ref[5:6, :] * _dot(a_ref[...], w_ref[...])


def _odd_out(h, mods, attn, w_out):
    tm = TM_FFN
    tile0 = R_CTX // tm
    return pl.pallas_call(
        _odd_out_kernel,
        grid=(R_LAT // tm,),
        in_specs=[
            pl.BlockSpec((tm, D), lambda i: (i + tile0, 0)),
            pl.BlockSpec((None, N_MOD, D), lambda i: (_who_flat(i + tile0, tm), 0, 0)),
            pl.BlockSpec((tm, D), lambda i: (i, 0)),
            pl.BlockSpec((D, D), lambda i: (0, 0)),
        ],
        out_specs=pl.BlockSpec((tm, D), lambda i: (i, 0)),
        out_shape=jax.ShapeDtypeStruct((R_LAT, D), f32),
        compiler_params=pltpu.CompilerParams(
            dimension_semantics=("parallel",), vmem_limit_bytes=32 * MIB),
        name="odd_out",
    )(h, mods, attn, w_out)


def _ffn_weights(w_in, w_out):
    wg = w_in[:, :D_FF].reshape(D, N_FC, FC).transpose(1, 0, 2).astype(bf16)
    wu = w_in[:, D_FF:].reshape(D, N_FC, FC).transpose(1, 0, 2).astype(bf16)
    wo = w_out.reshape(N_FC, FC, D).astype(bf16)
    return wg, wu, wo


def _rope_tables():
    rows = SEQ // GRID_W
    row, col = jnp.meshgrid(jnp.arange(rows), jnp.arange(GRID_W), indexing='ij')
    n_freq = DH // 4
    inv = ROPE_BASE ** (-jnp.arange(n_freq, dtype=f32) / n_freq)
    ang = jnp.concatenate([row.reshape(-1, 1).astype(f32) * inv,
                           col.reshape(-1, 1).astype(f32) * inv], axis=-1)
    cos = jnp.repeat(jnp.cos(ang), 2, axis=1)
    sin = jnp.repeat(jnp.sin(ang), 2, axis=1)
    sign = jnp.where(jnp.arange(DH) % 2 == 0, -1.0, 1.0).astype(f32)
    cos = jnp.tile(cos, (1, LANES // DH))
    sin = jnp.tile(sin * sign, (1, LANES // DH))
    cos = jnp.concatenate([jnp.ones((TM_PROJ, LANES), f32), cos], axis=0)
    sin = jnp.concatenate([jnp.zeros((TM_PROJ, LANES), f32), sin], axis=0)
    return cos, sin


def kernel(x, c, ctx, c_ctx, ada_w, ada_b, ffn_w_in, ffn_w_out, even_w_in, even_w_out, mlstm_conv,
           mlstm_gate_b, mlstm_norm, sgu_norm, sgu_ws, sgu_b, odd_w_qkv, odd_w_out, attn_sink, final_norm):
    cs = jnp.concatenate([c_ctx[None, :], c, jnp.zeros((16 - 1 - B, D), f32)], axis=0)
    mods = _modulation(cs, ada_w, ada_b)[:, :1 + B, :].reshape(2, 1 + B, N_MOD, D)

    h = jnp.concatenate([ctx.reshape(R_CTX, D), x.reshape(R_LAT, D)], axis=0)

    m0 = mods[0]
    h = _ffn(h, m0, *_ffn_weights(ffn_w_in[0, 0], ffn_w_out[0, 0]), mi=0)
    w_in = even_w_in[0]
    gate0 = 4 * W_A
    gate1 = gate0 + 4 * HEADS_A
    w_in = jnp.concatenate(
        [w_in[:, :gate0], w_in[:, gate1:], w_in[:, gate0:gate1],
         jnp.zeros((D, LANES - 4 * HEADS_A), f32)], axis=1).astype(bf16)
    qk, vo, uv, gates = _even_in(h, m0, w_in)
    gate_b = jnp.pad(mlstm_gate_b[0].reshape(1, 4 * HEADS_A), ((0, 0), (0, LANES - 4 * HEADS_A)))
    ha = _mlstm(qk, vo, gates, mlstm_conv[0], gate_b, mlstm_norm[0].reshape(HEADS_A, 1, LANES))
    sbx = jnp.repeat(sgu_b[0].T, LANES, axis=1)
    h = _even_out(h, m0, ha.reshape(B * TOK, W_A), uv, sgu_norm[0].reshape(1, W_A),
                  sgu_ws[0].astype(bf16), sbx, even_w_out[0].astype(bf16))
    h = _ffn(h, m0, *_ffn_weights(ffn_w_in[0, 1], ffn_w_out[0, 1]), mi=6)

    m1 = mods[1]
    h = _ffn(h, m1, *_ffn_weights(ffn_w_in[1, 0], ffn_w_out[1, 0]), mi=0)
    cos_t, sin_t = _rope_tables()
    qkv = _odd_in(h, m1, odd_w_qkv[0].astype(bf16), cos_t, sin_t)
    attn = _attention(qkv, attn_sink[0])
    h = _odd_out(h, m1, attn, odd_w_out[0].astype(bf16))
    out = _ffn(h, m1, *_ffn_weights(ffn_w_in[1, 1], ffn_w_out[1, 1]), mi=6,
               latent_only=True, final_norm=final_norm)
    return out.reshape(B, SEQ, D)
```

```python
import functools

import jax
import jax.numpy as jnp
from jax import lax
from jax.experimental import pallas as pl
from jax.experimental.pallas import tpu as pltpu

f32 = jnp.float32
bf16 = jnp.bfloat16

D = 1024
B = 8
SEQ = 2048
CTX = 256
TOK = CTX + SEQ
GRID_W = 64
N_MOD = 9
D_FF = 2816
EPS = 1e-6
HEADS_A = 4
CHUNK = 128
N_CHUNK = TOK // CHUNK
N_CTX_CHUNK = CTX // CHUNK
W_A = 512
EVEN_COLS = 3200
HEADS_C = 16
KV_HEADS = 4
GROUP = HEADS_C // KV_HEADS
DH = 64
QKV = (HEADS_C + 2 * KV_HEADS) * DH
N_BLK = SEQ // CHUNK
ROPE_BASE = 10000.0
LOG2E = 1.4426950408889634

R_CTX = B * CTX
R_LAT = B * SEQ
R_ALL = R_CTX + R_LAT

LANES = 128
TM_FFN = 512
TM_PROJ = 256
FC = 256
N_FC = D_FF // FC
MIB = 1024 * 1024


def _dot(a, b):
    return jnp.dot(a, b, preferred_element_type=f32)


def _dot_nt(a, b):
    return lax.dot_general(a, b, (((1,), (1,)), ((), ())), preferred_element_type=f32)


def _dot_tn(a, b):
    return lax.dot_general(a, b, (((0,), (0,)), ((), ())), preferred_element_type=f32)


def _sigmoid(x):
    return 1.0 / (1.0 + jnp.exp(-x))


def _split3(x):
    hi = x.astype(bf16)
    r1 = x - hi.astype(f32)
    mid = r1.astype(bf16)
    lo = (r1 - mid.astype(f32)).astype(bf16)
    return hi, mid, lo


def _modulated(h, shift, scale):
    ms = jnp.mean(h * h, axis=-1, keepdims=True)
    return h * lax.rsqrt(ms + EPS) * (1.0 + scale) + shift


def _mod_kernel(c_ref, w_ref, b_ref, o_ref):
    x = c_ref[...]
    s = x * _sigmoid(x)
    w = w_ref[...]
    s_hi = s.astype(bf16)
    s_lo = (s - s_hi.astype(f32)).astype(bf16)
    w_hi = w.astype(bf16)
    w_lo = (w - w_hi.astype(f32)).astype(bf16)
    o_ref[...] = _dot(s_hi, w_hi) + _dot(s_hi, w_lo) + _dot(s_lo, w_hi) + b_ref[...]


def _modulation(cs, ada_w, ada_b):
    depth = ada_w.shape[0]
    rows = cs.shape[0]
    n_col = N_MOD * D
    tn = 1024
    return pl.pallas_call(
        _mod_kernel,
        grid=(depth, n_col // tn),
        in_specs=[
            pl.BlockSpec((rows, D), lambda l, j: (0, 0)),
            pl.BlockSpec((None, D, tn), lambda l, j: (l, 0, j)),
            pl.BlockSpec((None, 1, tn), lambda l, j: (l, 0, j)),
        ],
        out_specs=pl.BlockSpec((None, rows, tn), lambda l, j: (l, 0, j)),
        out_shape=jax.ShapeDtypeStruct((depth, rows, n_col), f32),
        compiler_params=pltpu.CompilerParams(
            dimension_semantics=("parallel", "parallel"), vmem_limit_bytes=32 * MIB),
        name="modulation",
    )(cs, ada_w, ada_b.reshape(depth, 1, n_col))


def _who_flat(tile, tm):
    n_ctx = R_CTX // tm
    per_b = SEQ // tm
    return jnp.where(tile < n_ctx, 0, 1 + jnp.maximum(tile - n_ctx, 0) // per_b)


def _batch_block(tile):
    per_b = SEQ // TM_PROJ
    lat = jnp.maximum(tile - B, 0)
    blocks_b = TOK // TM_PROJ
    return jnp.where(tile < B, blocks_b * tile, blocks_b * (lat // per_b) + 1 + lat % per_b)


def _ffn_kernel(*refs, mi, final, split):
    refs = list(refs)
    if split:
        c_ref, x_ref = refs[0:2]
        refs = refs[2:]
        is_ctx = pl.program_id(0) < R_CTX // TM_FFN
        read_h = lambda: jnp.where(is_ctx, c_ref[...], x_ref[...])
    else:
        h_ref = refs.pop(0)
        read_h = lambda: h_ref[...]
    if final:
        mod_ref, wi_ref, wo_ref, fn_ref, o_ref, n_scr, acc_scr = refs
    else:
        mod_ref, wi_ref, wo_ref, o_ref, n_scr, acc_scr = refs
    n_scr[...] = _modulated(read_h(), mod_ref[mi:mi + 1, :], mod_ref[mi + 1:mi + 2, :]).astype(bf16)
    for j in range(N_FC):
        n = n_scr[...]
        g = _dot(n, wi_ref[:, j * FC:(j + 1) * FC])
        u = _dot(n, wi_ref[:, D_FF + j * FC:D_FF + (j + 1) * FC])
        a = (g * _sigmoid(g) * u).astype(bf16)
        y = _dot(a, wo_ref[j * FC:(j + 1) * FC, :])
        if j == 0:
            acc_scr[...] = y
        else:
            acc_scr[...] += y
    out = read_h() + (0.5 * mod_ref[mi + 2:mi + 3, :]) * acc_scr[...]
    if final:
        ms = jnp.mean(out * out, axis=-1, keepdims=True)
        out = out * lax.rsqrt(ms + EPS) * fn_ref[...]
    o_ref[...] = out


def _ffn(h, mods, w_in, w_out, *, mi, latent_only=False, final_norm=None):
    tm = TM_FFN
    tile0 = R_CTX // tm if latent_only else 0
    split = isinstance(h, tuple)
    const2 = lambda i: (0, 0)
    if split:
        n_ctx = R_CTX // tm
        rows_out = R_ALL
        in_specs = [
            pl.BlockSpec((tm, D), lambda i: (jnp.minimum(i, n_ctx - 1), 0)),
            pl.BlockSpec((tm, D), lambda i: (jnp.maximum(i - n_ctx, 0), 0)),
        ]
        args = list(h)
    else:
        rows_out = h.shape[0]
        in_specs = [pl.BlockSpec((tm, D), lambda i: (i, 0))]
        args = [h]
    in_specs += [
        pl.BlockSpec((None, N_MOD, D), lambda i: (_who_flat(i + tile0, tm), 0, 0)),
        pl.BlockSpec((D, 2 * D_FF), const2, pipeline_mode=pl.Buffered(1)),
        pl.BlockSpec((D_FF, D), const2, pipeline_mode=pl.Buffered(1)),
    ]
    args += [mods, w_in, w_out]
    if final_norm is not None:
        in_specs.append(pl.BlockSpec((1, D), const2))
        args.append(final_norm.reshape(1, D))
    return pl.pallas_call(
        functools.partial(_ffn_kernel, mi=mi, final=final_norm is not None, split=split),
        grid=(rows_out // tm,),
        in_specs=in_specs,
        out_specs=pl.BlockSpec((tm, D), lambda i: (i, 0)),
        out_shape=jax.ShapeDtypeStruct((rows_out, D), f32),
        scratch_shapes=[pltpu.VMEM((tm, D), bf16), pltpu.VMEM((tm, D), f32)],
        compiler_params=pltpu.CompilerParams(
            dimension_semantics=("parallel",), vmem_limit_bytes=48 * MIB),
        name="ffn_final" if final_norm is not None else "ffn",
    )(*args)


def _even_in_kernel(h_ref, mod_ref, w_ref, qk_ref, vo_ref, uv_ref, g_ref):
    n = _modulated(h_ref[...], mod_ref[3:4, :], mod_ref[4:5, :]).astype(bf16)
    qk_ref[...] = _dot(n, w_ref[:, 0:1024])
    vo_ref[...] = _dot(n, w_ref[:, 1024:2048]).astype(bf16)
    uv_ref[...] = _dot(n, w_ref[:, 2048:3072])
    g_ref[...] = _dot(n, w_ref[:, 3072:3200])


def _even_in(h, mods, w):
    tm = TM_PROJ
    out_map = lambda i: (_batch_block(i), 0)
    return pl.pallas_call(
        _even_in_kernel,
        grid=(R_ALL // tm,),
        in_specs=[
            pl.BlockSpec((tm, D), lambda i: (i, 0)),
            pl.BlockSpec((None, N_MOD, D), lambda i: (_who_flat(i, tm), 0, 0)),
            pl.BlockSpec((D, EVEN_COLS), lambda i: (0, 0), pipeline_mode=pl.Buffered(1)),
        ],
        out_specs=[
            pl.BlockSpec((tm, 1024), out_map),
            pl.BlockSpec((tm, 1024), out_map),
            pl.BlockSpec((tm, 1024), out_map),
            pl.BlockSpec((tm, LANES), out_map),
        ],
        out_shape=[
            jax.ShapeDtypeStruct((B * TOK, 1024), f32),
            jax.ShapeDtypeStruct((B * TOK, 1024), bf16),
            jax.ShapeDtypeStruct((B * TOK, 1024), f32),
            jax.ShapeDtypeStruct((B * TOK, LANES), f32),
        ],
        compiler_params=pltpu.CompilerParams(
            dimension_semantics=("parallel",), vmem_limit_bytes=40 * MIB),
        name="even_in",
    )(h, mods, w)


def _mlstm_kernel(q_ref, k_ref, v_ref, o_ref, g_ref, cq_ref, ck_ref, gb_ref, mn_ref, out_ref,
                  qs, ks, xc, xr, hf, hb, cst, nst, mst):
    head = pl.program_id(1)
    rowi = lax.broadcasted_iota(jnp.int32, (CHUNK, CHUNK), 0)
    coli = lax.broadcasted_iota(jnp.int32, (CHUNK, CHUNK), 1)
    lower = coli <= rowi
    upper = coli >= rowi
    tri = jnp.where(lower, 1.0, 0.0).astype(bf16)
    lane_shift = (LANES - head) % LANES

    for c in range(N_CHUNK):
        lo = c * CHUNK
        seq_start = lo in (0, CTX)
        seq_end = lo + CHUNK in (CTX, TOK)
        for src, dst, cw, scale in ((q_ref, qs, cq_ref, None), (k_ref, ks, ck_ref, CHUNK ** -0.5)):
            x = src[lo:lo + CHUNK, :]
            first = jnp.zeros((1, LANES), f32) if seq_start else src[lo - 1:lo, :]
            last = jnp.zeros((1, LANES), f32) if seq_end else src[lo + CHUNK:lo + CHUNK + 1, :]
            prev = jnp.where(rowi == 0, first, pltpu.roll(x, 1, 0))
            nxt = jnp.where(rowi == CHUNK - 1, last, pltpu.roll(x, CHUNK - 1, 0))
            y = cw[0:1, :] * prev + cw[1:2, :] * x + cw[2:3, :] * nxt
            y = y * _sigmoid(y)
            if scale is not None:
                y = y * scale
            dst[lo:lo + CHUNK, :] = y
        gt = pltpu.roll(g_ref[lo:lo + CHUNK, :] + gb_ref[...], lane_shift, 1)
        lf = jnp.minimum(gt, 0.0) - jnp.log1p(jnp.exp(-jnp.abs(gt)))
        hi, mid, lw = _split3(lf)
        pre = _dot(tri, hi) + _dot(tri, mid) + _dot(tri, lw)
        suf = pre[CHUNK - 1:CHUNK, :] - pre + lf
        x = jnp.where(coli == 4, pre, jnp.where(coli == 12, suf, gt))
        xc[lo:lo + CHUNK, :] = x
        xr[16 * c:16 * c + 16, :] = x.T[0:16, :]

    cst[...] = jnp.zeros(cst.shape, f32)
    nst[...] = jnp.zeros(nst.shape, f32)
    mst[...] = jnp.zeros(mst.shape, f32)

    def step(i, carry):
        for d in (0, 1):
            c = i if d == 0 else jnp.where(i < N_CTX_CHUNK, N_CTX_CHUNK - 1 - i, N_CHUNK + N_CTX_CHUNK - 1 - i)
            lo = pl.multiple_of(c * CHUNK, CHUNK)
            q = qs[pl.ds(lo, CHUNK), :]
            k = ks[pl.ds(lo, CHUNK), :]
            v = v_ref[pl.ds(lo, CHUNK), :]
            qb = q.astype(bf16)
            x_c = xc[pl.ds(lo, CHUNK), :]
            x_r = xr[pl.ds(pl.multiple_of(c * 16, 16), 16), :]
            ig_c = x_c[:, 8 * d:8 * d + 1]
            b_c = x_c[:, 8 * d + 4:8 * d + 5]
            ig_r = x_r[8 * d:8 * d + 1, :]
            b_r = x_r[8 * d + 4:8 * d + 5, :]
            c_prev = cst[d]
            n_prev = nst[d]
            m_prev = mst[d]
            if d == 0:
                g = b_c[CHUNK - 1:CHUNK, :]
                causal = lower
            else:
                g = b_c[0:1, :]
                causal = upper
            dm = jnp.where(causal, b_c - b_r + ig_r, -jnp.inf)
            e = b_c + m_prev
            m_t = jnp.maximum(e, jnp.max(dm, axis=1, keepdims=True))
            s = _dot_nt(qb, k.astype(bf16)) * jnp.exp(dm - m_t)
            inter = jnp.exp(e - m_t)
            num = _dot(s.astype(bf16), v) + inter * _dot(qb, c_prev.astype(bf16))
            den = jnp.sum(s, axis=1, keepdims=True) + inter * jnp.sum(q * n_prev, axis=1, keepdims=True)
            hout = num / jnp.maximum(jnp.abs(den), jnp.exp(-m_t))
            if d == 0:
                hf[pl.ds(lo, CHUNK), :] = hout
            else:
                hb[pl.ds(lo, CHUNK), :] = hout
            a_c = g - b_c + ig_c
            m_loc = jnp.max(a_c, axis=0, keepdims=True)
            kw = k * jnp.exp(a_c - m_loc)
            c_loc = _dot_tn(kw.astype(bf16), v)
            n_loc = jnp.sum(kw, axis=0, keepdims=True)
            m_new = jnp.maximum(g + m_prev, m_loc)
            dec = jnp.exp(g + m_prev - m_new)
            add = jnp.exp(m_loc - m_new)
            cst[d] = dec * c_prev + add * c_loc
            nst[d] = dec * n_prev + add * n_loc
            mst[d] = m_new
        return carry

    lax.fori_loop(0, N_CHUNK, step, 0)

    for c in range(N_CHUNK):
        lo = c * CHUNK
        hs = hf[lo:lo + CHUNK, :] + hb[lo:lo + CHUNK, :]
        hn = hs * lax.rsqrt(jnp.mean(hs * hs, axis=-1, keepdims=True) + EPS) * mn_ref[...]
        og = o_ref[lo:lo + CHUNK, :].astype(f32)
        out_ref[lo:lo + CHUNK, :] = (_sigmoid(og) * hn).astype(bf16)


def _mlstm(qk, vo, gates, conv_w, gate_b, mnorm):
    blk = lambda col0: pl.BlockSpec((None, TOK, LANES), lambda b, h: (b, 0, col0 + h))
    return pl.pallas_call(
        _mlstm_kernel,
        grid=(B, HEADS_A),
        in_specs=[
            blk(0), blk(HEADS_A),
            blk(0), blk(HEADS_A),
            pl.BlockSpec((None, TOK, LANES), lambda b, h: (b, 0, 0)),
            pl.BlockSpec((3, LANES), lambda b, h: (0, h)),
            pl.BlockSpec((3, LANES), lambda b, h: (0, HEADS_A + h)),
            pl.BlockSpec((1, LANES), lambda b, h: (0, 0)),
            pl.BlockSpec((None, 1, LANES), lambda b, h: (h, 0, 0)),
        ],
        out_specs=pl.BlockSpec((None, TOK, LANES), lambda b, h: (b, 0, h)),
        out_shape=jax.ShapeDtypeStruct((B, TOK, W_A), bf16),
        scratch_shapes=[
            pltpu.VMEM((TOK, LANES), f32),
            pltpu.VMEM((TOK, LANES), f32),
            pltpu.VMEM((TOK, LANES), f32),
            pltpu.VMEM((16 * N_CHUNK, LANES), f32),
            pltpu.VMEM((TOK, LANES), f32),
            pltpu.VMEM((TOK, LANES), f32),
            pltpu.VMEM((2, CHUNK, LANES), f32),
            pltpu.VMEM((2, 1, LANES), f32),
            pltpu.VMEM((2, 1, 1), f32),
        ],
        compiler_params=pltpu.CompilerParams(
            dimension_semantics=("parallel", "parallel"), vmem_limit_bytes=40 * MIB),
        name="mlstm",
    )(qk.reshape(B, TOK, 1024), qk.reshape(B, TOK, 1024), vo.reshape(B, TOK, 1024),
      vo.reshape(B, TOK, 1024), gates.reshape(B, TOK, LANES), conv_w, conv_w, gate_b, mnorm)


def _gelu_tanh(x):
    return x * (0.5 * (1.0 + jnp.tanh(0.7978845608028654 * (x + 0.044715 * (x * x * x)))))


def _even_out_kernel(h_ref, mod_ref, ha_ref, uv_ref, sg_ref, ws_ref, sb_ref, wo_ref, o_ref, hb_scr):
    u = _gelu_tanh(uv_ref[:, 0:W_A])
    v = _gelu_tanh(uv_ref[:, W_A:2 * W_A])
    vn = (v * lax.rsqrt(jnp.mean(v * v, axis=-1, keepdims=True) + EPS) * sg_ref[...]).astype(bf16)
    for n in range(TM_PROJ // CHUNK):
        r = slice(n * CHUNK, (n + 1) * CHUNK)
        for g in range(W_A // LANES):
            cs = slice(g * LANES, (g + 1) * LANES)
            mixed = _dot(ws_ref[g], vn[r, cs]) + sb_ref[:, cs]
            hb_scr[r, cs] = (u[r, cs] * mixed).astype(bf16)
    y = _dot(ha_ref[...], wo_ref[0:W_A, :]) + _dot(hb_scr[...], wo_ref[W_A:2 * W_A, :])
    o_ref[...] = h_ref[...] + mod_ref[5:6, :] * y


def _even_out(h, mods, ha, uv, sgu_norm, ws, sbx, w_out):
    tm = TM_PROJ
    in_map = lambda i: (_batch_block(i), 0)
    const2 = lambda i: (0, 0)
    return pl.pallas_call(
        _even_out_kernel,
        grid=(R_ALL // tm,),
        in_specs=[
            pl.BlockSpec((tm, D), lambda i: (i, 0)),
            pl.BlockSpec((None, N_MOD, D), lambda i: (_who_flat(i, tm), 0, 0)),
            pl.BlockSpec((tm, W_A), in_map),
            pl.BlockSpec((tm, 2 * W_A), in_map),
            pl.BlockSpec((1, W_A), const2),
            pl.BlockSpec((W_A // LANES, CHUNK, CHUNK), lambda i: (0, 0, 0)),
            pl.BlockSpec((CHUNK, W_A), const2),
            pl.BlockSpec((2 * W_A, D), const2),
        ],
        out_specs=pl.BlockSpec((tm, D), lambda i: (i, 0)),
        out_shape=jax.ShapeDtypeStruct((R_ALL, D), f32),
        scratch_shapes=[pltpu.VMEM((tm, W_A), bf16)],
        compiler_params=pltpu.CompilerParams(
            dimension_semantics=("parallel",), vmem_limit_bytes=32 * MIB),
        name="even_out",
    )(h, mods, ha, uv, sgu_norm, ws, sbx, w_out)


def _odd_in_kernel(h_ref, mod_ref, w_ref, cos_ref, sin_ref, o_ref):
    n = _modulated(h_ref[...], mod_ref[3:4, :], mod_ref[4:5, :]).astype(bf16)
    even_lane = lax.broadcasted_iota(jnp.int32, (TM_PROJ, LANES), 1) % 2 == 0
    cos = cos_ref[...]
    sin = sin_ref[...]
    n_rot = (HEADS_C + KV_HEADS) * DH // LANES
    for c in range(n_rot):
        t = _dot(n, w_ref[:, c * LANES:(c + 1) * LANES])
        swapped = jnp.where(even_lane, pltpu.roll(t, LANES - 1, 1), pltpu.roll(t, 1, 1))
        r = t * cos + swapped * sin
        if c < HEADS_C * DH // LANES:
            r = r * (DH ** -0.5 * LOG2E)
        o_ref[:, c * LANES:(c + 1) * LANES] = r.astype(bf16)
    v0 = n_rot * LANES
    o_ref[:, v0:QKV] = _dot(n, w_ref[:, v0:QKV]).astype(bf16)


def _odd_in(h, mods, w, cos_t, sin_t):
    tm = TM_PROJ
    per_b = SEQ // tm
    rope_map = lambda i: (jnp.where(i < B, 0, 1 + jnp.maximum(i - B, 0) % per_b), 0)
    return pl.pallas_call(
        _odd_in_kernel,
        grid=(R_ALL // tm,),
        in_specs=[
            pl.BlockSpec((tm, D), lambda i: (i, 0)),
            pl.BlockSpec((None, N_MOD, D), lambda i: (_who_flat(i, tm), 0, 0)),
            pl.BlockSpec((D, QKV), lambda i: (0, 0), pipeline_mode=pl.Buffered(1)),
            pl.BlockSpec((tm, LANES), rope_map),
            pl.BlockSpec((tm, LANES), rope_map),
        ],
        out_specs=pl.BlockSpec((tm, QKV), lambda i: (_batch_block(i), 0)),
        out_shape=jax.ShapeDtypeStruct((B * TOK, QKV), bf16),
        compiler_params=pltpu.CompilerParams(
            dimension_semantics=("parallel",), vmem_limit_bytes=32 * MIB),
        name="odd_in",
    )(h, mods, w, cos_t, sin_t)


def _attn_kernel(sink_ref, q_ref, kc_ref, kp_ref, k0_ref, kn_ref, vc_ref, vp_ref, v0_ref, vn_ref, o_ref,
                 p_scr, ot_scr):
    blk = pl.program_id(1)
    cols = GROUP * CHUNK
    kv_w = KV_HEADS * DH
    key = lax.broadcasted_iota(jnp.int32, (CHUNK, cols), 0)
    qry = lax.broadcasted_iota(jnp.int32, (CHUNK, cols), 1) % CHUNK
    far = 1 << 20
    prev_ok = key >= qry + jnp.where(blk > 0, 0, far)
    next_ok = key <= qry - jnp.where(blk < N_BLK - 1, 0, far)
    grp = lax.broadcasted_iota(jnp.int32, (1, cols), 1) // CHUNK
    lane_head = lax.broadcasted_iota(jnp.int32, (CHUNK, kv_w), 1) // DH
    neg = -1e30
    k_all = jnp.concatenate([kc_ref[...], kp_ref[...], k0_ref[...], kn_ref[...]], axis=0)
    v_all = jnp.concatenate([vc_ref[...], vp_ref[...], v0_ref[...], vn_ref[...]], axis=0)
    v_t = v_all.astype(f32).T.astype(bf16)
    n_key = CTX + 3 * CHUNK
    for j in range(KV_HEADS):
        keep = jnp.where(lane_head == j, 1.0, 0.0).astype(bf16)
        q = jnp.concatenate(
            [q_ref[:, g * kv_w:(g + 1) * kv_w] * keep for g in range(GROUP)], axis=0)
        sink = jnp.full((1, cols), sink_ref[j * GROUP], f32)
        for g in range(1, GROUP):
            sink = jnp.where(grp == g, sink_ref[j * GROUP + g], sink)
        sink = sink * LOG2E
        s = _dot_nt(k_all, q)
        s_c = s[0:CTX]
        s_p = jnp.where(prev_ok, s[CTX:CTX + CHUNK], neg)
        s_0 = s[CTX + CHUNK:CTX + 2 * CHUNK]
        s_n = jnp.where(next_ok, s[CTX + 2 * CHUNK:n_key], neg)
        m = jnp.maximum(
            jnp.maximum(jnp.max(s_c, axis=0, keepdims=True), jnp.max(s_p, axis=0, keepdims=True)),
            jnp.maximum(jnp.max(s_0, axis=0, keepdims=True), jnp.max(s_n, axis=0, keepdims=True)))
        m = jnp.maximum(m, sink)
        p_c = jnp.exp2(s_c - m)
        p_p = jnp.exp2(s_p - m)
        p_0 = jnp.exp2(s_0 - m)
        p_n = jnp.exp2(s_n - m)
        den = (jnp.exp2(sink - m) + jnp.sum(p_c, axis=0, keepdims=True) + jnp.sum(p_p, axis=0, keepdims=True)
               + jnp.sum(p_0, axis=0, keepdims=True) + jnp.sum(p_n, axis=0, keepdims=True))
        p_scr[j, 0:CTX, :] = p_c.astype(bf16)
        p_scr[j, CTX:CTX + CHUNK, :] = p_p.astype(bf16)
        p_scr[j, CTX + CHUNK:CTX + 2 * CHUNK, :] = p_0.astype(bf16)
        p_scr[j, CTX + 2 * CHUNK:n_key, :] = p_n.astype(bf16)
        ot_scr[j * DH:(j + 1) * DH, :] = _dot(v_t[j * DH:(j + 1) * DH, :], p_scr[j]) * (1.0 / den)
    for g in range(GROUP):
        o_ref[:, g * kv_w:(g + 1) * kv_w] = ot_scr[:, g * CHUNK:(g + 1) * CHUNK].T.astype(bf16)


def _attention(qkv, sink):
    qkv3 = qkv.reshape(B, TOK, QKV)
    kv_w = KV_HEADS * DH
    k_col = HEADS_C * DH // kv_w
    v_col = k_col + 1
    lat0 = CTX // CHUNK
    q_spec = pl.BlockSpec((None, CHUNK, HEADS_C * DH), lambda b, i: (b, lat0 + i, 0))

    def band(col, off):
        def index(b, i):
            return (b, lat0 + jnp.clip(i + off, 0, N_BLK - 1), col)
        return pl.BlockSpec((None, CHUNK, kv_w), index)

    ctx = lambda col: pl.BlockSpec((None, CTX, kv_w), lambda b, i: (b, 0, col))
    return pl.pallas_call(
        _attn_kernel,
        grid=(B, N_BLK),
        in_specs=[
            pl.BlockSpec(memory_space=pltpu.SMEM),
            q_spec,
            ctx(k_col), band(k_col, -1), band(k_col, 0), band(k_col, 1),
            ctx(v_col), band(v_col, -1), band(v_col, 0), band(v_col, 1),
        ],
        out_specs=pl.BlockSpec((CHUNK, HEADS_C * DH), lambda b, i: (b * N_BLK + i, 0)),
        out_shape=jax.ShapeDtypeStruct((R_LAT, HEADS_C * DH), bf16),
        scratch_shapes=[
            pltpu.VMEM((KV_HEADS, CTX + 3 * CHUNK, GROUP * CHUNK), bf16),
            pltpu.VMEM((KV_HEADS * DH, GROUP * CHUNK), f32),
        ],
        compiler_params=pltpu.CompilerParams(
            dimension_semantics=("parallel", "parallel"), vmem_limit_bytes=32 * MIB),
        name="window_attention",
    )(sink, qkv3, qkv3, qkv3, qkv3, qkv3, qkv3, qkv3, qkv3, qkv3)


def _odd_out_kernel(h_ref, mod_ref, a_ref, w_ref, o_ref):
    o_ref[...] = h_ref[...] + mod_ref[5:6, :] * _dot(a_ref[...], w_ref[...])


def _odd_out(h, mods, attn, w_out):
    tm = TM_FFN
    tile0 = R_CTX // tm
    return pl.pallas_call(
        _odd_out_kernel,
        grid=(R_LAT // tm,),
        in_specs=[
            pl.BlockSpec((tm, D), lambda i: (i + tile0, 0)),
            pl.BlockSpec((None, N_MOD, D), lambda i: (_who_flat(i + tile0, tm), 0, 0)),
            pl.BlockSpec((tm, D), lambda i: (i, 0)),
            pl.BlockSpec((D, D), lambda i: (0, 0)),
        ],
        out_specs=pl.BlockSpec((tm, D), lambda i: (i, 0)),
        out_shape=jax.ShapeDtypeStruct((R_LAT, D), f32),
        compiler_params=pltpu.CompilerParams(
            dimension_semantics=("parallel",), vmem_limit_bytes=32 * MIB),
        name="odd_out",
    )(h, mods, attn, w_out)


def _rope_tables():
    rows = SEQ // GRID_W
    row, col = jnp.meshgrid(jnp.arange(rows), jnp.arange(GRID_W), indexing='ij')
    n_freq = DH // 4
    inv = ROPE_BASE ** (-jnp.arange(n_freq, dtype=f32) / n_freq)
    ang = jnp.concatenate([row.reshape(-1, 1).astype(f32) * inv,
                           col.reshape(-1, 1).astype(f32) * inv], axis=-1)
    cos = jnp.repeat(jnp.cos(ang), 2, axis=1)
    sin = jnp.repeat(jnp.sin(ang), 2, axis=1)
    sign = jnp.where(jnp.arange(DH) % 2 == 0, -1.0, 1.0).astype(f32)
    cos = jnp.tile(cos, (1, LANES // DH))
    sin = jnp.tile(sin * sign, (1, LANES // DH))
    cos = jnp.concatenate([jnp.ones((TM_PROJ, LANES), f32), cos], axis=0)
    sin = jnp.concatenate([jnp.zeros((TM_PROJ, LANES), f32), sin], axis=0)
    return cos, sin


def kernel(x, c, ctx, c_ctx, ada_w, ada_b, ffn_w_in, ffn_w_out, even_w_in, even_w_out, mlstm_conv,
           mlstm_gate_b, mlstm_norm, sgu_norm, sgu_ws, sgu_b, odd_w_qkv, odd_w_out, attn_sink, final_norm):
    cs = jnp.concatenate([c_ctx[None, :], c, jnp.zeros((16 - 1 - B, D), f32)], axis=0)
    mods = _modulation(cs, ada_w, ada_b)[:, :1 + B, :].reshape(2, 1 + B, N_MOD, D)

    fw_in = ffn_w_in.astype(bf16)
    fw_out = ffn_w_out.astype(bf16)

    m0 = mods[0]
    h = _ffn((ctx.reshape(R_CTX, D), x.reshape(R_LAT, D)), m0, fw_in[0, 0], fw_out[0, 0], mi=0)
    w_in = even_w_in[0]
    gate0 = 4 * W_A
    gate1 = gate0 + 4 * HEADS_A
    w_in = jnp.concatenate(
        [w_in[:, :gate0], w_in[:, gate1:], w_in[:, gate0:gate1],
         jnp.zeros((D, LANES - 4 * HEADS_A), f32)], axis=1).astype(bf16)
    qk, vo, uv, gates = _even_in(h, m0, w_in)
    gate_b = jnp.pad(mlstm_gate_b[0].reshape(1, 4 * HEADS_A), ((0, 0), (0, LANES - 4 * HEADS_A)))
    ha = _mlstm(qk, vo, gates, mlstm_conv[0], gate_b, mlstm_norm[0].reshape(HEADS_A, 1, LANES))
    sbx = jnp.repeat(sgu_b[0].T, LANES, axis=1)
    h = _even_out(h, m0, ha.reshape(B * TOK, W_A), uv, sgu_norm[0].reshape(1, W_A),
                  sgu_ws[0].astype(bf16), sbx, even_w_out[0].astype(bf16))
    h = _ffn(h, m0, fw_in[0, 1], fw_out[0, 1], mi=6)

    m1 = mods[1]
    h = _ffn(h, m1, fw_in[1, 0], fw_out[1, 0], mi=0)
    cos_t, sin_t = _rope_tables()
    qdim = HEADS_C * DH
    w_q = odd_w_qkv[0][:, :qdim].reshape(D, KV_HEADS, GROUP, DH).transpose(0, 2, 1, 3).reshape(D, qdim)
    w_qkv = jnp.concatenate([w_q, odd_w_qkv[0][:, qdim:]], axis=1).astype(bf16)
    w_o = odd_w_out[0].reshape(KV_HEADS, GROUP, DH, D).transpose(1, 0, 2, 3).reshape(qdim, D).astype(bf16)
    qkv = _odd_in(h, m1, w_qkv, cos_t, sin_t)
    attn = _attention(qkv, attn_sink[0])
    h = _odd_out(h, m1, attn, w_o)
    out = _ffn(h, m1, fw_in[1, 1], fw_out[1, 1], mi=6,
               latent_only=True, final_norm=final_norm)
    return out.reshape(B, SEQ, D)
```

```python
import functools

import jax
import jax.numpy as jnp
from jax import lax
from jax.experimental import pallas as pl
from jax.experimental.pallas import tpu as pltpu

f32 = jnp.float32
bf16 = jnp.bfloat16

D = 1024
B = 8
SEQ = 2048
CTX = 256
TOK = CTX + SEQ
GRID_W = 64
N_MOD = 9
D_FF = 2816
EPS = 1e-6
HEADS_A = 4
CHUNK = 128
N_CHUNK = TOK // CHUNK
N_CTX_CHUNK = CTX // CHUNK
W_A = 512
EVEN_COLS = 3200
HEADS_C = 16
KV_HEADS = 4
GROUP = HEADS_C // KV_HEADS
DH = 64
QKV = (HEADS_C + 2 * KV_HEADS) * DH
N_BLK = SEQ // CHUNK
ROPE_BASE = 10000.0
LOG2E = 1.4426950408889634

R_CTX = B * CTX
R_LAT = B * SEQ
R_ALL = R_CTX + R_LAT

LANES = 128
TM_FFN = 512
TM_PROJ = 256
FC = 256
N_FC = D_FF // FC
MIB = 1024 * 1024


def _dot(a, b):
    return jnp.dot(a, b, preferred_element_type=f32)


def _dot_nt(a, b):
    return lax.dot_general(a, b, (((1,), (1,)), ((), ())), preferred_element_type=f32)


def _dot_tn(a, b):
    return lax.dot_general(a, b, (((0,), (0,)), ((), ())), preferred_element_type=f32)


def _sigmoid(x):
    return 1.0 / (1.0 + jnp.exp(-x))


def _split3(x):
    hi = x.astype(bf16)
    r1 = x - hi.astype(f32)
    mid = r1.astype(bf16)
    lo = (r1 - mid.astype(f32)).astype(bf16)
    return hi, mid, lo


def _modulated(h, shift, scale):
    ms = jnp.mean(h * h, axis=-1, keepdims=True)
    return h * lax.rsqrt(ms + EPS) * (1.0 + scale) + shift


def _mod_kernel(c_ref, w_ref, b_ref, o_ref):
    x = c_ref[...]
    s = x * _sigmoid(x)
    w = w_ref[...]
    s_hi = s.astype(bf16)
    s_lo = (s - s_hi.astype(f32)).astype(bf16)
    w_hi = w.astype(bf16)
    w_lo = (w - w_hi.astype(f32)).astype(bf16)
    o_ref[...] = _dot(s_hi, w_hi) + _dot(s_hi, w_lo) + _dot(s_lo, w_hi) + b_ref[...]


def _modulation(cs, ada_w, ada_b):
    depth = ada_w.shape[0]
    rows = cs.shape[0]
    n_col = N_MOD * D
    tn = 1024
    return pl.pallas_call(
        _mod_kernel,
        grid=(depth, n_col // tn),
        in_specs=[
            pl.BlockSpec((rows, D), lambda l, j: (0, 0)),
            pl.BlockSpec((None, D, tn), lambda l, j: (l, 0, j)),
            pl.BlockSpec((None, 1, tn), lambda l, j: (l, 0, j)),
        ],
        out_specs=pl.BlockSpec((None, rows, tn), lambda l, j: (l, 0, j)),
        out_shape=jax.ShapeDtypeStruct((depth, rows, n_col), f32),
        compiler_params=pltpu.CompilerParams(
            dimension_semantics=("parallel", "parallel"), vmem_limit_bytes=32 * MIB),
        name="modulation",
    )(cs, ada_w, ada_b.reshape(depth, 1, n_col))


def _who_flat(tile, tm):
    n_ctx = R_CTX // tm
    per_b = SEQ // tm
    return jnp.where(tile < n_ctx, 0, 1 + jnp.maximum(tile - n_ctx, 0) // per_b)


def _batch_block(tile):
    per_b = SEQ // TM_PROJ
    lat = jnp.maximum(tile - B, 0)
    blocks_b = TOK // TM_PROJ
    return jnp.where(tile < B, blocks_b * tile, blocks_b * (lat // per_b) + 1 + lat % per_b)


def _ffn_kernel(*refs, mi, final, split):
    refs = list(refs)
    if split:
        c_ref, x_ref = refs[0:2]
        refs = refs[2:]
        is_ctx = pl.program_id(0) < R_CTX // TM_FFN
        read_h = lambda: jnp.where(is_ctx, c_ref[...], x_ref[...])
    else:
        h_ref = refs.pop(0)
        read_h = lambda: h_ref[...]
    if final:
        mod_ref, wi_ref, wo_ref, fn_ref, o_ref, n_scr, acc_scr = refs
    else:
        mod_ref, wi_ref, wo_ref, o_ref, n_scr, acc_scr = refs
    n_scr[...] = _modulated(read_h(), mod_ref[mi:mi + 1, :], mod_ref[mi + 1:mi + 2, :]).astype(bf16)
    for j in range(N_FC):
        n = n_scr[...]
        g = _dot(n, wi_ref[:, j * FC:(j + 1) * FC])
        u = _dot(n, wi_ref[:, D_FF + j * FC:D_FF + (j + 1) * FC])
        a = (g * _sigmoid(g) * u).astype(bf16)
        y = _dot(a, wo_ref[j * FC:(j + 1) * FC, :])
        if j == 0:
            acc_scr[...] = y
        else:
            acc_scr[...] += y
    out = read_h() + (0.5 * mod_ref[mi + 2:mi + 3, :]) * acc_scr[...]
    if final:
        ms = jnp.mean(out * out, axis=-1, keepdims=True)
        out = out * lax.rsqrt(ms + EPS) * fn_ref[...]
    o_ref[...] = out


def _ffn(h, mods, w_in, w_out, *, sel, mi, latent_only=False, final_norm=None):
    tm = TM_FFN
    tile0 = R_CTX // tm if latent_only else 0
    split = isinstance(h, tuple)
    const2 = lambda i: (0, 0)
    if split:
        n_ctx = R_CTX // tm
        rows_out = R_ALL
        in_specs = [
            pl.BlockSpec((tm, D), lambda i: (jnp.minimum(i, n_ctx - 1), 0)),
            pl.BlockSpec((tm, D), lambda i: (jnp.maximum(i - n_ctx, 0), 0)),
        ]
        args = list(h)
    else:
        rows_out = h.shape[0]
        in_specs = [pl.BlockSpec((tm, D), lambda i: (i, 0))]
        args = [h]
    in_specs += [
        pl.BlockSpec((None, N_MOD, D), lambda i: (_who_flat(i + tile0, tm), 0, 0)),
        pl.BlockSpec((None, None, D, 2 * D_FF), lambda i: sel + (0, 0), pipeline_mode=pl.Buffered(1)),
        pl.BlockSpec((None, None, D_FF, D), lambda i: sel + (0, 0), pipeline_mode=pl.Buffered(1)),
    ]
    args += [mods, w_in, w_out]
    if final_norm is not None:
        in_specs.append(pl.BlockSpec((1, D), const2))
        args.append(final_norm.reshape(1, D))
    return pl.pallas_call(
        functools.partial(_ffn_kernel, mi=mi, final=final_norm is not None, split=split),
        grid=(rows_out // tm,),
        in_specs=in_specs,
        out_specs=pl.BlockSpec((tm, D), lambda i: (i, 0)),
        out_shape=jax.ShapeDtypeStruct((rows_out, D), f32),
        scratch_shapes=[pltpu.VMEM((tm, D), bf16), pltpu.VMEM((tm, D), f32)],
        compiler_params=pltpu.CompilerParams(
            dimension_semantics=("parallel",), vmem_limit_bytes=48 * MIB),
        name="ffn_final" if final_norm is not None else "ffn",
    )(*args)


def _even_in_kernel(h_ref, mod_ref, w_ref, qk_ref, vo_ref, uv_ref, g_ref):
    n = _modulated(h_ref[...], mod_ref[3:4, :], mod_ref[4:5, :]).astype(bf16)
    qk_ref[...] = _dot(n, w_ref[:, 0:1024])
    vo_ref[:, 0:W_A] = _dot(n, w_ref[:, 1024:1024 + W_A]).astype(bf16)
    vo_ref[:, W_A:2 * W_A] = _sigmoid(_dot(n, w_ref[:, 1024 + W_A:2048])).astype(bf16)
    uv_ref[...] = _dot(n, w_ref[:, 2048:3072])
    g_ref[...] = _dot(n, w_ref[:, 3072:3200])


def _even_in(h, mods, w):
    tm = TM_PROJ
    out_map = lambda i: (_batch_block(i), 0)
    return pl.pallas_call(
        _even_in_kernel,
        grid=(R_ALL // tm,),
        in_specs=[
            pl.BlockSpec((tm, D), lambda i: (i, 0)),
            pl.BlockSpec((None, N_MOD, D), lambda i: (_who_flat(i, tm), 0, 0)),
            pl.BlockSpec((D, EVEN_COLS), lambda i: (0, 0), pipeline_mode=pl.Buffered(1)),
        ],
        out_specs=[
            pl.BlockSpec((tm, 1024), out_map),
            pl.BlockSpec((tm, 1024), out_map),
            pl.BlockSpec((tm, 1024), out_map),
            pl.BlockSpec((tm, LANES), out_map),
        ],
        out_shape=[
            jax.ShapeDtypeStruct((B * TOK, 1024), f32),
            jax.ShapeDtypeStruct((B * TOK, 1024), bf16),
            jax.ShapeDtypeStruct((B * TOK, 1024), f32),
            jax.ShapeDtypeStruct((B * TOK, LANES), f32),
        ],
        compiler_params=pltpu.CompilerParams(
            dimension_semantics=("parallel",), vmem_limit_bytes=40 * MIB),
        name="even_in",
    )(h, mods, w)


CHUNKS_PER_ITER = 9


def _chunk_loop(body):
    def group(i, carry):
        for u in range(CHUNKS_PER_ITER):
            carry = body(i * CHUNKS_PER_ITER + u, carry)
        return carry
    lax.fori_loop(0, N_CHUNK // CHUNKS_PER_ITER, group, 0)


def _mlstm_kernel(q_ref, k_ref, v_ref, o_ref, g_ref, cq_ref, ck_ref, gb_ref, mn_ref, out_ref,
                  ks, qts, vts, xc, xr, cl, st, cst, mst):
    head = pl.program_id(1)
    rowi = lax.broadcasted_iota(jnp.int32, (CHUNK, CHUNK), 0)
    coli = lax.broadcasted_iota(jnp.int32, (CHUNK, CHUNK), 1)
    lower = coli <= rowi
    upper = coli >= rowi
    tri = jnp.where(lower, 1.0, 0.0).astype(bf16)

    for c in range(N_CHUNK):
        lo = c * CHUNK
        seq_start = lo in (0, CTX)
        seq_end = lo + CHUNK in (CTX, TOK)
        for src, cw, is_q in ((q_ref, cq_ref, True), (k_ref, ck_ref, False)):
            x = src[lo:lo + CHUNK, :]
            first = jnp.zeros((1, LANES), f32) if seq_start else src[lo - 1:lo, :]
            last = jnp.zeros((1, LANES), f32) if seq_end else src[lo + CHUNK:lo + CHUNK + 1, :]
            prev = jnp.where(rowi == 0, first, pltpu.roll(x, 1, 0))
            nxt = jnp.where(rowi == CHUNK - 1, last, pltpu.roll(x, CHUNK - 1, 0))
            y = cw[0:1, :] * prev + cw[1:2, :] * x + cw[2:3, :] * nxt
            y = y * _sigmoid(y)
            if is_q:
                qts[lo:lo + CHUNK, :] = y.T.astype(bf16)
            else:
                ks[lo:lo + CHUNK, :] = (y * CHUNK ** -0.5).astype(bf16)
        vts[lo:lo + CHUNK, :] = v_ref[lo:lo + CHUNK, :].astype(f32).T.astype(bf16)

    @pl.when(head == 0)
    def _():
        kind = (coli // HEADS_A) % 4
        for c in range(N_CHUNK):
            lo = c * CHUNK
            gt = g_ref[lo:lo + CHUNK, :] + gb_ref[...]
            lf = jnp.minimum(gt, 0.0) - jnp.log1p(jnp.exp(-jnp.abs(gt)))
            hi, mid, lw = _split3(lf)
            pre = _dot(tri, hi) + _dot(tri, mid) + _dot(tri, lw)
            suf = pre[CHUNK - 1:CHUNK, :] - pre + lf
            x = jnp.where(kind == 1, pre, jnp.where(kind == 3, suf, gt))
            xr[16 * c:16 * c + 16, :] = x.T[0:16, :]
            xc[lo:lo + CHUNK, :] = x

    lane_shift = (LANES - head) % LANES

    def gate_rows(c, d):
        row = c * 16 + 8 * d + head
        return xr[pl.ds(row, 1), :], xr[pl.ds(row + HEADS_A, 1), :]

    def local_state(c, carry):
        lo = pl.multiple_of(c * CHUNK, CHUNK)
        k = ks[pl.ds(lo, CHUNK), :]
        v_t = vts[pl.ds(lo, CHUNK), :].astype(f32)
        for d in (0, 1):
            ig_r, b_r = gate_rows(c, d)
            g = b_r[:, CHUNK - 1:CHUNK] if d == 0 else b_r[:, 0:1]
            a_r = g - b_r + ig_r
            m_loc = jnp.max(a_r, axis=1, keepdims=True)
            w_r = jnp.exp(a_r - m_loc)
            lhs = jnp.concatenate([v_t * w_r, jnp.broadcast_to(w_r, (16, LANES))], axis=0).astype(bf16)
            idx = d * N_CHUNK + c
            cl[idx] = _dot(lhs, k)
            st[idx, 0:1, :] = jnp.broadcast_to(m_loc, (1, LANES))
            st[idx, 1:2, :] = jnp.broadcast_to(g, (1, LANES))
        return carry

    _chunk_loop(local_state)

    cst[...] = jnp.zeros(cst.shape, f32)
    mst[...] = jnp.zeros(mst.shape, f32)

    def scan_step(i, carry):
        for d in (0, 1):
            c = i if d == 0 else jnp.where(i < N_CTX_CHUNK, N_CTX_CHUNK - 1 - i, N_CHUNK + N_CTX_CHUNK - 1 - i)
            idx = d * N_CHUNK + c
            c_loc = cl[idx]
            m_loc = st[idx, 0:1, :]
            g = st[idx, 1:2, :]
            c_prev = cst[d]
            m_prev = mst[d, 0:1, :]
            m_new = jnp.maximum(g + m_prev, m_loc)
            dec = jnp.exp(g + m_prev - m_new)
            add = jnp.exp(m_loc - m_new)
            cl[idx] = c_prev
            st[idx, 2:3, :] = m_prev
            cst[d] = dec * c_prev + add * c_loc
            mst[d, 0:1, :] = m_new
        return carry

    lax.fori_loop(0, N_CHUNK, scan_step, 0)

    def outputs(c, carry):
        lo = pl.multiple_of(c * CHUNK, CHUNK)
        k = ks[pl.ds(lo, CHUNK), :]
        q_t = qts[pl.ds(lo, CHUNK), :]
        v_t = vts[pl.ds(lo, CHUNK), :]
        s_t = _dot(k, q_t)
        x_c = pltpu.roll(xc[pl.ds(lo, CHUNK), :], lane_shift, 1)
        hs = None
        for d in (0, 1):
            _, b_r = gate_rows(c, d)
            idx = d * N_CHUNK + c
            r_c = x_c[:, 8 * d:8 * d + 1] - x_c[:, 8 * d + 4:8 * d + 5]
            dm = jnp.where(upper if d == 0 else lower, b_r + r_c, -jnp.inf)
            e_r = b_r + st[idx, 2:3, :]
            m_t = jnp.maximum(e_r, jnp.max(dm, axis=0, keepdims=True))
            p_t = s_t * jnp.exp(dm - m_t)
            inter = jnp.exp(e_r - m_t)
            u = _dot(cl[idx].astype(bf16), q_t)
            num = _dot(v_t, p_t.astype(bf16)) + inter * u[0:CHUNK, :]
            den = jnp.sum(p_t, axis=0, keepdims=True) + inter * u[CHUNK:CHUNK + 1, :]
            h_d = num * (1.0 / jnp.maximum(jnp.abs(den), jnp.exp(-m_t)))
            hs = h_d if hs is None else hs + h_d
        hn = hs * lax.rsqrt(jnp.mean(hs * hs, axis=0, keepdims=True) + EPS) * mn_ref[...]
        out_ref[pl.ds(lo, CHUNK), :] = (o_ref[pl.ds(lo, CHUNK), :].astype(f32) * hn.T).astype(bf16)
        return carry

    _chunk_loop(outputs)


def _mlstm(qk, vo, gates, conv_w, gate_b, mnorm):
    blk = lambda col0: pl.BlockSpec((None, TOK, LANES), lambda b, h: (b, 0, col0 + h))
    return pl.pallas_call(
        _mlstm_kernel,
        grid=(B, HEADS_A),
        in_specs=[
            blk(0), blk(HEADS_A),
            blk(0), blk(HEADS_A),
            pl.BlockSpec((None, TOK, LANES), lambda b, h: (b, 0, 0)),
            pl.BlockSpec((3, LANES), lambda b, h: (0, h)),
            pl.BlockSpec((3, LANES), lambda b, h: (0, HEADS_A + h)),
            pl.BlockSpec((1, LANES), lambda b, h: (0, 0)),
            pl.BlockSpec((None, CHUNK, LANES), lambda b, h: (h, 0, 0)),
        ],
        out_specs=pl.BlockSpec((None, TOK, LANES), lambda b, h: (b, 0, h)),
        out_shape=jax.ShapeDtypeStruct((B, TOK, W_A), bf16),
        scratch_shapes=[
            pltpu.VMEM((TOK, LANES), bf16),
            pltpu.VMEM((TOK, LANES), bf16),
            pltpu.VMEM((TOK, LANES), bf16),
            pltpu.VMEM((TOK, LANES), f32),
            pltpu.VMEM((16 * N_CHUNK, LANES), f32),
            pltpu.VMEM((2 * N_CHUNK, CHUNK + 16, LANES), f32),
            pltpu.VMEM((2 * N_CHUNK, 8, LANES), f32),
            pltpu.VMEM((2, CHUNK + 16, LANES), f32),
            pltpu.VMEM((2, 8, LANES), f32),
        ],
        compiler_params=pltpu.CompilerParams(
            dimension_semantics=("parallel", "arbitrary"), vmem_limit_bytes=40 * MIB),
        name="mlstm",
    )(qk.reshape(B, TOK, 1024), qk.reshape(B, TOK, 1024), vo.reshape(B, TOK, 1024),
      vo.reshape(B, TOK, 1024), gates.reshape(B, TOK, LANES), conv_w, conv_w, gate_b, mnorm)


def _gelu_tanh(x):
    return x * (0.5 * (1.0 + jnp.tanh(0.7978845608028654 * (x + 0.044715 * (x * x * x)))))


def _even_out_kernel(h_ref, mod_ref, ha_ref, uv_ref, sg_ref, ws_ref, sb_ref, wo_ref, o_ref, hb_scr):
    u = _gelu_tanh(uv_ref[:, 0:W_A])
    v = _gelu_tanh(uv_ref[:, W_A:2 * W_A])
    vn = (v * lax.rsqrt(jnp.mean(v * v, axis=-1, keepdims=True) + EPS) * sg_ref[...]).astype(bf16)
    for n in range(TM_PROJ // CHUNK):
        r = slice(n * CHUNK, (n + 1) * CHUNK)
        for g in range(W_A // LANES):
            cs = slice(g * LANES, (g + 1) * LANES)
            mixed = _dot(ws_ref[g], vn[r, cs]) + sb_ref[:, cs]
            hb_scr[r, cs] = (u[r, cs] * mixed).astype(bf16)
    y = _dot(ha_ref[...], wo_ref[0:W_A, :]) + _dot(hb_scr[...], wo_ref[W_A:2 * W_A, :])
    o_ref[...] = h_ref[...] + mod_ref[5:6, :] * y


def _even_out(h, mods, ha, uv, sgu_norm, ws, sbx, w_out):
    tm = TM_PROJ
    in_map = lambda i: (_batch_block(i), 0)
    const2 = lambda i: (0, 0)
    return pl.pallas_call(
        _even_out_kernel,
        grid=(R_ALL // tm,),
        in_specs=[
            pl.BlockSpec((tm, D), lambda i: (i, 0)),
            pl.BlockSpec((None, N_MOD, D), lambda i: (_who_flat(i, tm), 0, 0)),
            pl.BlockSpec((tm, W_A), in_map),
            pl.BlockSpec((tm, 2 * W_A), in_map),
            pl.BlockSpec((1, W_A), const2),
            pl.BlockSpec((W_A // LANES, CHUNK, CHUNK), lambda i: (0, 0, 0)),
            pl.BlockSpec((CHUNK, W_A), const2),
            pl.BlockSpec((2 * W_A, D), const2),
        ],
        out_specs=pl.BlockSpec((tm, D), lambda i: (i, 0)),
        out_shape=jax.ShapeDtypeStruct((R_ALL, D), f32),
        scratch_shapes=[pltpu.VMEM((tm, W_A), bf16)],
        compiler_params=pltpu.CompilerParams(
            dimension_semantics=("parallel",), vmem_limit_bytes=32 * MIB),
        name="even_out",
    )(h, mods, ha, uv, sgu_norm, ws, sbx, w_out)


def _odd_in_kernel(h_ref, mod_ref, w_ref, cos_ref, sin_ref, o_ref):
    n = _modulated(h_ref[...], mod_ref[3:4, :], mod_ref[4:5, :]).astype(bf16)
    even_lane = lax.broadcasted_iota(jnp.int32, (TM_PROJ, LANES), 1) % 2 == 0
    cos = cos_ref[...]
    sin = sin_ref[...]
    n_rot = (HEADS_C + KV_HEADS) * DH // LANES
    for c in range(n_rot):
        t = _dot(n, w_ref[:, c * LANES:(c + 1) * LANES])
        swapped = jnp.where(even_lane, pltpu.roll(t, LANES - 1, 1), pltpu.roll(t, 1, 1))
        r = t * cos + swapped * sin
        if c < HEADS_C * DH // LANES:
            r = r * (DH ** -0.5 * LOG2E)
        o_ref[:, c * LANES:(c + 1) * LANES] = r.astype(bf16)
    v0 = n_rot * LANES
    o_ref[:, v0:QKV] = _dot(n, w_ref[:, v0:QKV]).astype(bf16)


def _odd_in(h, mods, w, cos_t, sin_t):
    tm = TM_PROJ
    per_b = SEQ // tm
    rope_map = lambda i: (jnp.where(i < B, 0, 1 + jnp.maximum(i - B, 0) % per_b), 0)
    return pl.pallas_call(
        _odd_in_kernel,
        grid=(R_ALL // tm,),
        in_specs=[
            pl.BlockSpec((tm, D), lambda i: (i, 0)),
            pl.BlockSpec((None, N_MOD, D), lambda i: (_who_flat(i, tm), 0, 0)),
            pl.BlockSpec((D, QKV), lambda i: (0, 0), pipeline_mode=pl.Buffered(1)),
            pl.BlockSpec((tm, LANES), rope_map),
            pl.BlockSpec((tm, LANES), rope_map),
        ],
        out_specs=pl.BlockSpec((tm, QKV), lambda i: (_batch_block(i), 0)),
        out_shape=jax.ShapeDtypeStruct((B * TOK, QKV), bf16),
        compiler_params=pltpu.CompilerParams(
            dimension_semantics=("parallel",), vmem_limit_bytes=32 * MIB),
        name="odd_in",
    )(h, mods, w, cos_t, sin_t)


def _attn_kernel(sink_ref, q_ref, kc_ref, kp_ref, k0_ref, kn_ref, vc_ref, vp_ref, v0_ref, vn_ref, o_ref,
                 p_scr, ot_scr):
    blk = pl.program_id(1)
    cols = GROUP * CHUNK
    kv_w = KV_HEADS * DH
    key = lax.broadcasted_iota(jnp.int32, (CHUNK, cols), 0)
    qry = lax.broadcasted_iota(jnp.int32, (CHUNK, cols), 1) % CHUNK
    far = 1 << 20
    prev_ok = key >= qry + jnp.where(blk > 0, 0, far)
    next_ok = key <= qry - jnp.where(blk < N_BLK - 1, 0, far)
    grp = lax.broadcasted_iota(jnp.int32, (1, cols), 1) // CHUNK
    lane_head = lax.broadcasted_iota(jnp.int32, (CHUNK, kv_w), 1) // DH
    neg = -1e30
    k_all = jnp.concatenate([kc_ref[...], kp_ref[...], k0_ref[...], kn_ref[...]], axis=0)
    v_all = jnp.concatenate([vc_ref[...], vp_ref[...], v0_ref[...], vn_ref[...]], axis=0)
    v_t = v_all.astype(f32).T.astype(bf16)
    n_key = CTX + 3 * CHUNK
    for j in range(KV_HEADS):
        keep = jnp.where(lane_head == j, 1.0, 0.0).astype(bf16)
        q = jnp.concatenate(
            [q_ref[:, g * kv_w:(g + 1) * kv_w] * keep for g in range(GROUP)], axis=0)
        sink = jnp.full((1, cols), sink_ref[j * GROUP], f32)
        for g in range(1, GROUP):
            sink = jnp.where(grp == g, sink_ref[j * GROUP + g], sink)
        sink = sink * LOG2E
        s = _dot_nt(k_all, q)
        s_c = s[0:CTX]
        s_p = jnp.where(prev_ok, s[CTX:CTX + CHUNK], neg)
        s_0 = s[CTX + CHUNK:CTX + 2 * CHUNK]
        s_n = jnp.where(next_ok, s[CTX + 2 * CHUNK:n_key], neg)
        m = jnp.maximum(
            jnp.maximum(jnp.max(s_c, axis=0, keepdims=True), jnp.max(s_p, axis=0, keepdims=True)),
            jnp.maximum(jnp.max(s_0, axis=0, keepdims=True), jnp.max(s_n, axis=0, keepdims=True)))
        m = jnp.maximum(m, sink)
        p_c = jnp.exp2(s_c - m)
        p_p = jnp.exp2(s_p - m)
        p_0 = jnp.exp2(s_0 - m)
        p_n = jnp.exp2(s_n - m)
        den = (jnp.exp2(sink - m) + jnp.sum(p_c, axis=0, keepdims=True) + jnp.sum(p_p, axis=0, keepdims=True)
               + jnp.sum(p_0, axis=0, keepdims=True) + jnp.sum(p_n, axis=0, keepdims=True))
        p_scr[j, 0:CTX, :] = p_c.astype(bf16)
        p_scr[j, CTX:CTX + CHUNK, :] = p_p.astype(bf16)
        p_scr[j, CTX + CHUNK:CTX + 2 * CHUNK, :] = p_0.astype(bf16)
        p_scr[j, CTX + 2 * CHUNK:n_key, :] = p_n.astype(bf16)
        ot_scr[j * DH:(j + 1) * DH, :] = _dot(v_t[j * DH:(j + 1) * DH, :], p_scr[j]) * (1.0 / den)
    for g in range(GROUP):
        o_ref[:, g * kv_w:(g + 1) * kv_w] = ot_scr[:, g * CHUNK:(g + 1) * CHUNK].T.astype(bf16)


def _attention(qkv, sink):
    qkv3 = qkv.reshape(B, TOK, QKV)
    kv_w = KV_HEADS * DH
    k_col = HEADS_C * DH // kv_w
    v_col = k_col + 1
    lat0 = CTX // CHUNK
    q_spec = pl.BlockSpec((None, CHUNK, HEADS_C * DH), lambda b, i: (b, lat0 + i, 0))

    def band(col, off):
        def index(b, i):
            return (b, lat0 + jnp.clip(i + off, 0, N_BLK - 1), col)
        return pl.BlockSpec((None, CHUNK, kv_w), index)

    ctx = lambda col: pl.BlockSpec((None, CTX, kv_w), lambda b, i: (b, 0, col))
    return pl.pallas_call(
        _attn_kernel,
        grid=(B, N_BLK),
        in_specs=[
            pl.BlockSpec(memory_space=pltpu.SMEM),
            q_spec,
            ctx(k_col), band(k_col, -1), band(k_col, 0), band(k_col, 1),
            ctx(v_col), band(v_col, -1), band(v_col, 0), band(v_col, 1),
        ],
        out_specs=pl.BlockSpec((CHUNK, HEADS_C * DH), lambda b, i: (b * N_BLK + i, 0)),
        out_shape=jax.ShapeDtypeStruct((R_LAT, HEADS_C * DH), bf16),
        scratch_shapes=[
            pltpu.VMEM((KV_HEADS, CTX + 3 * CHUNK, GROUP * CHUNK), bf16),
            pltpu.VMEM((KV_HEADS * DH, GROUP * CHUNK), f32),
        ],
        compiler_params=pltpu.CompilerParams(
            dimension_semantics=("parallel", "parallel"), vmem_limit_bytes=32 * MIB),
        name="window_attention",
    )(sink, qkv3, qkv3, qkv3, qkv3, qkv3, qkv3, qkv3, qkv3, qkv3)


def _odd_out_kernel(h_ref, mod_ref, a_ref, w_ref, o_ref):
    o_ref[...] = h_ref[...] + mod_ref[5:6, :] * _dot(a_ref[...], w_ref[...])


def _odd_out(h, mods, attn, w_out):
    tm = TM_FFN
    tile0 = R_CTX // tm
    return pl.pallas_call(
        _odd_out_kernel,
        grid=(R_LAT // tm,),
        in_specs=[
            pl.BlockSpec((tm, D), lambda i: (i + tile0, 0)),
            pl.BlockSpec((None, N_MOD, D), lambda i: (_who_flat(i + tile0, tm), 0, 0)),
            pl.BlockSpec((tm, D), lambda i: (i, 0)),
            pl.BlockSpec((D, D), lambda i: (0, 0)),
        ],
        out_specs=pl.BlockSpec((tm, D), lambda i: (i, 0)),
        out_shape=jax.ShapeDtypeStruct((R_LAT, D), f32),
        compiler_params=pltpu.CompilerParams(
            dimension_semantics=("parallel",), vmem_limit_bytes=32 * MIB),
        name="odd_out",
    )(h, mods, attn, w_out)


def _rope_tables():
    rows = SEQ // GRID_W
    row, col = jnp.meshgrid(jnp.arange(rows), jnp.arange(GRID_W), indexing='ij')
    n_freq = DH // 4
    inv = ROPE_BASE ** (-jnp.arange(n_freq, dtype=f32) / n_freq)
    ang = jnp.concatenate([row.reshape(-1, 1).astype(f32) * inv,
                           col.reshape(-1, 1).astype(f32) * inv], axis=-1)
    cos = jnp.repeat(jnp.cos(ang), 2, axis=1)
    sin = jnp.repeat(jnp.sin(ang), 2, axis=1)
    sign = jnp.where(jnp.arange(DH) % 2 == 0, -1.0, 1.0).astype(f32)
    cos = jnp.tile(cos, (1, LANES // DH))
    sin = jnp.tile(sin * sign, (1, LANES // DH))
    cos = jnp.concatenate([jnp.ones((TM_PROJ, LANES), f32), cos], axis=0)
    sin = jnp.concatenate([jnp.zeros((TM_PROJ, LANES), f32), sin], axis=0)
    return cos, sin


def kernel(x, c, ctx, c_ctx, ada_w, ada_b, ffn_w_in, ffn_w_out, even_w_in, even_w_out, mlstm_conv,
           mlstm_gate_b, mlstm_norm, sgu_norm, sgu_ws, sgu_b, odd_w_qkv, odd_w_out, attn_sink, final_norm):
    cs = jnp.concatenate([c_ctx[None, :], c, jnp.zeros((16 - 1 - B, D), f32)], axis=0)
    mods = _modulation(cs, ada_w, ada_b)[:, :1 + B, :].reshape(2, 1 + B, N_MOD, D)

    fw_in = ffn_w_in.astype(bf16)
    fw_out = ffn_w_out.astype(bf16)

    m0 = mods[0]
    h = _ffn((ctx.reshape(R_CTX, D), x.reshape(R_LAT, D)), m0, fw_in, fw_out, sel=(0, 0), mi=0)
    w_in = even_w_in[0]
    gate0 = 4 * W_A
    gate1 = gate0 + 4 * HEADS_A
    w_in = jnp.concatenate(
        [w_in[:, :gate0], w_in[:, gate1:], w_in[:, gate0:gate1],
         jnp.zeros((D, LANES - 4 * HEADS_A), f32)], axis=1).astype(bf16)
    qk, vo, uv, gates = _even_in(h, m0, w_in)
    gate_b = jnp.pad(mlstm_gate_b[0].reshape(1, 4 * HEADS_A), ((0, 0), (0, LANES - 4 * HEADS_A)))
    mnorm_t = jnp.broadcast_to(mlstm_norm[0][:, :, None], (HEADS_A, CHUNK, LANES))
    ha = _mlstm(qk, vo, gates, mlstm_conv[0], gate_b, mnorm_t)
    sbx = jnp.repeat(sgu_b[0].T, LANES, axis=1)
    h = _even_out(h, m0, ha.reshape(B * TOK, W_A), uv, sgu_norm[0].reshape(1, W_A),
                  sgu_ws[0].astype(bf16), sbx, even_w_out[0].astype(bf16))
    h = _ffn(h, m0, fw_in, fw_out, sel=(0, 1), mi=6)

    m1 = mods[1]
    h = _ffn(h, m1, fw_in, fw_out, sel=(1, 0), mi=0)
    cos_t, sin_t = _rope_tables()
    qdim = HEADS_C * DH
    w_q = odd_w_qkv[0][:, :qdim].reshape(D, KV_HEADS, GROUP, DH).transpose(0, 2, 1, 3).reshape(D, qdim)
    w_qkv = jnp.concatenate([w_q, odd_w_qkv[0][:, qdim:]], axis=1).astype(bf16)
    w_o = odd_w_out[0].reshape(KV_HEADS, GROUP, DH, D).transpose(1, 0, 2, 3).reshape(qdim, D).astype(bf16)
    qkv = _odd_in(h, m1, w_qkv, cos_t, sin_t)
    attn = _attention(qkv, attn_sink[0])
    h = _odd_out(h, m1, attn, w_o)
    out = _ffn(h, m1, fw_in, fw_out, sel=(1, 1), mi=6,
               latent_only=True, final_norm=final_norm)
    return out.reshape(B, SEQ, D)
```

```python
import functools

import jax
import jax.numpy as jnp
from jax import lax
from jax.experimental import pallas as pl
from jax.experimental.pallas import tpu as pltpu

f32 = jnp.float32
bf16 = jnp.bfloat16

D = 1024
B = 8
SEQ = 2048
CTX = 256
TOK = CTX + SEQ
GRID_W = 64
N_MOD = 9
D_FF = 2816
EPS = 1e-6
HEADS_A = 4
CHUNK = 128
N_CHUNK = TOK // CHUNK
N_CTX_CHUNK = CTX // CHUNK
W_A = 512
EVEN_COLS = 3200
HEADS_C = 16
KV_HEADS = 4
GROUP = HEADS_C // KV_HEADS
DH = 64
QKV = (HEADS_C + 2 * KV_HEADS) * DH
N_BLK = SEQ // CHUNK
ROPE_BASE = 10000.0
LOG2E = 1.4426950408889634

R_CTX = B * CTX
R_LAT = B * SEQ
R_ALL = R_CTX + R_LAT

LANES = 128
TM_FFN = 512
TM_PROJ = 256
FC = 256
N_FC = D_FF // FC
MIB = 1024 * 1024


def _dot(a, b):
    return jnp.dot(a, b, preferred_element_type=f32)


def _dot_nt(a, b):
    return lax.dot_general(a, b, (((1,), (1,)), ((), ())), preferred_element_type=f32)


def _dot_tn(a, b):
    return lax.dot_general(a, b, (((0,), (0,)), ((), ())), preferred_element_type=f32)


def _sigmoid(x):
    return 1.0 / (1.0 + jnp.exp(-x))


def _split3(x):
    hi = x.astype(bf16)
    r1 = x - hi.astype(f32)
    mid = r1.astype(bf16)
    lo = (r1 - mid.astype(f32)).astype(bf16)
    return hi, mid, lo


def _modulated(h, shift, scale):
    ms = jnp.mean(h * h, axis=-1, keepdims=True)
    return h * lax.rsqrt(ms + EPS) * (1.0 + scale) + shift


def _mod_kernel(c_ref, w_ref, b_ref, o_ref):
    x = c_ref[...]
    s = x * _sigmoid(x)
    w = w_ref[...]
    s_hi = s.astype(bf16)
    s_lo = (s - s_hi.astype(f32)).astype(bf16)
    w_hi = w.astype(bf16)
    w_lo = (w - w_hi.astype(f32)).astype(bf16)
    o_ref[...] = _dot(s_hi, w_hi) + _dot(s_hi, w_lo) + _dot(s_lo, w_hi) + b_ref[...]


def _modulation(cs, ada_w, ada_b):
    depth = ada_w.shape[0]
    rows = cs.shape[0]
    n_col = N_MOD * D
    tn = 1024
    return pl.pallas_call(
        _mod_kernel,
        grid=(depth, n_col // tn),
        in_specs=[
            pl.BlockSpec((rows, D), lambda l, j: (0, 0)),
            pl.BlockSpec((None, D, tn), lambda l, j: (l, 0, j)),
            pl.BlockSpec((None, 1, tn), lambda l, j: (l, 0, j)),
        ],
        out_specs=pl.BlockSpec((None, rows, tn), lambda l, j: (l, 0, j)),
        out_shape=jax.ShapeDtypeStruct((depth, rows, n_col), f32),
        compiler_params=pltpu.CompilerParams(
            dimension_semantics=("parallel", "parallel"), vmem_limit_bytes=32 * MIB),
        name="modulation",
    )(cs, ada_w, ada_b.reshape(depth, 1, n_col))


def _who_flat(tile, tm):
    n_ctx = R_CTX // tm
    per_b = SEQ // tm
    return jnp.where(tile < n_ctx, 0, 1 + jnp.maximum(tile - n_ctx, 0) // per_b)


def _batch_block(tile):
    per_b = SEQ // TM_PROJ
    lat = jnp.maximum(tile - B, 0)
    blocks_b = TOK // TM_PROJ
    return jnp.where(tile < B, blocks_b * tile, blocks_b * (lat // per_b) + 1 + lat % per_b)


def _ffn_kernel(*refs, mi, final, split):
    refs = list(refs)
    if split:
        c_ref, x_ref = refs[0:2]
        refs = refs[2:]
        is_ctx = pl.program_id(0) < R_CTX // TM_FFN
        read_h = lambda: jnp.where(is_ctx, c_ref[...], x_ref[...])
    else:
        h_ref = refs.pop(0)
        read_h = lambda: h_ref[...]
    if final:
        a_ref, wa_ref, mod_ref, wi_ref, wo_ref, fn_ref, o_ref, n_scr, acc_scr, h_scr = refs
        h_scr[...] = h_ref[...] + mod_ref[5:6, :] * _dot(a_ref[...], wa_ref[...])
        read_h = lambda: h_scr[...]
    else:
        mod_ref, wi_ref, wo_ref, o_ref, n_scr, acc_scr = refs
    n_scr[...] = _modulated(read_h(), mod_ref[mi:mi + 1, :], mod_ref[mi + 1:mi + 2, :]).astype(bf16)
    for j in range(N_FC):
        n = n_scr[...]
        g = _dot(n, wi_ref[:, j * FC:(j + 1) * FC])
        u = _dot(n, wi_ref[:, D_FF + j * FC:D_FF + (j + 1) * FC])
        a = (g * _sigmoid(g) * u).astype(bf16)
        y = _dot(a, wo_ref[j * FC:(j + 1) * FC, :])
        if j == 0:
            acc_scr[...] = y
        else:
            acc_scr[...] += y
    out = read_h() + (0.5 * mod_ref[mi + 2:mi + 3, :]) * acc_scr[...]
    if final:
        ms = jnp.mean(out * out, axis=-1, keepdims=True)
        out = out * lax.rsqrt(ms + EPS) * fn_ref[...]
    o_ref[...] = out


def _ffn(h, mods, w_in, w_out, *, sel, mi, last=None):
    tm = TM_FFN
    tile0 = R_CTX // tm if last is not None else 0
    split = isinstance(h, tuple)
    const2 = lambda i: (0, 0)
    if split:
        n_ctx = R_CTX // tm
        rows_out = R_ALL
        in_specs = [
            pl.BlockSpec((tm, D), lambda i: (jnp.minimum(i, n_ctx - 1), 0)),
            pl.BlockSpec((tm, D), lambda i: (jnp.maximum(i - n_ctx, 0), 0)),
        ]
        args = list(h)
    else:
        rows_out = h.shape[0] - tile0 * tm
        in_specs = [pl.BlockSpec((tm, D), lambda i: (i + tile0, 0))]
        args = [h]
    scratch = [pltpu.VMEM((tm, D), bf16), pltpu.VMEM((tm, D), f32)]
    if last is not None:
        attn, w_attn, final_norm = last
        in_specs += [
            pl.BlockSpec((tm, D), lambda i: (i, 0)),
            pl.BlockSpec((D, D), const2, pipeline_mode=pl.Buffered(1)),
        ]
        args += [attn, w_attn]
        scratch.append(pltpu.VMEM((tm, D), f32))
    in_specs += [
        pl.BlockSpec((None, N_MOD, D), lambda i: (_who_flat(i + tile0, tm), 0, 0)),
        pl.BlockSpec((None, None, D, 2 * D_FF), lambda i: sel + (0, 0), pipeline_mode=pl.Buffered(1)),
        pl.BlockSpec((None, None, D_FF, D), lambda i: sel + (0, 0), pipeline_mode=pl.Buffered(1)),
    ]
    args += [mods, w_in, w_out]
    if last is not None:
        in_specs.append(pl.BlockSpec((1, D), const2))
        args.append(final_norm.reshape(1, D))
    return pl.pallas_call(
        functools.partial(_ffn_kernel, mi=mi, final=last is not None, split=split),
        grid=(rows_out // tm,),
        in_specs=in_specs,
        out_specs=pl.BlockSpec((tm, D), lambda i: (i, 0)),
        out_shape=jax.ShapeDtypeStruct((rows_out, D), f32),
        scratch_shapes=scratch,
        compiler_params=pltpu.CompilerParams(
            dimension_semantics=("parallel",), vmem_limit_bytes=48 * MIB),
        name="ffn_final" if last is not None else "ffn",
    )(*args)


def _even_in_kernel(h_ref, mod_ref, w_ref, qk_ref, vo_ref, uv_ref, g_ref):
    n = _modulated(h_ref[...], mod_ref[3:4, :], mod_ref[4:5, :]).astype(bf16)
    qk_ref[...] = _dot(n, w_ref[:, 0:1024])
    vo_ref[:, 0:W_A] = _dot(n, w_ref[:, 1024:1024 + W_A]).astype(bf16)
    vo_ref[:, W_A:2 * W_A] = _sigmoid(_dot(n, w_ref[:, 1024 + W_A:2048])).astype(bf16)
    uv_ref[...] = _gelu_tanh(_dot(n, w_ref[:, 2048:3072])).astype(bf16)
    g_ref[...] = _dot(n, w_ref[:, 3072:3200])


def _even_in(h, mods, w):
    tm = TM_PROJ
    out_map = lambda i: (_batch_block(i), 0)
    return pl.pallas_call(
        _even_in_kernel,
        grid=(R_ALL // tm,),
        in_specs=[
            pl.BlockSpec((tm, D), lambda i: (i, 0)),
            pl.BlockSpec((None, N_MOD, D), lambda i: (_who_flat(i, tm), 0, 0)),
            pl.BlockSpec((D, EVEN_COLS), lambda i: (0, 0), pipeline_mode=pl.Buffered(1)),
        ],
        out_specs=[
            pl.BlockSpec((tm, 1024), out_map),
            pl.BlockSpec((tm, 1024), out_map),
            pl.BlockSpec((tm, 1024), out_map),
            pl.BlockSpec((tm, LANES), out_map),
        ],
        out_shape=[
            jax.ShapeDtypeStruct((B * TOK, 1024), f32),
            jax.ShapeDtypeStruct((B * TOK, 1024), bf16),
            jax.ShapeDtypeStruct((B * TOK, 1024), bf16),
            jax.ShapeDtypeStruct((B * TOK, LANES), f32),
        ],
        compiler_params=pltpu.CompilerParams(
            dimension_semantics=("parallel",), vmem_limit_bytes=40 * MIB),
        name="even_in",
    )(h, mods, w)


CHUNKS_PER_ITER = 9


def _chunk_loop(body):
    def group(i, carry):
        for u in range(CHUNKS_PER_ITER):
            carry = body(i * CHUNKS_PER_ITER + u, carry)
        return carry
    lax.fori_loop(0, N_CHUNK // CHUNKS_PER_ITER, group, 0)


def _mlstm_kernel(q_ref, k_ref, v_ref, o_ref, g_ref, cq_ref, ck_ref, gb_ref, mn_ref, out_ref,
                  ks, qts, vts, xc, xr, cl, st, cst, mst):
    head = pl.program_id(1)
    rowi = lax.broadcasted_iota(jnp.int32, (CHUNK, CHUNK), 0)
    coli = lax.broadcasted_iota(jnp.int32, (CHUNK, CHUNK), 1)
    lower = coli <= rowi
    upper = coli >= rowi
    tri = jnp.where(lower, 1.0, 0.0).astype(bf16)

    for c in range(N_CHUNK):
        lo = c * CHUNK
        seq_start = lo in (0, CTX)
        seq_end = lo + CHUNK in (CTX, TOK)
        for src, cw, is_q in ((q_ref, cq_ref, True), (k_ref, ck_ref, False)):
            x = src[lo:lo + CHUNK, :]
            first = jnp.zeros((1, LANES), f32) if seq_start else src[lo - 1:lo, :]
            last = jnp.zeros((1, LANES), f32) if seq_end else src[lo + CHUNK:lo + CHUNK + 1, :]
            prev = jnp.where(rowi == 0, first, pltpu.roll(x, 1, 0))
            nxt = jnp.where(rowi == CHUNK - 1, last, pltpu.roll(x, CHUNK - 1, 0))
            y = cw[0:1, :] * prev + cw[1:2, :] * x + cw[2:3, :] * nxt
            y = y * _sigmoid(y)
            if is_q:
                qts[lo:lo + CHUNK, :] = y.T.astype(bf16)
            else:
                ks[lo:lo + CHUNK, :] = (y * CHUNK ** -0.5).astype(bf16)
        vts[lo:lo + CHUNK, :] = v_ref[lo:lo + CHUNK, :].astype(f32).T.astype(bf16)

    @pl.when(head == 0)
    def _():
        kind = (coli // HEADS_A) % 4
        for c in range(N_CHUNK):
            lo = c * CHUNK
            gt = g_ref[lo:lo + CHUNK, :] + gb_ref[...]
            lf = jnp.minimum(gt, 0.0) - jnp.log1p(jnp.exp(-jnp.abs(gt)))
            hi, mid, lw = _split3(lf)
            pre = _dot(tri, hi) + _dot(tri, mid) + _dot(tri, lw)
            suf = pre[CHUNK - 1:CHUNK, :] - pre + lf
            x = jnp.where(kind == 1, pre, jnp.where(kind == 3, suf, gt))
            xr[16 * c:16 * c + 16, :] = x.T[0:16, :]
            xc[lo:lo + CHUNK, :] = x

    lane_shift = (LANES - head) % LANES

    def gate_rows(c, d):
        row = c * 16 + 8 * d + head
        return xr[pl.ds(row, 1), :], xr[pl.ds(row + HEADS_A, 1), :]

    def local_state(c, carry):
        lo = pl.multiple_of(c * CHUNK, CHUNK)
        k = ks[pl.ds(lo, CHUNK), :]
        v_t = vts[pl.ds(lo, CHUNK), :].astype(f32)
        for d in (0, 1):
            ig_r, b_r = gate_rows(c, d)
            g = b_r[:, CHUNK - 1:CHUNK] if d == 0 else b_r[:, 0:1]
            a_r = g - b_r + ig_r
            m_loc = jnp.max(a_r, axis=1, keepdims=True)
            w_r = jnp.exp(a_r - m_loc)
            lhs = jnp.concatenate([v_t * w_r, jnp.broadcast_to(w_r, (16, LANES))], axis=0).astype(bf16)
            idx = d * N_CHUNK + c
            cl[idx] = _dot(lhs, k)
            st[idx, 0:1, :] = jnp.broadcast_to(m_loc, (1, LANES))
            st[idx, 1:2, :] = jnp.broadcast_to(g, (1, LANES))
        return carry

    _chunk_loop(local_state)

    cst[...] = jnp.zeros(cst.shape, f32)
    mst[...] = jnp.zeros(mst.shape, f32)

    def scan_step(i, carry):
        for d in (0, 1):
            c = i if d == 0 else jnp.where(i < N_CTX_CHUNK, N_CTX_CHUNK - 1 - i, N_CHUNK + N_CTX_CHUNK - 1 - i)
            idx = d * N_CHUNK + c
            c_loc = cl[idx]
            m_loc = st[idx, 0:1, :]
            g = st[idx, 1:2, :]
            c_prev = cst[d]
            m_prev = mst[d, 0:1, :]
            m_new = jnp.maximum(g + m_prev, m_loc)
            dec = jnp.exp(g + m_prev - m_new)
            add = jnp.exp(m_loc - m_new)
            cl[idx] = c_prev
            st[idx, 2:3, :] = m_prev
            cst[d] = dec * c_prev + add * c_loc
            mst[d, 0:1, :] = m_new
        return carry

    lax.fori_loop(0, N_CHUNK, scan_step, 0)

    def outputs(c, carry):
        lo = pl.multiple_of(c * CHUNK, CHUNK)
        k = ks[pl.ds(lo, CHUNK), :]
        q_t = qts[pl.ds(lo, CHUNK), :]
        v_t = vts[pl.ds(lo, CHUNK), :]
        s_t = _dot(k, q_t)
        x_c = pltpu.roll(xc[pl.ds(lo, CHUNK), :], lane_shift, 1)
        hs = None
        for d in (0, 1):
            _, b_r = gate_rows(c, d)
            idx = d * N_CHUNK + c
            r_c = x_c[:, 8 * d:8 * d + 1] - x_c[:, 8 * d + 4:8 * d + 5]
            dm = jnp.where(upper if d == 0 else lower, b_r + r_c, -jnp.inf)
            e_r = b_r + st[idx, 2:3, :]
            m_t = jnp.maximum(e_r, jnp.max(dm, axis=0, keepdims=True))
            p_t = s_t * jnp.exp(dm - m_t)
            inter = jnp.exp(e_r - m_t)
            u = _dot(cl[idx].astype(bf16), q_t)
            num = _dot(v_t, p_t.astype(bf16)) + inter * u[0:CHUNK, :]
            den = jnp.sum(p_t, axis=0, keepdims=True) + inter * u[CHUNK:CHUNK + 1, :]
            h_d = num * (1.0 / jnp.maximum(jnp.abs(den), jnp.exp(-m_t)))
            hs = h_d if hs is None else hs + h_d
        hn = hs * lax.rsqrt(jnp.mean(hs * hs, axis=0, keepdims=True) + EPS) * mn_ref[...]
        out_ref[pl.ds(lo, CHUNK), :] = (o_ref[pl.ds(lo, CHUNK), :].astype(f32) * hn.T).astype(bf16)
        return carry

    _chunk_loop(outputs)


def _mlstm(qk, vo, gates, conv_w, gate_b, mnorm):
    blk = lambda col0: pl.BlockSpec((None, TOK, LANES), lambda b, h: (b, 0, col0 + h))
    return pl.pallas_call(
        _mlstm_kernel,
        grid=(B, HEADS_A),
        in_specs=[
            blk(0), blk(HEADS_A),
            blk(0), blk(HEADS_A),
            pl.BlockSpec((None, TOK, LANES), lambda b, h: (b, 0, 0)),
            pl.BlockSpec((3, LANES), lambda b, h: (0, h)),
            pl.BlockSpec((3, LANES), lambda b, h: (0, HEADS_A + h)),
            pl.BlockSpec((1, LANES), lambda b, h: (0, 0)),
            pl.BlockSpec((None, CHUNK, LANES), lambda b, h: (h, 0, 0)),
        ],
        out_specs=pl.BlockSpec((None, TOK, LANES), lambda b, h: (b, 0, h)),
        out_shape=jax.ShapeDtypeStruct((B, TOK, W_A), bf16),
        scratch_shapes=[
            pltpu.VMEM((TOK, LANES), bf16),
            pltpu.VMEM((TOK, LANES), bf16),
            pltpu.VMEM((TOK, LANES), bf16),
            pltpu.VMEM((TOK, LANES), f32),
            pltpu.VMEM((16 * N_CHUNK, LANES), f32),
            pltpu.VMEM((2 * N_CHUNK, CHUNK + 16, LANES), f32),
            pltpu.VMEM((2 * N_CHUNK, 8, LANES), f32),
            pltpu.VMEM((2, CHUNK + 16, LANES), f32),
            pltpu.VMEM((2, 8, LANES), f32),
        ],
        compiler_params=pltpu.CompilerParams(
            dimension_semantics=("parallel", "arbitrary"), vmem_limit_bytes=40 * MIB),
        name="mlstm",
    )(qk.reshape(B, TOK, 1024), qk.reshape(B, TOK, 1024), vo.reshape(B, TOK, 1024),
      vo.reshape(B, TOK, 1024), gates.reshape(B, TOK, LANES), conv_w, conv_w, gate_b, mnorm)


def _gelu_tanh(x):
    return x * (0.5 * (1.0 + jnp.tanh(0.7978845608028654 * (x + 0.044715 * (x * x * x)))))


def _even_out_kernel(h_ref, mod_ref, ha_ref, uv_ref, sg_ref, ws_ref, sb_ref, wo_ref, o_ref, hb_scr):
    u = uv_ref[:, 0:W_A].astype(f32)
    v = uv_ref[:, W_A:2 * W_A].astype(f32)
    vn = (v * lax.rsqrt(jnp.mean(v * v, axis=-1, keepdims=True) + EPS) * sg_ref[...]).astype(bf16)
    n_chunk = TM_PROJ // CHUNK
    for g in range(W_A // LANES):
        cs = slice(g * LANES, (g + 1) * LANES)
        rhs = jnp.concatenate([vn[n * CHUNK:(n + 1) * CHUNK, cs] for n in range(n_chunk)], axis=1)
        mixed = _dot(ws_ref[g], rhs)
        for n in range(n_chunk):
            r = slice(n * CHUNK, (n + 1) * CHUNK)
            hb_scr[r, cs] = (u[r, cs] * (mixed[:, n * LANES:(n + 1) * LANES] + sb_ref[:, cs])).astype(bf16)
    y =_dot(ha_ref[...], wo_ref[0:W_A, :]) + _dot(hb_scr[...], wo_ref[W_A:2 * W_A, :])
    o_ref[...] = h_ref[...] + mod_ref[5:6, :] * y


def _even_out(h, mods, ha, uv, sgu_norm, ws, sbx, w_out):
    tm = TM_PROJ
    in_map = lambda i: (_batch_block(i), 0)
    const2 = lambda i: (0, 0)
    return pl.pallas_call(
        _even_out_kernel,
        grid=(R_ALL // tm,),
        in_specs=[
            pl.BlockSpec((tm, D), lambda i: (i, 0)),
            pl.BlockSpec((None, N_MOD, D), lambda i: (_who_flat(i, tm), 0, 0)),
            pl.BlockSpec((tm, W_A), in_map),
            pl.BlockSpec((tm, 2 * W_A), in_map),
            pl.BlockSpec((1, W_A), const2),
            pl.BlockSpec((W_A // LANES, CHUNK, CHUNK), lambda i: (0, 0, 0)),
            pl.BlockSpec((CHUNK, W_A), const2),
            pl.BlockSpec((2 * W_A, D), const2),
        ],
        out_specs=pl.BlockSpec((tm, D), lambda i: (i, 0)),
        out_shape=jax.ShapeDtypeStruct((R_ALL, D), f32),
        scratch_shapes=[pltpu.VMEM((tm, W_A), bf16)],
        compiler_params=pltpu.CompilerParams(
            dimension_semantics=("parallel",), vmem_limit_bytes=32 * MIB),
        name="even_out",
    )(h, mods, ha, uv, sgu_norm, ws, sbx, w_out)


def _odd_in_kernel(h_ref, mod_ref, w_ref, cos_ref, sin_ref, o_ref):
    n = _modulated(h_ref[...], mod_ref[3:4, :], mod_ref[4:5, :]).astype(bf16)
    cos = cos_ref[...]
    sin = sin_ref[...]
    n_rot = (HEADS_C + KV_HEADS) * DH // LANES
    y = _dot(n, w_ref[:, 0:n_rot * LANES])
    for c in range(0, n_rot, 2):
        x1 = y[:, c * LANES:(c + 1) * LANES]
        x2 = y[:, (c + 1) * LANES:(c + 2) * LANES]
        r1 = x1 * cos - x2 * sin
        r2 = x1 * sin + x2 * cos
        if c < HEADS_C * DH // LANES:
            r1 = r1 * (DH ** -0.5 * LOG2E)
            r2 = r2 * (DH ** -0.5 * LOG2E)
        o_ref[:, c * LANES:(c + 1) * LANES] = r1.astype(bf16)
        o_ref[:, (c + 1) * LANES:(c + 2) * LANES] = r2.astype(bf16)
    v0 = n_rot * LANES
    o_ref[:, v0:QKV] = _dot(n, w_ref[:, v0:QKV]).astype(bf16)


def _odd_in(h, mods, w, cos_t, sin_t):
    tm = TM_PROJ
    per_b = SEQ // tm
    rope_map = lambda i: (jnp.where(i < B, 0, 1 + jnp.maximum(i - B, 0) % per_b), 0)
    return pl.pallas_call(
        _odd_in_kernel,
        grid=(R_ALL // tm,),
        in_specs=[
            pl.BlockSpec((tm, D), lambda i: (i, 0)),
            pl.BlockSpec((None, N_MOD, D), lambda i: (_who_flat(i, tm), 0, 0)),
            pl.BlockSpec((D, QKV), lambda i: (0, 0), pipeline_mode=pl.Buffered(1)),
            pl.BlockSpec((tm, LANES), rope_map),
            pl.BlockSpec((tm, LANES), rope_map),
        ],
        out_specs=pl.BlockSpec((tm, QKV), lambda i: (_batch_block(i), 0)),
        out_shape=jax.ShapeDtypeStruct((B * TOK, QKV), bf16),
        compiler_params=pltpu.CompilerParams(
            dimension_semantics=("parallel",), vmem_limit_bytes=32 * MIB),
        name="odd_in",
    )(h, mods, w, cos_t, sin_t)


def _attn_kernel(sink_ref, q_ref, kc_ref, kp_ref, k0_ref, kn_ref, vc_ref, vp_ref, v0_ref, vn_ref, o_ref,
                 s_scr, p_scr, ot_scr):
    blk = pl.program_id(1)
    cols = GROUP * CHUNK
    kv_w = KV_HEADS * DH
    key = lax.broadcasted_iota(jnp.int32, (CHUNK, cols), 0)
    qry = lax.broadcasted_iota(jnp.int32, (CHUNK, cols), 1) % CHUNK
    far = 1 << 20
    prev_ok = key >= qry + jnp.where(blk > 0, 0, far)
    next_ok = key <= qry - jnp.where(blk < N_BLK - 1, 0, far)
    grp = lax.broadcasted_iota(jnp.int32, (1, cols), 1) // CHUNK
    lane_head = lax.broadcasted_iota(jnp.int32, (CHUNK, kv_w), 1) % LANES // (DH // 2)
    neg = -1e30
    k_all = jnp.concatenate([kc_ref[...], kp_ref[...], k0_ref[...], kn_ref[...]], axis=0)
    v_all = jnp.concatenate([vc_ref[...], vp_ref[...], v0_ref[...], vn_ref[...]], axis=0)
    v_t = v_all.astype(f32).T.astype(bf16)
    n_key = CTX + 3 * CHUNK
    for j in range(KV_HEADS):
        keep = jnp.where(lane_head == j, 1.0, 0.0).astype(bf16)
        q = jnp.concatenate(
            [q_ref[:, g * kv_w:(g + 1) * kv_w] * keep for g in range(GROUP)], axis=0)
        s_scr[j] = _dot_nt(k_all, q)

    def scores(j, blk):
        s = s_scr[j, blk * CHUNK:(blk + 1) * CHUNK, :]
        if blk == 2:
            s = jnp.where(prev_ok, s, neg)
        if blk == 4:
            s = jnp.where(next_ok, s, neg)
        return s

    n_slab = n_key // CHUNK
    for j in range(KV_HEADS):
        sink = jnp.full((1, cols), sink_ref[j * GROUP], f32)
        for g in range(1, GROUP):
            sink = jnp.where(grp == g, sink_ref[j * GROUP + g], sink)
        sink = sink * LOG2E
        m = sink
        for blk in range(n_slab):
            m = jnp.maximum(m, jnp.max(scores(j, blk), axis=0, keepdims=True))
        den = jnp.exp2(sink - m)
        for blk in range(n_slab):
            p = jnp.exp2(scores(j, blk) - m)
            den = den + jnp.sum(p, axis=0, keepdims=True)
            p_scr[j, blk * CHUNK:(blk + 1) * CHUNK, :] = p.astype(bf16)
        ot_scr[j * DH:(j + 1) * DH, :] = _dot(v_t[j * DH:(j + 1) * DH, :], p_scr[j]) * (1.0 / den)
    for g in range(GROUP):
        o_ref[:, g * kv_w:(g + 1) * kv_w] = ot_scr[:, g * CHUNK:(g + 1) * CHUNK].T.astype(bf16)


def _attention(qkv, sink):
    qkv3 = qkv.reshape(B, TOK, QKV)
    kv_w = KV_HEADS * DH
    k_col = HEADS_C * DH // kv_w
    v_col = k_col + 1
    lat0 = CTX // CHUNK
    q_spec = pl.BlockSpec((None, CHUNK, HEADS_C * DH), lambda b, i: (b, lat0 + i, 0))

    def band(col, off):
        def index(b, i):
            return (b, lat0 + jnp.clip(i + off, 0, N_BLK - 1), col)
        return pl.BlockSpec((None, CHUNK, kv_w), index)

    ctx = lambda col: pl.BlockSpec((None, CTX, kv_w), lambda b, i: (b, 0, col))
    return pl.pallas_call(
        _attn_kernel,
        grid=(B, N_BLK),
        in_specs=[
            pl.BlockSpec(memory_space=pltpu.SMEM),
            q_spec,
            ctx(k_col), band(k_col, -1), band(k_col, 0), band(k_col, 1),
            ctx(v_col), band(v_col, -1), band(v_col, 0), band(v_col, 1),
        ],
        out_specs=pl.BlockSpec((CHUNK, HEADS_C * DH), lambda b, i: (b * N_BLK + i, 0)),
        out_shape=jax.ShapeDtypeStruct((R_LAT, HEADS_C * DH), bf16),
        scratch_shapes=[
            pltpu.VMEM((KV_HEADS, CTX + 3 * CHUNK, GROUP * CHUNK), f32),
            pltpu.VMEM((KV_HEADS, CTX + 3 * CHUNK, GROUP * CHUNK), bf16),
            pltpu.VMEM((KV_HEADS * DH, GROUP * CHUNK), f32),
        ],
        compiler_params=pltpu.CompilerParams(
            dimension_semantics=("parallel", "parallel"), vmem_limit_bytes=32 * MIB),
        name="window_attention",
    )(sink, qkv3, qkv3, qkv3, qkv3, qkv3, qkv3, qkv3, qkv3, qkv3)


def _rope_tables():
    rows = SEQ // GRID_W
    row, col = jnp.meshgrid(jnp.arange(rows), jnp.arange(GRID_W), indexing='ij')
    n_freq = DH // 4
    inv = ROPE_BASE ** (-jnp.arange(n_freq, dtype=f32) / n_freq)
    ang = jnp.concatenate([row.reshape(-1, 1).astype(f32) * inv,
                           col.reshape(-1, 1).astype(f32) * inv], axis=-1)
    reps = 2 * LANES // DH
    cos = jnp.tile(jnp.cos(ang), (1, reps))
    sin = jnp.tile(jnp.sin(ang), (1, reps))
    cos = jnp.concatenate([jnp.ones((TM_PROJ, LANES), f32), cos], axis=0)
    sin = jnp.concatenate([jnp.zeros((TM_PROJ, LANES), f32), sin], axis=0)
    return cos, sin


def kernel(x, c, ctx, c_ctx, ada_w, ada_b, ffn_w_in, ffn_w_out, even_w_in, even_w_out, mlstm_conv,
           mlstm_gate_b, mlstm_norm, sgu_norm, sgu_ws, sgu_b, odd_w_qkv, odd_w_out, attn_sink, final_norm):
    cs = jnp.concatenate([c_ctx[None, :], c, jnp.zeros((16 - 1 - B, D), f32)], axis=0)
    mods = _modulation(cs, ada_w, ada_b)[:, :1 + B, :].reshape(2, 1 + B, N_MOD, D)

    fw_in = ffn_w_in.astype(bf16)
    fw_out = ffn_w_out.astype(bf16)

    m0 = mods[0]
    h = _ffn((ctx.reshape(R_CTX, D), x.reshape(R_LAT, D)), m0, fw_in, fw_out, sel=(0, 0), mi=0)
    w_in = even_w_in[0]
    gate0 = 4 * W_A
    gate1 = gate0 + 4 * HEADS_A
    w_in = jnp.concatenate(
        [w_in[:, :gate0], w_in[:, gate1:], w_in[:, gate0:gate1],
         jnp.zeros((D, LANES - 4 * HEADS_A), f32)], axis=1).astype(bf16)
    qk, vo, uv, gates = _even_in(h, m0, w_in)
    gate_b = jnp.pad(mlstm_gate_b[0].reshape(1, 4 * HEADS_A), ((0, 0), (0, LANES - 4 * HEADS_A)))
    mnorm_t = jnp.broadcast_to(mlstm_norm[0][:, :, None], (HEADS_A, CHUNK, LANES))
    ha = _mlstm(qk, vo, gates, mlstm_conv[0], gate_b, mnorm_t)
    sbx = jnp.repeat(sgu_b[0].T, LANES, axis=1)
    h = _even_out(h, m0, ha.reshape(B * TOK, W_A), uv, sgu_norm[0].reshape(1, W_A),
                  sgu_ws[0].astype(bf16), sbx, even_w_out[0].astype(bf16))
    h = _ffn(h, m0, fw_in, fw_out, sel=(0, 1), mi=6)

    m1 = mods[1]
    h = _ffn(h, m1, fw_in, fw_out, sel=(1, 0), mi=0)
    cos_t, sin_t = _rope_tables()
    qdim = HEADS_C * DH
    kdim = KV_HEADS * DH
    w_q = odd_w_qkv[0][:, :qdim].reshape(D, KV_HEADS, GROUP, DH // 2, 2).transpose(0, 2, 4, 1, 3).reshape(D, qdim)
    w_k = odd_w_qkv[0][:, qdim:qdim + kdim].reshape(D, KV_HEADS, DH // 2, 2).transpose(0, 3, 1, 2).reshape(D, kdim)
    w_qkv = jnp.concatenate([w_q, w_k, odd_w_qkv[0][:, qdim + kdim:]], axis=1).astype(bf16)
    w_o = odd_w_out[0].reshape(KV_HEADS, GROUP, DH, D).transpose(1, 0, 2, 3).reshape(qdim, D).astype(bf16)
    qkv = _odd_in(h, m1, w_qkv, cos_t, sin_t)
    attn = _attention(qkv, attn_sink[0])
    out = _ffn(h, m1, fw_in, fw_out, sel=(1, 1), mi=6, last=(attn, w_o, final_norm))
    return out.reshape(B, SEQ, D)
```

```python
import functools

import jax
import jax.numpy as jnp
from jax import lax
from jax.experimental import pallas as pl
from jax.experimental.pallas import tpu as pltpu

f32 = jnp.float32
bf16 = jnp.bfloat16

D = 1024
B = 8
SEQ = 2048
CTX = 256
TOK = CTX + SEQ
GRID_W = 64
N_MOD = 9
D_FF = 2816
EPS = 1e-6
HEADS_A = 4
CHUNK = 128
N_CHUNK = TOK // CHUNK
N_CTX_CHUNK = CTX // CHUNK
W_A = 512
EVEN_COLS = 3200
HEADS_C = 16
KV_HEADS = 4
GROUP = HEADS_C // KV_HEADS
DH = 64
QKV = (HEADS_C + 2 * KV_HEADS) * DH
N_BLK = SEQ // CHUNK
ROPE_BASE = 10000.0
LOG2E = 1.4426950408889634

R_CTX = B * CTX
R_LAT = B * SEQ
R_ALL = R_CTX + R_LAT

LANES = 128
TM_FFN = 512
TM_PROJ = 256
FC = 256
N_FC = D_FF // FC
MIB = 1024 * 1024


def _dot(a, b):
    return jnp.dot(a, b, preferred_element_type=f32)


def _dot_nt(a, b):
    return lax.dot_general(a, b, (((1,), (1,)), ((), ())), preferred_element_type=f32)


def _dot_tn(a, b):
    return lax.dot_general(a, b, (((0,), (0,)), ((), ())), preferred_element_type=f32)


def _sigmoid(x):
    return 1.0 / (1.0 + jnp.exp(-x))


def _split3(x):
    hi = x.astype(bf16)
    r1 = x - hi.astype(f32)
    mid = r1.astype(bf16)
    lo = (r1 - mid.astype(f32)).astype(bf16)
    return hi, mid, lo


def _modulated(h, shift, scale):
    ms = jnp.mean(h * h, axis=-1, keepdims=True)
    return h * lax.rsqrt(ms + EPS) * (1.0 + scale) + shift


def _mod_kernel(c_ref, w_ref, b_ref, o_ref):
    x = c_ref[...]
    s = x * _sigmoid(x)
    w = w_ref[...]
    s_hi = s.astype(bf16)
    s_lo = (s - s_hi.astype(f32)).astype(bf16)
    w_hi = w.astype(bf16)
    w_lo = (w - w_hi.astype(f32)).astype(bf16)
    o_ref[...] = _dot(s_hi, w_hi) + _dot(s_hi, w_lo) + _dot(s_lo, w_hi) + b_ref[...]


def _modulation(cs, ada_w, ada_b):
    depth = ada_w.shape[0]
    rows = cs.shape[0]
    n_col = N_MOD * D
    tn = 1024
    return pl.pallas_call(
        _mod_kernel,
        grid=(depth, n_col // tn),
        in_specs=[
            pl.BlockSpec((rows, D), lambda l, j: (0, 0)),
            pl.BlockSpec((None, D, tn), lambda l, j: (l, 0, j)),
            pl.BlockSpec((None, 1, tn), lambda l, j: (l, 0, j)),
        ],
        out_specs=pl.BlockSpec((None, rows, tn), lambda l, j: (l, 0, j)),
        out_shape=jax.ShapeDtypeStruct((depth, rows, n_col), f32),
        compiler_params=pltpu.CompilerParams(
            dimension_semantics=("parallel", "parallel"), vmem_limit_bytes=32 * MIB),
        name="modulation",
    )(cs, ada_w, ada_b.reshape(depth, 1, n_col))


def _who_flat(tile, tm):
    n_ctx = R_CTX // tm
    per_b = SEQ // tm
    return jnp.where(tile < n_ctx, 0, 1 + jnp.maximum(tile - n_ctx, 0) // per_b)


def _batch_block(tile):
    per_b = SEQ // TM_PROJ
    lat = jnp.maximum(tile - B, 0)
    blocks_b = TOK // TM_PROJ
    return jnp.where(tile < B, blocks_b * tile, blocks_b * (lat // per_b) + 1 + lat % per_b)


W_CHUNKS = 16


def _fetch_cast(src, dst, stage, sem):
    rows = dst.shape[0] // W_CHUNKS

    def piece(c):
        return pltpu.make_async_copy(src.at[pl.ds(c * rows, rows), :], stage.at[c % 2], sem.at[c % 2])

    piece(0).start()
    for c in range(W_CHUNKS):
        if c + 1 < W_CHUNKS:
            piece(c + 1).start()
        piece(c).wait()
        dst[c * rows:(c + 1) * rows, :] = stage[c % 2].astype(bf16)


def _ffn_kernel(*refs, mi, final, split, sel):
    refs = list(refs)
    if split:
        c_ref, x_ref = refs[0:2]
        refs = refs[2:]
        is_ctx = pl.program_id(0) < R_CTX // TM_FFN
        read_h = lambda: jnp.where(is_ctx, c_ref[...], x_ref[...])
    else:
        h_ref = refs.pop(0)
        read_h = lambda: h_ref[...]
    wi_ref, wo_ref, wi_stage, wo_stage, w_sem = refs[-5:]
    refs = refs[:-5]
    if final:
        a_ref, wa_ref, mod_ref, wi_hbm, wo_hbm, fn_ref, o_ref, n_scr, acc_scr, h_scr = refs
    else:
        mod_ref, wi_hbm, wo_hbm, o_ref, n_scr, acc_scr = refs

    @pl.when(pl.program_id(0) == 0)
    def _():
        _fetch_cast(wi_hbm.at[sel[0], sel[1]], wi_ref, wi_stage, w_sem.at[0])
        _fetch_cast(wo_hbm.at[sel[0], sel[1]], wo_ref, wo_stage, w_sem.at[1])

    if final:
        h_scr[...] = h_ref[...] + mod_ref[5:6, :] * _dot(a_ref[...], wa_ref[...])
        read_h = lambda: h_scr[...]
    n_scr[...] = _modulated(read_h(), mod_ref[mi:mi + 1, :], mod_ref[mi + 1:mi + 2, :]).astype(bf16)
    for j in range(N_FC):
        n = n_scr[...]
        g = _dot(n, wi_ref[:, j * FC:(j + 1) * FC])
        u = _dot(n, wi_ref[:, D_FF + j * FC:D_FF + (j + 1) * FC])
        a = (g * _sigmoid(g) * u).astype(bf16)
        y = _dot(a, wo_ref[j * FC:(j + 1) * FC, :])
        if j == 0:
            acc_scr[...] = y
        else:
            acc_scr[...] += y
    out = read_h() + (0.5 * mod_ref[mi + 2:mi + 3, :]) * acc_scr[...]
    if final:
        ms = jnp.mean(out * out, axis=-1, keepdims=True)
        out = out * lax.rsqrt(ms + EPS) * fn_ref[...]
    o_ref[...] = out


def _ffn(h, mods, w_in, w_out, *, sel, mi, last=None):
    tm = TM_FFN
    tile0 = R_CTX // tm if last is not None else 0
    split = isinstance(h, tuple)
    const2 = lambda i: (0, 0)
    if split:
        n_ctx = R_CTX // tm
        rows_out = R_ALL
        in_specs = [
            pl.BlockSpec((tm, D), lambda i: (jnp.minimum(i, n_ctx - 1), 0)),
            pl.BlockSpec((tm, D), lambda i: (jnp.maximum(i - n_ctx, 0), 0)),
        ]
        args = list(h)
    else:
        rows_out = h.shape[0] - tile0 * tm
        in_specs = [pl.BlockSpec((tm, D), lambda i: (i + tile0, 0))]
        args = [h]
    scratch = [pltpu.VMEM((tm, D), bf16), pltpu.VMEM((tm, D), f32)]
    if last is not None:
        attn, w_attn, final_norm = last
        in_specs += [
            pl.BlockSpec((tm, D), lambda i: (i, 0)),
            pl.BlockSpec((D, D), const2, pipeline_mode=pl.Buffered(1)),
        ]
        args += [attn, w_attn]
        scratch.append(pltpu.VMEM((tm, D), f32))
    in_specs += [
        pl.BlockSpec((None, N_MOD, D), lambda i: (_who_flat(i + tile0, tm), 0, 0)),
        pl.BlockSpec(memory_space=pl.ANY),
        pl.BlockSpec(memory_space=pl.ANY),
    ]
    args += [mods, w_in, w_out]
    if last is not None:
        in_specs.append(pl.BlockSpec((1, D), const2))
        args.append(final_norm.reshape(1, D))
    scratch += [
        pltpu.VMEM((D, 2 * D_FF), bf16),
        pltpu.VMEM((D_FF, D), bf16),
        pltpu.VMEM((2, D // W_CHUNKS, 2 * D_FF), f32),
        pltpu.VMEM((2, D_FF // W_CHUNKS, D), f32),
        pltpu.SemaphoreType.DMA((2, 2)),
    ]
    return pl.pallas_call(
        functools.partial(_ffn_kernel, mi=mi, final=last is not None, split=split, sel=sel),
        grid=(rows_out // tm,),
        in_specs=in_specs,
        out_specs=pl.BlockSpec((tm, D), lambda i: (i, 0)),
        out_shape=jax.ShapeDtypeStruct((rows_out, D), f32),
        scratch_shapes=scratch,
        compiler_params=pltpu.CompilerParams(
            dimension_semantics=("arbitrary",), vmem_limit_bytes=52 * MIB),
        name="ffn_final" if last is not None else "ffn",
    )(*args)


def _even_in_kernel(h_ref, mod_ref, w_ref, qk_ref, vo_ref, uv_ref, g_ref):
    n = _modulated(h_ref[...], mod_ref[3:4, :], mod_ref[4:5, :]).astype(bf16)
    qk_ref[...] = _dot(n, w_ref[:, 0:1024])
    vo_ref[:, 0:W_A] = _dot(n, w_ref[:, 1024:1024 + W_A]).astype(bf16)
    vo_ref[:, W_A:2 * W_A] = _sigmoid(_dot(n, w_ref[:, 1024 + W_A:2048])).astype(bf16)
    uv_ref[...] = _gelu_tanh(_dot(n, w_ref[:, 2048:3072])).astype(bf16)
    g_ref[...] = _dot(n, w_ref[:, 3072:3200])


def _even_in(h, mods, w):
    tm = TM_PROJ
    out_map = lambda i: (_batch_block(i), 0)
    return pl.pallas_call(
        _even_in_kernel,
        grid=(R_ALL // tm,),
        in_specs=[
            pl.BlockSpec((tm, D), lambda i: (i, 0)),
            pl.BlockSpec((None, N_MOD, D), lambda i: (_who_flat(i, tm), 0, 0)),
            pl.BlockSpec((D, EVEN_COLS), lambda i: (0, 0), pipeline_mode=pl.Buffered(1)),
        ],
        out_specs=[
            pl.BlockSpec((tm, 1024), out_map),
            pl.BlockSpec((tm, 1024), out_map),
            pl.BlockSpec((tm, 1024), out_map),
            pl.BlockSpec((tm, LANES), out_map),
        ],
        out_shape=[
            jax.ShapeDtypeStruct((B * TOK, 1024), f32),
            jax.ShapeDtypeStruct((B * TOK, 1024), bf16),
            jax.ShapeDtypeStruct((B * TOK, 1024), bf16),
            jax.ShapeDtypeStruct((B * TOK, LANES), f32),
        ],
        compiler_params=pltpu.CompilerParams(
            dimension_semantics=("parallel",), vmem_limit_bytes=40 * MIB),
        name="even_in",
    )(h, mods, w)


CHUNKS_PER_ITER = 9


def _chunk_loop(body):
    def group(i, carry):
        for u in range(CHUNKS_PER_ITER):
            carry = body(i * CHUNKS_PER_ITER + u, carry)
        return carry
    lax.fori_loop(0, N_CHUNK // CHUNKS_PER_ITER, group, 0)


def _mlstm_kernel(q_ref, k_ref, v_ref, o_ref, g_ref, cq_ref, ck_ref, gb_ref, mn_ref, out_ref,
                  ks, qts, vts, xc, xr, cl, st, cst, mst):
    head = pl.program_id(1)
    rowi = lax.broadcasted_iota(jnp.int32, (CHUNK, CHUNK), 0)
    coli = lax.broadcasted_iota(jnp.int32, (CHUNK, CHUNK), 1)
    lower = coli <= rowi
    upper = coli >= rowi
    tri = jnp.where(lower, 1.0, 0.0).astype(bf16)

    for c in range(N_CHUNK):
        lo = c * CHUNK
        seq_start = lo in (0, CTX)
        seq_end = lo + CHUNK in (CTX, TOK)
        for src, cw, is_q in ((q_ref, cq_ref, True), (k_ref, ck_ref, False)):
            x = src[lo:lo + CHUNK, :]
            first = jnp.zeros((1, LANES), f32) if seq_start else src[lo - 1:lo, :]
            last = jnp.zeros((1, LANES), f32) if seq_end else src[lo + CHUNK:lo + CHUNK + 1, :]
            prev = jnp.where(rowi == 0, first, pltpu.roll(x, 1, 0))
            nxt = jnp.where(rowi == CHUNK - 1, last, pltpu.roll(x, CHUNK - 1, 0))
            y = cw[0:1, :] * prev + cw[1:2, :] * x + cw[2:3, :] * nxt
            y = y * _sigmoid(y)
            if is_q:
                qts[lo:lo + CHUNK, :] = y.T.astype(bf16)
            else:
                ks[lo:lo + CHUNK, :] = (y * CHUNK ** -0.5).astype(bf16)
        vts[lo:lo + CHUNK, :] = v_ref[lo:lo + CHUNK, :].astype(f32).T.astype(bf16)

    @pl.when(head == 0)
    def _():
        kind = (coli // HEADS_A) % 4
        for c in range(N_CHUNK):
            lo = c * CHUNK
            gt = g_ref[lo:lo + CHUNK, :] + gb_ref[...]
            lf = jnp.minimum(gt, 0.0) - jnp.log1p(jnp.exp(-jnp.abs(gt)))
            hi, mid, lw = _split3(lf)
            pre = _dot(tri, hi) + _dot(tri, mid) + _dot(tri, lw)
            suf = pre[CHUNK - 1:CHUNK, :] - pre + lf
            x = jnp.where(kind == 1, pre, jnp.where(kind == 3, suf, gt))
            xr[16 * c:16 * c + 16, :] = x.T[0:16, :]
            xc[lo:lo + CHUNK, :] = x

    lane_shift = (LANES - head) % LANES

    def gate_rows(c, d):
        row = c * 16 + 8 * d + head
        return xr[pl.ds(row, 1), :], xr[pl.ds(row + HEADS_A, 1), :]

    def local_state(c, carry):
        lo = pl.multiple_of(c * CHUNK, CHUNK)
        k = ks[pl.ds(lo, CHUNK), :]
        v_t = vts[pl.ds(lo, CHUNK), :].astype(f32)
        for d in (0, 1):
            ig_r, b_r = gate_rows(c, d)
            g = b_r[:, CHUNK - 1:CHUNK] if d == 0 else b_r[:, 0:1]
            a_r = g - b_r + ig_r
            m_loc = jnp.max(a_r, axis=1, keepdims=True)
            w_r = jnp.exp(a_r - m_loc)
            lhs = jnp.concatenate([v_t * w_r, jnp.broadcast_to(w_r, (16, LANES))], axis=0).astype(bf16)
            idx = d * N_CHUNK + c
            cl[idx] = _dot(lhs, k)
            st[idx, 0:1, :] = jnp.broadcast_to(m_loc, (1, LANES))
            st[idx, 1:2, :] = jnp.broadcast_to(g, (1, LANES))
        return carry

    _chunk_loop(local_state)

    cst[...] = jnp.zeros(cst.shape, f32)
    mst[...] = jnp.zeros(mst.shape, f32)

    def scan_step(i, carry):
        for d in (0, 1):
            c = i if d == 0 else jnp.where(i < N_CTX_CHUNK, N_CTX_CHUNK - 1 - i, N_CHUNK + N_CTX_CHUNK - 1 - i)
            idx = d * N_CHUNK + c
            c_loc = cl[idx]
            m_loc = st[idx, 0:1, :]
            g = st[idx, 1:2, :]
            c_prev = cst[d]
            m_prev = mst[d, 0:1, :]
            m_new = jnp.maximum(g + m_prev, m_loc)
            dec = jnp.exp(g + m_prev - m_new)
            add = jnp.exp(m_loc - m_new)
            cl[idx] = c_prev
            st[idx, 2:3, :] = m_prev
            cst[d] = dec * c_prev + add * c_loc
            mst[d, 0:1, :] = m_new
        return carry

    lax.fori_loop(0, N_CHUNK, scan_step, 0)

    def outputs(c, carry):
        lo = pl.multiple_of(c * CHUNK, CHUNK)
        k = ks[pl.ds(lo, CHUNK), :]
        q_t = qts[pl.ds(lo, CHUNK), :]
        v_t = vts[pl.ds(lo, CHUNK), :]
        s_t = _dot(k, q_t)
        x_c = pltpu.roll(xc[pl.ds(lo, CHUNK), :], lane_shift, 1)
        hs = None
        for d in (0, 1):
            _, b_r = gate_rows(c, d)
            idx = d * N_CHUNK + c
            r_c = x_c[:, 8 * d:8 * d + 1] - x_c[:, 8 * d + 4:8 * d + 5]
            dm = jnp.where(upper if d == 0 else lower, b_r + r_c, -jnp.inf)
            e_r = b_r + st[idx, 2:3, :]
            m_t = jnp.maximum(e_r, jnp.max(dm, axis=0, keepdims=True))
            p_t = s_t * jnp.exp(dm - m_t)
            inter = jnp.exp(e_r - m_t)
            u = _dot(cl[idx].astype(bf16), q_t)
            num = _dot(v_t, p_t.astype(bf16)) + inter * u[0:CHUNK, :]
            den = jnp.sum(p_t, axis=0, keepdims=True) + inter * u[CHUNK:CHUNK + 1, :]
            h_d = num * (1.0 / jnp.maximum(jnp.abs(den), jnp.exp(-m_t)))
            hs = h_d if hs is None else hs + h_d
        hn = hs * lax.rsqrt(jnp.mean(hs * hs, axis=0, keepdims=True) + EPS) * mn_ref[...]
        out_ref[pl.ds(lo, CHUNK), :] = (o_ref[pl.ds(lo, CHUNK), :].astype(f32) * hn.T).astype(bf16)
        return carry

    _chunk_loop(outputs)


def _mlstm(qk, vo, gates, conv_w, gate_b, mnorm):
    blk = lambda col0: pl.BlockSpec((None, TOK, LANES), lambda b, h: (b, 0, col0 + h))
    return pl.pallas_call(
        _mlstm_kernel,
        grid=(B, HEADS_A),
        in_specs=[
            blk(0), blk(HEADS_A),
            blk(0), blk(HEADS_A),
            pl.BlockSpec((None, TOK, LANES), lambda b, h: (b, 0, 0)),
            pl.BlockSpec((3, LANES), lambda b, h: (0, h)),
            pl.BlockSpec((3, LANES), lambda b, h: (0, HEADS_A + h)),
            pl.BlockSpec((1, LANES), lambda b, h: (0, 0)),
            pl.BlockSpec((None, CHUNK, LANES), lambda b, h: (h, 0, 0)),
        ],
        out_specs=pl.BlockSpec((None, TOK, LANES), lambda b, h: (b, 0, h)),
        out_shape=jax.ShapeDtypeStruct((B, TOK, W_A), bf16),
        scratch_shapes=[
            pltpu.VMEM((TOK, LANES), bf16),
            pltpu.VMEM((TOK, LANES), bf16),
            pltpu.VMEM((TOK, LANES), bf16),
            pltpu.VMEM((TOK, LANES), f32),
            pltpu.VMEM((16 * N_CHUNK, LANES), f32),
            pltpu.VMEM((2 * N_CHUNK, CHUNK + 16, LANES), f32),
            pltpu.VMEM((2 * N_CHUNK, 8, LANES), f32),
            pltpu.VMEM((2, CHUNK + 16, LANES), f32),
            pltpu.VMEM((2, 8, LANES), f32),
        ],
        compiler_params=pltpu.CompilerParams(
            dimension_semantics=("parallel", "arbitrary"), vmem_limit_bytes=40 * MIB),
        name="mlstm",
    )(qk.reshape(B, TOK, 1024), qk.reshape(B, TOK, 1024), vo.reshape(B, TOK, 1024),
      vo.reshape(B, TOK, 1024), gates.reshape(B, TOK, LANES), conv_w, conv_w, gate_b, mnorm)


def _gelu_tanh(x):
    return x * (0.5 * (1.0 + jnp.tanh(0.7978845608028654 * (x + 0.044715 * (x * x * x)))))


def _even_out_kernel(h_ref, mod_ref, ha_ref, uv_ref, sg_ref, ws_ref, sb_ref, wo_ref, o_ref, hb_scr):
    u = uv_ref[:, 0:W_A].astype(f32)
    v = uv_ref[:, W_A:2 * W_A].astype(f32)
    vn = (v * lax.rsqrt(jnp.mean(v * v, axis=-1, keepdims=True) + EPS) * sg_ref[...]).astype(bf16)
    n_chunk = TM_PROJ // CHUNK
    for g in range(W_A // LANES):
        cs = slice(g * LANES, (g + 1) * LANES)
        rhs = jnp.concatenate([vn[n * CHUNK:(n + 1) * CHUNK, cs] for n in range(n_chunk)], axis=1)
        mixed = _dot(ws_ref[g], rhs)
        for n in range(n_chunk):
            r = slice(n * CHUNK, (n + 1) * CHUNK)
            hb_scr[r, cs] = (u[r, cs] * (mixed[:, n * LANES:(n + 1) * LANES] + sb_ref[:, cs])).astype(bf16)
    y =_dot(ha_ref[...], wo_ref[0:W_A, :]) + _dot(hb_scr[...], wo_ref[W_A:2 * W_A, :])
    o_ref[...] = h_ref[...] + mod_ref[5:6, :] * y


def _even_out(h, mods, ha, uv, sgu_norm, ws, sbx, w_out):
    tm = TM_PROJ
    in_map = lambda i: (_batch_block(i), 0)
    const2 = lambda i: (0, 0)
    return pl.pallas_call(
        _even_out_kernel,
        grid=(R_ALL // tm,),
        in_specs=[
            pl.BlockSpec((tm, D), lambda i: (i, 0)),
            pl.BlockSpec((None, N_MOD, D), lambda i: (_who_flat(i, tm), 0, 0)),
            pl.BlockSpec((tm, W_A), in_map),
            pl.BlockSpec((tm, 2 * W_A), in_map),
            pl.BlockSpec((1, W_A), const2),
            pl.BlockSpec((W_A // LANES, CHUNK, CHUNK), lambda i: (0, 0, 0)),
            pl.BlockSpec((CHUNK, W_A), const2),
            pl.BlockSpec((2 * W_A, D), const2),
        ],
        out_specs=pl.BlockSpec((tm, D), lambda i: (i, 0)),
        out_shape=jax.ShapeDtypeStruct((R_ALL, D), f32),
        scratch_shapes=[pltpu.VMEM((tm, W_A), bf16)],
        compiler_params=pltpu.CompilerParams(
            dimension_semantics=("parallel",), vmem_limit_bytes=32 * MIB),
        name="even_out",
    )(h, mods, ha, uv, sgu_norm, ws, sbx, w_out)


def _odd_in_kernel(h_ref, mod_ref, w_ref, cos_ref, sin_ref, o_ref):
    n = _modulated(h_ref[...], mod_ref[3:4, :], mod_ref[4:5, :]).astype(bf16)
    cos = cos_ref[...]
    sin = sin_ref[...]
    n_rot = (HEADS_C + KV_HEADS) * DH // LANES
    y = _dot(n, w_ref[:, 0:n_rot * LANES])
    for c in range(0, n_rot, 2):
        x1 = y[:, c * LANES:(c + 1) * LANES]
        x2 = y[:, (c + 1) * LANES:(c + 2) * LANES]
        r1 = x1 * cos - x2 * sin
        r2 = x1 * sin + x2 * cos
        if c < HEADS_C * DH // LANES:
            r1 = r1 * (DH ** -0.5 * LOG2E)
            r2 = r2 * (DH ** -0.5 * LOG2E)
        o_ref[:, c * LANES:(c + 1) * LANES] = r1.astype(bf16)
        o_ref[:, (c + 1) * LANES:(c + 2) * LANES] = r2.astype(bf16)
    v0 = n_rot * LANES
    o_ref[:, v0:QKV] = _dot(n, w_ref[:, v0:QKV]).astype(bf16)


def _odd_in(h, mods, w, cos_t, sin_t):
    tm = TM_PROJ
    per_b = SEQ // tm
    rope_map = lambda i: (jnp.where(i < B, 0, 1 + jnp.maximum(i - B, 0) % per_b), 0)
    return pl.pallas_call(
        _odd_in_kernel,
        grid=(R_ALL // tm,),
        in_specs=[
            pl.BlockSpec((tm, D), lambda i: (i, 0)),
            pl.BlockSpec((None, N_MOD, D), lambda i: (_who_flat(i, tm), 0, 0)),
            pl.BlockSpec((D, QKV), lambda i: (0, 0), pipeline_mode=pl.Buffered(1)),
            pl.BlockSpec((tm, LANES), rope_map),
            pl.BlockSpec((tm, LANES), rope_map),
        ],
        out_specs=pl.BlockSpec((tm, QKV), lambda i: (_batch_block(i), 0)),
        out_shape=jax.ShapeDtypeStruct((B * TOK, QKV), bf16),
        compiler_params=pltpu.CompilerParams(
            dimension_semantics=("parallel",), vmem_limit_bytes=32 * MIB),
        name="odd_in",
    )(h, mods, w, cos_t, sin_t)


def _attn_kernel(sink_ref, q_ref, kc_ref, kp_ref, k0_ref, kn_ref, vc_ref, vp_ref, v0_ref, vn_ref, o_ref,
                 k_scr, vt_scr, q_scr, s_scr, p_scr, ot_scr):
    blk = pl.program_id(1)
    cols = GROUP * CHUNK
    kv_w = KV_HEADS * DH
    key = lax.broadcasted_iota(jnp.int32, (CHUNK, CHUNK), 0)
    qry = lax.broadcasted_iota(jnp.int32, (CHUNK, CHUNK), 1)
    far = 1 << 20
    prev_ok = key >= qry + jnp.where(blk > 0, 0, far)
    next_ok = key <= qry - jnp.where(blk < N_BLK - 1, 0, far)
    lane_head = lax.broadcasted_iota(jnp.int32, (CHUNK, kv_w), 1) % LANES // (DH // 2)
    neg = -1e30
    n_key = CTX + 3 * CHUNK
    row0 = 0
    for k_part, v_part in ((kc_ref, vc_ref), (kp_ref, vp_ref), (k0_ref, v0_ref), (kn_ref, vn_ref)):
        rows = k_part.shape[0]
        k_scr[row0:row0 + rows, :] = k_part[...]
        v_t = v_part[...].astype(f32).T.astype(bf16)
        for j in range(KV_HEADS):
            vt_scr[j, 0:DH, row0:row0 + rows] = v_t[j * DH:(j + 1) * DH, :]
        row0 += rows
    for j in range(KV_HEADS):
        vt_scr[j, DH:DH + 16, :] = jnp.ones((16, n_key), bf16)
    def score_dot(j):
        keep = jnp.where(lane_head == j, 1.0, 0.0).astype(bf16)
        for g in range(GROUP):
            q_scr[j, g * CHUNK:(g + 1) * CHUNK, :] = q_ref[:, g * kv_w:(g + 1) * kv_w] * keep
        s_scr[j] = _dot_nt(k_scr[...], q_scr[j])

    def scores(j, g, slab):
        s = s_scr[j, slab * CHUNK:(slab + 1) * CHUNK, g * CHUNK:(g + 1) * CHUNK]
        if slab == 2:
            s = jnp.where(prev_ok, s, neg)
        if slab == 4:
            s = jnp.where(next_ok, s, neg)
        return s

    n_slab = n_key // CHUNK
    for j in range(KV_HEADS):
        score_dot(j)
    for j in range(KV_HEADS):
        sink_terms = []
        for g in range(GROUP):
            sink = jnp.full((1, CHUNK), sink_ref[j * GROUP + g] * LOG2E, f32)
            m8 = None
            for slab in range(n_slab):
                part = jnp.max(scores(j, g, slab).reshape(CHUNK // 8, 8, CHUNK), axis=0)
                m8 = part if m8 is None else jnp.maximum(m8, part)
            m = jnp.maximum(sink, jnp.max(m8, axis=0, keepdims=True))
            for slab in range(n_slab):
                p = jnp.exp2(scores(j, g, slab) - m)
                p_scr[j, slab * CHUNK:(slab + 1) * CHUNK, g * CHUNK:(g + 1) * CHUNK] = p.astype(bf16)
            sink_terms.append(jnp.exp2(sink - m))
        acc = _dot(vt_scr[j], p_scr[j])
        den = acc[DH:DH + 1, :] + jnp.concatenate(sink_terms, axis=1)
        ot_scr[j * DH:(j + 1) * DH, :] = acc[0:DH, :] * (1.0 / den)
    for g in range(GROUP):
        o_ref[:, g * kv_w:(g + 1) * kv_w] = ot_scr[:, g * CHUNK:(g + 1) * CHUNK].T.astype(bf16)


def _attention(qkv, sink):
    qkv3 = qkv.reshape(B, TOK, QKV)
    kv_w = KV_HEADS * DH
    k_col = HEADS_C * DH // kv_w
    v_col = k_col + 1
    lat0 = CTX // CHUNK
    q_spec = pl.BlockSpec((None, CHUNK, HEADS_C * DH), lambda b, i: (b, lat0 + i, 0))

    def band(col, off):
        def index(b, i):
            return (b, lat0 + jnp.clip(i + off, 0, N_BLK - 1), col)
        return pl.BlockSpec((None, CHUNK, kv_w), index)

    ctx = lambda col: pl.BlockSpec((None, CTX, kv_w), lambda b, i: (b, 0, col))
    return pl.pallas_call(
        _attn_kernel,
        grid=(B, N_BLK),
        in_specs=[
            pl.BlockSpec(memory_space=pltpu.SMEM),
            q_spec,
            ctx(k_col), band(k_col, -1), band(k_col, 0), band(k_col, 1),
            ctx(v_col), band(v_col, -1), band(v_col, 0), band(v_col, 1),
        ],
        out_specs=pl.BlockSpec((CHUNK, HEADS_C * DH), lambda b, i: (b * N_BLK + i, 0)),
        out_shape=jax.ShapeDtypeStruct((R_LAT, HEADS_C * DH), bf16),
        scratch_shapes=[
            pltpu.VMEM((CTX + 3 * CHUNK, KV_HEADS * DH), bf16),
            pltpu.VMEM((KV_HEADS, DH + 16, CTX + 3 * CHUNK), bf16),
            pltpu.VMEM((KV_HEADS, GROUP * CHUNK, KV_HEADS * DH), bf16),
            pltpu.VMEM((KV_HEADS, CTX + 3 * CHUNK, GROUP * CHUNK), f32),
            pltpu.VMEM((KV_HEADS, CTX + 3 * CHUNK, GROUP * CHUNK), bf16),
            pltpu.VMEM((KV_HEADS * DH, GROUP * CHUNK), f32),
        ],
        compiler_params=pltpu.CompilerParams(
            dimension_semantics=("parallel", "parallel"), vmem_limit_bytes=32 * MIB),
        name="window_attention",
    )(sink, qkv3, qkv3, qkv3, qkv3, qkv3, qkv3, qkv3, qkv3, qkv3)


def _rope_tables():
    rows = SEQ // GRID_W
    row, col = jnp.meshgrid(jnp.arange(rows), jnp.arange(GRID_W), indexing='ij')
    n_freq = DH // 4
    inv = ROPE_BASE ** (-jnp.arange(n_freq, dtype=f32) / n_freq)
    ang = jnp.concatenate([row.reshape(-1, 1).astype(f32) * inv,
                           col.reshape(-1, 1).astype(f32) * inv], axis=-1)
    reps = 2 * LANES // DH
    cos = jnp.tile(jnp.cos(ang), (1, reps))
    sin = jnp.tile(jnp.sin(ang), (1, reps))
    cos = jnp.concatenate([jnp.ones((TM_PROJ, LANES), f32), cos], axis=0)
    sin = jnp.concatenate([jnp.zeros((TM_PROJ, LANES), f32), sin], axis=0)
    return cos, sin


def kernel(x, c, ctx, c_ctx, ada_w, ada_b, ffn_w_in, ffn_w_out, even_w_in, even_w_out, mlstm_conv,
           mlstm_gate_b, mlstm_norm, sgu_norm, sgu_ws, sgu_b, odd_w_qkv, odd_w_out, attn_sink, final_norm):
    cs = jnp.concatenate([c_ctx[None, :], c, jnp.zeros((16 - 1 - B, D), f32)], axis=0)
    mods = _modulation(cs, ada_w, ada_b)[:, :1 + B, :].reshape(2, 1 + B, N_MOD, D)

    fw_in = ffn_w_in
    fw_out = ffn_w_out

    m0 = mods[0]
    h = _ffn((ctx.reshape(R_CTX, D), x.reshape(R_LAT, D)), m0, fw_in, fw_out, sel=(0, 0), mi=0)
    w_in = even_w_in[0]
    gate0 = 4 * W_A
    gate1 = gate0 + 4 * HEADS_A
    w_in = jnp.concatenate(
        [w_in[:, :gate0], w_in[:, gate1:], w_in[:, gate0:gate1],
         jnp.zeros((D, LANES - 4 * HEADS_A), f32)], axis=1).astype(bf16)
    qk, vo, uv, gates = _even_in(h, m0, w_in)
    gate_b = jnp.pad(mlstm_gate_b[0].reshape(1, 4 * HEADS_A), ((0, 0), (0, LANES - 4 * HEADS_A)))
    mnorm_t = jnp.broadcast_to(mlstm_norm[0][:, :, None], (HEADS_A, CHUNK, LANES))
    ha = _mlstm(qk, vo, gates, mlstm_conv[0], gate_b, mnorm_t)
    sbx = jnp.repeat(sgu_b[0].T, LANES, axis=1)
    h = _even_out(h, m0, ha.reshape(B * TOK, W_A), uv, sgu_norm[0].reshape(1, W_A),
                  sgu_ws[0].astype(bf16), sbx, even_w_out[0].astype(bf16))
    h = _ffn(h, m0, fw_in, fw_out, sel=(0, 1), mi=6)

    m1 = mods[1]
    h = _ffn(h, m1, fw_in, fw_out, sel=(1, 0), mi=0)
    cos_t, sin_t = _rope_tables()
    qdim = HEADS_C * DH
    kdim = KV_HEADS * DH
    w_q = odd_w_qkv[0][:, :qdim].reshape(D, KV_HEADS, GROUP, DH // 2, 2).transpose(0, 2, 4, 1, 3).reshape(D, qdim)
    w_k = odd_w_qkv[0][:, qdim:qdim + kdim].reshape(D, KV_HEADS, DH // 2, 2).transpose(0, 3, 1, 2).reshape(D, kdim)
    w_qkv = jnp.concatenate([w_q, w_k, odd_w_qkv[0][:, qdim + kdim:]], axis=1).astype(bf16)
    w_o = odd_w_out[0].reshape(KV_HEADS, GROUP, DH, D).transpose(1, 0, 2, 3).reshape(qdim, D).astype(bf16)
    qkv = _odd_in(h, m1, w_qkv, cos_t, sin_t)
    attn = _attention(qkv, attn_sink[0])
    out = _ffn(h, m1, fw_in, fw_out, sel=(1, 1), mi=6, last=(attn, w_o, final_norm))
    return out.reshape(B, SEQ, D)
```

```python
import functools

import jax
import jax.numpy as jnp
from jax import lax
from jax.experimental import pallas as pl
from jax.experimental.pallas import tpu as pltpu

f32 = jnp.float32
bf16 = jnp.bfloat16

D = 1024
B = 8
SEQ = 2048
CTX = 256
TOK = CTX + SEQ
GRID_W = 64
N_MOD = 9
D_FF = 2816
EPS = 1e-6
HEADS_A = 4
CHUNK = 128
N_CHUNK = TOK // CHUNK
N_CTX_CHUNK = CTX // CHUNK
W_A = 512
EVEN_COLS = 3200
HEADS_C = 16
KV_HEADS = 4
GROUP = HEADS_C // KV_HEADS
DH = 64
QKV = (HEADS_C + 2 * KV_HEADS) * DH
N_BLK = SEQ // CHUNK
ROPE_BASE = 10000.0
LOG2E = 1.4426950408889634

R_CTX = B * CTX
R_LAT = B * SEQ
R_ALL = R_CTX + R_LAT

LANES = 128
TM_FFN = 512
TM_PROJ = 256
FC = 256
N_FC = D_FF // FC
MIB = 1024 * 1024


def _dot(a, b):
    return jnp.dot(a, b, preferred_element_type=f32)


def _dot_nt(a, b):
    return lax.dot_general(a, b, (((1,), (1,)), ((), ())), preferred_element_type=f32)


def _dot_tn(a, b):
    return lax.dot_general(a, b, (((0,), (0,)), ((), ())), preferred_element_type=f32)


def _sigmoid(x):
    return 1.0 / (1.0 + jnp.exp(-x))


def _split3(x):
    hi = x.astype(bf16)
    r1 = x - hi.astype(f32)
    mid = r1.astype(bf16)
    lo = (r1 - mid.astype(f32)).astype(bf16)
    return hi, mid, lo


def _modulated(h, shift, scale):
    ms = jnp.mean(h * h, axis=-1, keepdims=True)
    return h * lax.rsqrt(ms + EPS) * (1.0 + scale) + shift


def _mod_kernel(c_ref, w_ref, b_ref, o_ref):
    x = c_ref[...]
    s = x * _sigmoid(x)
    w = w_ref[...]
    s_hi = s.astype(bf16)
    s_lo = (s - s_hi.astype(f32)).astype(bf16)
    w_hi = w.astype(bf16)
    w_lo = (w - w_hi.astype(f32)).astype(bf16)
    o_ref[...] = _dot(s_hi, w_hi) + _dot(s_hi, w_lo) + _dot(s_lo, w_hi) + b_ref[...]


def _modulation(cs, ada_w, ada_b):
    depth = ada_w.shape[0]
    rows = cs.shape[0]
    n_col = N_MOD * D
    tn = 1024
    return pl.pallas_call(
        _mod_kernel,
        grid=(depth, n_col // tn),
        in_specs=[
            pl.BlockSpec((rows, D), lambda l, j: (0, 0)),
            pl.BlockSpec((None, D, tn), lambda l, j: (l, 0, j)),
            pl.BlockSpec((None, 1, tn), lambda l, j: (l, 0, j)),
        ],
        out_specs=pl.BlockSpec((None, rows, tn), lambda l, j: (l, 0, j)),
        out_shape=jax.ShapeDtypeStruct((depth, rows, n_col), f32),
        compiler_params=pltpu.CompilerParams(
            dimension_semantics=("parallel", "parallel"), vmem_limit_bytes=32 * MIB),
        name="modulation",
    )(cs, ada_w, ada_b.reshape(depth, 1, n_col))


def _who_flat(tile, tm):
    n_ctx = R_CTX // tm
    per_b = SEQ // tm
    return jnp.where(tile < n_ctx, 0, 1 + jnp.maximum(tile - n_ctx, 0) // per_b)


def _batch_block(tile):
    per_b = SEQ // TM_PROJ
    lat = jnp.maximum(tile - B, 0)
    blocks_b = TOK // TM_PROJ
    return jnp.where(tile < B, blocks_b * tile, blocks_b * (lat // per_b) + 1 + lat % per_b)


W_CHUNKS = 16
W_SLOTS = 4


def _fetch_cast(src, dst, stage, sem):
    rows = dst.shape[0] // W_CHUNKS

    def piece(c):
        slot = c % W_SLOTS
        return pltpu.make_async_copy(src.at[pl.ds(c * rows, rows), :], stage.at[slot], sem.at[slot])

    for c in range(W_SLOTS - 1):
        piece(c).start()
    for c in range(W_CHUNKS):
        if c + W_SLOTS - 1 < W_CHUNKS:
            piece(c + W_SLOTS - 1).start()
        piece(c).wait()
        dst[c * rows:(c + 1) * rows, :] = stage[c % W_SLOTS].astype(bf16)


def _ffn_kernel(*refs, mi, final, split, sel):
    refs = list(refs)
    if split:
        c_ref, x_ref = refs[0:2]
        refs = refs[2:]
        is_ctx = pl.program_id(0) < R_CTX // TM_FFN
        read_h = lambda: jnp.where(is_ctx, c_ref[...], x_ref[...])
    else:
        h_ref = refs.pop(0)
        read_h = lambda: h_ref[...]
    wi_ref, wo_ref, wi_stage, wo_stage, wi_sem, wo_sem = refs[-6:]
    refs = refs[:-6]
    if final:
        a_ref, wa_ref, mod_ref, wi_hbm, wo_hbm, fn_ref, o_ref, n_scr, acc_scr, h_scr = refs
    else:
        mod_ref, wi_hbm, wo_hbm, o_ref, n_scr, acc_scr = refs

    @pl.when(pl.program_id(0) == 0)
    def _():
        _fetch_cast(wi_hbm.at[sel[0], sel[1]], wi_ref, wi_stage, wi_sem)
        _fetch_cast(wo_hbm.at[sel[0], sel[1]], wo_ref, wo_stage, wo_sem)

    if final:
        h_scr[...] = h_ref[...] + mod_ref[5:6, :] * _dot(a_ref[...], wa_ref[...])
        read_h = lambda: h_scr[...]
    n_scr[...] = _modulated(read_h(), mod_ref[mi:mi + 1, :], mod_ref[mi + 1:mi + 2, :]).astype(bf16)
    for j in range(N_FC):
        n = n_scr[...]
        g = _dot(n, wi_ref[:, j * FC:(j + 1) * FC])
        u = _dot(n, wi_ref[:, D_FF + j * FC:D_FF + (j + 1) * FC])
        a = (g * _sigmoid(g) * u).astype(bf16)
        y = _dot(a, wo_ref[j * FC:(j + 1) * FC, :])
        if j == 0:
            acc_scr[...] = y
        else:
            acc_scr[...] += y
    out = read_h() + (0.5 * mod_ref[mi + 2:mi + 3, :]) * acc_scr[...]
    if final:
        ms = jnp.mean(out * out, axis=-1, keepdims=True)
        out = out * lax.rsqrt(ms + EPS) * fn_ref[...]
    o_ref[...] = out


def _ffn(h, mods, w_in, w_out, *, sel, mi, last=None):
    tm = TM_FFN
    tile0 = R_CTX // tm if last is not None else 0
    split = isinstance(h, tuple)
    const2 = lambda i: (0, 0)
    if split:
        n_ctx = R_CTX // tm
        rows_out = R_ALL
        in_specs = [
            pl.BlockSpec((tm, D), lambda i: (jnp.minimum(i, n_ctx - 1), 0)),
            pl.BlockSpec((tm, D), lambda i: (jnp.maximum(i - n_ctx, 0), 0)),
        ]
        args = list(h)
    else:
        rows_out = h.shape[0] - tile0 * tm
        in_specs = [pl.BlockSpec((tm, D), lambda i: (i + tile0, 0))]
        args = [h]
    scratch = [pltpu.VMEM((tm, D), bf16), pltpu.VMEM((tm, D), f32)]
    if last is not None:
        attn, w_attn, final_norm = last
        in_specs += [
            pl.BlockSpec((tm, D), lambda i: (i, 0)),
            pl.BlockSpec((D, D), const2, pipeline_mode=pl.Buffered(1)),
        ]
        args += [attn, w_attn]
        scratch.append(pltpu.VMEM((tm, D), f32))
    in_specs += [
        pl.BlockSpec((None, N_MOD, D), lambda i: (_who_flat(i + tile0, tm), 0, 0)),
        pl.BlockSpec(memory_space=pl.ANY),
        pl.BlockSpec(memory_space=pl.ANY),
    ]
    args += [mods, w_in, w_out]
    if last is not None:
        in_specs.append(pl.BlockSpec((1, D), const2))
        args.append(final_norm.reshape(1, D))
    scratch += [
        pltpu.VMEM((D, 2 * D_FF), bf16),
        pltpu.VMEM((D_FF, D), bf16),
        pltpu.VMEM((W_SLOTS, D // W_CHUNKS, 2 * D_FF), f32),
        pltpu.VMEM((W_SLOTS, D_FF // W_CHUNKS, D), f32),
        pltpu.SemaphoreType.DMA((W_SLOTS,)),
        pltpu.SemaphoreType.DMA((W_SLOTS,)),
    ]
    return pl.pallas_call(
        functools.partial(_ffn_kernel, mi=mi, final=last is not None, split=split, sel=sel),
        grid=(rows_out // tm,),
        in_specs=in_specs,
        out_specs=pl.BlockSpec((tm, D), lambda i: (i, 0)),
        out_shape=jax.ShapeDtypeStruct((rows_out, D), f32),
        scratch_shapes=scratch,
        compiler_params=pltpu.CompilerParams(
            dimension_semantics=("arbitrary",), vmem_limit_bytes=52 * MIB),
        name="ffn_final" if last is not None else "ffn",
    )(*args)


def _even_in_kernel(h_ref, mod_ref, w_ref, qk_ref, vo_ref, uv_ref, g_ref):
    n = _modulated(h_ref[...], mod_ref[3:4, :], mod_ref[4:5, :]).astype(bf16)
    qk_ref[...] = _dot(n, w_ref[:, 0:1024])
    vo_ref[:, 0:W_A] = _dot(n, w_ref[:, 1024:1024 + W_A]).astype(bf16)
    vo_ref[:, W_A:2 * W_A] = _sigmoid(_dot(n, w_ref[:, 1024 + W_A:2048])).astype(bf16)
    uv_ref[...] = _gelu_tanh(_dot(n, w_ref[:, 2048:3072])).astype(bf16)
    g_ref[...] = _dot(n, w_ref[:, 3072:3200])


def _even_in(h, mods, w):
    tm = TM_PROJ
    out_map = lambda i: (_batch_block(i), 0)
    return pl.pallas_call(
        _even_in_kernel,
        grid=(R_ALL // tm,),
        in_specs=[
            pl.BlockSpec((tm, D), lambda i: (i, 0)),
            pl.BlockSpec((None, N_MOD, D), lambda i: (_who_flat(i, tm), 0, 0)),
            pl.BlockSpec((D, EVEN_COLS), lambda i: (0, 0), pipeline_mode=pl.Buffered(1)),
        ],
        out_specs=[
            pl.BlockSpec((tm, 1024), out_map),
            pl.BlockSpec((tm, 1024), out_map),
            pl.BlockSpec((tm, 1024), out_map),
            pl.BlockSpec((tm, LANES), out_map),
        ],
        out_shape=[
            jax.ShapeDtypeStruct((B * TOK, 1024), f32),
            jax.ShapeDtypeStruct((B * TOK, 1024), bf16),
            jax.ShapeDtypeStruct((B * TOK, 1024), bf16),
            jax.ShapeDtypeStruct((B * TOK, LANES), f32),
        ],
        compiler_params=pltpu.CompilerParams(
            dimension_semantics=("parallel",), vmem_limit_bytes=40 * MIB),
        name="even_in",
    )(h, mods, w)


N_AUG = CHUNK + 16
CHUNKS_PER_ITER = 9


def _chunk_loop(body):
    def group(i, carry):
        for u in range(CHUNKS_PER_ITER):
            carry = body(i * CHUNKS_PER_ITER + u, carry)
        return carry
    lax.fori_loop(0, N_CHUNK // CHUNKS_PER_ITER, group, 0)


def _mlstm_kernel(q_ref, k_ref, v_ref, o_ref, g_ref, cq_ref, ck_ref, gb_ref, mn_ref, out_ref,
                  ks, qts, vts, xc, xr, cl, st, cst, mst):
    head = pl.program_id(1)
    rowi = lax.broadcasted_iota(jnp.int32, (CHUNK, CHUNK), 0)
    coli = lax.broadcasted_iota(jnp.int32, (CHUNK, CHUNK), 1)
    lower = coli <= rowi
    upper = coli >= rowi
    tri = jnp.where(lower, 1.0, 0.0).astype(bf16)

    for c in range(N_CHUNK):
        lo = c * CHUNK
        seq_start = lo in (0, CTX)
        seq_end = lo + CHUNK in (CTX, TOK)
        for src, cw, is_q in ((q_ref, cq_ref, True), (k_ref, ck_ref, False)):
            x = src[lo:lo + CHUNK, :]
            first = jnp.zeros((1, LANES), f32) if seq_start else src[lo - 1:lo, :]
            last = jnp.zeros((1, LANES), f32) if seq_end else src[lo + CHUNK:lo + CHUNK + 1, :]
            prev = jnp.where(rowi == 0, first, pltpu.roll(x, 1, 0))
            nxt = jnp.where(rowi == CHUNK - 1, last, pltpu.roll(x, CHUNK - 1, 0))
            y = cw[0:1, :] * prev + cw[1:2, :] * x + cw[2:3, :] * nxt
            y = y * _sigmoid(y)
            if is_q:
                qts[lo:lo + CHUNK, :] = y.T.astype(bf16)
            else:
                ks[lo:lo + CHUNK, :] = (y * CHUNK ** -0.5).astype(bf16)
        vts[lo:lo + CHUNK, :] = v_ref[lo:lo + CHUNK, :].astype(f32).T.astype(bf16)

    @pl.when(head == 0)
    def _():
        kind = (coli // HEADS_A) % 4
        for c in range(N_CHUNK):
            lo = c * CHUNK
            gt = g_ref[lo:lo + CHUNK, :] + gb_ref[...]
            lf = jnp.minimum(gt, 0.0) - jnp.log1p(jnp.exp(-jnp.abs(gt)))
            hi, mid, lw = _split3(lf)
            pre = _dot(tri, hi) + _dot(tri, mid) + _dot(tri, lw)
            suf = pre[CHUNK - 1:CHUNK, :] - pre + lf
            x = jnp.where(kind == 1, pre, jnp.where(kind == 3, suf, gt))
            xr[16 * c:16 * c + 16, :] = x.T[0:16, :]
            xc[lo:lo + CHUNK, :] = x

    lane_shift = (LANES - head) % LANES

    def gate_rows(c, d):
        row = c * 16 + 8 * d + head
        return xr[pl.ds(row, 1), :], xr[pl.ds(row + HEADS_A, 1), :]

    def local_state(c, carry):
        lo = pl.multiple_of(c * CHUNK, CHUNK)
        k = ks[pl.ds(lo, CHUNK), :]
        v_t = vts[pl.ds(lo, CHUNK), :].astype(f32)
        lhs = []
        for d in (0, 1):
            ig_r, b_r = gate_rows(c, d)
            g = b_r[:, CHUNK - 1:CHUNK] if d == 0 else b_r[:, 0:1]
            a_r = g - b_r + ig_r
            m_loc = jnp.max(a_r, axis=1, keepdims=True)
            w_r = jnp.exp(a_r - m_loc)
            lhs += [v_t * w_r, jnp.broadcast_to(w_r, (16, LANES))]
            st[d * N_CHUNK + c, 0:1, :] = jnp.broadcast_to(m_loc, (1, LANES))
            st[d * N_CHUNK + c, 1:2, :] = jnp.broadcast_to(g, (1, LANES))
        both = _dot(jnp.concatenate(lhs, axis=0).astype(bf16), k)
        cl[c] = both[0:N_AUG, :]
        cl[N_CHUNK + c] = both[N_AUG:2 * N_AUG, :]
        return carry

    _chunk_loop(local_state)

    cst[...] = jnp.zeros(cst.shape, f32)
    mst[...] = jnp.zeros(mst.shape, f32)

    def scan_step(i, carry):
        for d in (0, 1):
            c = i if d == 0 else jnp.where(i < N_CTX_CHUNK, N_CTX_CHUNK - 1 - i, N_CHUNK + N_CTX_CHUNK - 1 - i)
            idx = d * N_CHUNK + c
            c_loc = cl[idx]
            m_loc = st[idx, 0:1, :]
            g = st[idx, 1:2, :]
            c_prev = cst[d]
            m_prev = mst[d, 0:1, :]
            m_new = jnp.maximum(g + m_prev, m_loc)
            dec = jnp.exp(g + m_prev - m_new)
            add = jnp.exp(m_loc - m_new)
            cl[idx] = c_prev
            st[idx, 2:3, :] = m_prev
            cst[d] = dec * c_prev + add * c_loc
            mst[d, 0:1, :] = m_new
        return carry

    lax.fori_loop(0, N_CHUNK, scan_step, 0)

    def outputs(c, carry):
        lo = pl.multiple_of(c * CHUNK, CHUNK)
        k = ks[pl.ds(lo, CHUNK), :]
        q_t = qts[pl.ds(lo, CHUNK), :]
        v_aug = jnp.concatenate([vts[pl.ds(lo, CHUNK), :], jnp.ones((16, LANES), bf16)], axis=0)
        q_f = q_t.astype(f32)
        s_t = _dot(k, q_t)
        x_c = pltpu.roll(xc[pl.ds(lo, CHUNK), :], lane_shift, 1)
        hs = None
        for d in (0, 1):
            _, b_r = gate_rows(c, d)
            idx = d * N_CHUNK + c
            r_c = x_c[:, 8 * d:8 * d + 1] - x_c[:, 8 * d + 4:8 * d + 5]
            dm = jnp.where(upper if d == 0 else lower, b_r + r_c, -jnp.inf)
            e_r = b_r + st[idx, 2:3, :]
            m_t = jnp.maximum(e_r, jnp.max(dm, axis=0, keepdims=True))
            p_t = s_t * jnp.exp(dm - m_t)
            inter = jnp.exp(e_r - m_t)
            lhs = jnp.concatenate([v_aug, cl[idx].astype(bf16)], axis=1)
            rhs = jnp.concatenate([p_t, q_f * inter], axis=0).astype(bf16)
            nd = _dot(lhs, rhs)
            den = nd[CHUNK:CHUNK + 1, :]
            h_d = nd[0:CHUNK, :] * (1.0 / jnp.maximum(jnp.abs(den), jnp.exp(-m_t)))
            hs = h_d if hs is None else hs + h_d
        hn = hs * lax.rsqrt(jnp.mean(hs * hs, axis=0, keepdims=True) + EPS) * mn_ref[...]
        out_ref[pl.ds(lo, CHUNK), :] = (o_ref[pl.ds(lo, CHUNK), :].astype(f32) * hn.T).astype(bf16)
        return carry

    _chunk_loop(outputs)


def _mlstm(qk, vo, gates, conv_w, gate_b, mnorm):
    blk = lambda col0: pl.BlockSpec((None, TOK, LANES), lambda b, h: (b, 0, col0 + h))
    return pl.pallas_call(
        _mlstm_kernel,
        grid=(B, HEADS_A),
        in_specs=[
            blk(0), blk(HEADS_A),
            blk(0), blk(HEADS_A),
            pl.BlockSpec((None, TOK, LANES), lambda b, h: (b, 0, 0)),
            pl.BlockSpec((3, LANES), lambda b, h: (0, h)),
            pl.BlockSpec((3, LANES), lambda b, h: (0, HEADS_A + h)),
            pl.BlockSpec((1, LANES), lambda b, h: (0, 0)),
            pl.BlockSpec((None, CHUNK, LANES), lambda b, h: (h, 0, 0)),
        ],
        out_specs=pl.BlockSpec((None, TOK, LANES), lambda b, h: (b, 0, h)),
        out_shape=jax.ShapeDtypeStruct((B, TOK, W_A), bf16),
        scratch_shapes=[
            pltpu.VMEM((TOK, LANES), bf16),
            pltpu.VMEM((TOK, LANES), bf16),
            pltpu.VMEM((TOK, LANES), bf16),
            pltpu.VMEM((TOK, LANES), f32),
            pltpu.VMEM((16 * N_CHUNK, LANES), f32),
            pltpu.VMEM((2 * N_CHUNK, CHUNK + 16, LANES), f32),
            pltpu.VMEM((2 * N_CHUNK, 8, LANES), f32),
            pltpu.VMEM((2, CHUNK + 16, LANES), f32),
            pltpu.VMEM((2, 8, LANES), f32),
        ],
        compiler_params=pltpu.CompilerParams(
            dimension_semantics=("parallel", "arbitrary"), vmem_limit_bytes=40 * MIB),
        name="mlstm",
    )(qk.reshape(B, TOK, 1024), qk.reshape(B, TOK, 1024), vo.reshape(B, TOK, 1024),
      vo.reshape(B, TOK, 1024), gates.reshape(B, TOK, LANES), conv_w, conv_w, gate_b, mnorm)


def _gelu_tanh(x):
    return x * (0.5 * (1.0 + jnp.tanh(0.7978845608028654 * (x + 0.044715 * (x * x * x)))))


def _even_out_kernel(h_ref, mod_ref, ha_ref, uv_ref, sg_ref, ws_ref, sb_ref, wo_ref, o_ref, hb_scr):
    u = uv_ref[:, 0:W_A].astype(f32)
    v = uv_ref[:, W_A:2 * W_A].astype(f32)
    vn = (v * lax.rsqrt(jnp.mean(v * v, axis=-1, keepdims=True) + EPS) * sg_ref[...]).astype(bf16)
    n_chunk = TM_PROJ // CHUNK
    for g in range(W_A // LANES):
        cs = slice(g * LANES, (g + 1) * LANES)
        rhs = jnp.concatenate([vn[n * CHUNK:(n + 1) * CHUNK, cs] for n in range(n_chunk)], axis=1)
        mixed = _dot(ws_ref[g], rhs)
        for n in range(n_chunk):
            r = slice(n * CHUNK, (n + 1) * CHUNK)
            hb_scr[r, cs] = (u[r, cs] * (mixed[:, n * LANES:(n + 1) * LANES] + sb_ref[:, cs])).astype(bf16)
    y =_dot(ha_ref[...], wo_ref[0:W_A, :]) + _dot(hb_scr[...], wo_ref[W_A:2 * W_A, :])
    o_ref[...] = h_ref[...] + mod_ref[5:6, :] * y


def _even_out(h, mods, ha, uv, sgu_norm, ws, sbx, w_out):
    tm = TM_PROJ
    in_map = lambda i: (_batch_block(i), 0)
    const2 = lambda i: (0, 0)
    return pl.pallas_call(
        _even_out_kernel,
        grid=(R_ALL // tm,),
        in_specs=[
            pl.BlockSpec((tm, D), lambda i: (i, 0)),
            pl.BlockSpec((None, N_MOD, D), lambda i: (_who_flat(i, tm), 0, 0)),
            pl.BlockSpec((tm, W_A), in_map),
            pl.BlockSpec((tm, 2 * W_A), in_map),
            pl.BlockSpec((1, W_A), const2),
            pl.BlockSpec((W_A // LANES, CHUNK, CHUNK), lambda i: (0, 0, 0)),
            pl.BlockSpec((CHUNK, W_A), const2),
            pl.BlockSpec((2 * W_A, D), const2),
        ],
        out_specs=pl.BlockSpec((tm, D), lambda i: (i, 0)),
        out_shape=jax.ShapeDtypeStruct((R_ALL, D), f32),
        scratch_shapes=[pltpu.VMEM((tm, W_A), bf16)],
        compiler_params=pltpu.CompilerParams(
            dimension_semantics=("parallel",), vmem_limit_bytes=32 * MIB),
        name="even_out",
    )(h, mods, ha, uv, sgu_norm, ws, sbx, w_out)


def _odd_in_kernel(h_ref, mod_ref, w_ref, cos_ref, sin_ref, o_ref):
    n = _modulated(h_ref[...], mod_ref[3:4, :], mod_ref[4:5, :]).astype(bf16)
    cos = cos_ref[...]
    sin = sin_ref[...]
    n_rot = (HEADS_C + KV_HEADS) * DH // LANES
    y = _dot(n, w_ref[:, 0:n_rot * LANES])
    for c in range(0, n_rot, 2):
        x1 = y[:, c * LANES:(c + 1) * LANES]
        x2 = y[:, (c + 1) * LANES:(c + 2) * LANES]
        r1 = x1 * cos - x2 * sin
        r2 = x1 * sin + x2 * cos
        if c < HEADS_C * DH // LANES:
            r1 = r1 * (DH ** -0.5 * LOG2E)
            r2 = r2 * (DH ** -0.5 * LOG2E)
        o_ref[:, c * LANES:(c + 1) * LANES] = r1.astype(bf16)
        o_ref[:, (c + 1) * LANES:(c + 2) * LANES] = r2.astype(bf16)
    v0 = n_rot * LANES
    o_ref[:, v0:QKV] = _dot(n, w_ref[:, v0:QKV]).astype(bf16)


def _odd_in(h, mods, w, cos_t, sin_t):
    tm = TM_PROJ
    per_b = SEQ // tm
    rope_map = lambda i: (jnp.where(i < B, 0, 1 + jnp.maximum(i - B, 0) % per_b), 0)
    return pl.pallas_call(
        _odd_in_kernel,
        grid=(R_ALL // tm,),
        in_specs=[
            pl.BlockSpec((tm, D), lambda i: (i, 0)),
            pl.BlockSpec((None, N_MOD, D), lambda i: (_who_flat(i, tm), 0, 0)),
            pl.BlockSpec((D, QKV), lambda i: (0, 0), pipeline_mode=pl.Buffered(1)),
            pl.BlockSpec((tm, LANES), rope_map),
            pl.BlockSpec((tm, LANES), rope_map),
        ],
        out_specs=pl.BlockSpec((tm, QKV), lambda i: (_batch_block(i), 0)),
        out_shape=jax.ShapeDtypeStruct((B * TOK, QKV), bf16),
        compiler_params=pltpu.CompilerParams(
            dimension_semantics=("parallel",), vmem_limit_bytes=32 * MIB),
        name="odd_in",
    )(h, mods, w, cos_t, sin_t)


def _attn_kernel(sink_ref, q_ref, kc_ref, kp_ref, k0_ref, kn_ref, vc_ref, vp_ref, v0_ref, vn_ref, o_ref,
                 k_scr, vt_scr, q_scr, s_scr, p_scr, ot_scr):
    blk = pl.program_id(1)
    cols = GROUP * CHUNK
    kv_w = KV_HEADS * DH
    key = lax.broadcasted_iota(jnp.int32, (CHUNK, CHUNK), 0)
    qry = lax.broadcasted_iota(jnp.int32, (CHUNK, CHUNK), 1)
    far = 1 << 20
    prev_ok = key >= qry + jnp.where(blk > 0, 0, far)
    next_ok = key <= qry - jnp.where(blk < N_BLK - 1, 0, far)
    lane_head = lax.broadcasted_iota(jnp.int32, (CHUNK, kv_w), 1) % LANES // (DH // 2)
    neg = -1e30
    n_key = CTX + 3 * CHUNK
    row0 = 0
    for k_part, v_part in ((kc_ref, vc_ref), (kp_ref, vp_ref), (k0_ref, v0_ref), (kn_ref, vn_ref)):
        rows = k_part.shape[0]
        k_scr[row0:row0 + rows, :] = k_part[...]
        v_t = v_part[...].astype(f32).T.astype(bf16)
        for j in range(KV_HEADS):
            vt_scr[j, 0:DH, row0:row0 + rows] = v_t[j * DH:(j + 1) * DH, :]
        row0 += rows
    for j in range(KV_HEADS):
        vt_scr[j, DH:DH + 16, :] = jnp.ones((16, n_key), bf16)
    def score_dot(j):
        keep = jnp.where(lane_head == j, 1.0, 0.0).astype(bf16)
        for g in range(GROUP):
            q_scr[j, g * CHUNK:(g + 1) * CHUNK, :] = q_ref[:, g * kv_w:(g + 1) * kv_w] * keep
        s_scr[j] = _dot_nt(k_scr[...], q_scr[j])

    def scores(j, g, slab):
        s = s_scr[j, slab * CHUNK:(slab + 1) * CHUNK, g * CHUNK:(g + 1) * CHUNK]
        if slab == 2:
            s = jnp.where(prev_ok, s, neg)
        if slab == 4:
            s = jnp.where(next_ok, s, neg)
        return s

    n_slab = n_key // CHUNK
    for j in range(KV_HEADS):
        score_dot(j)
    for j in range(KV_HEADS):
        sink_terms = []
        for g in range(GROUP):
            sink = jnp.full((1, CHUNK), sink_ref[j * GROUP + g] * LOG2E, f32)
            m8 = None
            for slab in range(n_slab):
                part = jnp.max(scores(j, g, slab).reshape(CHUNK // 8, 8, CHUNK), axis=0)
                m8 = part if m8 is None else jnp.maximum(m8, part)
            m = jnp.maximum(sink, jnp.max(m8, axis=0, keepdims=True))
            for slab in range(n_slab):
                p = jnp.exp2(scores(j, g, slab) - m)
                p_scr[j, slab * CHUNK:(slab + 1) * CHUNK, g * CHUNK:(g + 1) * CHUNK] = p.astype(bf16)
            sink_terms.append(jnp.exp2(sink - m))
        acc = _dot(vt_scr[j], p_scr[j])
        den = acc[DH:DH + 1, :] + jnp.concatenate(sink_terms, axis=1)
        ot_scr[j * DH:(j + 1) * DH, :] = acc[0:DH, :] * (1.0 / den)
    for g in range(GROUP):
        o_ref[:, g * kv_w:(g + 1) * kv_w] = ot_scr[:, g * CHUNK:(g + 1) * CHUNK].T.astype(bf16)


def _attention(qkv, sink):
    qkv3 = qkv.reshape(B, TOK, QKV)
    kv_w = KV_HEADS * DH
    k_col = HEADS_C * DH // kv_w
    v_col = k_col + 1
    lat0 = CTX // CHUNK
    q_spec = pl.BlockSpec((None, CHUNK, HEADS_C * DH), lambda b, i: (b, lat0 + i, 0))

    def band(col, off):
        def index(b, i):
            return (b, lat0 + jnp.clip(i + off, 0, N_BLK - 1), col)
        return pl.BlockSpec((None, CHUNK, kv_w), index)

    ctx = lambda col: pl.BlockSpec((None, CTX, kv_w), lambda b, i: (b, 0, col))
    return pl.pallas_call(
        _attn_kernel,
        grid=(B, N_BLK),
        in_specs=[
            pl.BlockSpec(memory_space=pltpu.SMEM),
            q_spec,
            ctx(k_col), band(k_col, -1), band(k_col, 0), band(k_col, 1),
            ctx(v_col), band(v_col, -1), band(v_col, 0), band(v_col, 1),
        ],
        out_specs=pl.BlockSpec((CHUNK, HEADS_C * DH), lambda b, i: (b * N_BLK + i, 0)),
        out_shape=jax.ShapeDtypeStruct((R_LAT, HEADS_C * DH), bf16),
        scratch_shapes=[
            pltpu.VMEM((CTX + 3 * CHUNK, KV_HEADS * DH), bf16),
            pltpu.VMEM((KV_HEADS, DH + 16, CTX + 3 * CHUNK), bf16),
            pltpu.VMEM((KV_HEADS, GROUP * CHUNK, KV_HEADS * DH), bf16),
            pltpu.VMEM((KV_HEADS, CTX + 3 * CHUNK, GROUP * CHUNK), f32),
            pltpu.VMEM((KV_HEADS, CTX + 3 * CHUNK, GROUP * CHUNK), bf16),
            pltpu.VMEM((KV_HEADS * DH, GROUP * CHUNK), f32),
        ],
        compiler_params=pltpu.CompilerParams(
            dimension_semantics=("parallel", "parallel"), vmem_limit_bytes=32 * MIB),
        name="window_attention",
    )(sink, qkv3, qkv3, qkv3, qkv3, qkv3, qkv3, qkv3, qkv3, qkv3)


def _rope_tables():
    rows = SEQ // GRID_W
    row, col = jnp.meshgrid(jnp.arange(rows), jnp.arange(GRID_W), indexing='ij')
    n_freq = DH // 4
    inv = ROPE_BASE ** (-jnp.arange(n_freq, dtype=f32) / n_freq)
    ang = jnp.concatenate([row.reshape(-1, 1).astype(f32) * inv,
                           col.reshape(-1, 1).astype(f32) * inv], axis=-1)
    reps = 2 * LANES // DH
    cos = jnp.tile(jnp.cos(ang), (1, reps))
    sin = jnp.tile(jnp.sin(ang), (1, reps))
    cos = jnp.concatenate([jnp.ones((TM_PROJ, LANES), f32), cos], axis=0)
    sin = jnp.concatenate([jnp.zeros((TM_PROJ, LANES), f32), sin], axis=0)
    return cos, sin


def kernel(x, c, ctx, c_ctx, ada_w, ada_b, ffn_w_in, ffn_w_out, even_w_in, even_w_out, mlstm_conv,
           mlstm_gate_b, mlstm_norm, sgu_norm, sgu_ws, sgu_b, odd_w_qkv, odd_w_out, attn_sink, final_norm):
    cs = jnp.concatenate([c_ctx[None, :], c, jnp.zeros((16 - 1 - B, D), f32)], axis=0)
    mods = _modulation(cs, ada_w, ada_b)[:, :1 + B, :].reshape(2, 1 + B, N_MOD, D)

    fw_in = ffn_w_in
    fw_out = ffn_w_out

    m0 = mods[0]
    h = _ffn((ctx.reshape(R_CTX, D), x.reshape(R_LAT, D)), m0, fw_in, fw_out, sel=(0, 0), mi=0)
    w_in = even_w_in[0]
    gate0 = 4 * W_A
    gate1 = gate0 + 4 * HEADS_A
    w_in = jnp.concatenate(
        [w_in[:, :gate0], w_in[:, gate1:], w_in[:, gate0:gate1],
         jnp.zeros((D, LANES - 4 * HEADS_A), f32)], axis=1).astype(bf16)
    qk, vo, uv, gates = _even_in(h, m0, w_in)
    gate_b = jnp.pad(mlstm_gate_b[0].reshape(1, 4 * HEADS_A), ((0, 0), (0, LANES - 4 * HEADS_A)))
    mnorm_t = jnp.broadcast_to(mlstm_norm[0][:, :, None], (HEADS_A, CHUNK, LANES))
    ha = _mlstm(qk, vo, gates, mlstm_conv[0], gate_b, mnorm_t)
    sbx = jnp.repeat(sgu_b[0].T, LANES, axis=1)
    h = _even_out(h, m0, ha.reshape(B * TOK, W_A), uv, sgu_norm[0].reshape(1, W_A),
                  sgu_ws[0].astype(bf16), sbx, even_w_out[0].astype(bf16))
    h = _ffn(h, m0, fw_in, fw_out, sel=(0, 1), mi=6)

    m1 = mods[1]
    h = _ffn(h, m1, fw_in, fw_out, sel=(1, 0), mi=0)
    cos_t, sin_t = _rope_tables()
    qdim = HEADS_C * DH
    kdim = KV_HEADS * DH
    w_q = odd_w_qkv[0][:, :qdim].reshape(D, KV_HEADS, GROUP, DH // 2, 2).transpose(0, 2, 4, 1, 3).reshape(D, qdim)
    w_k = odd_w_qkv[0][:, qdim:qdim + kdim].reshape(D, KV_HEADS, DH // 2, 2).transpose(0, 3, 1, 2).reshape(D, kdim)
    w_qkv = jnp.concatenate([w_q, w_k, odd_w_qkv[0][:, qdim + kdim:]], axis=1).astype(bf16)
    w_o = odd_w_out[0].reshape(KV_HEADS, GROUP, DH, D).transpose(1, 0, 2, 3).reshape(qdim, D).astype(bf16)
    qkv = _odd_in(h, m1, w_qkv, cos_t, sin_t)
    attn = _attention(qkv, attn_sink[0])
    out = _ffn(h, m1, fw_in, fw_out, sel=(1, 1), mi=6, last=(attn, w_o, final_norm))
    return out.reshape(B, SEQ, D)
```

```python
import functools

import jax
import jax.numpy as jnp
from jax import lax
from jax.experimental import pallas as pl
from jax.experimental.pallas import tpu as pltpu

f32 = jnp.float32
bf16 = jnp.bfloat16

D = 1024
B = 8
SEQ = 2048
CTX = 256
TOK = CTX + SEQ
GRID_W = 64
N_MOD = 9
D_FF = 2816
EPS = 1e-6
HEADS_A = 4
CHUNK = 128
N_CHUNK = TOK // CHUNK
N_CTX_CHUNK = CTX // CHUNK
W_A = 512
EVEN_COLS = 3200
HEADS_C = 16
KV_HEADS = 4
GROUP = HEADS_C // KV_HEADS
DH = 64
QKV = (HEADS_C + 2 * KV_HEADS) * DH
N_BLK = SEQ // CHUNK
ROPE_BASE = 10000.0
LOG2E = 1.4426950408889634

R_CTX = B * CTX
R_LAT = B * SEQ
R_ALL = R_CTX + R_LAT

LANES = 128
TM_FFN = 512
TM_PROJ = 512
FC = 256
N_FC = D_FF // FC
MIB = 1024 * 1024


def _dot(a, b):
    return jnp.dot(a, b, preferred_element_type=f32)


def _dot_nt(a, b):
    return lax.dot_general(a, b, (((1,), (1,)), ((), ())), preferred_element_type=f32)


def _dot_tn(a, b):
    return lax.dot_general(a, b, (((0,), (0,)), ((), ())), preferred_element_type=f32)


def _sigmoid(x):
    return 1.0 / (1.0 + jnp.exp(-x))


def _split3(x):
    hi = x.astype(bf16)
    r1 = x - hi.astype(f32)
    mid = r1.astype(bf16)
    lo = (r1 - mid.astype(f32)).astype(bf16)
    return hi, mid, lo


def _modulated(h, shift, scale):
    ms = jnp.mean(h * h, axis=-1, keepdims=True)
    return h * lax.rsqrt(ms + EPS) * (1.0 + scale) + shift


def _mod_kernel(c_ref, w_ref, b_ref, o_ref):
    x = c_ref[...]
    s = x * _sigmoid(x)
    w = w_ref[...]
    s_hi = s.astype(bf16)
    s_lo = (s - s_hi.astype(f32)).astype(bf16)
    w_hi = w.astype(bf16)
    w_lo = (w - w_hi.astype(f32)).astype(bf16)
    o_ref[...] = _dot(s_hi, w_hi) + _dot(s_hi, w_lo) + _dot(s_lo, w_hi) + b_ref[...]


def _modulation(cs, ada_w, ada_b):
    depth = ada_w.shape[0]
    rows = cs.shape[0]
    n_col = N_MOD * D
    tn = 1024
    return pl.pallas_call(
        _mod_kernel,
        grid=(depth, n_col // tn),
        in_specs=[
            pl.BlockSpec((rows, D), lambda l, j: (0, 0)),
            pl.BlockSpec((None, D, tn), lambda l, j: (l, 0, j)),
            pl.BlockSpec((None, 1, tn), lambda l, j: (l, 0, j)),
        ],
        out_specs=pl.BlockSpec((None, rows, tn), lambda l, j: (l, 0, j)),
        out_shape=jax.ShapeDtypeStruct((depth, rows, n_col), f32),
        compiler_params=pltpu.CompilerParams(
            dimension_semantics=("parallel", "parallel"), vmem_limit_bytes=32 * MIB),
        name="modulation",
    )(cs, ada_w, ada_b.reshape(depth, 1, n_col))


def _who_flat(tile, tm):
    n_ctx = R_CTX // tm
    per_b = SEQ // tm
    return jnp.where(tile < n_ctx, 0, 1 + jnp.maximum(tile - n_ctx, 0) // per_b)


W_CHUNKS = 16
W_SLOTS = 4


def _fetch_cast(src, dst, stage, sem):
    rows = dst.shape[0] // W_CHUNKS

    def piece(c):
        slot = c % W_SLOTS
        return pltpu.make_async_copy(src.at[pl.ds(c * rows, rows), :], stage.at[slot], sem.at[slot])

    for c in range(W_SLOTS - 1):
        piece(c).start()
    for c in range(W_CHUNKS):
        if c + W_SLOTS - 1 < W_CHUNKS:
            piece(c + W_SLOTS - 1).start()
        piece(c).wait()
        dst[c * rows:(c + 1) * rows, :] = stage[c % W_SLOTS].astype(bf16)


def _gelu_tanh(x):
    return x * (0.5 * (1.0 + jnp.tanh(0.7978845608028654 * (x + 0.044715 * (x * x * x)))))


def _even_mix(ha, uv_ref, sg_ref, ws_ref, sb_ref, wm_ref, hb_scr):
    u = uv_ref[:, 0:W_A].astype(f32)
    v = uv_ref[:, W_A:2 * W_A].astype(f32)
    vn = (v * lax.rsqrt(jnp.mean(v * v, axis=-1, keepdims=True) + EPS) * sg_ref[...]).astype(bf16)
    n_chunk = TM_FFN // CHUNK
    for g in range(W_A // LANES):
        cs = slice(g * LANES, (g + 1) * LANES)
        rhs = jnp.concatenate([vn[n * CHUNK:(n + 1) * CHUNK, cs] for n in range(n_chunk)], axis=1)
        mixed = _dot(ws_ref[g], rhs)
        for n in range(n_chunk):
            r = slice(n * CHUNK, (n + 1) * CHUNK)
            hb_scr[r, cs] = (u[r, cs] * (mixed[:, n * LANES:(n + 1) * LANES] + sb_ref[:, cs])).astype(bf16)
    return _dot(ha, wm_ref[0:W_A, :]) + _dot(hb_scr[...], wm_ref[W_A:2 * W_A, :])


def _ffn_kernel(*refs, mi, mixer, final, split, sel):
    refs = list(refs)
    is_ctx = pl.program_id(0) < R_CTX // TM_FFN
    if split:
        c_ref, x_ref = refs[0:2]
        refs = refs[2:]
        read_h = lambda: jnp.where(is_ctx, c_ref[...], x_ref[...])
    else:
        h_ref = refs.pop(0)
        read_h = lambda: h_ref[...]
    wi_ref, wo_ref, wi_stage, wo_stage, wi_sem, wo_sem = refs[-6:]
    refs = refs[:-6]
    if mixer == "attn":
        a_ref, wm_ref = refs[0:2]
        refs = refs[2:]
    elif mixer == "even":
        hac_ref, hax_ref, uv_ref, sg_ref, ws_ref, sb_ref, wm_ref = refs[0:7]
        refs = refs[7:]
    mod_ref, wi_hbm, wo_hbm = refs[0:3]
    refs = refs[3:]
    if final:
        fn_ref = refs.pop(0)
    o_ref, n_scr, acc_scr = refs[0:3]
    refs = refs[3:]

    @pl.when(pl.program_id(0) == 0)
    def _():
        _fetch_cast(wi_hbm.at[sel[0], sel[1]], wi_ref, wi_stage, wi_sem)
        _fetch_cast(wo_hbm.at[sel[0], sel[1]], wo_ref, wo_stage, wo_sem)

    if mixer is not None:
        h_scr = refs.pop(0)
        if mixer == "attn":
            y = _dot(a_ref[...], wm_ref[...])
        else:
            ha = jnp.where(is_ctx, hac_ref[...], hax_ref[...])
            y = _even_mix(ha, uv_ref, sg_ref, ws_ref, sb_ref, wm_ref, refs.pop(0))
        h_scr[...] = read_h() + mod_ref[5:6, :] * y
        read_h = lambda: h_scr[...]
    n_scr[...] = _modulated(read_h(), mod_ref[mi:mi + 1, :], mod_ref[mi + 1:mi + 2, :]).astype(bf16)
    for j in range(N_FC):
        n = n_scr[...]
        g = _dot(n, wi_ref[:, j * FC:(j + 1) * FC])
        u = _dot(n, wi_ref[:, D_FF + j * FC:D_FF + (j + 1) * FC])
        a = (g * _sigmoid(g) * u).astype(bf16)
        y = _dot(a, wo_ref[j * FC:(j + 1) * FC, :])
        if j == 0:
            acc_scr[...] = y
        else:
            acc_scr[...] += y
    out = read_h() + (0.5 * mod_ref[mi + 2:mi + 3, :]) * acc_scr[...]
    if final:
        ms = jnp.mean(out * out, axis=-1, keepdims=True)
        out = out * lax.rsqrt(ms + EPS) * fn_ref[...]
    o_ref[...] = out


def _ffn(h, mods, w_in, w_out, *, sel, mi, even=None, last=None):
    tm = TM_FFN
    n_ctx = R_CTX // tm
    tile0 = n_ctx if last is not None else 0
    split = isinstance(h, tuple)
    const2 = lambda i: (0, 0)
    ctx_map = lambda i: (jnp.minimum(i, n_ctx - 1), 0)
    lat_map = lambda i: (jnp.maximum(i - n_ctx, 0), 0)
    if split:
        rows_out = R_ALL
        in_specs = [pl.BlockSpec((tm, D), ctx_map), pl.BlockSpec((tm, D), lat_map)]
        args = list(h)
    else:
        rows_out = h.shape[0] - tile0 * tm
        in_specs = [pl.BlockSpec((tm, D), lambda i: (i + tile0, 0))]
        args = [h]
    scratch = [pltpu.VMEM((tm, D), bf16), pltpu.VMEM((tm, D), f32)]
    mixer = None
    if last is not None:
        mixer = "attn"
        attn, w_attn, final_norm = last
        in_specs += [
            pl.BlockSpec((tm, D), lambda i: (i, 0)),
            pl.BlockSpec((D, D), const2, pipeline_mode=pl.Buffered(1)),
        ]
        args += [attn, w_attn]
        scratch.append(pltpu.VMEM((tm, D), f32))
    elif even is not None:
        mixer = "even"
        in_specs += [
            pl.BlockSpec((tm, W_A), ctx_map),
            pl.BlockSpec((tm, W_A), lat_map),
            pl.BlockSpec((tm, 2 * W_A), lambda i: (i, 0)),
            pl.BlockSpec((1, W_A), const2),
            pl.BlockSpec((W_A // LANES, CHUNK, CHUNK), lambda i: (0, 0, 0)),
            pl.BlockSpec((CHUNK, W_A), const2),
            pl.BlockSpec((2 * W_A, D), const2, pipeline_mode=pl.Buffered(1)),
        ]
        args += list(even)
        scratch += [pltpu.VMEM((tm, D), f32), pltpu.VMEM((tm, W_A), bf16)]
    in_specs += [
        pl.BlockSpec((None, N_MOD, D), lambda i: (_who_flat(i + tile0, tm), 0, 0)),
        pl.BlockSpec(memory_space=pl.ANY),
        pl.BlockSpec(memory_space=pl.ANY),
    ]
    args += [mods, w_in, w_out]
    if last is not None:
        in_specs.append(pl.BlockSpec((1, D), const2))
        args.append(final_norm.reshape(1, D))
    scratch += [
        pltpu.VMEM((D, 2 * D_FF), bf16),
        pltpu.VMEM((D_FF, D), bf16),
        pltpu.VMEM((W_SLOTS, D // W_CHUNKS, 2 * D_FF), f32),
        pltpu.VMEM((W_SLOTS, D_FF // W_CHUNKS, D), f32),
        pltpu.SemaphoreType.DMA((W_SLOTS,)),
        pltpu.SemaphoreType.DMA((W_SLOTS,)),
    ]
    return pl.pallas_call(
        functools.partial(_ffn_kernel, mi=mi, mixer=mixer, final=last is not None, split=split, sel=sel),
        grid=(rows_out // tm,),
        in_specs=in_specs,
        out_specs=pl.BlockSpec((tm, D), lambda i: (i, 0)),
        out_shape=jax.ShapeDtypeStruct((rows_out, D), f32),
        scratch_shapes=scratch,
        compiler_params=pltpu.CompilerParams(
            dimension_semantics=("arbitrary",), vmem_limit_bytes=56 * MIB),
        name={None: "ffn", "even": "ffn_even", "attn": "ffn_final"}[mixer],
    )(*args)


def _even_in_kernel(h_ref, mod_ref, w_ref, qk_ref, vo_ref, uv_ref, g_ref):
    n = _modulated(h_ref[...], mod_ref[3:4, :], mod_ref[4:5, :]).astype(bf16)
    qk_ref[...] = _dot(n, w_ref[:, 0:1024])
    vo_ref[:, 0:W_A] = _dot(n, w_ref[:, 1024:1024 + W_A]).astype(bf16)
    vo_ref[:, W_A:2 * W_A] = _sigmoid(_dot(n, w_ref[:, 1024 + W_A:2048])).astype(bf16)
    uv_ref[...] = _gelu_tanh(_dot(n, w_ref[:, 2048:3072])).astype(bf16)
    g_ref[...] = _dot(n, w_ref[:, 3072:3200])


def _even_in(h, mods, w):
    tm = TM_PROJ
    out_map = lambda i: (i, 0)
    return pl.pallas_call(
        _even_in_kernel,
        grid=(R_ALL // tm,),
        in_specs=[
            pl.BlockSpec((tm, D), lambda i: (i, 0)),
            pl.BlockSpec((None, N_MOD, D), lambda i: (_who_flat(i, tm), 0, 0)),
            pl.BlockSpec((D, EVEN_COLS), lambda i: (0, 0), pipeline_mode=pl.Buffered(1)),
        ],
        out_specs=[
            pl.BlockSpec((tm, 1024), out_map),
            pl.BlockSpec((tm, 1024), out_map),
            pl.BlockSpec((tm, 1024), out_map),
            pl.BlockSpec((tm, LANES), out_map),
        ],
        out_shape=[
            jax.ShapeDtypeStruct((R_ALL, 1024), f32),
            jax.ShapeDtypeStruct((R_ALL, 1024), bf16),
            jax.ShapeDtypeStruct((R_ALL, 1024), bf16),
            jax.ShapeDtypeStruct((R_ALL, LANES), f32),
        ],
        compiler_params=pltpu.CompilerParams(
            dimension_semantics=("parallel",), vmem_limit_bytes=40 * MIB),
        name="even_in",
    )(h, mods, w)


N_AUG = CHUNK + 16
CHUNKS_PER_ITER = 9
LAT_CHUNKS_PER_ITER = 8


def _chunk_loop(body):
    def group(i, carry):
        for u in range(CHUNKS_PER_ITER):
            carry = body(i * CHUNKS_PER_ITER + u, carry)
        return carry
    lax.fori_loop(0, N_CHUNK // CHUNKS_PER_ITER, group, 0)


def _mlstm_kernel(qc_ref, ql_ref, kc_ref, kl_ref, vc_ref, vl_ref, oc_ref, ol_ref, gc_ref, gl_ref,
                  cq_ref, ck_ref, gb_ref, mn_ref, outc_ref, outl_ref,
                  ks, qts, vts, xc, xr, cl, st, cst, mst):
    head = pl.program_id(1)
    rowi = lax.broadcasted_iota(jnp.int32, (CHUNK, CHUNK), 0)
    coli = lax.broadcasted_iota(jnp.int32, (CHUNK, CHUNK), 1)
    lower = coli <= rowi
    upper = coli >= rowi
    tri = jnp.where(lower, 1.0, 0.0).astype(bf16)

    def part(ctx_ref, lat_ref, c):
        return (ctx_ref, c * CHUNK) if c < N_CTX_CHUNK else (lat_ref, (c - N_CTX_CHUNK) * CHUNK)

    for c in range(N_CHUNK):
        lo = c * CHUNK
        for refs, cw, is_q in (((qc_ref, ql_ref), cq_ref, True), ((kc_ref, kl_ref), ck_ref, False)):
            src, at = part(*refs, c)
            x = src[at:at + CHUNK, :]
            first = jnp.zeros((1, LANES), f32) if at == 0 else src[at - 1:at, :]
            last = jnp.zeros((1, LANES), f32) if at + CHUNK == src.shape[0] else src[at + CHUNK:at + CHUNK + 1, :]
            prev = jnp.where(rowi == 0, first, pltpu.roll(x, 1, 0))
            nxt = jnp.where(rowi == CHUNK - 1, last, pltpu.roll(x, CHUNK - 1, 0))
            y = cw[0:1, :] * prev + cw[1:2, :] * x + cw[2:3, :] * nxt
            y = y * _sigmoid(y)
            if is_q:
                qts[lo:lo + CHUNK, :] = y.T.astype(bf16)
            else:
                ks[lo:lo + CHUNK, :] = (y * CHUNK ** -0.5).astype(bf16)
        src, at = part(vc_ref, vl_ref, c)
        vts[lo:lo + CHUNK, :] = src[at:at + CHUNK, :].astype(f32).T.astype(bf16)

    @pl.when(head == 0)
    def _():
        kind = (coli // HEADS_A) % 4
        for c in range(N_CHUNK):
            lo = c * CHUNK
            src, at = part(gc_ref, gl_ref, c)
            gt = src[at:at + CHUNK, :] + gb_ref[...]
            lf = jnp.minimum(gt, 0.0) - jnp.log1p(jnp.exp(-jnp.abs(gt)))
            hi, mid, lw = _split3(lf)
            pre = _dot(tri, hi) + _dot(tri, mid) + _dot(tri, lw)
            suf = pre[CHUNK - 1:CHUNK, :] - pre + lf
            x = jnp.where(kind == 1, pre, jnp.where(kind == 3, suf, gt))
            xr[16 * c:16 * c + 16, :] = x.T[0:16, :]
            xc[lo:lo + CHUNK, :] = x

    lane_shift = (LANES - head) % LANES

    def gate_rows(c, d):
        row = c * 16 + 8 * d + head
        return xr[pl.ds(row, 1), :], xr[pl.ds(row + HEADS_A, 1), :]

    def local_state(c, carry):
        lo = pl.multiple_of(c * CHUNK, CHUNK)
        k = ks[pl.ds(lo, CHUNK), :]
        v_t = vts[pl.ds(lo, CHUNK), :].astype(f32)
        lhs = []
        for d in (0, 1):
            ig_r, b_r = gate_rows(c, d)
            g = b_r[:, CHUNK - 1:CHUNK] if d == 0 else b_r[:, 0:1]
            a_r = g - b_r + ig_r
            m_loc = jnp.max(a_r, axis=1, keepdims=True)
            w_r = jnp.exp(a_r - m_loc)
            lhs += [v_t * w_r, jnp.broadcast_to(w_r, (16, LANES))]
            st[d * N_CHUNK + c, 0:1, :] = jnp.broadcast_to(m_loc, (1, LANES))
            st[d * N_CHUNK + c, 1:2, :] = jnp.broadcast_to(g, (1, LANES))
        both = _dot(jnp.concatenate(lhs, axis=0).astype(bf16), k)
        cl[c] = both[0:N_AUG, :]
        cl[N_CHUNK + c] = both[N_AUG:2 * N_AUG, :]
        return carry

    _chunk_loop(local_state)

    cst[...] = jnp.zeros(cst.shape, f32)
    mst[...] = jnp.zeros(mst.shape, f32)

    def scan_step(i, carry):
        for d in (0, 1):
            c = i if d == 0 else jnp.where(i < N_CTX_CHUNK, N_CTX_CHUNK - 1 - i, N_CHUNK + N_CTX_CHUNK - 1 - i)
            idx = d * N_CHUNK + c
            c_loc = cl[idx]
            m_loc = st[idx, 0:1, :]
            g = st[idx, 1:2, :]
            c_prev = cst[d]
            m_prev = mst[d, 0:1, :]
            m_new = jnp.maximum(g + m_prev, m_loc)
            dec = jnp.exp(g + m_prev - m_new)
            add = jnp.exp(m_loc - m_new)
            cl[idx] = c_prev
            st[idx, 2:3, :] = m_prev
            cst[d] = dec * c_prev + add * c_loc
            mst[d, 0:1, :] = m_new
        return carry

    lax.fori_loop(0, N_CHUNK, scan_step, 0)

    def outputs(c, o_ref, out_ref, at):
        lo = c * CHUNK if isinstance(c, int) else pl.multiple_of(c * CHUNK, CHUNK)
        k = ks[pl.ds(lo, CHUNK), :]
        q_t = qts[pl.ds(lo, CHUNK), :]
        v_aug = jnp.concatenate([vts[pl.ds(lo, CHUNK), :], jnp.ones((16, LANES), bf16)], axis=0)
        q_f = q_t.astype(f32)
        s_t = _dot(k, q_t)
        x_c = pltpu.roll(xc[pl.ds(lo, CHUNK), :], lane_shift, 1)
        hs = None
        for d in (0, 1):
            _, b_r = gate_rows(c, d)
            idx = d * N_CHUNK + c
            r_c = x_c[:, 8 * d:8 * d + 1] - x_c[:, 8 * d + 4:8 * d + 5]
            dm = jnp.where(upper if d == 0 else lower, b_r + r_c, -jnp.inf)
            e_r = b_r + st[idx, 2:3, :]
            m_t = jnp.maximum(e_r, jnp.max(dm, axis=0, keepdims=True))
            p_t = s_t * jnp.exp(dm - m_t)
            inter = jnp.exp(e_r - m_t)
            lhs = jnp.concatenate([v_aug, cl[idx].astype(bf16)], axis=1)
            rhs = jnp.concatenate([p_t, q_f * inter], axis=0).astype(bf16)
            nd = _dot(lhs, rhs)
            den = nd[CHUNK:CHUNK + 1, :]
            h_d = nd[0:CHUNK, :] * (1.0 / jnp.maximum(jnp.abs(den), jnp.exp(-m_t)))
            hs = h_d if hs is None else hs + h_d
        hn = hs * lax.rsqrt(jnp.mean(hs * hs, axis=0, keepdims=True) + EPS) * mn_ref[...]
        out_ref[pl.ds(at, CHUNK), :] = (o_ref[pl.ds(at, CHUNK), :].astype(f32) * hn.T).astype(bf16)

    def latent_outputs(i, carry):
        for u in range(LAT_CHUNKS_PER_ITER):
            j = i * LAT_CHUNKS_PER_ITER + u
            outputs(N_CTX_CHUNK + j, ol_ref, outl_ref, pl.multiple_of(j * CHUNK, CHUNK))
        return carry

    lax.fori_loop(0, (N_CHUNK - N_CTX_CHUNK) // LAT_CHUNKS_PER_ITER, latent_outputs, 0)
    for c in range(N_CTX_CHUNK):
        outputs(c, oc_ref, outc_ref, c * CHUNK)


def _mlstm(qk, vo, gates, conv_w, gate_b, mnorm):
    lat0 = R_CTX // SEQ
    ctx = lambda col0: pl.BlockSpec((CTX, LANES), lambda b, h: (b, col0 + h))
    lat = lambda col0: pl.BlockSpec((SEQ, LANES), lambda b, h: (lat0 + b, col0 + h))
    return pl.pallas_call(
        _mlstm_kernel,
        grid=(B, HEADS_A),
        in_specs=[
            ctx(0), lat(0), ctx(HEADS_A), lat(HEADS_A),
            ctx(0), lat(0), ctx(HEADS_A), lat(HEADS_A),
            pl.BlockSpec((CTX, LANES), lambda b, h: (b, 0)),
            pl.BlockSpec((SEQ, LANES), lambda b, h: (lat0 + b, 0)),
            pl.BlockSpec((3, LANES), lambda b, h: (0, h)),
            pl.BlockSpec((3, LANES), lambda b, h: (0, HEADS_A + h)),
            pl.BlockSpec((1, LANES), lambda b, h: (0, 0)),
            pl.BlockSpec((None, CHUNK, LANES), lambda b, h: (h, 0, 0)),
        ],
        out_specs=[
            pl.BlockSpec((CTX, LANES), lambda b, h: (b, h)),
            pl.BlockSpec((SEQ, LANES), lambda b, h: (b, h)),
        ],
        out_shape=[
            jax.ShapeDtypeStruct((R_CTX, W_A), bf16),
            jax.ShapeDtypeStruct((R_LAT, W_A), bf16),
        ],
        scratch_shapes=[
            pltpu.VMEM((TOK, LANES), bf16),
            pltpu.VMEM((TOK, LANES), bf16),
            pltpu.VMEM((TOK, LANES), bf16),
            pltpu.VMEM((TOK, LANES), f32),
            pltpu.VMEM((16 * N_CHUNK, LANES), f32),
            pltpu.VMEM((2 * N_CHUNK, CHUNK + 16, LANES), f32),
            pltpu.VMEM((2 * N_CHUNK, 8, LANES), f32),
            pltpu.VMEM((2, CHUNK + 16, LANES), f32),
            pltpu.VMEM((2, 8, LANES), f32),
        ],
        compiler_params=pltpu.CompilerParams(
            dimension_semantics=("parallel", "arbitrary"), vmem_limit_bytes=40 * MIB),
        name="mlstm",
    )(qk, qk, qk, qk, vo, vo, vo, vo, gates, gates, conv_w, conv_w, gate_b, mnorm)


def _odd_in_kernel(h_ref, mod_ref, w_ref, cos_ref, sin_ref, o_ref):
    n = _modulated(h_ref[...], mod_ref[3:4, :], mod_ref[4:5, :]).astype(bf16)
    cos = cos_ref[...]
    sin = sin_ref[...]
    n_rot = (HEADS_C + KV_HEADS) * DH // LANES
    y = _dot(n, w_ref[:, 0:n_rot * LANES])
    for c in range(0, n_rot, 2):
        x1 = y[:, c * LANES:(c + 1) * LANES]
        x2 = y[:, (c + 1) * LANES:(c + 2) * LANES]
        r1 = x1 * cos - x2 * sin
        r2 = x1 * sin + x2 * cos
        if c < HEADS_C * DH // LANES:
            r1 = r1 * (DH ** -0.5 * LOG2E)
            r2 = r2 * (DH ** -0.5 * LOG2E)
        o_ref[:, c * LANES:(c + 1) * LANES] = r1.astype(bf16)
        o_ref[:, (c + 1) * LANES:(c + 2) * LANES] = r2.astype(bf16)
    v0 = n_rot * LANES
    o_ref[:, v0:QKV] = _dot(n, w_ref[:, v0:QKV]).astype(bf16)


def _odd_in(h, mods, w, cos_t, sin_t):
    tm = TM_PROJ
    n_ctx = R_CTX // tm
    per_b = SEQ // tm
    rope_map = lambda i: (jnp.where(i < n_ctx, 0, 1 + jnp.maximum(i - n_ctx, 0) % per_b), 0)
    return pl.pallas_call(
        _odd_in_kernel,
        grid=(R_ALL // tm,),
        in_specs=[
            pl.BlockSpec((tm, D), lambda i: (i, 0)),
            pl.BlockSpec((None, N_MOD, D), lambda i: (_who_flat(i, tm), 0, 0)),
            pl.BlockSpec((D, QKV), lambda i: (0, 0), pipeline_mode=pl.Buffered(1)),
            pl.BlockSpec((tm, LANES), rope_map),
            pl.BlockSpec((tm, LANES), rope_map),
        ],
        out_specs=pl.BlockSpec((tm, QKV), lambda i: (i, 0)),
        out_shape=jax.ShapeDtypeStruct((R_ALL, QKV), bf16),
        compiler_params=pltpu.CompilerParams(
            dimension_semantics=("parallel",), vmem_limit_bytes=32 * MIB),
        name="odd_in",
    )(h, mods, w, cos_t, sin_t)


def _attn_kernel(sink_ref, q_ref, kc_ref, kp_ref, k0_ref, kn_ref, vc_ref, vp_ref, v0_ref, vn_ref, o_ref,
                 k_scr, vt_scr, q_scr, s_scr, p_scr, ot_scr):
    blk = pl.program_id(1)
    cols = GROUP * CHUNK
    kv_w = KV_HEADS * DH
    key = lax.broadcasted_iota(jnp.int32, (CHUNK, CHUNK), 0)
    qry = lax.broadcasted_iota(jnp.int32, (CHUNK, CHUNK), 1)
    far = 1 << 20
    prev_ok = key >= qry + jnp.where(blk > 0, 0, far)
    next_ok = key <= qry - jnp.where(blk < N_BLK - 1, 0, far)
    lane_head = lax.broadcasted_iota(jnp.int32, (CHUNK, kv_w), 1) % LANES // (DH // 2)
    neg = -1e30
    n_key = CTX + 3 * CHUNK
    row0 = 0
    for k_part, v_part in ((kc_ref, vc_ref), (kp_ref, vp_ref), (k0_ref, v0_ref), (kn_ref, vn_ref)):
        rows = k_part.shape[0]
        k_scr[row0:row0 + rows, :] = k_part[...]
        v_t = v_part[...].astype(f32).T.astype(bf16)
        for j in range(KV_HEADS):
            vt_scr[j, 0:DH, row0:row0 + rows] = v_t[j * DH:(j + 1) * DH, :]
        row0 += rows
    for j in range(KV_HEADS):
        vt_scr[j, DH:DH + 16, :] = jnp.ones((16, n_key), bf16)
    def score_dot(j):
        keep = jnp.where(lane_head == j, 1.0, 0.0).astype(bf16)
        for g in range(GROUP):
            q_scr[j, g * CHUNK:(g + 1) * CHUNK, :] = q_ref[:, g * kv_w:(g + 1) * kv_w] * keep
        s_scr[j] = _dot_nt(k_scr[...], q_scr[j])

    def scores(j, g, slab):
        s = s_scr[j, slab * CHUNK:(slab + 1) * CHUNK, g * CHUNK:(g + 1) * CHUNK]
        if slab == 2:
            s = jnp.where(prev_ok, s, neg)
        if slab == 4:
            s = jnp.where(next_ok, s, neg)
        return s

    n_slab = n_key // CHUNK
    for j in range(KV_HEADS):
        score_dot(j)
    for j in range(KV_HEADS):
        sink_terms = []
        for g in range(GROUP):
            sink = jnp.full((1, CHUNK), sink_ref[j * GROUP + g] * LOG2E, f32)
            m8 = None
            for slab in range(n_slab):
                part = jnp.max(scores(j, g, slab).reshape(CHUNK // 8, 8, CHUNK), axis=0)
                m8 = part if m8 is None else jnp.maximum(m8, part)
            m = jnp.maximum(sink, jnp.max(m8, axis=0, keepdims=True))
            for slab in range(n_slab):
                p = jnp.exp2(scores(j, g, slab) - m)
                p_scr[j, slab * CHUNK:(slab + 1) * CHUNK, g * CHUNK:(g + 1) * CHUNK] = p.astype(bf16)
            sink_terms.append(jnp.exp2(sink - m))
        acc = _dot(vt_scr[j], p_scr[j])
        den = acc[DH:DH + 1, :] + jnp.concatenate(sink_terms, axis=1)
        ot_scr[j * DH:(j + 1) * DH, :] = acc[0:DH, :] * (1.0 / den)
    for g in range(GROUP):
        o_ref[:, g * kv_w:(g + 1) * kv_w] = ot_scr[:, g * CHUNK:(g + 1) * CHUNK].T.astype(bf16)


def _attention(qkv, sink):
    kv_w = KV_HEADS * DH
    k_col = HEADS_C * DH // kv_w
    v_col = k_col + 1
    lat0 = R_CTX // CHUNK
    q_spec = pl.BlockSpec((CHUNK, HEADS_C * DH), lambda b, i: (lat0 + b * N_BLK + i, 0))

    def band(col, off):
        def index(b, i):
            return (lat0 + b * N_BLK + jnp.clip(i + off, 0, N_BLK - 1), col)
        return pl.BlockSpec((CHUNK, kv_w), index)

    ctx = lambda col: pl.BlockSpec((CTX, kv_w), lambda b, i: (b, col))
    return pl.pallas_call(
        _attn_kernel,
        grid=(B, N_BLK),
        in_specs=[
            pl.BlockSpec(memory_space=pltpu.SMEM),
            q_spec,
            ctx(k_col), band(k_col, -1), band(k_col, 0), band(k_col, 1),
            ctx(v_col), band(v_col, -1), band(v_col, 0), band(v_col, 1),
        ],
        out_specs=pl.BlockSpec((CHUNK, HEADS_C * DH), lambda b, i: (b * N_BLK + i, 0)),
        out_shape=jax.ShapeDtypeStruct((R_LAT, HEADS_C * DH), bf16),
        scratch_shapes=[
            pltpu.VMEM((CTX + 3 * CHUNK, KV_HEADS * DH), bf16),
            pltpu.VMEM((KV_HEADS, DH + 16, CTX + 3 * CHUNK), bf16),
            pltpu.VMEM((KV_HEADS, GROUP * CHUNK, KV_HEADS * DH), bf16),
            pltpu.VMEM((KV_HEADS, CTX + 3 * CHUNK, GROUP * CHUNK), f32),
            pltpu.VMEM((KV_HEADS, CTX + 3 * CHUNK, GROUP * CHUNK), bf16),
            pltpu.VMEM((KV_HEADS * DH, GROUP * CHUNK), f32),
        ],
        compiler_params=pltpu.CompilerParams(
            dimension_semantics=("parallel", "parallel"), vmem_limit_bytes=32 * MIB),
        name="window_attention",
    )(sink, qkv, qkv, qkv, qkv, qkv, qkv, qkv, qkv, qkv)


def _rope_tables():
    rows = SEQ // GRID_W
    row, col = jnp.meshgrid(jnp.arange(rows), jnp.arange(GRID_W), indexing='ij')
    n_freq = DH // 4
    inv = ROPE_BASE ** (-jnp.arange(n_freq, dtype=f32) / n_freq)
    ang = jnp.concatenate([row.reshape(-1, 1).astype(f32) * inv,
                           col.reshape(-1, 1).astype(f32) * inv], axis=-1)
    reps = 2 * LANES // DH
    cos = jnp.tile(jnp.cos(ang), (1, reps))
    sin = jnp.tile(jnp.sin(ang), (1, reps))
    cos = jnp.concatenate([jnp.ones((TM_PROJ, LANES), f32), cos], axis=0)
    sin = jnp.concatenate([jnp.zeros((TM_PROJ, LANES), f32), sin], axis=0)
    return cos, sin


def kernel(x, c, ctx, c_ctx, ada_w, ada_b, ffn_w_in, ffn_w_out, even_w_in, even_w_out, mlstm_conv,
           mlstm_gate_b, mlstm_norm, sgu_norm, sgu_ws, sgu_b, odd_w_qkv, odd_w_out, attn_sink, final_norm):
    cs = jnp.concatenate([c_ctx[None, :], c, jnp.zeros((16 - 1 - B, D), f32)], axis=0)
    mods = _modulation(cs, ada_w, ada_b)[:, :1 + B, :].reshape(2, 1 + B, N_MOD, D)

    fw_in = ffn_w_in
    fw_out = ffn_w_out

    m0 = mods[0]
    h = _ffn((ctx.reshape(R_CTX, D), x.reshape(R_LAT, D)), m0, fw_in, fw_out, sel=(0, 0), mi=0)
    w_in = even_w_in[0]
    gate0 = 4 * W_A
    gate1 = gate0 + 4 * HEADS_A
    w_in = jnp.concatenate(
        [w_in[:, :gate0], w_in[:, gate1:], w_in[:, gate0:gate1],
         jnp.zeros((D, LANES - 4 * HEADS_A), f32)], axis=1).astype(bf16)
    qk, vo, uv, gates = _even_in(h, m0, w_in)
    gate_b = jnp.pad(mlstm_gate_b[0].reshape(1, 4 * HEADS_A), ((0, 0), (0, LANES - 4 * HEADS_A)))
    mnorm_t = jnp.broadcast_to(mlstm_norm[0][:, :, None], (HEADS_A, CHUNK, LANES))
    ha_ctx, ha_lat = _mlstm(qk, vo, gates, mlstm_conv[0], gate_b, mnorm_t)
    sbx = jnp.repeat(sgu_b[0].T, LANES, axis=1)
    h = _ffn(h, m0, fw_in, fw_out, sel=(0, 1), mi=6,
             even=(ha_ctx, ha_lat, uv, sgu_norm[0].reshape(1, W_A), sgu_ws[0].astype(bf16), sbx,
                   even_w_out[0].astype(bf16)))

    m1 = mods[1]
    h = _ffn(h, m1, fw_in, fw_out, sel=(1, 0), mi=0)
    cos_t, sin_t = _rope_tables()
    qdim = HEADS_C * DH
    kdim = KV_HEADS * DH
    w_q = odd_w_qkv[0][:, :qdim].reshape(D, KV_HEADS, GROUP, DH // 2, 2).transpose(0, 2, 4, 1, 3).reshape(D, qdim)
    w_k = odd_w_qkv[0][:, qdim:qdim + kdim].reshape(D, KV_HEADS, DH // 2, 2).transpose(0, 3, 1, 2).reshape(D, kdim)
    w_qkv = jnp.concatenate([w_q, w_k, odd_w_qkv[0][:, qdim + kdim:]], axis=1).astype(bf16)
    w_o = odd_w_out[0].reshape(KV_HEADS, GROUP, DH, D).transpose(1, 0, 2, 3).reshape(qdim, D).astype(bf16)
    qkv = _odd_in(h, m1, w_qkv, cos_t, sin_t)
    attn = _attention(qkv, attn_sink[0])
    out = _ffn(h, m1, fw_in, fw_out, sel=(1, 1), mi=6, last=(attn, w_o, final_norm))
    return out.reshape(B, SEQ, D)
```

```python
import functools

import jax
import jax.numpy as jnp
from jax import lax
from jax.experimental import pallas as pl
from jax.experimental.pallas import tpu as pltpu

f32 = jnp.float32
bf16 = jnp.bfloat16

D = 1024
B = 8
SEQ = 2048
CTX = 256
TOK = CTX + SEQ
GRID_W = 64
N_MOD = 9
D_FF = 2816
EPS = 1e-6
HEADS_A = 4
CHUNK = 128
N_CHUNK = TOK // CHUNK
N_CTX_CHUNK = CTX // CHUNK
W_A = 512
EVEN_COLS = 3200
HEADS_C = 16
KV_HEADS = 4
GROUP = HEADS_C // KV_HEADS
DH = 64
QKV = (HEADS_C + 2 * KV_HEADS) * DH
N_BLK = SEQ // CHUNK
ROPE_BASE = 10000.0
LOG2E = 1.4426950408889634

R_CTX = B * CTX
R_LAT = B * SEQ
R_ALL = R_CTX + R_LAT

LANES = 128
TM_FFN = 512
TM_PROJ = 512
FC = 256
N_FC = D_FF // FC
MIB = 1024 * 1024


def _dot(a, b):
    return jnp.dot(a, b, preferred_element_type=f32)


def _dot_nt(a, b):
    return lax.dot_general(a, b, (((1,), (1,)), ((), ())), preferred_element_type=f32)


def _dot_tn(a, b):
    return lax.dot_general(a, b, (((0,), (0,)), ((), ())), preferred_element_type=f32)


def _sigmoid(x):
    return 1.0 / (1.0 + jnp.exp(-x))


def _split3(x):
    hi = x.astype(bf16)
    r1 = x - hi.astype(f32)
    mid = r1.astype(bf16)
    lo = (r1 - mid.astype(f32)).astype(bf16)
    return hi, mid, lo


def _modulated(h, shift, scale):
    ms = jnp.mean(h * h, axis=-1, keepdims=True)
    return h * lax.rsqrt(ms + EPS) * (1.0 + scale) + shift


def _mod_kernel(c_ref, w_ref, b_ref, o_ref):
    x = c_ref[...]
    s = x * _sigmoid(x)
    w = w_ref[...]
    s_hi = s.astype(bf16)
    s_lo = (s - s_hi.astype(f32)).astype(bf16)
    w_hi = w.astype(bf16)
    w_lo = (w - w_hi.astype(f32)).astype(bf16)
    o_ref[...] = _dot(s_hi, w_hi) + _dot(s_hi, w_lo) + _dot(s_lo, w_hi) + b_ref[...]


def _modulation(cs, ada_w, ada_b):
    depth = ada_w.shape[0]
    rows = cs.shape[0]
    n_col = N_MOD * D
    tn = 1024
    return pl.pallas_call(
        _mod_kernel,
        grid=(depth, n_col // tn),
        in_specs=[
            pl.BlockSpec((rows, D), lambda l, j: (0, 0)),
            pl.BlockSpec((None, D, tn), lambda l, j: (l, 0, j)),
            pl.BlockSpec((None, 1, tn), lambda l, j: (l, 0, j)),
        ],
        out_specs=pl.BlockSpec((None, rows, tn), lambda l, j: (l, 0, j)),
        out_shape=jax.ShapeDtypeStruct((depth, rows, n_col), f32),
        compiler_params=pltpu.CompilerParams(
            dimension_semantics=("parallel", "parallel"), vmem_limit_bytes=32 * MIB),
        name="modulation",
    )(cs, ada_w, ada_b.reshape(depth, 1, n_col))


def _who_flat(tile, tm):
    n_ctx = R_CTX // tm
    per_b = SEQ // tm
    return jnp.where(tile < n_ctx, 0, 1 + jnp.maximum(tile - n_ctx, 0) // per_b)


W_CHUNKS = 16
W_SLOTS = 4


def _fetch_cast(src, dst, stage, sem):
    rows = dst.shape[0] // W_CHUNKS

    def piece(c):
        slot = c % W_SLOTS
        return pltpu.make_async_copy(src.at[pl.ds(c * rows, rows), :], stage.at[slot], sem.at[slot])

    for c in range(W_SLOTS - 1):
        piece(c).start()
    for c in range(W_CHUNKS):
        if c + W_SLOTS - 1 < W_CHUNKS:
            piece(c + W_SLOTS - 1).start()
        piece(c).wait()
        dst[c * rows:(c + 1) * rows, :] = stage[c % W_SLOTS].astype(bf16)


def _gelu_tanh(x):
    return x * (0.5 * (1.0 + jnp.tanh(0.7978845608028654 * (x + 0.044715 * (x * x * x)))))


def _even_mix(ha, uv_ref, sg_ref, ws_ref, sb_ref, wm_ref, hb_scr):
    u = uv_ref[:, 0:W_A].astype(f32)
    v = uv_ref[:, W_A:2 * W_A].astype(f32)
    vn = (v * lax.rsqrt(jnp.mean(v * v, axis=-1, keepdims=True) + EPS) * sg_ref[...]).astype(bf16)
    n_chunk = TM_FFN // CHUNK
    for g in range(W_A // LANES):
        cs = slice(g * LANES, (g + 1) * LANES)
        rhs = jnp.concatenate([vn[n * CHUNK:(n + 1) * CHUNK, cs] for n in range(n_chunk)], axis=1)
        mixed = _dot(ws_ref[g], rhs)
        for n in range(n_chunk):
            r = slice(n * CHUNK, (n + 1) * CHUNK)
            hb_scr[r, cs] = (u[r, cs] * (mixed[:, n * LANES:(n + 1) * LANES] + sb_ref[:, cs])).astype(bf16)
    return _dot(ha, wm_ref[0:W_A, :]) + _dot(hb_scr[...], wm_ref[W_A:2 * W_A, :])


def _ffn_kernel(*refs, mi, mixer, final, split, sel):
    refs = list(refs)
    is_ctx = pl.program_id(0) < R_CTX // TM_FFN
    if split:
        c_ref, x_ref = refs[0:2]
        refs = refs[2:]
        read_h = lambda: jnp.where(is_ctx, c_ref[...], x_ref[...])
    else:
        h_ref = refs.pop(0)
        read_h = lambda: h_ref[...]
    wi_ref, wo_ref, wi_stage, wo_stage, wi_sem, wo_sem = refs[-6:]
    refs = refs[:-6]
    if mixer == "attn":
        a_ref, wm_ref = refs[0:2]
        refs = refs[2:]
    elif mixer == "even":
        hac_ref, hax_ref, uv_ref, sg_ref, ws_ref, sb_ref, wm_ref = refs[0:7]
        refs = refs[7:]
    mod_ref, wi_hbm, wo_hbm = refs[0:3]
    refs = refs[3:]
    if final:
        fn_ref = refs.pop(0)
    o_ref, n_scr, acc_scr = refs[0:3]
    refs = refs[3:]

    @pl.when(pl.program_id(0) == 0)
    def _():
        _fetch_cast(wi_hbm.at[sel[0], sel[1]], wi_ref, wi_stage, wi_sem)
        _fetch_cast(wo_hbm.at[sel[0], sel[1]], wo_ref, wo_stage, wo_sem)

    if mixer is not None:
        h_scr = refs.pop(0)
        if mixer == "attn":
            y = _dot(a_ref[...], wm_ref[...])
        else:
            ha = jnp.where(is_ctx, hac_ref[...], hax_ref[...])
            y = _even_mix(ha, uv_ref, sg_ref, ws_ref, sb_ref, wm_ref, refs.pop(0))
        h_scr[...] = read_h() + mod_ref[5:6, :] * y
        read_h = lambda: h_scr[...]
    n_scr[...] = _modulated(read_h(), mod_ref[mi:mi + 1, :], mod_ref[mi + 1:mi + 2, :]).astype(bf16)
    for j in range(N_FC):
        n = n_scr[...]
        g = _dot(n, wi_ref[:, j * FC:(j + 1) * FC])
        u = _dot(n, wi_ref[:, D_FF + j * FC:D_FF + (j + 1) * FC])
        a = (g * _sigmoid(g) * u).astype(bf16)
        y = _dot(a, wo_ref[j * FC:(j + 1) * FC, :])
        if j == 0:
            acc_scr[...] = y
        else:
            acc_scr[...] += y
    out = read_h() + (0.5 * mod_ref[mi + 2:mi + 3, :]) * acc_scr[...]
    if final:
        ms = jnp.mean(out * out, axis=-1, keepdims=True)
        out = out * lax.rsqrt(ms + EPS) * fn_ref[...]
    o_ref[...] = out


def _ffn(h, mods, w_in, w_out, *, sel, mi, even=None, last=None):
    tm = TM_FFN
    n_ctx = R_CTX // tm
    tile0 = n_ctx if last is not None else 0
    split = isinstance(h, tuple)
    const2 = lambda i: (0, 0)
    ctx_map = lambda i: (jnp.minimum(i, n_ctx - 1), 0)
    lat_map = lambda i: (jnp.maximum(i - n_ctx, 0), 0)
    if split:
        rows_out = R_ALL
        in_specs = [pl.BlockSpec((tm, D), ctx_map), pl.BlockSpec((tm, D), lat_map)]
        args = list(h)
    else:
        rows_out = h.shape[0] - tile0 * tm
        in_specs = [pl.BlockSpec((tm, D), lambda i: (i + tile0, 0))]
        args = [h]
    scratch = [pltpu.VMEM((tm, D), bf16), pltpu.VMEM((tm, D), f32)]
    mixer = None
    if last is not None:
        mixer = "attn"
        attn, w_attn, final_norm = last
        in_specs += [
            pl.BlockSpec((tm, D), lambda i: (i, 0)),
            pl.BlockSpec((D, D), const2, pipeline_mode=pl.Buffered(1)),
        ]
        args += [attn, w_attn]
        scratch.append(pltpu.VMEM((tm, D), f32))
    elif even is not None:
        mixer = "even"
        in_specs += [
            pl.BlockSpec((tm, W_A), ctx_map),
            pl.BlockSpec((tm, W_A), lat_map),
            pl.BlockSpec((tm, 2 * W_A), lambda i: (i, 0)),
            pl.BlockSpec((1, W_A), const2),
            pl.BlockSpec((W_A // LANES, CHUNK, CHUNK), lambda i: (0, 0, 0)),
            pl.BlockSpec((CHUNK, W_A), const2),
            pl.BlockSpec((2 * W_A, D), const2, pipeline_mode=pl.Buffered(1)),
        ]
        args += list(even)
        scratch += [pltpu.VMEM((tm, D), f32), pltpu.VMEM((tm, W_A), bf16)]
    in_specs += [
        pl.BlockSpec((None, N_MOD, D), lambda i: (_who_flat(i + tile0, tm), 0, 0)),
        pl.BlockSpec(memory_space=pl.ANY),
        pl.BlockSpec(memory_space=pl.ANY),
    ]
    args += [mods, w_in, w_out]
    if last is not None:
        in_specs.append(pl.BlockSpec((1, D), const2))
        args.append(final_norm.reshape(1, D))
    scratch += [
        pltpu.VMEM((D, 2 * D_FF), bf16),
        pltpu.VMEM((D_FF, D), bf16),
        pltpu.VMEM((W_SLOTS, D // W_CHUNKS, 2 * D_FF), f32),
        pltpu.VMEM((W_SLOTS, D_FF // W_CHUNKS, D), f32),
        pltpu.SemaphoreType.DMA((W_SLOTS,)),
        pltpu.SemaphoreType.DMA((W_SLOTS,)),
    ]
    return pl.pallas_call(
        functools.partial(_ffn_kernel, mi=mi, mixer=mixer, final=last is not None, split=split, sel=sel),
        grid=(rows_out // tm,),
        in_specs=in_specs,
        out_specs=pl.BlockSpec((tm, D), lambda i: (i, 0)),
        out_shape=jax.ShapeDtypeStruct((rows_out, D), f32),
        scratch_shapes=scratch,
        compiler_params=pltpu.CompilerParams(
            dimension_semantics=("arbitrary",), vmem_limit_bytes=56 * MIB),
        name={None: "ffn", "even": "ffn_even", "attn": "ffn_final"}[mixer],
    )(*args)


def _even_in_kernel(h_ref, mod_ref, w_ref, qk_ref, vo_ref, uv_ref, g_ref):
    n = _modulated(h_ref[...], mod_ref[3:4, :], mod_ref[4:5, :]).astype(bf16)
    qk_ref[...] = _dot(n, w_ref[:, 0:1024])
    vo_ref[:, 0:W_A] = _dot(n, w_ref[:, 1024:1024 + W_A]).astype(bf16)
    vo_ref[:, W_A:2 * W_A] = _sigmoid(_dot(n, w_ref[:, 1024 + W_A:2048])).astype(bf16)
    uv_ref[...] = _gelu_tanh(_dot(n, w_ref[:, 2048:3072])).astype(bf16)
    g_ref[...] = _dot(n, w_ref[:, 3072:3200])


def _even_in(h, mods, w):
    tm = TM_PROJ
    out_map = lambda i: (i, 0)
    return pl.pallas_call(
        _even_in_kernel,
        grid=(R_ALL // tm,),
        in_specs=[
            pl.BlockSpec((tm, D), lambda i: (i, 0)),
            pl.BlockSpec((None, N_MOD, D), lambda i: (_who_flat(i, tm), 0, 0)),
            pl.BlockSpec((D, EVEN_COLS), lambda i: (0, 0), pipeline_mode=pl.Buffered(1)),
        ],
        out_specs=[
            pl.BlockSpec((tm, 1024), out_map),
            pl.BlockSpec((tm, 1024), out_map),
            pl.BlockSpec((tm, 1024), out_map),
            pl.BlockSpec((tm, LANES), out_map),
        ],
        out_shape=[
            jax.ShapeDtypeStruct((R_ALL, 1024), f32),
            jax.ShapeDtypeStruct((R_ALL, 1024), bf16),
            jax.ShapeDtypeStruct((R_ALL, 1024), bf16),
            jax.ShapeDtypeStruct((R_ALL, LANES), f32),
        ],
        compiler_params=pltpu.CompilerParams(
            dimension_semantics=("parallel",), vmem_limit_bytes=40 * MIB),
        name="even_in",
    )(h, mods, w)


N_AUG = CHUNK + 16
CHUNKS_PER_ITER = 9


def _chunk_loop(body):
    def group(i, carry):
        for u in range(CHUNKS_PER_ITER):
            carry = body(i * CHUNKS_PER_ITER + u, carry)
        return carry
    lax.fori_loop(0, N_CHUNK // CHUNKS_PER_ITER, group, 0)


def _mlstm_kernel(qc_ref, ql_ref, kc_ref, kl_ref, vc_ref, vl_ref, oc_ref, ol_ref, gc_ref, gl_ref,
                  cq_ref, ck_ref, gb_ref, mn_ref, outc_ref, outl_ref,
                  ks, qts, vts, og, hg, xc, xr, cl, st, cst, mst):
    head = pl.program_id(1)
    rowi = lax.broadcasted_iota(jnp.int32, (CHUNK, CHUNK), 0)
    coli = lax.broadcasted_iota(jnp.int32, (CHUNK, CHUNK), 1)
    lower = coli <= rowi
    upper = coli >= rowi
    tri = jnp.where(lower, 1.0, 0.0).astype(bf16)

    def part(ctx_ref, lat_ref, c):
        return (ctx_ref, c * CHUNK) if c < N_CTX_CHUNK else (lat_ref, (c - N_CTX_CHUNK) * CHUNK)

    for c in range(N_CHUNK):
        lo = c * CHUNK
        for refs, cw, is_q in (((qc_ref, ql_ref), cq_ref, True), ((kc_ref, kl_ref), ck_ref, False)):
            src, at = part(*refs, c)
            x = src[at:at + CHUNK, :]
            first = jnp.zeros((1, LANES), f32) if at == 0 else src[at - 1:at, :]
            last = jnp.zeros((1, LANES), f32) if at + CHUNK == src.shape[0] else src[at + CHUNK:at + CHUNK + 1, :]
            prev = jnp.where(rowi == 0, first, pltpu.roll(x, 1, 0))
            nxt = jnp.where(rowi == CHUNK - 1, last, pltpu.roll(x, CHUNK - 1, 0))
            y = cw[0:1, :] * prev + cw[1:2, :] * x + cw[2:3, :] * nxt
            y = y * _sigmoid(y)
            if is_q:
                qts[lo:lo + CHUNK, :] = y.T.astype(bf16)
            else:
                ks[lo:lo + CHUNK, :] = (y * CHUNK ** -0.5).astype(bf16)
        src, at = part(vc_ref, vl_ref, c)
        vts[lo:lo + CHUNK, :] = src[at:at + CHUNK, :].astype(f32).T.astype(bf16)
        src, at = part(oc_ref, ol_ref, c)
        og[lo:lo + CHUNK, :] = src[at:at + CHUNK, :]

    @pl.when(head == 0)
    def _():
        kind = (coli // HEADS_A) % 4
        for c in range(N_CHUNK):
            lo = c * CHUNK
            src, at = part(gc_ref, gl_ref, c)
            gt = src[at:at + CHUNK, :] + gb_ref[...]
            lf = jnp.minimum(gt, 0.0) - jnp.log1p(jnp.exp(-jnp.abs(gt)))
            hi, mid, lw = _split3(lf)
            pre = _dot(tri, hi) + _dot(tri, mid) + _dot(tri, lw)
            suf = pre[CHUNK - 1:CHUNK, :] - pre + lf
            x = jnp.where(kind == 1, pre, jnp.where(kind == 3, suf, gt))
            xr[16 * c:16 * c + 16, :] = x.T[0:16, :]
            xc[lo:lo + CHUNK, :] = x

    lane_shift = (LANES - head) % LANES

    def gate_rows(c, d):
        row = c * 16 + 8 * d + head
        return xr[pl.ds(row, 1), :], xr[pl.ds(row + HEADS_A, 1), :]

    def local_state(c, carry):
        lo = pl.multiple_of(c * CHUNK, CHUNK)
        k = ks[pl.ds(lo, CHUNK), :]
        v_t = vts[pl.ds(lo, CHUNK), :].astype(f32)
        lhs = []
        for d in (0, 1):
            ig_r, b_r = gate_rows(c, d)
            g = b_r[:, CHUNK - 1:CHUNK] if d == 0 else b_r[:, 0:1]
            a_r = g - b_r + ig_r
            m_loc = jnp.max(a_r, axis=1, keepdims=True)
            w_r = jnp.exp(a_r - m_loc)
            lhs += [v_t * w_r, jnp.broadcast_to(w_r, (16, LANES))]
            st[d * N_CHUNK + c, 0:1, :] = jnp.broadcast_to(m_loc, (1, LANES))
            st[d * N_CHUNK + c, 1:2, :] = jnp.broadcast_to(g, (1, LANES))
        both = _dot(jnp.concatenate(lhs, axis=0).astype(bf16), k)
        cl[c] = both[0:N_AUG, :]
        cl[N_CHUNK + c] = both[N_AUG:2 * N_AUG, :]
        return carry

    _chunk_loop(local_state)

    cst[...] = jnp.zeros(cst.shape, f32)
    mst[...] = jnp.zeros(mst.shape, f32)

    def scan_step(i, carry):
        for d in (0, 1):
            c = i if d == 0 else jnp.where(i < N_CTX_CHUNK, N_CTX_CHUNK - 1 - i, N_CHUNK + N_CTX_CHUNK - 1 - i)
            idx = d * N_CHUNK + c
            c_loc = cl[idx]
            m_loc = st[idx, 0:1, :]
            g = st[idx, 1:2, :]
            c_prev = cst[d]
            m_prev = mst[d, 0:1, :]
            m_new = jnp.maximum(g + m_prev, m_loc)
            dec = jnp.exp(g + m_prev - m_new)
            add = jnp.exp(m_loc - m_new)
            cl[idx] = c_prev
            st[idx, 2:3, :] = m_prev
            cst[d] = dec * c_prev + add * c_loc
            mst[d, 0:1, :] = m_new
        return carry

    lax.fori_loop(0, N_CHUNK, scan_step, 0)

    def outputs(c, carry):
        lo = pl.multiple_of(c * CHUNK, CHUNK)
        k = ks[pl.ds(lo, CHUNK), :]
        q_t = qts[pl.ds(lo, CHUNK), :]
        v_aug = jnp.concatenate([vts[pl.ds(lo, CHUNK), :], jnp.ones((16, LANES), bf16)], axis=0)
        q_f = q_t.astype(f32)
        s_t = _dot(k, q_t)
        x_c = pltpu.roll(xc[pl.ds(lo, CHUNK), :], lane_shift, 1)
        hs = None
        for d in (0, 1):
            _, b_r = gate_rows(c, d)
            idx = d * N_CHUNK + c
            r_c = x_c[:, 8 * d:8 * d + 1] - x_c[:, 8 * d + 4:8 * d + 5]
            dm = jnp.where(upper if d == 0 else lower, b_r + r_c, -jnp.inf)
            e_r = b_r + st[idx, 2:3, :]
            m_t = jnp.maximum(e_r, jnp.max(dm, axis=0, keepdims=True))
            p_t = s_t * jnp.exp(dm - m_t)
            inter = jnp.exp(e_r - m_t)
            lhs = jnp.concatenate([v_aug, cl[idx].astype(bf16)], axis=1)
            rhs = jnp.concatenate([p_t, q_f * inter], axis=0).astype(bf16)
            nd = _dot(lhs, rhs)
            den = nd[CHUNK:CHUNK + 1, :]
            h_d = nd[0:CHUNK, :] * (1.0 / jnp.maximum(jnp.abs(den), jnp.exp(-m_t)))
            hs = h_d if hs is None else hs + h_d
        hn = hs * lax.rsqrt(jnp.mean(hs * hs, axis=0, keepdims=True) + EPS) * mn_ref[...]
        hg[pl.ds(lo, CHUNK), :] = (og[pl.ds(lo, CHUNK), :].astype(f32) * hn.T).astype(bf16)
        return carry

    _chunk_loop(outputs)
    outc_ref[...] = hg[0:CTX, :]
    outl_ref[...] = hg[CTX:TOK, :]


def _mlstm(qk, vo, gates, conv_w, gate_b, mnorm):
    lat0 = R_CTX // SEQ
    ctx = lambda col0: pl.BlockSpec((CTX, LANES), lambda b, h: (b, col0 + h))
    lat = lambda col0: pl.BlockSpec((SEQ, LANES), lambda b, h: (lat0 + b, col0 + h))
    return pl.pallas_call(
        _mlstm_kernel,
        grid=(B, HEADS_A),
        in_specs=[
            ctx(0), lat(0), ctx(HEADS_A), lat(HEADS_A),
            ctx(0), lat(0), ctx(HEADS_A), lat(HEADS_A),
            pl.BlockSpec((CTX, LANES), lambda b, h: (b, 0)),
            pl.BlockSpec((SEQ, LANES), lambda b, h: (lat0 + b, 0)),
            pl.BlockSpec((3, LANES), lambda b, h: (0, h)),
            pl.BlockSpec((3, LANES), lambda b, h: (0, HEADS_A + h)),
            pl.BlockSpec((1, LANES), lambda b, h: (0, 0)),
            pl.BlockSpec((None, CHUNK, LANES), lambda b, h: (h, 0, 0)),
        ],
        out_specs=[
            pl.BlockSpec((CTX, LANES), lambda b, h: (b, h)),
            pl.BlockSpec((SEQ, LANES), lambda b, h: (b, h)),
        ],
        out_shape=[
            jax.ShapeDtypeStruct((R_CTX, W_A), bf16),
            jax.ShapeDtypeStruct((R_LAT, W_A), bf16),
        ],
        scratch_shapes=[
            pltpu.VMEM((TOK, LANES), bf16),
            pltpu.VMEM((TOK, LANES), bf16),
            pltpu.VMEM((TOK, LANES), bf16),
            pltpu.VMEM((TOK, LANES), bf16),
            pltpu.VMEM((TOK, LANES), bf16),
            pltpu.VMEM((TOK, LANES), f32),
            pltpu.VMEM((16 * N_CHUNK, LANES), f32),
            pltpu.VMEM((2 * N_CHUNK, CHUNK + 16, LANES), f32),
            pltpu.VMEM((2 * N_CHUNK, 8, LANES), f32),
            pltpu.VMEM((2, CHUNK + 16, LANES), f32),
            pltpu.VMEM((2, 8, LANES), f32),
        ],
        compiler_params=pltpu.CompilerParams(
            dimension_semantics=("parallel", "arbitrary"), vmem_limit_bytes=40 * MIB),
        name="mlstm",
    )(qk, qk, qk, qk, vo, vo, vo, vo, gates, gates, conv_w, conv_w, gate_b, mnorm)


def _odd_in_kernel(h_ref, mod_ref, w_ref, cos_ref, sin_ref, o_ref):
    n = _modulated(h_ref[...], mod_ref[3:4, :], mod_ref[4:5, :]).astype(bf16)
    cos = cos_ref[...]
    sin = sin_ref[...]
    n_rot = (HEADS_C + KV_HEADS) * DH // LANES
    y = _dot(n, w_ref[:, 0:n_rot * LANES])
    for c in range(0, n_rot, 2):
        x1 = y[:, c * LANES:(c + 1) * LANES]
        x2 = y[:, (c + 1) * LANES:(c + 2) * LANES]
        r1 = x1 * cos - x2 * sin
        r2 = x1 * sin + x2 * cos
        if c < HEADS_C * DH // LANES:
            r1 = r1 * (DH ** -0.5 * LOG2E)
            r2 = r2 * (DH ** -0.5 * LOG2E)
        o_ref[:, c * LANES:(c + 1) * LANES] = r1.astype(bf16)
        o_ref[:, (c + 1) * LANES:(c + 2) * LANES] = r2.astype(bf16)
    v0 = n_rot * LANES
    o_ref[:, v0:QKV] = _dot(n, w_ref[:, v0:QKV]).astype(bf16)


def _odd_in(h, mods, w, cos_t, sin_t):
    tm = TM_PROJ
    n_ctx = R_CTX // tm
    per_b = SEQ // tm
    rope_map = lambda i: (jnp.where(i < n_ctx, 0, 1 + jnp.maximum(i - n_ctx, 0) % per_b), 0)
    return pl.pallas_call(
        _odd_in_kernel,
        grid=(R_ALL // tm,),
        in_specs=[
            pl.BlockSpec((tm, D), lambda i: (i, 0)),
            pl.BlockSpec((None, N_MOD, D), lambda i: (_who_flat(i, tm), 0, 0)),
            pl.BlockSpec((D, QKV), lambda i: (0, 0), pipeline_mode=pl.Buffered(1)),
            pl.BlockSpec((tm, LANES), rope_map),
            pl.BlockSpec((tm, LANES), rope_map),
        ],
        out_specs=pl.BlockSpec((tm, QKV), lambda i: (i, 0)),
        out_shape=jax.ShapeDtypeStruct((R_ALL, QKV), bf16),
        compiler_params=pltpu.CompilerParams(
            dimension_semantics=("parallel",), vmem_limit_bytes=32 * MIB),
        name="odd_in",
    )(h, mods, w, cos_t, sin_t)


def _attn_kernel(sink_ref, q_ref, kc_ref, kl_ref, vc_ref, vl_ref, o_ref,
                 k_scr, vt_scr, q_scr, s_scr, p_scr, ot_scr):
    kv_w = KV_HEADS * DH
    key = lax.broadcasted_iota(jnp.int32, (CHUNK, CHUNK), 0)
    qry = lax.broadcasted_iota(jnp.int32, (CHUNK, CHUNK), 1)
    far = 1 << 20
    lane_head = lax.broadcasted_iota(jnp.int32, (CHUNK, kv_w), 1) % LANES // (DH // 2)
    neg = -1e30
    n_key = CTX + 3 * CHUNK
    n_slab = n_key // CHUNK

    def stage_keys(k_rows, v_rows, row0):
        rows = k_rows.shape[0]
        k_scr[row0:row0 + rows, :] = k_rows
        v_t = v_rows.astype(f32).T.astype(bf16)
        for j in range(KV_HEADS):
            vt_scr[j, 0:DH, row0:row0 + rows] = v_t[j * DH:(j + 1) * DH, :]

    stage_keys(kc_ref[...], vc_ref[...], 0)
    for j in range(KV_HEADS):
        vt_scr[j, DH:DH + 16, :] = jnp.ones((16, n_key), bf16)

    def block(blk, carry):
        q0 = pl.multiple_of(blk * CHUNK, CHUNK)
        prev_ok = key >= qry + jnp.where(blk > 0, 0, far)
        next_ok = key <= qry - jnp.where(blk < N_BLK - 1, 0, far)
        for n, off in enumerate((-1, 0, 1)):
            src = pl.multiple_of(jnp.clip(blk + off, 0, N_BLK - 1) * CHUNK, CHUNK)
            stage_keys(kl_ref[pl.ds(src, CHUNK), :], vl_ref[pl.ds(src, CHUNK), :], CTX + n * CHUNK)

        for j in range(KV_HEADS):
            keep = jnp.where(lane_head == j, 1.0, 0.0).astype(bf16)
            for g in range(GROUP):
                q_scr[j, g * CHUNK:(g + 1) * CHUNK, :] = q_ref[pl.ds(q0, CHUNK), g * kv_w:(g + 1) * kv_w] * keep
            s_scr[j] = _dot_nt(k_scr[...], q_scr[j])

        def scores(j, g, slab):
            s = s_scr[j, slab * CHUNK:(slab + 1) * CHUNK, g * CHUNK:(g + 1) * CHUNK]
            if slab == 2:
                s = jnp.where(prev_ok, s, neg)
            if slab == 4:
                s = jnp.where(next_ok, s, neg)
            return s

        for j in range(KV_HEADS):
            sink_terms = []
            for g in range(GROUP):
                sink = jnp.full((1, CHUNK), sink_ref[j * GROUP + g] * LOG2E, f32)
                m8 = None
                for slab in range(n_slab):
                    part = jnp.max(scores(j, g, slab).reshape(CHUNK // 8, 8, CHUNK), axis=0)
                    m8 = part if m8 is None else jnp.maximum(m8, part)
                m = jnp.maximum(sink, jnp.max(m8, axis=0, keepdims=True))
                for slab in range(n_slab):
                    p = jnp.exp2(scores(j, g, slab) - m)
                    p_scr[j, slab * CHUNK:(slab + 1) * CHUNK, g * CHUNK:(g + 1) * CHUNK] = p.astype(bf16)
                sink_terms.append(jnp.exp2(sink - m))
            acc = _dot(vt_scr[j], p_scr[j])
            den = acc[DH:DH + 1, :] + jnp.concatenate(sink_terms, axis=1)
            ot_scr[j * DH:(j + 1) * DH, :] = acc[0:DH, :] * (1.0 / den)
        for g in range(GROUP):
            o_ref[pl.ds(q0, CHUNK), g * kv_w:(g + 1) * kv_w] = (
                ot_scr[:, g * CHUNK:(g + 1) * CHUNK].T.astype(bf16))
        return carry

    lax.fori_loop(0, N_BLK, block, 0)


def _attention(qkv, sink):
    kv_w = KV_HEADS * DH
    k_col = HEADS_C * DH // kv_w
    v_col = k_col + 1
    lat0 = R_CTX // SEQ
    ctx = lambda col: pl.BlockSpec((CTX, kv_w), lambda b: (b, col))
    lat = lambda col: pl.BlockSpec((SEQ, kv_w), lambda b: (lat0 + b, col))
    return pl.pallas_call(
        _attn_kernel,
        grid=(B,),
        in_specs=[
            pl.BlockSpec(memory_space=pltpu.SMEM),
            pl.BlockSpec((SEQ, HEADS_C * DH), lambda b: (lat0 + b, 0)),
            ctx(k_col), lat(k_col), ctx(v_col), lat(v_col),
        ],
        out_specs=pl.BlockSpec((SEQ, HEADS_C * DH), lambda b: (b, 0)),
        out_shape=jax.ShapeDtypeStruct((R_LAT, HEADS_C * DH), bf16),
        scratch_shapes=[
            pltpu.VMEM((CTX + 3 * CHUNK, KV_HEADS * DH), bf16),
            pltpu.VMEM((KV_HEADS, DH + 16, CTX + 3 * CHUNK), bf16),
            pltpu.VMEM((KV_HEADS, GROUP * CHUNK, KV_HEADS * DH), bf16),
            pltpu.VMEM((KV_HEADS, CTX + 3 * CHUNK, GROUP * CHUNK), f32),
            pltpu.VMEM((KV_HEADS, CTX + 3 * CHUNK, GROUP * CHUNK), bf16),
            pltpu.VMEM((KV_HEADS * DH, GROUP * CHUNK), f32),
        ],
        compiler_params=pltpu.CompilerParams(
            dimension_semantics=("parallel",), vmem_limit_bytes=40 * MIB),
        name="window_attention",
    )(sink, qkv, qkv, qkv, qkv, qkv)


def _rope_tables():
    rows = SEQ // GRID_W
    row, col = jnp.meshgrid(jnp.arange(rows), jnp.arange(GRID_W), indexing='ij')
    n_freq = DH // 4
    inv = ROPE_BASE ** (-jnp.arange(n_freq, dtype=f32) / n_freq)
    ang = jnp.concatenate([row.reshape(-1, 1).astype(f32) * inv,
                           col.reshape(-1, 1).astype(f32) * inv], axis=-1)
    reps = 2 * LANES // DH
    cos = jnp.tile(jnp.cos(ang), (1, reps))
    sin = jnp.tile(jnp.sin(ang), (1, reps))
    cos = jnp.concatenate([jnp.ones((TM_PROJ, LANES), f32), cos], axis=0)
    sin = jnp.concatenate([jnp.zeros((TM_PROJ, LANES), f32), sin], axis=0)
    return cos, sin


def kernel(x, c, ctx, c_ctx, ada_w, ada_b, ffn_w_in, ffn_w_out, even_w_in, even_w_out, mlstm_conv,
           mlstm_gate_b, mlstm_norm, sgu_norm, sgu_ws, sgu_b, odd_w_qkv, odd_w_out, attn_sink, final_norm):
    cs = jnp.concatenate([c_ctx[None, :], c, jnp.zeros((16 - 1 - B, D), f32)], axis=0)
    mods = _modulation(cs, ada_w, ada_b)[:, :1 + B, :].reshape(2, 1 + B, N_MOD, D)

    fw_in = ffn_w_in
    fw_out = ffn_w_out

    m0 = mods[0]
    h = _ffn((ctx.reshape(R_CTX, D), x.reshape(R_LAT, D)), m0, fw_in, fw_out, sel=(0, 0), mi=0)
    w_in = even_w_in[0]
    gate0 = 4 * W_A
    gate1 = gate0 + 4 * HEADS_A
    w_in = jnp.concatenate(
        [w_in[:, :gate0], w_in[:, gate1:], w_in[:, gate0:gate1],
         jnp.zeros((D, LANES - 4 * HEADS_A), f32)], axis=1).astype(bf16)
    qk, vo, uv, gates = _even_in(h, m0, w_in)
    gate_b = jnp.pad(mlstm_gate_b[0].reshape(1, 4 * HEADS_A), ((0, 0), (0, LANES - 4 * HEADS_A)))
    mnorm_t = jnp.broadcast_to(mlstm_norm[0][:, :, None], (HEADS_A, CHUNK, LANES))
    ha_ctx, ha_lat = _mlstm(qk, vo, gates, mlstm_conv[0], gate_b, mnorm_t)
    sbx = jnp.repeat(sgu_b[0].T, LANES, axis=1)
    h = _ffn(h, m0, fw_in, fw_out, sel=(0, 1), mi=6,
             even=(ha_ctx, ha_lat, uv, sgu_norm[0].reshape(1, W_A), sgu_ws[0].astype(bf16), sbx,
                   even_w_out[0].astype(bf16)))

    m1 = mods[1]
    h = _ffn(h, m1, fw_in, fw_out, sel=(1, 0), mi=0)
    cos_t, sin_t = _rope_tables()
    qdim = HEADS_C * DH
    kdim = KV_HEADS * DH
    w_q = odd_w_qkv[0][:, :qdim].reshape(D, KV_HEADS, GROUP, DH // 2, 2).transpose(0, 2, 4, 1, 3).reshape(D, qdim)
    w_k = odd_w_qkv[0][:, qdim:qdim + kdim].reshape(D, KV_HEADS, DH // 2, 2).transpose(0, 3, 1, 2).reshape(D, kdim)
    w_qkv = jnp.concatenate([w_q, w_k, odd_w_qkv[0][:, qdim + kdim:]], axis=1).astype(bf16)
    w_o = odd_w_out[0].reshape(KV_HEADS, GROUP, DH, D).transpose(1, 0, 2, 3).reshape(qdim, D).astype(bf16)
    qkv = _odd_in(h, m1, w_qkv, cos_t, sin_t)
    attn = _attention(qkv, attn_sink[0])
    out = _ffn(h, m1, fw_in, fw_out, sel=(1, 1), mi=6, last=(attn, w_o, final_norm))
    return out.reshape(B, SEQ, D)
```

```python
import functools

import jax
import jax.numpy as jnp
from jax import lax
from jax.experimental import pallas as pl
from jax.experimental.pallas import tpu as pltpu

f32 = jnp.float32
bf16 = jnp.bfloat16

D = 1024
B = 8
SEQ = 2048
CTX = 256
TOK = CTX + SEQ
GRID_W = 64
N_MOD = 9
D_FF = 2816
EPS = 1e-6
HEADS_A = 4
CHUNK = 128
N_CHUNK = TOK // CHUNK
N_CTX_CHUNK = CTX // CHUNK
W_A = 512
EVEN_COLS = 3200
HEADS_C = 16
KV_HEADS = 4
GROUP = HEADS_C // KV_HEADS
DH = 64
QKV = (HEADS_C + 2 * KV_HEADS) * DH
N_BLK = SEQ // CHUNK
ROPE_BASE = 10000.0
LOG2E = 1.4426950408889634

R_CTX = B * CTX
R_LAT = B * SEQ
R_ALL = R_CTX + R_LAT

LANES = 128
TM_FFN = 512
TM_PROJ = 512
FC = 256
N_FC = D_FF // FC
MIB = 1024 * 1024


def _dot(a, b):
    return jnp.dot(a, b, preferred_element_type=f32)


def _dot_nt(a, b):
    return lax.dot_general(a, b, (((1,), (1,)), ((), ())), preferred_element_type=f32)


def _dot_tn(a, b):
    return lax.dot_general(a, b, (((0,), (0,)), ((), ())), preferred_element_type=f32)


def _sigmoid(x):
    return 1.0 / (1.0 + jnp.exp(-x))


def _split3(x):
    hi = x.astype(bf16)
    r1 = x - hi.astype(f32)
    mid = r1.astype(bf16)
    lo = (r1 - mid.astype(f32)).astype(bf16)
    return hi, mid, lo


def _modulated(h, shift, scale):
    ms = jnp.mean(h * h, axis=-1, keepdims=True)
    return h * lax.rsqrt(ms + EPS) * (1.0 + scale) + shift


def _mod_kernel(c_ref, w_ref, b_ref, o_ref):
    x = c_ref[...]
    s = x * _sigmoid(x)
    w = w_ref[...]
    s_hi = s.astype(bf16)
    s_lo = (s - s_hi.astype(f32)).astype(bf16)
    w_hi = w.astype(bf16)
    w_lo = (w - w_hi.astype(f32)).astype(bf16)
    o_ref[...] = _dot(s_hi, w_hi) + _dot(s_hi, w_lo) + _dot(s_lo, w_hi) + b_ref[...]


def _modulation(cs, ada_w, ada_b):
    depth = ada_w.shape[0]
    rows = cs.shape[0]
    n_col = N_MOD * D
    tn = 1024
    return pl.pallas_call(
        _mod_kernel,
        grid=(depth, n_col // tn),
        in_specs=[
            pl.BlockSpec((rows, D), lambda l, j: (0, 0)),
            pl.BlockSpec((None, D, tn), lambda l, j: (l, 0, j)),
            pl.BlockSpec((None, 1, tn), lambda l, j: (l, 0, j)),
        ],
        out_specs=pl.BlockSpec((None, rows, tn), lambda l, j: (l, 0, j)),
        out_shape=jax.ShapeDtypeStruct((depth, rows, n_col), f32),
        compiler_params=pltpu.CompilerParams(
            dimension_semantics=("parallel", "parallel"), vmem_limit_bytes=32 * MIB),
        name="modulation",
    )(cs, ada_w, ada_b.reshape(depth, 1, n_col))


def _who_flat(tile, tm):
    n_ctx = R_CTX // tm
    per_b = SEQ // tm
    return jnp.where(tile < n_ctx, 0, 1 + jnp.maximum(tile - n_ctx, 0) // per_b)


W_CHUNKS = 16
W_SLOTS = 4


def _fetch_cast(src, dst, stage, sem):
    rows = dst.shape[0] // W_CHUNKS

    def piece(c):
        slot = c % W_SLOTS
        return pltpu.make_async_copy(src.at[pl.ds(c * rows, rows), :], stage.at[slot], sem.at[slot])

    for c in range(W_SLOTS - 1):
        piece(c).start()
    for c in range(W_CHUNKS):
        if c + W_SLOTS - 1 < W_CHUNKS:
            piece(c + W_SLOTS - 1).start()
        piece(c).wait()
        dst[c * rows:(c + 1) * rows, :] = stage[c % W_SLOTS].astype(bf16)


def _gelu_tanh(x):
    return x * (0.5 * (1.0 + jnp.tanh(0.7978845608028654 * (x + 0.044715 * (x * x * x)))))


def _even_mix(ha, uv_ref, sg_ref, ws_ref, sb_ref, wm_ref, hb_scr):
    u = uv_ref[:, 0:W_A].astype(f32)
    v = uv_ref[:, W_A:2 * W_A].astype(f32)
    vn = (v * lax.rsqrt(jnp.mean(v * v, axis=-1, keepdims=True) + EPS) * sg_ref[...]).astype(bf16)
    n_chunk = TM_FFN // CHUNK
    for g in range(W_A // LANES):
        cs = slice(g * LANES, (g + 1) * LANES)
        rhs = jnp.concatenate([vn[n * CHUNK:(n + 1) * CHUNK, cs] for n in range(n_chunk)], axis=1)
        mixed = _dot(ws_ref[g], rhs)
        for n in range(n_chunk):
            r = slice(n * CHUNK, (n + 1) * CHUNK)
            hb_scr[r, cs] = (u[r, cs] * (mixed[:, n * LANES:(n + 1) * LANES] + sb_ref[:, cs])).astype(bf16)
    return _dot(ha, wm_ref[0:W_A, :]) + _dot(hb_scr[...], wm_ref[W_A:2 * W_A, :])


def _ffn_kernel(*refs, mi, mixer, final, split, sel):
    refs = list(refs)
    is_ctx = pl.program_id(0) < R_CTX // TM_FFN
    if split:
        c_ref, x_ref = refs[0:2]
        refs = refs[2:]
        read_h = lambda: jnp.where(is_ctx, c_ref[...], x_ref[...])
    else:
        h_ref = refs.pop(0)
        read_h = lambda: h_ref[...]
    wi_ref, wo_ref, wi_stage, wo_stage, wi_sem, wo_sem = refs[-6:]
    refs = refs[:-6]
    if mixer == "attn":
        a_ref, wm_ref = refs[0:2]
        refs = refs[2:]
    elif mixer == "even":
        hac_ref, hax_ref, uv_ref, sg_ref, ws_ref, sb_ref, wm_ref = refs[0:7]
        refs = refs[7:]
    mod_ref, wi_hbm, wo_hbm = refs[0:3]
    refs = refs[3:]
    if final:
        fn_ref = refs.pop(0)
    o_ref, n_scr, acc_scr = refs[0:3]
    refs = refs[3:]

    @pl.when(pl.program_id(0) == 0)
    def _():
        _fetch_cast(wi_hbm.at[sel[0], sel[1]], wi_ref, wi_stage, wi_sem)
        _fetch_cast(wo_hbm.at[sel[0], sel[1]], wo_ref, wo_stage, wo_sem)

    if mixer is not None:
        h_scr = refs.pop(0)
        if mixer == "attn":
            y = _dot(a_ref[...], wm_ref[...])
        else:
            ha = jnp.where(is_ctx, hac_ref[...], hax_ref[...])
            y = _even_mix(ha, uv_ref, sg_ref, ws_ref, sb_ref, wm_ref, refs.pop(0))
        h_scr[...] = read_h() + mod_ref[5:6, :] * y
        read_h = lambda: h_scr[...]
    n_scr[...] = _modulated(read_h(), mod_ref[mi:mi + 1, :], mod_ref[mi + 1:mi + 2, :]).astype(bf16)
    for j in range(N_FC):
        n = n_scr[...]
        g = _dot(n, wi_ref[:, j * FC:(j + 1) * FC])
        u = _dot(n, wi_ref[:, D_FF + j * FC:D_FF + (j + 1) * FC])
        a = (g * _sigmoid(g) * u).astype(bf16)
        y = _dot(a, wo_ref[j * FC:(j + 1) * FC, :])
        if j == 0:
            acc_scr[...] = y
        else:
            acc_scr[...] += y
    out = read_h() + (0.5 * mod_ref[mi + 2:mi + 3, :]) * acc_scr[...]
    if final:
        ms = jnp.mean(out * out, axis=-1, keepdims=True)
        out = out * lax.rsqrt(ms + EPS) * fn_ref[...]
    o_ref[...] = out


def _ffn(h, mods, w_in, w_out, *, sel, mi, even=None, last=None):
    tm = TM_FFN
    n_ctx = R_CTX // tm
    tile0 = n_ctx if last is not None else 0
    split = isinstance(h, tuple)
    const2 = lambda i: (0, 0)
    ctx_map = lambda i: (jnp.minimum(i, n_ctx - 1), 0)
    lat_map = lambda i: (jnp.maximum(i - n_ctx, 0), 0)
    if split:
        rows_out = R_ALL
        in_specs = [pl.BlockSpec((tm, D), ctx_map), pl.BlockSpec((tm, D), lat_map)]
        args = list(h)
    else:
        rows_out = h.shape[0] - tile0 * tm
        in_specs = [pl.BlockSpec((tm, D), lambda i: (i + tile0, 0))]
        args = [h]
    scratch = [pltpu.VMEM((tm, D), bf16), pltpu.VMEM((tm, D), f32)]
    mixer = None
    if last is not None:
        mixer = "attn"
        attn, w_attn, final_norm = last
        in_specs += [
            pl.BlockSpec((tm, D), lambda i: (i, 0)),
            pl.BlockSpec((D, D), const2, pipeline_mode=pl.Buffered(1)),
        ]
        args += [attn, w_attn]
        scratch.append(pltpu.VMEM((tm, D), f32))
    elif even is not None:
        mixer = "even"
        in_specs += [
            pl.BlockSpec((tm, W_A), ctx_map),
            pl.BlockSpec((tm, W_A), lat_map),
            pl.BlockSpec((tm, 2 * W_A), lambda i: (i, 0)),
            pl.BlockSpec((1, W_A), const2),
            pl.BlockSpec((W_A // LANES, CHUNK, CHUNK), lambda i: (0, 0, 0)),
            pl.BlockSpec((CHUNK, W_A), const2),
            pl.BlockSpec((2 * W_A, D), const2, pipeline_mode=pl.Buffered(1)),
        ]
        args += list(even)
        scratch += [pltpu.VMEM((tm, D), f32), pltpu.VMEM((tm, W_A), bf16)]
    in_specs += [
        pl.BlockSpec((None, N_MOD, D), lambda i: (_who_flat(i + tile0, tm), 0, 0)),
        pl.BlockSpec(memory_space=pl.ANY),
        pl.BlockSpec(memory_space=pl.ANY),
    ]
    args += [mods, w_in, w_out]
    if last is not None:
        in_specs.append(pl.BlockSpec((1, D), const2))
        args.append(final_norm.reshape(1, D))
    scratch += [
        pltpu.VMEM((D, 2 * D_FF), bf16),
        pltpu.VMEM((D_FF, D), bf16),
        pltpu.VMEM((W_SLOTS, D // W_CHUNKS, 2 * D_FF), f32),
        pltpu.VMEM((W_SLOTS, D_FF // W_CHUNKS, D), f32),
        pltpu.SemaphoreType.DMA((W_SLOTS,)),
        pltpu.SemaphoreType.DMA((W_SLOTS,)),
    ]
    return pl.pallas_call(
        functools.partial(_ffn_kernel, mi=mi, mixer=mixer, final=last is not None, split=split, sel=sel),
        grid=(rows_out // tm,),
        in_specs=in_specs,
        out_specs=pl.BlockSpec((tm, D), lambda i: (i, 0)),
        out_shape=jax.ShapeDtypeStruct((rows_out, D), f32),
        scratch_shapes=scratch,
        compiler_params=pltpu.CompilerParams(
            dimension_semantics=("arbitrary",), vmem_limit_bytes=56 * MIB),
        name={None: "ffn", "even": "ffn_even", "attn": "ffn_final"}[mixer],
    )(*args)


HALO = 8


def _even_in_kernel(h_ref, hp_ref, hn_ref, mod_ref, w_ref, cw_ref, qk_ref, vo_ref, uv_ref, g_ref):
    tile = pl.program_id(0)
    tm = TM_PROJ
    n_chunk = tm // CHUNK
    shift, scale = mod_ref[3:4, :], mod_ref[4:5, :]
    n = _modulated(h_ref[...], shift, scale)
    halo = _modulated(jnp.concatenate([hp_ref[...], hn_ref[...]], axis=0), shift, scale)
    n_ext = jnp.concatenate([halo[0:HALO], n, halo[HALO:2 * HALO]], axis=0).astype(bf16)
    n = n.astype(bf16)

    row = lax.broadcasted_iota(jnp.int32, (tm, LANES), 0)
    is_ctx = tile < R_CTX // tm
    pos = jnp.where(is_ctx, row % CTX, row + (jnp.maximum(tile - R_CTX // tm, 0) % (SEQ // tm)) * tm)
    seq_start = pos == 0
    seq_end = pos == jnp.where(is_ctx, CTX - 1, SEQ - 1)

    p = _dot(n_ext, w_ref[:, 0:2 * W_A])
    cur = p[HALO:HALO + tm]
    prv = pltpu.roll(p, 1, 0)[HALO:HALO + tm]
    nxt = pltpu.roll(p, tm + 2 * HALO - 1, 0)[HALO:HALO + tm]
    for cb in range(2 * HEADS_A):
        cs = slice(cb * LANES, (cb + 1) * LANES)
        y = (cw_ref[0:1, cs] * jnp.where(seq_start, 0.0, prv[:, cs]) + cw_ref[1:2, cs] * cur[:, cs]
             + cw_ref[2:3, cs] * jnp.where(seq_end, 0.0, nxt[:, cs]))
        y = y * _sigmoid(y)
        if cb < HEADS_A:
            for c in range(n_chunk):
                r = slice(c * CHUNK, (c + 1) * CHUNK)
                qk_ref[r, cs] = y[r, :].T.astype(bf16)
        else:
            qk_ref[:, cs] = (y * CHUNK ** -0.5).astype(bf16)

    v = _dot(n, w_ref[:, 2 * W_A:3 * W_A])
    for hd in range(HEADS_A):
        cs = slice(hd * LANES, (hd + 1) * LANES)
        for c in range(n_chunk):
            r = slice(c * CHUNK, (c + 1) * CHUNK)
            vo_ref[r, cs] = v[r, cs].T.astype(bf16)
    vo_ref[:, W_A:2 * W_A] = _sigmoid(_dot(n, w_ref[:, 3 * W_A:4 * W_A])).astype(bf16)
    uv_ref[...] = _gelu_tanh(_dot(n, w_ref[:, 2048:3072])).astype(bf16)
    g_ref[...] = _dot(n, w_ref[:, 3072:3200])


def _even_in(h, mods, w, conv_w):
    tm = TM_PROJ
    out_map = lambda i: (i, 0)
    halo_blocks = tm // HALO
    last_halo = R_ALL // HALO - 1
    return pl.pallas_call(
        _even_in_kernel,
        grid=(R_ALL // tm,),
        in_specs=[
            pl.BlockSpec((tm, D), lambda i: (i, 0)),
            pl.BlockSpec((HALO, D), lambda i: (jnp.maximum(i * halo_blocks - 1, 0), 0)),
            pl.BlockSpec((HALO, D), lambda i: (jnp.minimum((i + 1) * halo_blocks, last_halo), 0)),
            pl.BlockSpec((None, N_MOD, D), lambda i: (_who_flat(i, tm), 0, 0)),
            pl.BlockSpec((D, EVEN_COLS), lambda i: (0, 0), pipeline_mode=pl.Buffered(1)),
            pl.BlockSpec((3, 2 * W_A), lambda i: (0, 0)),
        ],
        out_specs=[
            pl.BlockSpec((tm, 1024), out_map),
            pl.BlockSpec((tm, 1024), out_map),
            pl.BlockSpec((tm, 1024), out_map),
            pl.BlockSpec((tm, LANES), out_map),
        ],
        out_shape=[
            jax.ShapeDtypeStruct((R_ALL, 1024), bf16),
            jax.ShapeDtypeStruct((R_ALL, 1024), bf16),
            jax.ShapeDtypeStruct((R_ALL, 1024), bf16),
            jax.ShapeDtypeStruct((R_ALL, LANES), f32),
        ],
        compiler_params=pltpu.CompilerParams(
            dimension_semantics=("parallel",), vmem_limit_bytes=40 * MIB),
        name="even_in",
    )(h, h, h, mods, w, conv_w)


N_AUG = CHUNK + 16
CHUNKS_PER_ITER = 9


def _chunk_loop(body):
    def group(i, carry):
        for u in range(CHUNKS_PER_ITER):
            carry = body(i * CHUNKS_PER_ITER + u, carry)
        return carry
    lax.fori_loop(0, N_CHUNK // CHUNKS_PER_ITER, group, 0)


def _mlstm_kernel(qc_ref, ql_ref, kc_ref, kl_ref, vc_ref, vl_ref, oc_ref, ol_ref, gc_ref, gl_ref,
                  gb_ref, mn_ref, outc_ref, outl_ref,
                  ks, qts, vts, og, hg, xc, xr, cl, st, cst, mst):
    head = pl.program_id(1)
    rowi = lax.broadcasted_iota(jnp.int32, (CHUNK, CHUNK), 0)
    coli = lax.broadcasted_iota(jnp.int32, (CHUNK, CHUNK), 1)
    lower = coli <= rowi
    upper = coli >= rowi
    tri = jnp.where(lower, 1.0, 0.0).astype(bf16)

    def part(ctx_ref, lat_ref, c):
        return (ctx_ref, c * CHUNK) if c < N_CTX_CHUNK else (lat_ref, (c - N_CTX_CHUNK) * CHUNK)

    for c in range(N_CHUNK):
        lo = c * CHUNK
        for dst, refs in ((qts, (qc_ref, ql_ref)), (ks, (kc_ref, kl_ref)), (vts, (vc_ref, vl_ref)),
                          (og, (oc_ref, ol_ref))):
            src, at = part(*refs, c)
            dst[lo:lo + CHUNK, :] = src[at:at + CHUNK, :]

    @pl.when(head == 0)
    def _():
        kind = (coli // HEADS_A) % 4
        for c in range(N_CHUNK):
            lo = c * CHUNK
            src, at = part(gc_ref, gl_ref, c)
            gt = src[at:at + CHUNK, :] + gb_ref[...]
            lf = jnp.minimum(gt, 0.0) - jnp.log1p(jnp.exp(-jnp.abs(gt)))
            hi, mid, lw = _split3(lf)
            pre = _dot(tri, hi) + _dot(tri, mid) + _dot(tri, lw)
            suf = pre[CHUNK - 1:CHUNK, :] - pre + lf
            x = jnp.where(kind == 1, pre, jnp.where(kind == 3, suf, gt))
            xr[16 * c:16 * c + 16, :] = x.T[0:16, :]
            xc[lo:lo + CHUNK, :] = x

    lane_shift = (LANES - head) % LANES

    def gate_rows(c, d):
        row = c * 16 + 8 * d + head
        return xr[pl.ds(row, 1), :], xr[pl.ds(row + HEADS_A, 1), :]

    def local_state(c, carry):
        lo = pl.multiple_of(c * CHUNK, CHUNK)
        k = ks[pl.ds(lo, CHUNK), :]
        v_t = vts[pl.ds(lo, CHUNK), :].astype(f32)
        lhs = []
        for d in (0, 1):
            ig_r, b_r = gate_rows(c, d)
            g = b_r[:, CHUNK - 1:CHUNK] if d == 0 else b_r[:, 0:1]
            a_r = g - b_r + ig_r
            m_loc = jnp.max(a_r, axis=1, keepdims=True)
            w_r = jnp.exp(a_r - m_loc)
            lhs += [v_t * w_r, jnp.broadcast_to(w_r, (16, LANES))]
            st[d * N_CHUNK + c, 0:1, :] = jnp.broadcast_to(m_loc, (1, LANES))
            st[d * N_CHUNK + c, 1:2, :] = jnp.broadcast_to(g, (1, LANES))
        both = _dot(jnp.concatenate(lhs, axis=0).astype(bf16), k)
        cl[c] = both[0:N_AUG, :]
        cl[N_CHUNK + c] = both[N_AUG:2 * N_AUG, :]
        return carry

    _chunk_loop(local_state)

    cst[...] = jnp.zeros(cst.shape, f32)
    mst[...] = jnp.zeros(mst.shape, f32)

    def scan_step(i, carry):
        for d in (0, 1):
            c = i if d == 0 else jnp.where(i < N_CTX_CHUNK, N_CTX_CHUNK - 1 - i, N_CHUNK + N_CTX_CHUNK - 1 - i)
            idx = d * N_CHUNK + c
            c_loc = cl[idx]
            m_loc = st[idx, 0:1, :]
            g = st[idx, 1:2, :]
            c_prev = cst[d]
            m_prev = mst[d, 0:1, :]
            m_new = jnp.maximum(g + m_prev, m_loc)
            dec = jnp.exp(g + m_prev - m_new)
            add = jnp.exp(m_loc - m_new)
            cl[idx] = c_prev
            st[idx, 2:3, :] = m_prev
            cst[d] = dec * c_prev + add * c_loc
            mst[d, 0:1, :] = m_new
        return carry

    lax.fori_loop(0, N_CHUNK, scan_step, 0)

    def outputs(c, carry):
        lo = pl.multiple_of(c * CHUNK, CHUNK)
        k = ks[pl.ds(lo, CHUNK), :]
        q_t = qts[pl.ds(lo, CHUNK), :]
        v_aug = jnp.concatenate([vts[pl.ds(lo, CHUNK), :], jnp.ones((16, LANES), bf16)], axis=0)
        q_f = q_t.astype(f32)
        s_t = _dot(k, q_t)
        x_c = pltpu.roll(xc[pl.ds(lo, CHUNK), :], lane_shift, 1)
        hs = None
        for d in (0, 1):
            _, b_r = gate_rows(c, d)
            idx = d * N_CHUNK + c
            r_c = x_c[:, 8 * d:8 * d + 1] - x_c[:, 8 * d + 4:8 * d + 5]
            dm = jnp.where(upper if d == 0 else lower, b_r + r_c, -jnp.inf)
            e_r = b_r + st[idx, 2:3, :]
            m_t = jnp.maximum(e_r, jnp.max(dm, axis=0, keepdims=True))
            p_t = s_t * jnp.exp(dm - m_t)
            inter = jnp.exp(e_r - m_t)
            lhs = jnp.concatenate([v_aug, cl[idx].astype(bf16)], axis=1)
            rhs = jnp.concatenate([p_t, q_f * inter], axis=0).astype(bf16)
            nd = _dot(lhs, rhs)
            den = nd[CHUNK:CHUNK + 1, :]
            h_d = nd[0:CHUNK, :] * (1.0 / jnp.maximum(jnp.abs(den), jnp.exp(-m_t)))
            hs = h_d if hs is None else hs + h_d
        hn = hs * lax.rsqrt(jnp.mean(hs * hs, axis=0, keepdims=True) + EPS) * mn_ref[...]
        hg[pl.ds(lo, CHUNK), :] = (og[pl.ds(lo, CHUNK), :].astype(f32) * hn.T).astype(bf16)
        return carry

    _chunk_loop(outputs)
    outc_ref[...] = hg[0:CTX, :]
    outl_ref[...] = hg[CTX:TOK, :]


def _mlstm(qk, vo, gates, gate_b, mnorm):
    lat0 = R_CTX // SEQ
    ctx = lambda col0: pl.BlockSpec((CTX, LANES), lambda b, h: (b, col0 + h))
    lat = lambda col0: pl.BlockSpec((SEQ, LANES), lambda b, h: (lat0 + b, col0 + h))
    return pl.pallas_call(
        _mlstm_kernel,
        grid=(B, HEADS_A),
        in_specs=[
            ctx(0), lat(0), ctx(HEADS_A), lat(HEADS_A),
            ctx(0), lat(0), ctx(HEADS_A), lat(HEADS_A),
            pl.BlockSpec((CTX, LANES), lambda b, h: (b, 0)),
            pl.BlockSpec((SEQ, LANES), lambda b, h: (lat0 + b, 0)),
            pl.BlockSpec((1, LANES), lambda b, h: (0, 0)),
            pl.BlockSpec((None, CHUNK, LANES), lambda b, h: (h, 0, 0)),
        ],
        out_specs=[
            pl.BlockSpec((CTX, LANES), lambda b, h: (b, h)),
            pl.BlockSpec((SEQ, LANES), lambda b, h: (b, h)),
        ],
        out_shape=[
            jax.ShapeDtypeStruct((R_CTX, W_A), bf16),
            jax.ShapeDtypeStruct((R_LAT, W_A), bf16),
        ],
        scratch_shapes=[
            pltpu.VMEM((TOK, LANES), bf16),
            pltpu.VMEM((TOK, LANES), bf16),
            pltpu.VMEM((TOK, LANES), bf16),
            pltpu.VMEM((TOK, LANES), bf16),
            pltpu.VMEM((TOK, LANES), bf16),
            pltpu.VMEM((TOK, LANES), f32),
            pltpu.VMEM((16 * N_CHUNK, LANES), f32),
            pltpu.VMEM((2 * N_CHUNK, CHUNK + 16, LANES), f32),
            pltpu.VMEM((2 * N_CHUNK, 8, LANES), f32),
            pltpu.VMEM((2, CHUNK + 16, LANES), f32),
            pltpu.VMEM((2, 8, LANES), f32),
        ],
        compiler_params=pltpu.CompilerParams(
            dimension_semantics=("parallel", "arbitrary"), vmem_limit_bytes=40 * MIB),
        name="mlstm",
    )(qk, qk, qk, qk, vo, vo, vo, vo, gates, gates, gate_b, mnorm)


def _odd_in_kernel(h_ref, mod_ref, w_ref, cos_ref, sin_ref, o_ref):
    n = _modulated(h_ref[...], mod_ref[3:4, :], mod_ref[4:5, :]).astype(bf16)
    cos = cos_ref[...]
    sin = sin_ref[...]
    n_rot = (HEADS_C + KV_HEADS) * DH // LANES
    y = _dot(n, w_ref[:, 0:n_rot * LANES])
    for c in range(0, n_rot, 2):
        x1 = y[:, c * LANES:(c + 1) * LANES]
        x2 = y[:, (c + 1) * LANES:(c + 2) * LANES]
        r1 = x1 * cos - x2 * sin
        r2 = x1 * sin + x2 * cos
        if c < HEADS_C * DH // LANES:
            r1 = r1 * (DH ** -0.5 * LOG2E)
            r2 = r2 * (DH ** -0.5 * LOG2E)
        o_ref[:, c * LANES:(c + 1) * LANES] = r1.astype(bf16)
        o_ref[:, (c + 1) * LANES:(c + 2) * LANES] = r2.astype(bf16)
    v0 = n_rot * LANES
    o_ref[:, v0:QKV] = _dot(n, w_ref[:, v0:QKV]).astype(bf16)


def _odd_in(h, mods, w, cos_t, sin_t):
    tm = TM_PROJ
    n_ctx = R_CTX // tm
    per_b = SEQ // tm
    rope_map = lambda i: (jnp.where(i < n_ctx, 0, 1 + jnp.maximum(i - n_ctx, 0) % per_b), 0)
    return pl.pallas_call(
        _odd_in_kernel,
        grid=(R_ALL // tm,),
        in_specs=[
            pl.BlockSpec((tm, D), lambda i: (i, 0)),
            pl.BlockSpec((None, N_MOD, D), lambda i: (_who_flat(i, tm), 0, 0)),
            pl.BlockSpec((D, QKV), lambda i: (0, 0), pipeline_mode=pl.Buffered(1)),
            pl.BlockSpec((tm, LANES), rope_map),
            pl.BlockSpec((tm, LANES), rope_map),
        ],
        out_specs=pl.BlockSpec((tm, QKV), lambda i: (i, 0)),
        out_shape=jax.ShapeDtypeStruct((R_ALL, QKV), bf16),
        compiler_params=pltpu.CompilerParams(
            dimension_semantics=("parallel",), vmem_limit_bytes=32 * MIB),
        name="odd_in",
    )(h, mods, w, cos_t, sin_t)


def _attn_kernel(sink_ref, q_ref, kc_ref, kl_ref, vc_ref, vl_ref, o_ref,
                 k_scr, vt_scr, q_scr, s_scr, p_scr, ot_scr):
    kv_w = KV_HEADS * DH
    key = lax.broadcasted_iota(jnp.int32, (CHUNK, CHUNK), 0)
    qry = lax.broadcasted_iota(jnp.int32, (CHUNK, CHUNK), 1)
    far = 1 << 20
    lane_head = lax.broadcasted_iota(jnp.int32, (CHUNK, kv_w), 1) % LANES // (DH // 2)
    neg = -1e30
    n_key = CTX + 3 * CHUNK
    n_slab = n_key // CHUNK

    def stage_keys(k_rows, v_rows, row0):
        rows = k_rows.shape[0]
        k_scr[row0:row0 + rows, :] = k_rows
        v_t = v_rows.astype(f32).T.astype(bf16)
        for j in range(KV_HEADS):
            vt_scr[j, 0:DH, row0:row0 + rows] = v_t[j * DH:(j + 1) * DH, :]

    stage_keys(kc_ref[...], vc_ref[...], 0)
    for j in range(KV_HEADS):
        vt_scr[j, DH:DH + 16, :] = jnp.ones((16, n_key), bf16)

    def block(blk, carry):
        q0 = pl.multiple_of(blk * CHUNK, CHUNK)
        prev_ok = key >= qry + jnp.where(blk > 0, 0, far)
        next_ok = key <= qry - jnp.where(blk < N_BLK - 1, 0, far)
        for n, off in enumerate((-1, 0, 1)):
            src = pl.multiple_of(jnp.clip(blk + off, 0, N_BLK - 1) * CHUNK, CHUNK)
            stage_keys(kl_ref[pl.ds(src, CHUNK), :], vl_ref[pl.ds(src, CHUNK), :], CTX + n * CHUNK)

        for j in range(KV_HEADS):
            keep = jnp.where(lane_head == j, 1.0, 0.0).astype(bf16)
            for g in range(GROUP):
                q_scr[j, g * CHUNK:(g + 1) * CHUNK, :] = q_ref[pl.ds(q0, CHUNK), g * kv_w:(g + 1) * kv_w] * keep
            s_scr[j] = _dot_nt(k_scr[...], q_scr[j])

        def scores(j, g, slab):
            s = s_scr[j, slab * CHUNK:(slab + 1) * CHUNK, g * CHUNK:(g + 1) * CHUNK]
            if slab == 2:
                s = jnp.where(prev_ok, s, neg)
            if slab == 4:
                s = jnp.where(next_ok, s, neg)
            return s

        for j in range(KV_HEADS):
            sink_terms = []
            for g in range(GROUP):
                sink = jnp.full((1, CHUNK), sink_ref[j * GROUP + g] * LOG2E, f32)
                m8 = None
                for slab in range(n_slab):
                    part = jnp.max(scores(j, g, slab).reshape(CHUNK // 8, 8, CHUNK), axis=0)
                    m8 = part if m8 is None else jnp.maximum(m8, part)
                m = jnp.maximum(sink, jnp.max(m8, axis=0, keepdims=True))
                for slab in range(n_slab):
                    p = jnp.exp2(scores(j, g, slab) - m)
                    p_scr[j, slab * CHUNK:(slab + 1) * CHUNK, g * CHUNK:(g + 1) * CHUNK] = p.astype(bf16)
                sink_terms.append(jnp.exp2(sink - m))
            acc = _dot(vt_scr[j], p_scr[j])
            den = acc[DH:DH + 1, :] + jnp.concatenate(sink_terms, axis=1)
            ot_scr[j * DH:(j + 1) * DH, :] = acc[0:DH, :] * (1.0 / den)
        for g in range(GROUP):
            o_ref[pl.ds(q0, CHUNK), g * kv_w:(g + 1) * kv_w] = (
                ot_scr[:, g * CHUNK:(g + 1) * CHUNK].T.astype(bf16))
        return carry

    lax.fori_loop(0, N_BLK, block, 0)


def _attention(qkv, sink):
    kv_w = KV_HEADS * DH
    k_col = HEADS_C * DH // kv_w
    v_col = k_col + 1
    lat0 = R_CTX // SEQ
    ctx = lambda col: pl.BlockSpec((CTX, kv_w), lambda b: (b, col))
    lat = lambda col: pl.BlockSpec((SEQ, kv_w), lambda b: (lat0 + b, col))
    return pl.pallas_call(
        _attn_kernel,
        grid=(B,),
        in_specs=[
            pl.BlockSpec(memory_space=pltpu.SMEM),
            pl.BlockSpec((SEQ, HEADS_C * DH), lambda b: (lat0 + b, 0)),
            ctx(k_col), lat(k_col), ctx(v_col), lat(v_col),
        ],
        out_specs=pl.BlockSpec((SEQ, HEADS_C * DH), lambda b: (b, 0)),
        out_shape=jax.ShapeDtypeStruct((R_LAT, HEADS_C * DH), bf16),
        scratch_shapes=[
            pltpu.VMEM((CTX + 3 * CHUNK, KV_HEADS * DH), bf16),
            pltpu.VMEM((KV_HEADS, DH + 16, CTX + 3 * CHUNK), bf16),
            pltpu.VMEM((KV_HEADS, GROUP * CHUNK, KV_HEADS * DH), bf16),
            pltpu.VMEM((KV_HEADS, CTX + 3 * CHUNK, GROUP * CHUNK), f32),
            pltpu.VMEM((KV_HEADS, CTX + 3 * CHUNK, GROUP * CHUNK), bf16),
            pltpu.VMEM((KV_HEADS * DH, GROUP * CHUNK), f32),
        ],
        compiler_params=pltpu.CompilerParams(
            dimension_semantics=("parallel",), vmem_limit_bytes=40 * MIB),
        name="window_attention",
    )(sink, qkv, qkv, qkv, qkv, qkv)


def _rope_tables():
    rows = SEQ // GRID_W
    row, col = jnp.meshgrid(jnp.arange(rows), jnp.arange(GRID_W), indexing='ij')
    n_freq = DH // 4
    inv = ROPE_BASE ** (-jnp.arange(n_freq, dtype=f32) / n_freq)
    ang = jnp.concatenate([row.reshape(-1, 1).astype(f32) * inv,
                           col.reshape(-1, 1).astype(f32) * inv], axis=-1)
    reps = 2 * LANES // DH
    cos = jnp.tile(jnp.cos(ang), (1, reps))
    sin = jnp.tile(jnp.sin(ang), (1, reps))
    cos = jnp.concatenate([jnp.ones((TM_PROJ, LANES), f32), cos], axis=0)
    sin = jnp.concatenate([jnp.zeros((TM_PROJ, LANES), f32), sin], axis=0)
    return cos, sin


def kernel(x, c, ctx, c_ctx, ada_w, ada_b, ffn_w_in, ffn_w_out, even_w_in, even_w_out, mlstm_conv,
           mlstm_gate_b, mlstm_norm, sgu_norm, sgu_ws, sgu_b, odd_w_qkv, odd_w_out, attn_sink, final_norm):
    cs = jnp.concatenate([c_ctx[None, :], c, jnp.zeros((16 - 1 - B, D), f32)], axis=0)
    mods = _modulation(cs, ada_w, ada_b)[:, :1 + B, :].reshape(2, 1 + B, N_MOD, D)

    fw_in = ffn_w_in
    fw_out = ffn_w_out

    m0 = mods[0]
    h = _ffn((ctx.reshape(R_CTX, D), x.reshape(R_LAT, D)), m0, fw_in, fw_out, sel=(0, 0), mi=0)
    w_in = even_w_in[0]
    gate0 = 4 * W_A
    gate1 = gate0 + 4 * HEADS_A
    w_in = jnp.concatenate(
        [w_in[:, :gate0], w_in[:, gate1:], w_in[:, gate0:gate1],
         jnp.zeros((D, LANES - 4 * HEADS_A), f32)], axis=1).astype(bf16)
    qk, vo, uv, gates = _even_in(h, m0, w_in, mlstm_conv[0])
    gate_b = jnp.pad(mlstm_gate_b[0].reshape(1, 4 * HEADS_A), ((0, 0), (0, LANES - 4 * HEADS_A)))
    mnorm_t = jnp.broadcast_to(mlstm_norm[0][:, :, None], (HEADS_A, CHUNK, LANES))
    ha_ctx, ha_lat = _mlstm(qk, vo, gates, gate_b, mnorm_t)
    sbx = jnp.repeat(sgu_b[0].T, LANES, axis=1)
    h = _ffn(h, m0, fw_in, fw_out, sel=(0, 1), mi=6,
             even=(ha_ctx, ha_lat, uv, sgu_norm[0].reshape(1, W_A), sgu_ws[0].astype(bf16), sbx,
                   even_w_out[0].astype(bf16)))

    m1 = mods[1]
    h = _ffn(h, m1, fw_in, fw_out, sel=(1, 0), mi=0)
    cos_t, sin_t = _rope_tables()
    qdim = HEADS_C * DH
    kdim = KV_HEADS * DH
    w_q = odd_w_qkv[0][:, :qdim].reshape(D, KV_HEADS, GROUP, DH // 2, 2).transpose(0, 2, 4, 1, 3).reshape(D, qdim)
    w_k = odd_w_qkv[0][:, qdim:qdim + kdim].reshape(D, KV_HEADS, DH // 2, 2).transpose(0, 3, 1, 2).reshape(D, kdim)
    w_qkv = jnp.concatenate([w_q, w_k, odd_w_qkv[0][:, qdim + kdim:]], axis=1).astype(bf16)
    w_o = odd_w_out[0].reshape(KV_HEADS, GROUP, DH, D).transpose(1, 0, 2, 3).reshape(qdim, D).astype(bf16)
    qkv = _odd_in(h, m1, w_qkv, cos_t, sin_t)
    attn = _attention(qkv, attn_sink[0])
    out = _ffn(h, m1, fw_in, fw_out, sel=(1, 1), mi=6, last=(attn, w_o, final_norm))
    return out.reshape(B, SEQ, D)
```

```python
import functools

import jax
import jax.numpy as jnp
from jax import lax
from jax.experimental import pallas as pl
from jax.experimental.pallas import tpu as pltpu

f32 = jnp.float32
bf16 = jnp.bfloat16

D = 1024
B = 8
SEQ = 2048
CTX = 256
TOK = CTX + SEQ
GRID_W = 64
N_MOD = 9
D_FF = 2816
EPS = 1e-6
HEADS_A = 4
CHUNK = 128
N_CHUNK = TOK // CHUNK
N_CTX_CHUNK = CTX // CHUNK
W_A = 512
EVEN_COLS = 3200
HEADS_C = 16
KV_HEADS = 4
GROUP = HEADS_C // KV_HEADS
DH = 64
QKV = (HEADS_C + 2 * KV_HEADS) * DH
N_BLK = SEQ // CHUNK
ROPE_BASE = 10000.0
LOG2E = 1.4426950408889634

R_CTX = B * CTX
R_LAT = B * SEQ
R_ALL = R_CTX + R_LAT

LANES = 128
TM_FFN = 512
TM_PROJ = 512
FC = 256
N_FC = D_FF // FC
MIB = 1024 * 1024


def _dot(a, b):
    return jnp.dot(a, b, preferred_element_type=f32)


def _dot_nt(a, b):
    return lax.dot_general(a, b, (((1,), (1,)), ((), ())), preferred_element_type=f32)


def _dot_tn(a, b):
    return lax.dot_general(a, b, (((0,), (0,)), ((), ())), preferred_element_type=f32)


def _sigmoid(x):
    return 1.0 / (1.0 + jnp.exp(-x))


def _split3(x):
    hi = x.astype(bf16)
    r1 = x - hi.astype(f32)
    mid = r1.astype(bf16)
    lo = (r1 - mid.astype(f32)).astype(bf16)
    return hi, mid, lo


def _modulated(h, shift, scale):
    ms = jnp.mean(h * h, axis=-1, keepdims=True)
    return h * lax.rsqrt(ms + EPS) * (1.0 + scale) + shift


def _mod_kernel(c_ref, w_ref, b_ref, o_ref):
    x = c_ref[...]
    s = x * _sigmoid(x)
    w = w_ref[...]
    s_hi = s.astype(bf16)
    s_lo = (s - s_hi.astype(f32)).astype(bf16)
    w_hi = w.astype(bf16)
    w_lo = (w - w_hi.astype(f32)).astype(bf16)
    o_ref[...] = _dot(s_hi, w_hi) + _dot(s_hi, w_lo) + _dot(s_lo, w_hi) + b_ref[...]


def _modulation(cs, ada_w, ada_b):
    depth = ada_w.shape[0]
    rows = cs.shape[0]
    n_col = N_MOD * D
    tn = 1024
    return pl.pallas_call(
        _mod_kernel,
        grid=(depth, n_col // tn),
        in_specs=[
            pl.BlockSpec((rows, D), lambda l, j: (0, 0)),
            pl.BlockSpec((None, D, tn), lambda l, j: (l, 0, j)),
            pl.BlockSpec((None, 1, tn), lambda l, j: (l, 0, j)),
        ],
        out_specs=pl.BlockSpec((None, rows, tn), lambda l, j: (l, 0, j)),
        out_shape=jax.ShapeDtypeStruct((depth, rows, n_col), f32),
        compiler_params=pltpu.CompilerParams(
            dimension_semantics=("parallel", "parallel"), vmem_limit_bytes=32 * MIB),
        name="modulation",
    )(cs, ada_w, ada_b.reshape(depth, 1, n_col))


def _who_flat(tile, tm):
    n_ctx = R_CTX // tm
    per_b = SEQ // tm
    return jnp.where(tile < n_ctx, 0, 1 + jnp.maximum(tile - n_ctx, 0) // per_b)


W_CHUNKS = 16
W_SLOTS = 4


def _fetch_cast(src, dst, stage, sem):
    rows = dst.shape[0] // W_CHUNKS

    def piece(c):
        slot = c % W_SLOTS
        return pltpu.make_async_copy(src.at[pl.ds(c * rows, rows), :], stage.at[slot], sem.at[slot])

    for c in range(W_SLOTS - 1):
        piece(c).start()
    for c in range(W_CHUNKS):
        if c + W_SLOTS - 1 < W_CHUNKS:
            piece(c + W_SLOTS - 1).start()
        piece(c).wait()
        dst[c * rows:(c + 1) * rows, :] = stage[c % W_SLOTS].astype(bf16)


def _gelu_tanh(x):
    return x * (0.5 * (1.0 + jnp.tanh(0.7978845608028654 * (x + 0.044715 * (x * x * x)))))


def _even_mix(ha, uv_ref, sg_ref, ws_ref, sb_ref, wm_ref, hb_scr):
    u = uv_ref[:, 0:W_A].astype(f32)
    v = uv_ref[:, W_A:2 * W_A].astype(f32)
    vn = (v * lax.rsqrt(jnp.mean(v * v, axis=-1, keepdims=True) + EPS) * sg_ref[...]).astype(bf16)
    n_chunk = TM_FFN // CHUNK
    for g in range(W_A // LANES):
        cs = slice(g * LANES, (g + 1) * LANES)
        rhs = jnp.concatenate([vn[n * CHUNK:(n + 1) * CHUNK, cs] for n in range(n_chunk)], axis=1)
        mixed = _dot(ws_ref[g], rhs)
        for n in range(n_chunk):
            r = slice(n * CHUNK, (n + 1) * CHUNK)
            hb_scr[r, cs] = (u[r, cs] * (mixed[:, n * LANES:(n + 1) * LANES] + sb_ref[:, cs])).astype(bf16)
    return _dot(ha, wm_ref[0:W_A, :]) + _dot(hb_scr[...], wm_ref[W_A:2 * W_A, :])


def _ffn_kernel(*refs, mi, mixer, final, split, sel):
    refs = list(refs)
    is_ctx = pl.program_id(0) < R_CTX // TM_FFN
    if split:
        c_ref, x_ref = refs[0:2]
        refs = refs[2:]
        read_h = lambda: jnp.where(is_ctx, c_ref[...], x_ref[...])
    else:
        h_ref = refs.pop(0)
        read_h = lambda: h_ref[...]
    wi_ref, wo_ref, wi_stage, wo_stage, wi_sem, wo_sem = refs[-6:]
    refs = refs[:-6]
    if mixer == "attn":
        a_ref, wm_ref = refs[0:2]
        refs = refs[2:]
    elif mixer == "even":
        hac_ref, hax_ref, uv_ref, sg_ref, ws_ref, sb_ref, wm_ref = refs[0:7]
        refs = refs[7:]
    mod_ref, wi_hbm, wo_hbm = refs[0:3]
    refs = refs[3:]
    if final:
        fn_ref = refs.pop(0)
    o_ref, n_scr, acc_scr = refs[0:3]
    refs = refs[3:]

    @pl.when(pl.program_id(0) == 0)
    def _():
        _fetch_cast(wi_hbm.at[sel[0], sel[1]], wi_ref, wi_stage, wi_sem)
        _fetch_cast(wo_hbm.at[sel[0], sel[1]], wo_ref, wo_stage, wo_sem)

    if mixer is not None:
        h_scr = refs.pop(0)
        if mixer == "attn":
            y = _dot(a_ref[...], wm_ref[...])
        else:
            ha = jnp.where(is_ctx, hac_ref[...], hax_ref[...])
            y = _even_mix(ha, uv_ref, sg_ref, ws_ref, sb_ref, wm_ref, refs.pop(0))
        h_scr[...] = read_h() + mod_ref[5:6, :] * y
        read_h = lambda: h_scr[...]
    n_scr[...] = _modulated(read_h(), mod_ref[mi:mi + 1, :], mod_ref[mi + 1:mi + 2, :]).astype(bf16)
    for j in range(N_FC):
        n = n_scr[...]
        g = _dot(n, wi_ref[:, j * FC:(j + 1) * FC])
        u = _dot(n, wi_ref[:, D_FF + j * FC:D_FF + (j + 1) * FC])
        a = (g * _sigmoid(g) * u).astype(bf16)
        y = _dot(a, wo_ref[j * FC:(j + 1) * FC, :])
        if j == 0:
            acc_scr[...] = y
        else:
            acc_scr[...] += y
    out = read_h() + (0.5 * mod_ref[mi + 2:mi + 3, :]) * acc_scr[...]
    if final:
        ms = jnp.mean(out * out, axis=-1, keepdims=True)
        out = out * lax.rsqrt(ms + EPS) * fn_ref[...]
    o_ref[...] = out


def _ffn(h, mods, w_in, w_out, *, sel, mi, even=None, last=None):
    tm = TM_FFN
    n_ctx = R_CTX // tm
    tile0 = n_ctx if last is not None else 0
    split = isinstance(h, tuple)
    const2 = lambda i: (0, 0)
    ctx_map = lambda i: (jnp.minimum(i, n_ctx - 1), 0)
    lat_map = lambda i: (jnp.maximum(i - n_ctx, 0), 0)
    if split:
        rows_out = R_ALL
        in_specs = [pl.BlockSpec((tm, D), ctx_map), pl.BlockSpec((tm, D), lat_map)]
        args = list(h)
    else:
        rows_out = h.shape[0] - tile0 * tm
        in_specs = [pl.BlockSpec((tm, D), lambda i: (i + tile0, 0))]
        args = [h]
    scratch = [pltpu.VMEM((tm, D), bf16), pltpu.VMEM((tm, D), f32)]
    mixer = None
    if last is not None:
        mixer = "attn"
        attn, w_attn, final_norm = last
        in_specs += [
            pl.BlockSpec((tm, D), lambda i: (i, 0)),
            pl.BlockSpec((D, D), const2, pipeline_mode=pl.Buffered(1)),
        ]
        args += [attn, w_attn]
        scratch.append(pltpu.VMEM((tm, D), f32))
    elif even is not None:
        mixer = "even"
        in_specs += [
            pl.BlockSpec((tm, W_A), ctx_map),
            pl.BlockSpec((tm, W_A), lat_map),
            pl.BlockSpec((tm, 2 * W_A), lambda i: (i, 0)),
            pl.BlockSpec((1, W_A), const2),
            pl.BlockSpec((W_A // LANES, CHUNK, CHUNK), lambda i: (0, 0, 0)),
            pl.BlockSpec((CHUNK, W_A), const2),
            pl.BlockSpec((2 * W_A, D), const2, pipeline_mode=pl.Buffered(1)),
        ]
        args += list(even)
        scratch += [pltpu.VMEM((tm, D), f32), pltpu.VMEM((tm, W_A), bf16)]
    in_specs += [
        pl.BlockSpec((None, N_MOD, D), lambda i: (_who_flat(i + tile0, tm), 0, 0)),
        pl.BlockSpec(memory_space=pl.ANY),
        pl.BlockSpec(memory_space=pl.ANY),
    ]
    args += [mods, w_in, w_out]
    if last is not None:
        in_specs.append(pl.BlockSpec((1, D), const2))
        args.append(final_norm.reshape(1, D))
    scratch += [
        pltpu.VMEM((D, 2 * D_FF), bf16),
        pltpu.VMEM((D_FF, D), bf16),
        pltpu.VMEM((W_SLOTS, D // W_CHUNKS, 2 * D_FF), f32),
        pltpu.VMEM((W_SLOTS, D_FF // W_CHUNKS, D), f32),
        pltpu.SemaphoreType.DMA((W_SLOTS,)),
        pltpu.SemaphoreType.DMA((W_SLOTS,)),
    ]
    return pl.pallas_call(
        functools.partial(_ffn_kernel, mi=mi, mixer=mixer, final=last is not None, split=split, sel=sel),
        grid=(rows_out // tm,),
        in_specs=in_specs,
        out_specs=pl.BlockSpec((tm, D), lambda i: (i, 0)),
        out_shape=jax.ShapeDtypeStruct((rows_out, D), f32),
        scratch_shapes=scratch,
        compiler_params=pltpu.CompilerParams(
            dimension_semantics=("arbitrary",), vmem_limit_bytes=56 * MIB),
        name={None: "ffn", "even": "ffn_even", "attn": "ffn_final"}[mixer],
    )(*args)


HALO = 8


def _even_in_kernel(h_ref, hp_ref, hn_ref, mod_ref, w_ref, cw_ref, qk_ref, vo_ref, uv_ref, g_ref):
    tile = pl.program_id(0)
    tm = TM_PROJ
    n_chunk = tm // CHUNK
    shift, scale = mod_ref[3:4, :], mod_ref[4:5, :]
    n = _modulated(h_ref[...], shift, scale)
    halo = _modulated(jnp.concatenate([hp_ref[...], hn_ref[...]], axis=0), shift, scale)
    n_ext = jnp.concatenate([halo[0:HALO], n, halo[HALO:2 * HALO]], axis=0).astype(bf16)
    n = n.astype(bf16)

    row = lax.broadcasted_iota(jnp.int32, (tm, LANES), 0)
    is_ctx = tile < R_CTX // tm
    pos = jnp.where(is_ctx, row % CTX, row + (jnp.maximum(tile - R_CTX // tm, 0) % (SEQ // tm)) * tm)
    seq_start = pos == 0
    seq_end = pos == jnp.where(is_ctx, CTX - 1, SEQ - 1)

    p = _dot(n_ext, w_ref[:, 0:2 * W_A])
    cur = p[HALO:HALO + tm]
    prv = pltpu.roll(p, 1, 0)[HALO:HALO + tm]
    nxt = pltpu.roll(p, tm + 2 * HALO - 1, 0)[HALO:HALO + tm]
    for cb in range(2 * HEADS_A):
        cs = slice(cb * LANES, (cb + 1) * LANES)
        y = (cw_ref[0:1, cs] * jnp.where(seq_start, 0.0, prv[:, cs]) + cw_ref[1:2, cs] * cur[:, cs]
             + cw_ref[2:3, cs] * jnp.where(seq_end, 0.0, nxt[:, cs]))
        y = y * _sigmoid(y)
        if cb < HEADS_A:
            for c in range(n_chunk):
                r = slice(c * CHUNK, (c + 1) * CHUNK)
                qk_ref[r, cs] = y[r, :].T.astype(bf16)
        else:
            qk_ref[:, cs] = (y * CHUNK ** -0.5).astype(bf16)

    v = _dot(n, w_ref[:, 2 * W_A:3 * W_A])
    for hd in range(HEADS_A):
        cs = slice(hd * LANES, (hd + 1) * LANES)
        for c in range(n_chunk):
            r = slice(c * CHUNK, (c + 1) * CHUNK)
            vo_ref[r, cs] = v[r, cs].T.astype(bf16)
    vo_ref[:, W_A:2 * W_A] = _sigmoid(_dot(n, w_ref[:, 3 * W_A:4 * W_A])).astype(bf16)
    uv_ref[...] = _gelu_tanh(_dot(n, w_ref[:, 2048:3072])).astype(bf16)
    g_ref[...] = _dot(n, w_ref[:, 3072:3200])


def _even_in(h, mods, w, conv_w):
    tm = TM_PROJ
    out_map = lambda i: (i, 0)
    halo_blocks = tm // HALO
    last_halo = R_ALL // HALO - 1
    return pl.pallas_call(
        _even_in_kernel,
        grid=(R_ALL // tm,),
        in_specs=[
            pl.BlockSpec((tm, D), lambda i: (i, 0)),
            pl.BlockSpec((HALO, D), lambda i: (jnp.maximum(i * halo_blocks - 1, 0), 0)),
            pl.BlockSpec((HALO, D), lambda i: (jnp.minimum((i + 1) * halo_blocks, last_halo), 0)),
            pl.BlockSpec((None, N_MOD, D), lambda i: (_who_flat(i, tm), 0, 0)),
            pl.BlockSpec((D, EVEN_COLS), lambda i: (0, 0), pipeline_mode=pl.Buffered(1)),
            pl.BlockSpec((3, 2 * W_A), lambda i: (0, 0)),
        ],
        out_specs=[
            pl.BlockSpec((tm, 1024), out_map),
            pl.BlockSpec((tm, 1024), out_map),
            pl.BlockSpec((tm, 1024), out_map),
            pl.BlockSpec((tm, LANES), out_map),
        ],
        out_shape=[
            jax.ShapeDtypeStruct((R_ALL, 1024), bf16),
            jax.ShapeDtypeStruct((R_ALL, 1024), bf16),
            jax.ShapeDtypeStruct((R_ALL, 1024), bf16),
            jax.ShapeDtypeStruct((R_ALL, LANES), f32),
        ],
        compiler_params=pltpu.CompilerParams(
            dimension_semantics=("parallel",), vmem_limit_bytes=40 * MIB),
        name="even_in",
    )(h, h, h, mods, w, conv_w)


N_AUG = CHUNK + 16
CHUNKS_PER_ITER = 9


def _chunk_loop(body):
    def group(i, carry):
        for u in range(CHUNKS_PER_ITER):
            carry = body(i * CHUNKS_PER_ITER + u, carry)
        return carry
    lax.fori_loop(0, N_CHUNK // CHUNKS_PER_ITER, group, 0)


def _mlstm_kernel(qc_ref, ql_ref, kc_ref, kl_ref, vc_ref, vl_ref, oc_ref, ol_ref, gc_ref, gl_ref,
                  gb_ref, mn_ref, outc_ref, outl_ref,
                  ks, qts, vts, og, hg, xc, xr, cl, st, cst, mst):
    head = pl.program_id(1)
    rowi = lax.broadcasted_iota(jnp.int32, (CHUNK, CHUNK), 0)
    coli = lax.broadcasted_iota(jnp.int32, (CHUNK, CHUNK), 1)
    lower = coli <= rowi
    upper = coli >= rowi
    tri = jnp.where(lower, 1.0, 0.0).astype(bf16)

    def part(ctx_ref, lat_ref, c):
        return (ctx_ref, c * CHUNK) if c < N_CTX_CHUNK else (lat_ref, (c - N_CTX_CHUNK) * CHUNK)

    for c in range(N_CHUNK):
        lo = c * CHUNK
        for dst, refs in ((qts, (qc_ref, ql_ref)), (ks, (kc_ref, kl_ref)), (vts, (vc_ref, vl_ref)),
                          (og, (oc_ref, ol_ref))):
            src, at = part(*refs, c)
            dst[lo:lo + CHUNK, :] = src[at:at + CHUNK, :]

    @pl.when(head == 0)
    def _():
        kind = (coli // HEADS_A) % 4
        for c in range(N_CHUNK):
            lo = c * CHUNK
            src, at = part(gc_ref, gl_ref, c)
            gt = src[at:at + CHUNK, :] + gb_ref[...]
            lf = jnp.minimum(gt, 0.0) - jnp.log1p(jnp.exp(-jnp.abs(gt)))
            hi, mid, lw = _split3(lf)
            pre = _dot(tri, hi) + _dot(tri, mid) + _dot(tri, lw)
            suf = pre[CHUNK - 1:CHUNK, :] - pre + lf
            x = jnp.where(kind == 1, pre, jnp.where(kind == 3, suf, gt))
            xr[16 * c:16 * c + 16, :] = x.T[0:16, :]
            xc[lo:lo + CHUNK, :] = x

    lane_shift = (LANES - head) % LANES

    def gate_rows(c, d):
        row = c * 16 + 8 * d + head
        return xr[pl.ds(row, 1), :], xr[pl.ds(row + HEADS_A, 1), :]

    def local_state(c, carry):
        lo = pl.multiple_of(c * CHUNK, CHUNK)
        k = ks[pl.ds(lo, CHUNK), :]
        v_t = vts[pl.ds(lo, CHUNK), :].astype(f32)
        lhs = []
        for d in (0, 1):
            ig_r, b_r = gate_rows(c, d)
            g = b_r[:, CHUNK - 1:CHUNK] if d == 0 else b_r[:, 0:1]
            a_r = g - b_r + ig_r
            m_loc = jnp.max(a_r, axis=1, keepdims=True)
            w_r = jnp.exp(a_r - m_loc)
            lhs += [v_t * w_r, jnp.broadcast_to(w_r, (16, LANES))]
            st[d * N_CHUNK + c, 0:1, :] = jnp.broadcast_to(m_loc, (1, LANES))
            st[d * N_CHUNK + c, 1:2, :] = jnp.broadcast_to(g, (1, LANES))
        both = _dot(jnp.concatenate(lhs, axis=0).astype(bf16), k)
        cl[c] = both[0:N_AUG, :]
        cl[N_CHUNK + c] = both[N_AUG:2 * N_AUG, :]
        return carry

    _chunk_loop(local_state)

    cst[...] = jnp.zeros(cst.shape, f32)
    mst[...] = jnp.zeros(mst.shape, f32)

    def scan_step(i, carry):
        for d in (0, 1):
            c = i if d == 0 else jnp.where(i < N_CTX_CHUNK, N_CTX_CHUNK - 1 - i, N_CHUNK + N_CTX_CHUNK - 1 - i)
            idx = d * N_CHUNK + c
            c_loc = cl[idx]
            m_loc = st[idx, 0:1, :]
            g = st[idx, 1:2, :]
            c_prev = cst[d]
            m_prev = mst[d, 0:1, :]
            m_new = jnp.maximum(g + m_prev, m_loc)
            dec = jnp.exp(g + m_prev - m_new)
            add = jnp.exp(m_loc - m_new)
            cl[idx] = c_prev
            st[idx, 2:3, :] = m_prev
            cst[d] = dec * c_prev + add * c_loc
            mst[d, 0:1, :] = m_new
        return carry

    lax.fori_loop(0, N_CHUNK, scan_step, 0)

    def outputs(c, carry):
        lo = pl.multiple_of(c * CHUNK, CHUNK)
        k = ks[pl.ds(lo, CHUNK), :]
        q_t = qts[pl.ds(lo, CHUNK), :]
        v_aug = jnp.concatenate([vts[pl.ds(lo, CHUNK), :], jnp.ones((16, LANES), bf16)], axis=0)
        q_f = q_t.astype(f32)
        s_t = _dot(k, q_t)
        x_c = pltpu.roll(xc[pl.ds(lo, CHUNK), :], lane_shift, 1)
        hs = None
        for d in (0, 1):
            _, b_r = gate_rows(c, d)
            idx = d * N_CHUNK + c
            r_c = x_c[:, 8 * d:8 * d + 1] - x_c[:, 8 * d + 4:8 * d + 5]
            dm = jnp.where(upper if d == 0 else lower, b_r + r_c, -jnp.inf)
            e_r = b_r + st[idx, 2:3, :]
            m_t = jnp.maximum(e_r, jnp.max(dm, axis=0, keepdims=True))
            p_t = s_t * jnp.exp(dm - m_t)
            inter = jnp.exp(e_r - m_t)
            lhs = jnp.concatenate([v_aug, cl[idx].astype(bf16)], axis=1)
            rhs = jnp.concatenate([p_t, q_f * inter], axis=0).astype(bf16)
            nd = _dot(lhs, rhs)
            den = nd[CHUNK:CHUNK + 1, :]
            h_d = nd[0:CHUNK, :] * (1.0 / jnp.maximum(jnp.abs(den), jnp.exp(-m_t)))
            hs = h_d if hs is None else hs + h_d
        hn = hs * lax.rsqrt(jnp.mean(hs * hs, axis=0, keepdims=True) + EPS) * mn_ref[...]
        hg[pl.ds(lo, CHUNK), :] = (og[pl.ds(lo, CHUNK), :].astype(f32) * hn.T).astype(bf16)
        return carry

    _chunk_loop(outputs)
    outc_ref[...] = hg[0:CTX, :]
    outl_ref[...] = hg[CTX:TOK, :]


def _mlstm(qk, vo, gates, gate_b, mnorm):
    lat0 = R_CTX // SEQ
    ctx = lambda col0: pl.BlockSpec((CTX, LANES), lambda b, h: (b, col0 + h))
    lat = lambda col0: pl.BlockSpec((SEQ, LANES), lambda b, h: (lat0 + b, col0 + h))
    return pl.pallas_call(
        _mlstm_kernel,
        grid=(B, HEADS_A),
        in_specs=[
            ctx(0), lat(0), ctx(HEADS_A), lat(HEADS_A),
            ctx(0), lat(0), ctx(HEADS_A), lat(HEADS_A),
            pl.BlockSpec((CTX, LANES), lambda b, h: (b, 0)),
            pl.BlockSpec((SEQ, LANES), lambda b, h: (lat0 + b, 0)),
            pl.BlockSpec((1, LANES), lambda b, h: (0, 0)),
            pl.BlockSpec((None, CHUNK, LANES), lambda b, h: (h, 0, 0)),
        ],
        out_specs=[
            pl.BlockSpec((CTX, LANES), lambda b, h: (b, h)),
            pl.BlockSpec((SEQ, LANES), lambda b, h: (b, h)),
        ],
        out_shape=[
            jax.ShapeDtypeStruct((R_CTX, W_A), bf16),
            jax.ShapeDtypeStruct((R_LAT, W_A), bf16),
        ],
        scratch_shapes=[
            pltpu.VMEM((TOK, LANES), bf16),
            pltpu.VMEM((TOK, LANES), bf16),
            pltpu.VMEM((TOK, LANES), bf16),
            pltpu.VMEM((TOK, LANES), bf16),
            pltpu.VMEM((TOK, LANES), bf16),
            pltpu.VMEM((TOK, LANES), f32),
            pltpu.VMEM((16 * N_CHUNK, LANES), f32),
            pltpu.VMEM((2 * N_CHUNK, CHUNK + 16, LANES), f32),
            pltpu.VMEM((2 * N_CHUNK, 8, LANES), f32),
            pltpu.VMEM((2, CHUNK + 16, LANES), f32),
            pltpu.VMEM((2, 8, LANES), f32),
        ],
        compiler_params=pltpu.CompilerParams(
            dimension_semantics=("parallel", "arbitrary"), vmem_limit_bytes=40 * MIB),
        name="mlstm",
    )(qk, qk, qk, qk, vo, vo, vo, vo, gates, gates, gate_b, mnorm)


def _odd_in_kernel(h_ref, mod_ref, w_ref, cos_ref, sin_ref, o_ref):
    n = _modulated(h_ref[...], mod_ref[3:4, :], mod_ref[4:5, :]).astype(bf16)
    cos = cos_ref[...]
    sin = sin_ref[...]
    n_rot = (HEADS_C + KV_HEADS) * DH // LANES
    y = _dot(n, w_ref[:, 0:n_rot * LANES])
    for c in range(0, n_rot, 2):
        x1 = y[:, c * LANES:(c + 1) * LANES]
        x2 = y[:, (c + 1) * LANES:(c + 2) * LANES]
        r1 = x1 * cos - x2 * sin
        r2 = x1 * sin + x2 * cos
        if c < HEADS_C * DH // LANES:
            r1 = r1 * (DH ** -0.5 * LOG2E)
            r2 = r2 * (DH ** -0.5 * LOG2E)
        o_ref[:, c * LANES:(c + 1) * LANES] = r1.astype(bf16)
        o_ref[:, (c + 1) * LANES:(c + 2) * LANES] = r2.astype(bf16)
    v0 = n_rot * LANES
    o_ref[:, v0:QKV] = _dot(n, w_ref[:, v0:QKV]).astype(bf16)


def _odd_in(h, mods, w, cos_t, sin_t):
    tm = TM_PROJ
    n_ctx = R_CTX // tm
    per_b = SEQ // tm
    rope_map = lambda i: (jnp.where(i < n_ctx, 0, 1 + jnp.maximum(i - n_ctx, 0) % per_b), 0)
    return pl.pallas_call(
        _odd_in_kernel,
        grid=(R_ALL // tm,),
        in_specs=[
            pl.BlockSpec((tm, D), lambda i: (i, 0)),
            pl.BlockSpec((None, N_MOD, D), lambda i: (_who_flat(i, tm), 0, 0)),
            pl.BlockSpec((D, QKV), lambda i: (0, 0), pipeline_mode=pl.Buffered(1)),
            pl.BlockSpec((tm, LANES), rope_map),
            pl.BlockSpec((tm, LANES), rope_map),
        ],
        out_specs=pl.BlockSpec((tm, QKV), lambda i: (i, 0)),
        out_shape=jax.ShapeDtypeStruct((R_ALL, QKV), bf16),
        compiler_params=pltpu.CompilerParams(
            dimension_semantics=("parallel",), vmem_limit_bytes=32 * MIB),
        name="odd_in",
    )(h, mods, w, cos_t, sin_t)


def _attn_kernel(sink_ref, q_ref, kc_ref, kl_ref, vc_ref, vl_ref, o_ref,
                 k_scr, vt_scr, q_scr, s_scr, p_scr, ot_scr):
    kv_w = KV_HEADS * DH
    key = lax.broadcasted_iota(jnp.int32, (CHUNK, CHUNK), 0)
    qry = lax.broadcasted_iota(jnp.int32, (CHUNK, CHUNK), 1)
    far = 1 << 20
    dim_head = lax.broadcasted_iota(jnp.int32, (kv_w, CHUNK), 0) % LANES // (DH // 2)
    neg = -1e30
    n_key = CTX + 3 * CHUNK
    n_slab = n_key // CHUNK

    def stage_keys(k_rows, v_rows, row0):
        rows = k_rows.shape[0]
        k_scr[row0:row0 + rows, :] = k_rows
        v_t = v_rows.astype(f32).T.astype(bf16)
        for j in range(KV_HEADS):
            vt_scr[j, 0:DH, row0:row0 + rows] = v_t[j * DH:(j + 1) * DH, :]

    stage_keys(kc_ref[...], vc_ref[...], 0)
    for j in range(KV_HEADS):
        vt_scr[j, DH:DH + 16, :] = jnp.ones((16, n_key), bf16)

    def block(blk, carry):
        q0 = pl.multiple_of(blk * CHUNK, CHUNK)
        prev_ok = key >= qry + jnp.where(blk > 0, 0, far)
        next_ok = key <= qry - jnp.where(blk < N_BLK - 1, 0, far)
        for n, off in enumerate((-1, 0, 1)):
            src = pl.multiple_of(jnp.clip(blk + off, 0, N_BLK - 1) * CHUNK, CHUNK)
            stage_keys(kl_ref[pl.ds(src, CHUNK), :], vl_ref[pl.ds(src, CHUNK), :], CTX + n * CHUNK)

        for g in range(GROUP):
            q_t = q_ref[pl.ds(q0, CHUNK), g * kv_w:(g + 1) * kv_w].astype(f32).T.astype(bf16)
            for j in range(KV_HEADS):
                q_scr[j, :, g * CHUNK:(g + 1) * CHUNK] = q_t * jnp.where(dim_head == j, 1.0, 0.0).astype(bf16)
        for j in range(KV_HEADS):
            s_scr[j] = _dot(k_scr[...], q_scr[j])

        def scores(j, g, slab):
            s = s_scr[j, slab * CHUNK:(slab + 1) * CHUNK, g * CHUNK:(g + 1) * CHUNK]
            if slab == 2:
                s = jnp.where(prev_ok, s, neg)
            if slab == 4:
                s = jnp.where(next_ok, s, neg)
            return s

        for j in range(KV_HEADS):
            sink_terms = []
            for g in range(GROUP):
                sink = jnp.full((1, CHUNK), sink_ref[j * GROUP + g] * LOG2E, f32)
                m8 = None
                for slab in range(n_slab):
                    part = jnp.max(scores(j, g, slab).reshape(CHUNK // 8, 8, CHUNK), axis=0)
                    m8 = part if m8 is None else jnp.maximum(m8, part)
                m = jnp.maximum(sink, jnp.max(m8, axis=0, keepdims=True))
                for slab in range(n_slab):
                    p = jnp.exp2(scores(j, g, slab) - m)
                    p_scr[j, slab * CHUNK:(slab + 1) * CHUNK, g * CHUNK:(g + 1) * CHUNK] = p.astype(bf16)
                sink_terms.append(jnp.exp2(sink - m))
            acc = _dot(vt_scr[j], p_scr[j])
            den = acc[DH:DH + 1, :] + jnp.concatenate(sink_terms, axis=1)
            ot_scr[j * DH:(j + 1) * DH, :] = acc[0:DH, :] * (1.0 / den)
        for g in range(GROUP):
            o_ref[pl.ds(q0, CHUNK), g * kv_w:(g + 1) * kv_w] = (
                ot_scr[:, g * CHUNK:(g + 1) * CHUNK].T.astype(bf16))
        return carry

    lax.fori_loop(0, N_BLK, block, 0)


def _attention(qkv, sink):
    kv_w = KV_HEADS * DH
    k_col = HEADS_C * DH // kv_w
    v_col = k_col + 1
    lat0 = R_CTX // SEQ
    ctx = lambda col: pl.BlockSpec((CTX, kv_w), lambda b: (b, col))
    lat = lambda col: pl.BlockSpec((SEQ, kv_w), lambda b: (lat0 + b, col))
    return pl.pallas_call(
        _attn_kernel,
        grid=(B,),
        in_specs=[
            pl.BlockSpec(memory_space=pltpu.SMEM),
            pl.BlockSpec((SEQ, HEADS_C * DH), lambda b: (lat0 + b, 0)),
            ctx(k_col), lat(k_col), ctx(v_col), lat(v_col),
        ],
        out_specs=pl.BlockSpec((SEQ, HEADS_C * DH), lambda b: (b, 0)),
        out_shape=jax.ShapeDtypeStruct((R_LAT, HEADS_C * DH), bf16),
        scratch_shapes=[
            pltpu.VMEM((CTX + 3 * CHUNK, KV_HEADS * DH), bf16),
            pltpu.VMEM((KV_HEADS, DH + 16, CTX + 3 * CHUNK), bf16),
            pltpu.VMEM((KV_HEADS, KV_HEADS * DH, GROUP * CHUNK), bf16),
            pltpu.VMEM((KV_HEADS, CTX + 3 * CHUNK, GROUP * CHUNK), f32),
            pltpu.VMEM((KV_HEADS, CTX + 3 * CHUNK, GROUP * CHUNK), bf16),
            pltpu.VMEM((KV_HEADS * DH, GROUP * CHUNK), f32),
        ],
        compiler_params=pltpu.CompilerParams(
            dimension_semantics=("parallel",), vmem_limit_bytes=40 * MIB),
        name="window_attention",
    )(sink, qkv, qkv, qkv, qkv, qkv)


def _rope_tables():
    rows = SEQ // GRID_W
    row, col = jnp.meshgrid(jnp.arange(rows), jnp.arange(GRID_W), indexing='ij')
    n_freq = DH // 4
    inv = ROPE_BASE ** (-jnp.arange(n_freq, dtype=f32) / n_freq)
    ang = jnp.concatenate([row.reshape(-1, 1).astype(f32) * inv,
                           col.reshape(-1, 1).astype(f32) * inv], axis=-1)
    reps = 2 * LANES // DH
    cos = jnp.tile(jnp.cos(ang), (1, reps))
    sin = jnp.tile(jnp.sin(ang), (1, reps))
    cos = jnp.concatenate([jnp.ones((TM_PROJ, LANES), f32), cos], axis=0)
    sin = jnp.concatenate([jnp.zeros((TM_PROJ, LANES), f32), sin], axis=0)
    return cos, sin


def kernel(x, c, ctx, c_ctx, ada_w, ada_b, ffn_w_in, ffn_w_out, even_w_in, even_w_out, mlstm_conv,
           mlstm_gate_b, mlstm_norm, sgu_norm, sgu_ws, sgu_b, odd_w_qkv, odd_w_out, attn_sink, final_norm):
    cs = jnp.concatenate([c_ctx[None, :], c, jnp.zeros((16 - 1 - B, D), f32)], axis=0)
    mods = _modulation(cs, ada_w, ada_b)[:, :1 + B, :].reshape(2, 1 + B, N_MOD, D)

    fw_in = ffn_w_in
    fw_out = ffn_w_out

    m0 = mods[0]
    h = _ffn((ctx.reshape(R_CTX, D), x.reshape(R_LAT, D)), m0, fw_in, fw_out, sel=(0, 0), mi=0)
    w_in = even_w_in[0]
    gate0 = 4 * W_A
    gate1 = gate0 + 4 * HEADS_A
    w_in = jnp.concatenate(
        [w_in[:, :gate0], w_in[:, gate1:], w_in[:, gate0:gate1],
         jnp.zeros((D, LANES - 4 * HEADS_A), f32)], axis=1).astype(bf16)
    qk, vo, uv, gates = _even_in(h, m0, w_in, mlstm_conv[0])
    gate_b = jnp.pad(mlstm_gate_b[0].reshape(1, 4 * HEADS_A), ((0, 0), (0, LANES - 4 * HEADS_A)))
    mnorm_t = jnp.broadcast_to(mlstm_norm[0][:, :, None], (HEADS_A, CHUNK, LANES))
    ha_ctx, ha_lat = _mlstm(qk, vo, gates, gate_b, mnorm_t)
    sbx = jnp.repeat(sgu_b[0].T, LANES, axis=1)
    h = _ffn(h, m0, fw_in, fw_out, sel=(0, 1), mi=6,
             even=(ha_ctx, ha_lat, uv, sgu_norm[0].reshape(1, W_A), sgu_ws[0].astype(bf16), sbx,
                   even_w_out[0].astype(bf16)))

    m1 = mods[1]
    h = _ffn(h, m1, fw_in, fw_out, sel=(1, 0), mi=0)
    cos_t, sin_t = _rope_tables()
    qdim = HEADS_C * DH
    kdim = KV_HEADS * DH
    w_q = odd_w_qkv[0][:, :qdim].reshape(D, KV_HEADS, GROUP, DH // 2, 2).transpose(0, 2, 4, 1, 3).reshape(D, qdim)
    w_k = odd_w_qkv[0][:, qdim:qdim + kdim].reshape(D, KV_HEADS, DH // 2, 2).transpose(0, 3, 1, 2).reshape(D, kdim)
    w_qkv = jnp.concatenate([w_q, w_k, odd_w_qkv[0][:, qdim + kdim:]], axis=1).astype(bf16)
    w_o = odd_w_out[0].reshape(KV_HEADS, GROUP, DH, D).transpose(1, 0, 2, 3).reshape(qdim, D).astype(bf16)
    qkv = _odd_in(h, m1, w_qkv, cos_t, sin_t)
    attn = _attention(qkv, attn_sink[0])
    out = _ffn(h, m1, fw_in, fw_out, sel=(1, 1), mi=6, last=(attn, w_o, final_norm))
    return out.reshape(B, SEQ, D)
```

```python
import functools

import numpy as np
import jax
import jax.numpy as jnp
from jax import lax
from jax.experimental import pallas as pl
from jax.experimental.pallas import tpu as pltpu

f32 = jnp.float32
bf16 = jnp.bfloat16

D = 1024
B = 8
SEQ = 2048
CTX = 256
TOK = CTX + SEQ
GRID_W = 64
N_MOD = 9
D_FF = 2816
EPS = 1e-6
HEADS_A = 4
CHUNK = 128
N_CHUNK = TOK // CHUNK
N_CTX_CHUNK = CTX // CHUNK
W_A = 512
EVEN_COLS = 3200
HEADS_C = 16
KV_HEADS = 4
GROUP = HEADS_C // KV_HEADS
DH = 64
QKV = (HEADS_C + 2 * KV_HEADS) * DH
N_BLK = SEQ // CHUNK
ROPE_BASE = 10000.0
LOG2E = 1.4426950408889634

R_CTX = B * CTX
R_LAT = B * SEQ
R_ALL = R_CTX + R_LAT

LANES = 128
TM_FFN = 512
TM_PROJ = 512
FC = 256
N_FC = D_FF // FC
MIB = 1024 * 1024


def _dot(a, b):
    return jnp.dot(a, b, preferred_element_type=f32)


def _dot_nt(a, b):
    return lax.dot_general(a, b, (((1,), (1,)), ((), ())), preferred_element_type=f32)


def _dot_tn(a, b):
    return lax.dot_general(a, b, (((0,), (0,)), ((), ())), preferred_element_type=f32)


def _sigmoid(x):
    return 1.0 / (1.0 + jnp.exp(-x))


def _split3(x):
    hi = x.astype(bf16)
    r1 = x - hi.astype(f32)
    mid = r1.astype(bf16)
    lo = (r1 - mid.astype(f32)).astype(bf16)
    return hi, mid, lo


def _modulated(h, shift, scale):
    ms = jnp.mean(h * h, axis=-1, keepdims=True)
    return h * lax.rsqrt(ms + EPS) * (1.0 + scale) + shift


def _mod_kernel(c_ref, w_ref, b_ref, o_ref):
    x = c_ref[...]
    s = x * _sigmoid(x)
    w = w_ref[...]
    s_hi = s.astype(bf16)
    s_lo = (s - s_hi.astype(f32)).astype(bf16)
    w_hi = w.astype(bf16)
    w_lo = (w - w_hi.astype(f32)).astype(bf16)
    o_ref[...] = _dot(s_hi, w_hi) + _dot(s_hi, w_lo) + _dot(s_lo, w_hi) + b_ref[...]


def _modulation(cs, ada_w, ada_b):
    depth = ada_w.shape[0]
    rows = cs.shape[0]
    n_col = N_MOD * D
    tn = 1024
    return pl.pallas_call(
        _mod_kernel,
        grid=(depth, n_col // tn),
        in_specs=[
            pl.BlockSpec((rows, D), lambda l, j: (0, 0)),
            pl.BlockSpec((None, D, tn), lambda l, j: (l, 0, j)),
            pl.BlockSpec((None, 1, tn), lambda l, j: (l, 0, j)),
        ],
        out_specs=pl.BlockSpec((None, rows, tn), lambda l, j: (l, 0, j)),
        out_shape=jax.ShapeDtypeStruct((depth, rows, n_col), f32),
        compiler_params=pltpu.CompilerParams(
            dimension_semantics=("parallel", "parallel"), vmem_limit_bytes=32 * MIB),
        name="modulation",
    )(cs, ada_w, ada_b.reshape(depth, 1, n_col))


def _who_flat(tile, tm):
    n_ctx = R_CTX // tm
    per_b = SEQ // tm
    return jnp.where(tile < n_ctx, 0, 1 + jnp.maximum(tile - n_ctx, 0) // per_b)


W_CHUNKS = 16
W_SLOTS = 4


def _fetch_cast(src, dst, stage, sem, place=None):
    rows = dst.shape[0] // W_CHUNKS

    def piece(c):
        slot = c % W_SLOTS
        return pltpu.make_async_copy(src.at[pl.ds(c * rows, rows), :], stage.at[slot], sem.at[slot])

    for c in range(W_SLOTS - 1):
        piece(c).start()
    for c in range(W_CHUNKS):
        if c + W_SLOTS - 1 < W_CHUNKS:
            piece(c + W_SLOTS - 1).start()
        piece(c).wait()
        if place is None:
            dst[c * rows:(c + 1) * rows, :] = stage[c % W_SLOTS].astype(bf16)
        else:
            place(dst, slice(c * rows, (c + 1) * rows), stage[c % W_SLOTS])


def _gelu_tanh(x):
    return x * (0.5 * (1.0 + jnp.tanh(0.7978845608028654 * (x + 0.044715 * (x * x * x)))))


def _even_mix(ha, uv_ref, sg_ref, ws_ref, sb_ref, wm_ref, hb_scr):
    u = uv_ref[:, 0:W_A].astype(f32)
    v = uv_ref[:, W_A:2 * W_A].astype(f32)
    vn = (v * lax.rsqrt(jnp.mean(v * v, axis=-1, keepdims=True) + EPS) * sg_ref[...]).astype(bf16)
    n_chunk = TM_FFN // CHUNK
    for g in range(W_A // LANES):
        cs = slice(g * LANES, (g + 1) * LANES)
        rhs = jnp.concatenate([vn[n * CHUNK:(n + 1) * CHUNK, cs] for n in range(n_chunk)], axis=1)
        mixed = _dot(ws_ref[g], rhs)
        for n in range(n_chunk):
            r = slice(n * CHUNK, (n + 1) * CHUNK)
            hb_scr[r, cs] = (u[r, cs] * (mixed[:, n * LANES:(n + 1) * LANES] + sb_ref[:, cs])).astype(bf16)
    return _dot(ha, wm_ref[0:W_A, :]) + _dot(hb_scr[...], wm_ref[W_A:2 * W_A, :])


def _ffn_kernel(*refs, mi, mixer, final, split, sel):
    refs = list(refs)
    is_ctx = pl.program_id(0) < R_CTX // TM_FFN
    if split:
        c_ref, x_ref = refs[0:2]
        refs = refs[2:]
        read_h = lambda: jnp.where(is_ctx, c_ref[...], x_ref[...])
    else:
        h_ref = refs.pop(0)
        read_h = lambda: h_ref[...]
    wi_ref, wo_ref, wi_stage, wo_stage, wi_sem, wo_sem = refs[-6:]
    refs = refs[:-6]
    if mixer == "attn":
        a_ref, wm_ref = refs[0:2]
        refs = refs[2:]
    elif mixer == "even":
        hac_ref, hax_ref, uv_ref, sg_ref, ws_ref, sb_ref, wm_ref = refs[0:7]
        refs = refs[7:]
    mod_ref, wi_hbm, wo_hbm = refs[0:3]
    refs = refs[3:]
    if final:
        fn_ref = refs.pop(0)
    o_ref, n_scr, acc_scr = refs[0:3]
    refs = refs[3:]

    @pl.when(pl.program_id(0) == 0)
    def _():
        _fetch_cast(wi_hbm.at[sel[0], sel[1]], wi_ref, wi_stage, wi_sem)
        _fetch_cast(wo_hbm.at[sel[0], sel[1]], wo_ref, wo_stage, wo_sem)

    if mixer is not None:
        h_scr = refs.pop(0)
        if mixer == "attn":
            y = _dot(a_ref[...], wm_ref[...])
        else:
            ha = jnp.where(is_ctx, hac_ref[...], hax_ref[...])
            y = _even_mix(ha, uv_ref, sg_ref, ws_ref, sb_ref, wm_ref, refs.pop(0))
        h_scr[...] = read_h() + mod_ref[5:6, :] * y
        read_h = lambda: h_scr[...]
    n_scr[...] = _modulated(read_h(), mod_ref[mi:mi + 1, :], mod_ref[mi + 1:mi + 2, :]).astype(bf16)
    for j in range(N_FC):
        n = n_scr[...]
        g = _dot(n, wi_ref[:, j * FC:(j + 1) * FC])
        u = _dot(n, wi_ref[:, D_FF + j * FC:D_FF + (j + 1) * FC])
        a = (g * _sigmoid(g) * u).astype(bf16)
        y = _dot(a, wo_ref[j * FC:(j + 1) * FC, :])
        if j == 0:
            acc_scr[...] = y
        else:
            acc_scr[...] += y
    out = read_h() + (0.5 * mod_ref[mi + 2:mi + 3, :]) * acc_scr[...]
    if final:
        ms = jnp.mean(out * out, axis=-1, keepdims=True)
        out = out * lax.rsqrt(ms + EPS) * fn_ref[...]
    o_ref[...] = out


def _ffn(h, mods, w_in, w_out, *, sel, mi, even=None, last=None):
    tm = TM_FFN
    n_ctx = R_CTX // tm
    tile0 = n_ctx if last is not None else 0
    split = isinstance(h, tuple)
    const2 = lambda i: (0, 0)
    ctx_map = lambda i: (jnp.minimum(i, n_ctx - 1), 0)
    lat_map = lambda i: (jnp.maximum(i - n_ctx, 0), 0)
    if split:
        rows_out = R_ALL
        in_specs = [pl.BlockSpec((tm, D), ctx_map), pl.BlockSpec((tm, D), lat_map)]
        args = list(h)
    else:
        rows_out = h.shape[0] - tile0 * tm
        in_specs = [pl.BlockSpec((tm, D), lambda i: (i + tile0, 0))]
        args = [h]
    scratch = [pltpu.VMEM((tm, D), bf16), pltpu.VMEM((tm, D), f32)]
    mixer = None
    if last is not None:
        mixer = "attn"
        attn, w_attn, final_norm = last
        in_specs += [
            pl.BlockSpec((tm, D), lambda i: (i, 0)),
            pl.BlockSpec((D, D), const2, pipeline_mode=pl.Buffered(1)),
        ]
        args += [attn, w_attn]
        scratch.append(pltpu.VMEM((tm, D), f32))
    elif even is not None:
        mixer = "even"
        in_specs += [
            pl.BlockSpec((tm, W_A), ctx_map),
            pl.BlockSpec((tm, W_A), lat_map),
            pl.BlockSpec((tm, 2 * W_A), lambda i: (i, 0)),
            pl.BlockSpec((1, W_A), const2),
            pl.BlockSpec((W_A // LANES, CHUNK, CHUNK), lambda i: (0, 0, 0)),
            pl.BlockSpec((CHUNK, W_A), const2),
            pl.BlockSpec((2 * W_A, D), const2, pipeline_mode=pl.Buffered(1)),
        ]
        args += list(even)
        scratch += [pltpu.VMEM((tm, D), f32), pltpu.VMEM((tm, W_A), bf16)]
    in_specs += [
        pl.BlockSpec((None, N_MOD, D), lambda i: (_who_flat(i + tile0, tm), 0, 0)),
        pl.BlockSpec(memory_space=pl.ANY),
        pl.BlockSpec(memory_space=pl.ANY),
    ]
    args += [mods, w_in, w_out]
    if last is not None:
        in_specs.append(pl.BlockSpec((1, D), const2))
        args.append(final_norm.reshape(1, D))
    scratch += [
        pltpu.VMEM((D, 2 * D_FF), bf16),
        pltpu.VMEM((D_FF, D), bf16),
        pltpu.VMEM((W_SLOTS, D // W_CHUNKS, 2 * D_FF), f32),
        pltpu.VMEM((W_SLOTS, D_FF // W_CHUNKS, D), f32),
        pltpu.SemaphoreType.DMA((W_SLOTS,)),
        pltpu.SemaphoreType.DMA((W_SLOTS,)),
    ]
    return pl.pallas_call(
        functools.partial(_ffn_kernel, mi=mi, mixer=mixer, final=last is not None, split=split, sel=sel),
        grid=(rows_out // tm,),
        in_specs=in_specs,
        out_specs=pl.BlockSpec((tm, D), lambda i: (i, 0)),
        out_shape=jax.ShapeDtypeStruct((rows_out, D), f32),
        scratch_shapes=scratch,
        compiler_params=pltpu.CompilerParams(
            dimension_semantics=("arbitrary",), vmem_limit_bytes=56 * MIB),
        name={None: "ffn", "even": "ffn_even", "attn": "ffn_final"}[mixer],
    )(*args)


HALO = 8


def _place_even_w(dst, rows, piece):
    g0 = 4 * W_A
    g1 = g0 + 4 * HEADS_A
    dst[rows, 0:g0] = piece[:, 0:g0].astype(bf16)
    dst[rows, g0:g0 + 2 * W_A] = piece[:, g1:g1 + 2 * W_A].astype(bf16)
    lane = lax.broadcasted_iota(jnp.int32, (piece.shape[0], LANES), 1)
    dst[rows, g0 + 2 * W_A:EVEN_COLS] = jnp.where(lane < 4 * HEADS_A, piece[:, g0:g0 + LANES], 0.0).astype(bf16)


def _even_in_kernel(h_ref, hp_ref, hn_ref, mod_ref, w_hbm, cw_ref, qk_ref, vo_ref, uv_ref, g_ref,
                    w_ref, w_stage, w_sem):
    tile = pl.program_id(0)

    @pl.when(tile == 0)
    def _():
        _fetch_cast(w_hbm.at[0], w_ref, w_stage, w_sem, place=_place_even_w)

    tm = TM_PROJ
    n_chunk = tm // CHUNK
    shift, scale = mod_ref[3:4, :], mod_ref[4:5, :]
    n = _modulated(h_ref[...], shift, scale)
    halo = _modulated(jnp.concatenate([hp_ref[...], hn_ref[...]], axis=0), shift, scale)
    n_ext = jnp.concatenate([halo[0:HALO], n, halo[HALO:2 * HALO]], axis=0).astype(bf16)
    n = n.astype(bf16)

    row = lax.broadcasted_iota(jnp.int32, (tm, LANES), 0)
    is_ctx = tile < R_CTX // tm
    pos = jnp.where(is_ctx, row % CTX, row + (jnp.maximum(tile - R_CTX // tm, 0) % (SEQ // tm)) * tm)
    seq_start = pos == 0
    seq_end = pos == jnp.where(is_ctx, CTX - 1, SEQ - 1)

    p = _dot(n_ext, w_ref[:, 0:2 * W_A])
    cur = p[HALO:HALO + tm]
    prv = pltpu.roll(p, 1, 0)[HALO:HALO + tm]
    nxt = pltpu.roll(p, tm + 2 * HALO - 1, 0)[HALO:HALO + tm]
    for cb in range(2 * HEADS_A):
        cs = slice(cb * LANES, (cb + 1) * LANES)
        y = (cw_ref[0:1, cs] * jnp.where(seq_start, 0.0, prv[:, cs]) + cw_ref[1:2, cs] * cur[:, cs]
             + cw_ref[2:3, cs] * jnp.where(seq_end, 0.0, nxt[:, cs]))
        y = y * _sigmoid(y)
        if cb < HEADS_A:
            for c in range(n_chunk):
                r = slice(c * CHUNK, (c + 1) * CHUNK)
                qk_ref[r, cs] = y[r, :].T.astype(bf16)
        else:
            qk_ref[:, cs] = (y * CHUNK ** -0.5).astype(bf16)

    v = _dot(n, w_ref[:, 2 * W_A:3 * W_A])
    for hd in range(HEADS_A):
        cs = slice(hd * LANES, (hd + 1) * LANES)
        for c in range(n_chunk):
            r = slice(c * CHUNK, (c + 1) * CHUNK)
            vo_ref[r, cs] = v[r, cs].T.astype(bf16)
    vo_ref[:, W_A:2 * W_A] = _sigmoid(_dot(n, w_ref[:, 3 * W_A:4 * W_A])).astype(bf16)
    uv_ref[...] = _gelu_tanh(_dot(n, w_ref[:, 2048:3072])).astype(bf16)
    g_ref[...] = _dot(n, w_ref[:, 3072:3200])


def _even_in(h, mods, w, conv_w):
    tm = TM_PROJ
    out_map = lambda i: (i, 0)
    halo_blocks = tm // HALO
    last_halo = R_ALL // HALO - 1
    return pl.pallas_call(
        _even_in_kernel,
        grid=(R_ALL // tm,),
        in_specs=[
            pl.BlockSpec((tm, D), lambda i: (i, 0)),
            pl.BlockSpec((HALO, D), lambda i: (jnp.maximum(i * halo_blocks - 1, 0), 0)),
            pl.BlockSpec((HALO, D), lambda i: (jnp.minimum((i + 1) * halo_blocks, last_halo), 0)),
            pl.BlockSpec((None, N_MOD, D), lambda i: (_who_flat(i, tm), 0, 0)),
            pl.BlockSpec(memory_space=pl.ANY),
            pl.BlockSpec((3, 2 * W_A), lambda i: (0, 0)),
        ],
        out_specs=[
            pl.BlockSpec((tm, 1024), out_map),
            pl.BlockSpec((tm, 1024), out_map),
            pl.BlockSpec((tm, 1024), out_map),
            pl.BlockSpec((tm, LANES), out_map),
        ],
        out_shape=[
            jax.ShapeDtypeStruct((R_ALL, 1024), bf16),
            jax.ShapeDtypeStruct((R_ALL, 1024), bf16),
            jax.ShapeDtypeStruct((R_ALL, 1024), bf16),
            jax.ShapeDtypeStruct((R_ALL, LANES), f32),
        ],
        scratch_shapes=[
            pltpu.VMEM((D, EVEN_COLS), bf16),
            pltpu.VMEM((W_SLOTS, D // W_CHUNKS, w.shape[-1]), f32),
            pltpu.SemaphoreType.DMA((W_SLOTS,)),
        ],
        compiler_params=pltpu.CompilerParams(
            dimension_semantics=("arbitrary",), vmem_limit_bytes=40 * MIB),
        name="even_in",
    )(h, h, h, mods, w, conv_w)


N_AUG = CHUNK + 16
CHUNKS_PER_ITER = 9


def _chunk_loop(body):
    def group(i, carry):
        for u in range(CHUNKS_PER_ITER):
            carry = body(i * CHUNKS_PER_ITER + u, carry)
        return carry
    lax.fori_loop(0, N_CHUNK // CHUNKS_PER_ITER, group, 0)


def _mlstm_kernel(qc_ref, ql_ref, kc_ref, kl_ref, vc_ref, vl_ref, oc_ref, ol_ref, gc_ref, gl_ref,
                  gb_ref, mn_ref, outc_ref, outl_ref,
                  ks, qts, vts, og, hg, xc, xr, cl, st, cst, mst):
    head = pl.program_id(1)
    rowi = lax.broadcasted_iota(jnp.int32, (CHUNK, CHUNK), 0)
    coli = lax.broadcasted_iota(jnp.int32, (CHUNK, CHUNK), 1)
    lower = coli <= rowi
    upper = coli >= rowi
    tri = jnp.where(lower, 1.0, 0.0).astype(bf16)

    def part(ctx_ref, lat_ref, c):
        return (ctx_ref, c * CHUNK) if c < N_CTX_CHUNK else (lat_ref, (c - N_CTX_CHUNK) * CHUNK)

    for c in range(N_CHUNK):
        lo = c * CHUNK
        for dst, refs in ((qts, (qc_ref, ql_ref)), (ks, (kc_ref, kl_ref)), (vts, (vc_ref, vl_ref)),
                          (og, (oc_ref, ol_ref))):
            src, at = part(*refs, c)
            dst[lo:lo + CHUNK, :] = src[at:at + CHUNK, :]

    @pl.when(head == 0)
    def _():
        kind = (coli // HEADS_A) % 4
        for c in range(N_CHUNK):
            lo = c * CHUNK
            src, at = part(gc_ref, gl_ref, c)
            gt = src[at:at + CHUNK, :] + gb_ref[...]
            lf = jnp.minimum(gt, 0.0) - jnp.log1p(jnp.exp(-jnp.abs(gt)))
            hi, mid, lw = _split3(lf)
            pre = _dot(tri, hi) + _dot(tri, mid) + _dot(tri, lw)
            suf = pre[CHUNK - 1:CHUNK, :] - pre + lf
            x = jnp.where(kind == 1, pre, jnp.where(kind == 3, suf, gt))
            xr[16 * c:16 * c + 16, :] = x.T[0:16, :]
            xc[lo:lo + CHUNK, :] = x

    lane_shift = (LANES - head) % LANES

    def gate_rows(c, d):
        row = c * 16 + 8 * d + head
        return xr[pl.ds(row, 1), :], xr[pl.ds(row + HEADS_A, 1), :]

    def local_state(c, carry):
        lo = pl.multiple_of(c * CHUNK, CHUNK)
        k = ks[pl.ds(lo, CHUNK), :]
        v_t = vts[pl.ds(lo, CHUNK), :].astype(f32)
        lhs = []
        for d in (0, 1):
            ig_r, b_r = gate_rows(c, d)
            g = b_r[:, CHUNK - 1:CHUNK] if d == 0 else b_r[:, 0:1]
            a_r = g - b_r + ig_r
            m_loc = jnp.max(a_r, axis=1, keepdims=True)
            w_r = jnp.exp(a_r - m_loc)
            lhs += [v_t * w_r, jnp.broadcast_to(w_r, (16, LANES))]
            st[d * N_CHUNK + c, 0:1, :] = jnp.broadcast_to(m_loc, (1, LANES))
            st[d * N_CHUNK + c, 1:2, :] = jnp.broadcast_to(g, (1, LANES))
        both = _dot(jnp.concatenate(lhs, axis=0).astype(bf16), k)
        cl[c] = both[0:N_AUG, :]
        cl[N_CHUNK + c] = both[N_AUG:2 * N_AUG, :]
        return carry

    _chunk_loop(local_state)

    cst[...] = jnp.zeros(cst.shape, f32)
    mst[...] = jnp.zeros(mst.shape, f32)

    def scan_step(i, carry):
        for d in (0, 1):
            c = i if d == 0 else jnp.where(i < N_CTX_CHUNK, N_CTX_CHUNK - 1 - i, N_CHUNK + N_CTX_CHUNK - 1 - i)
            idx = d * N_CHUNK + c
            c_loc = cl[idx]
            m_loc = st[idx, 0:1, :]
            g = st[idx, 1:2, :]
            c_prev = cst[d]
            m_prev = mst[d, 0:1, :]
            m_new = jnp.maximum(g + m_prev, m_loc)
            dec = jnp.exp(g + m_prev - m_new)
            add = jnp.exp(m_loc - m_new)
            cl[idx] = c_prev
            st[idx, 2:3, :] = m_prev
            cst[d] = dec * c_prev + add * c_loc
            mst[d, 0:1, :] = m_new
        return carry

    lax.fori_loop(0, N_CHUNK, scan_step, 0)

    def outputs(c, carry):
        lo = pl.multiple_of(c * CHUNK, CHUNK)
        k = ks[pl.ds(lo, CHUNK), :]
        q_t = qts[pl.ds(lo, CHUNK), :]
        v_aug = jnp.concatenate([vts[pl.ds(lo, CHUNK), :], jnp.ones((16, LANES), bf16)], axis=0)
        q_f = q_t.astype(f32)
        s_t = _dot(k, q_t)
        x_c = pltpu.roll(xc[pl.ds(lo, CHUNK), :], lane_shift, 1)
        hs = None
        for d in (0, 1):
            _, b_r = gate_rows(c, d)
            idx = d * N_CHUNK + c
            r_c = x_c[:, 8 * d:8 * d + 1] - x_c[:, 8 * d + 4:8 * d + 5]
            dm = jnp.where(upper if d == 0 else lower, b_r + r_c, -jnp.inf)
            e_r = b_r + st[idx, 2:3, :]
            m_t = jnp.maximum(e_r, jnp.max(dm, axis=0, keepdims=True))
            p_t = s_t * jnp.exp(dm - m_t)
            inter = jnp.exp(e_r - m_t)
            lhs = jnp.concatenate([v_aug, cl[idx].astype(bf16)], axis=1)
            rhs = jnp.concatenate([p_t, q_f * inter], axis=0).astype(bf16)
            nd = _dot(lhs, rhs)
            den = nd[CHUNK:CHUNK + 1, :]
            h_d = nd[0:CHUNK, :] * (1.0 / jnp.maximum(jnp.abs(den), jnp.exp(-m_t)))
            hs = h_d if hs is None else hs + h_d
        hn = hs * lax.rsqrt(jnp.mean(hs * hs, axis=0, keepdims=True) + EPS) * mn_ref[...]
        hg[pl.ds(lo, CHUNK), :] = (og[pl.ds(lo, CHUNK), :].astype(f32) * hn.T).astype(bf16)
        return carry

    _chunk_loop(outputs)
    outc_ref[...] = hg[0:CTX, :]
    outl_ref[...] = hg[CTX:TOK, :]


def _mlstm(qk, vo, gates, gate_b, mnorm):
    lat0 = R_CTX // SEQ
    ctx = lambda col0: pl.BlockSpec((CTX, LANES), lambda b, h: (b, col0 + h))
    lat = lambda col0: pl.BlockSpec((SEQ, LANES), lambda b, h: (lat0 + b, col0 + h))
    return pl.pallas_call(
        _mlstm_kernel,
        grid=(B, HEADS_A),
        in_specs=[
            ctx(0), lat(0), ctx(HEADS_A), lat(HEADS_A),
            ctx(0), lat(0), ctx(HEADS_A), lat(HEADS_A),
            pl.BlockSpec((CTX, LANES), lambda b, h: (b, 0)),
            pl.BlockSpec((SEQ, LANES), lambda b, h: (lat0 + b, 0)),
            pl.BlockSpec((1, LANES), lambda b, h: (0, 0)),
            pl.BlockSpec((None, CHUNK, LANES), lambda b, h: (h, 0, 0)),
        ],
        out_specs=[
            pl.BlockSpec((CTX, LANES), lambda b, h: (b, h)),
            pl.BlockSpec((SEQ, LANES), lambda b, h: (b, h)),
        ],
        out_shape=[
            jax.ShapeDtypeStruct((R_CTX, W_A), bf16),
            jax.ShapeDtypeStruct((R_LAT, W_A), bf16),
        ],
        scratch_shapes=[
            pltpu.VMEM((TOK, LANES), bf16),
            pltpu.VMEM((TOK, LANES), bf16),
            pltpu.VMEM((TOK, LANES), bf16),
            pltpu.VMEM((TOK, LANES), bf16),
            pltpu.VMEM((TOK, LANES), bf16),
            pltpu.VMEM((TOK, LANES), f32),
            pltpu.VMEM((16 * N_CHUNK, LANES), f32),
            pltpu.VMEM((2 * N_CHUNK, CHUNK + 16, LANES), f32),
            pltpu.VMEM((2 * N_CHUNK, 8, LANES), f32),
            pltpu.VMEM((2, CHUNK + 16, LANES), f32),
            pltpu.VMEM((2, 8, LANES), f32),
        ],
        compiler_params=pltpu.CompilerParams(
            dimension_semantics=("parallel", "arbitrary"), vmem_limit_bytes=40 * MIB),
        name="mlstm",
    )(qk, qk, qk, qk, vo, vo, vo, vo, gates, gates, gate_b, mnorm)


def _odd_in_kernel(h_ref, mod_ref, w_ref, cos_ref, sin_ref, o_ref):
    n = _modulated(h_ref[...], mod_ref[3:4, :], mod_ref[4:5, :]).astype(bf16)
    cos = cos_ref[...]
    sin = sin_ref[...]
    n_rot = (HEADS_C + KV_HEADS) * DH // LANES
    y = _dot(n, w_ref[:, 0:n_rot * LANES])
    for c in range(0, n_rot, 2):
        x1 = y[:, c * LANES:(c + 1) * LANES]
        x2 = y[:, (c + 1) * LANES:(c + 2) * LANES]
        r1 = x1 * cos - x2 * sin
        r2 = x1 * sin + x2 * cos
        if c < HEADS_C * DH // LANES:
            r1 = r1 * (DH ** -0.5 * LOG2E)
            r2 = r2 * (DH ** -0.5 * LOG2E)
        o_ref[:, c * LANES:(c + 1) * LANES] = r1.astype(bf16)
        o_ref[:, (c + 1) * LANES:(c + 2) * LANES] = r2.astype(bf16)
    v0 = n_rot * LANES
    o_ref[:, v0:QKV] = _dot(n, w_ref[:, v0:QKV]).astype(bf16)


def _odd_in(h, mods, w, cos_t, sin_t):
    tm = TM_PROJ
    n_ctx = R_CTX // tm
    per_b = SEQ // tm
    rope_map = lambda i: (jnp.where(i < n_ctx, 0, 1 + jnp.maximum(i - n_ctx, 0) % per_b), 0)
    return pl.pallas_call(
        _odd_in_kernel,
        grid=(R_ALL // tm,),
        in_specs=[
            pl.BlockSpec((tm, D), lambda i: (i, 0)),
            pl.BlockSpec((None, N_MOD, D), lambda i: (_who_flat(i, tm), 0, 0)),
            pl.BlockSpec((D, QKV), lambda i: (0, 0), pipeline_mode=pl.Buffered(1)),
            pl.BlockSpec((tm, LANES), rope_map),
            pl.BlockSpec((tm, LANES), rope_map),
        ],
        out_specs=pl.BlockSpec((tm, QKV), lambda i: (i, 0)),
        out_shape=jax.ShapeDtypeStruct((R_ALL, QKV), bf16),
        compiler_params=pltpu.CompilerParams(
            dimension_semantics=("parallel",), vmem_limit_bytes=32 * MIB),
        name="odd_in",
    )(h, mods, w, cos_t, sin_t)


def _attn_kernel(sink_ref, q_ref, kc_ref, kl_ref, vc_ref, vl_ref, o_ref,
                 k_scr, vt_scr, q_scr, s_scr, p_scr, ot_scr):
    kv_w = KV_HEADS * DH
    key = lax.broadcasted_iota(jnp.int32, (CHUNK, CHUNK), 0)
    qry = lax.broadcasted_iota(jnp.int32, (CHUNK, CHUNK), 1)
    far = 1 << 20
    lane_head = lax.broadcasted_iota(jnp.int32, (CHUNK, kv_w), 1) % LANES // (DH // 2)
    neg = -1e30
    n_key = CTX + 3 * CHUNK
    n_slab = n_key // CHUNK

    def stage_keys(k_rows, v_rows, row0):
        rows = k_rows.shape[0]
        k_scr[row0:row0 + rows, :] = k_rows
        v_t = v_rows.astype(f32).T.astype(bf16)
        for j in range(KV_HEADS):
            vt_scr[j, 0:DH, row0:row0 + rows] = v_t[j * DH:(j + 1) * DH, :]

    stage_keys(kc_ref[...], vc_ref[...], 0)
    for j in range(KV_HEADS):
        vt_scr[j, DH:DH + 16, :] = jnp.ones((16, n_key), bf16)

    def block(blk, carry):
        q0 = pl.multiple_of(blk * CHUNK, CHUNK)
        prev_ok = key >= qry + jnp.where(blk > 0, 0, far)
        next_ok = key <= qry - jnp.where(blk < N_BLK - 1, 0, far)
        for n, off in enumerate((-1, 0, 1)):
            src = pl.multiple_of(jnp.clip(blk + off, 0, N_BLK - 1) * CHUNK, CHUNK)
            stage_keys(kl_ref[pl.ds(src, CHUNK), :], vl_ref[pl.ds(src, CHUNK), :], CTX + n * CHUNK)

        for j in range(KV_HEADS):
            keep = jnp.where(lane_head == j, 1.0, 0.0).astype(bf16)
            for g in range(GROUP):
                q_scr[j, g * CHUNK:(g + 1) * CHUNK, :] = q_ref[pl.ds(q0, CHUNK), g * kv_w:(g + 1) * kv_w] * keep
            s_scr[j] = _dot_nt(k_scr[...], q_scr[j])

        def scores(j, g, slab):
            s = s_scr[j, slab * CHUNK:(slab + 1) * CHUNK, g * CHUNK:(g + 1) * CHUNK]
            if slab == 2:
                s = jnp.where(prev_ok, s, neg)
            if slab == 4:
                s = jnp.where(next_ok, s, neg)
            return s

        for j in range(KV_HEADS):
            sink_terms = []
            for g in range(GROUP):
                sink = jnp.full((1, CHUNK), sink_ref[j * GROUP + g] * LOG2E, f32)
                m8 = None
                for slab in range(n_slab):
                    part = jnp.max(scores(j, g, slab).reshape(CHUNK // 8, 8, CHUNK), axis=0)
                    m8 = part if m8 is None else jnp.maximum(m8, part)
                m = jnp.maximum(sink, jnp.max(m8, axis=0, keepdims=True))
                for slab in range(n_slab):
                    p = jnp.exp2(scores(j, g, slab) - m)
                    p_scr[j, slab * CHUNK:(slab + 1) * CHUNK, g * CHUNK:(g + 1) * CHUNK] = p.astype(bf16)
                sink_terms.append(jnp.exp2(sink - m))
            acc = _dot(vt_scr[j], p_scr[j])
            den = acc[DH:DH + 1, :] + jnp.concatenate(sink_terms, axis=1)
            ot_scr[j * DH:(j + 1) * DH, :] = acc[0:DH, :] * (1.0 / den)
        for g in range(GROUP):
            o_ref[pl.ds(q0, CHUNK), g * kv_w:(g + 1) * kv_w] = (
                ot_scr[:, g * CHUNK:(g + 1) * CHUNK].T.astype(bf16))
        return carry

    lax.fori_loop(0, N_BLK, block, 0)


def _attention(qkv, sink):
    kv_w = KV_HEADS * DH
    k_col = HEADS_C * DH // kv_w
    v_col = k_col + 1
    lat0 = R_CTX // SEQ
    ctx = lambda col: pl.BlockSpec((CTX, kv_w), lambda b: (b, col))
    lat = lambda col: pl.BlockSpec((SEQ, kv_w), lambda b: (lat0 + b, col))
    return pl.pallas_call(
        _attn_kernel,
        grid=(B,),
        in_specs=[
            pl.BlockSpec(memory_space=pltpu.SMEM),
            pl.BlockSpec((SEQ, HEADS_C * DH), lambda b: (lat0 + b, 0)),
            ctx(k_col), lat(k_col), ctx(v_col), lat(v_col),
        ],
        out_specs=pl.BlockSpec((SEQ, HEADS_C * DH), lambda b: (b, 0)),
        out_shape=jax.ShapeDtypeStruct((R_LAT, HEADS_C * DH), bf16),
        scratch_shapes=[
            pltpu.VMEM((CTX + 3 * CHUNK, KV_HEADS * DH), bf16),
            pltpu.VMEM((KV_HEADS, DH + 16, CTX + 3 * CHUNK), bf16),
            pltpu.VMEM((KV_HEADS, GROUP * CHUNK, KV_HEADS * DH), bf16),
            pltpu.VMEM((KV_HEADS, CTX + 3 * CHUNK, GROUP * CHUNK), f32),
            pltpu.VMEM((KV_HEADS, CTX + 3 * CHUNK, GROUP * CHUNK), bf16),
            pltpu.VMEM((KV_HEADS * DH, GROUP * CHUNK), f32),
        ],
        compiler_params=pltpu.CompilerParams(
            dimension_semantics=("parallel",), vmem_limit_bytes=40 * MIB),
        name="window_attention",
    )(sink, qkv, qkv, qkv, qkv, qkv)


def _rope_tables():
    rows = SEQ // GRID_W
    row, col = np.meshgrid(np.arange(rows), np.arange(GRID_W), indexing='ij')
    n_freq = DH // 4
    inv = (np.float32(ROPE_BASE) ** (-np.arange(n_freq, dtype=np.float32) / np.float32(n_freq))).astype(np.float32)
    ang = np.concatenate([row.reshape(-1, 1).astype(np.float32) * inv,
                          col.reshape(-1, 1).astype(np.float32) * inv], axis=-1)
    reps = 2 * LANES // DH
    cos = np.tile(np.cos(ang).astype(np.float32), (1, reps))
    sin = np.tile(np.sin(ang).astype(np.float32), (1, reps))
    cos = np.concatenate([np.ones((TM_PROJ, LANES), np.float32), cos], axis=0)
    sin = np.concatenate([np.zeros((TM_PROJ, LANES), np.float32), sin], axis=0)
    return jnp.asarray(cos), jnp.asarray(sin)


def kernel(x, c, ctx, c_ctx, ada_w, ada_b, ffn_w_in, ffn_w_out, even_w_in, even_w_out, mlstm_conv,
           mlstm_gate_b, mlstm_norm, sgu_norm, sgu_ws, sgu_b, odd_w_qkv, odd_w_out, attn_sink, final_norm):
    cs = jnp.concatenate([c_ctx[None, :], c, jnp.zeros((16 - 1 - B, D), f32)], axis=0)
    mods = _modulation(cs, ada_w, ada_b)[:, :1 + B, :].reshape(2, 1 + B, N_MOD, D)

    fw_in = ffn_w_in
    fw_out = ffn_w_out

    m0 = mods[0]
    h = _ffn((ctx.reshape(R_CTX, D), x.reshape(R_LAT, D)), m0, fw_in, fw_out, sel=(0, 0), mi=0)
    qk, vo, uv, gates = _even_in(h, m0, even_w_in, mlstm_conv[0])
    gate_b = jnp.pad(mlstm_gate_b[0].reshape(1, 4 * HEADS_A), ((0, 0), (0, LANES - 4 * HEADS_A)))
    mnorm_t = jnp.broadcast_to(mlstm_norm[0][:, :, None], (HEADS_A, CHUNK, LANES))
    ha_ctx, ha_lat = _mlstm(qk, vo, gates, gate_b, mnorm_t)
    sbx = jnp.repeat(sgu_b[0].T, LANES, axis=1)
    h = _ffn(h, m0, fw_in, fw_out, sel=(0, 1), mi=6,
             even=(ha_ctx, ha_lat, uv, sgu_norm[0].reshape(1, W_A), sgu_ws[0].astype(bf16), sbx,
                   even_w_out[0].astype(bf16)))

    m1 = mods[1]
    h = _ffn(h, m1, fw_in, fw_out, sel=(1, 0), mi=0)
    cos_t, sin_t = _rope_tables()
    qdim = HEADS_C * DH
    kdim = KV_HEADS * DH
    w_q = odd_w_qkv[0][:, :qdim].reshape(D, KV_HEADS, GROUP, DH // 2, 2).transpose(0, 2, 4, 1, 3).reshape(D, qdim)
    w_k = odd_w_qkv[0][:, qdim:qdim + kdim].reshape(D, KV_HEADS, DH // 2, 2).transpose(0, 3, 1, 2).reshape(D, kdim)
    w_qkv = jnp.concatenate([w_q, w_k, odd_w_qkv[0][:, qdim + kdim:]], axis=1).astype(bf16)
    w_o = odd_w_out[0].reshape(KV_HEADS, GROUP, DH, D).transpose(1, 0, 2, 3).reshape(qdim, D).astype(bf16)
    qkv = _odd_in(h, m1, w_qkv, cos_t, sin_t)
    attn = _attention(qkv, attn_sink[0])
    out = _ffn(h, m1, fw_in, fw_out, sel=(1, 1), mi=6, last=(attn, w_o, final_norm))
    return out.reshape(B, SEQ, D)
```

```python
import functools

import numpy as np
import jax
import jax.numpy as jnp
from jax import lax
from jax.experimental import pallas as pl
from jax.experimental.pallas import tpu as pltpu

f32 = jnp.float32
bf16 = jnp.bfloat16

D = 1024
B = 8
SEQ = 2048
CTX = 256
TOK = CTX + SEQ
GRID_W = 64
N_MOD = 9
D_FF = 2816
EPS = 1e-6
HEADS_A = 4
CHUNK = 128
N_CHUNK = TOK // CHUNK
N_CTX_CHUNK = CTX // CHUNK
W_A = 512
EVEN_COLS = 3200
HEADS_C = 16
KV_HEADS = 4
GROUP = HEADS_C // KV_HEADS
DH = 64
QKV = (HEADS_C + 2 * KV_HEADS) * DH
N_BLK = SEQ // CHUNK
ROPE_BASE = 10000.0
LOG2E = 1.4426950408889634

R_CTX = B * CTX
R_LAT = B * SEQ
R_ALL = R_CTX + R_LAT

LANES = 128
TM_FFN = 512
TM_FFN_WIDE = 1024
TM_PROJ = 512
FC = 256
N_FC = D_FF // FC
MIB = 1024 * 1024


def _dot(a, b):
    return jnp.dot(a, b, preferred_element_type=f32)


def _dot_nt(a, b):
    return lax.dot_general(a, b, (((1,), (1,)), ((), ())), preferred_element_type=f32)


def _dot_tn(a, b):
    return lax.dot_general(a, b, (((0,), (0,)), ((), ())), preferred_element_type=f32)


def _sigmoid(x):
    return 1.0 / (1.0 + jnp.exp(-x))


def _split3(x):
    hi = x.astype(bf16)
    r1 = x - hi.astype(f32)
    mid = r1.astype(bf16)
    lo = (r1 - mid.astype(f32)).astype(bf16)
    return hi, mid, lo


def _modulated(h, shift, scale):
    ms = jnp.mean(h * h, axis=-1, keepdims=True)
    return h * lax.rsqrt(ms + EPS) * (1.0 + scale) + shift


def _mod_kernel(c_ref, w_ref, b_ref, o_ref):
    x = c_ref[...]
    s = x * _sigmoid(x)
    w = w_ref[...]
    s_hi = s.astype(bf16)
    s_lo = (s - s_hi.astype(f32)).astype(bf16)
    w_hi = w.astype(bf16)
    w_lo = (w - w_hi.astype(f32)).astype(bf16)
    o_ref[...] = _dot(s_hi, w_hi) + _dot(s_hi, w_lo) + _dot(s_lo, w_hi) + b_ref[...]


def _modulation(cs, ada_w, ada_b):
    depth = ada_w.shape[0]
    rows = cs.shape[0]
    n_col = N_MOD * D
    tn = 1024
    return pl.pallas_call(
        _mod_kernel,
        grid=(depth, n_col // tn),
        in_specs=[
            pl.BlockSpec((rows, D), lambda l, j: (0, 0)),
            pl.BlockSpec((None, D, tn), lambda l, j: (l, 0, j)),
            pl.BlockSpec((None, 1, tn), lambda l, j: (l, 0, j)),
        ],
        out_specs=pl.BlockSpec((None, rows, tn), lambda l, j: (l, 0, j)),
        out_shape=jax.ShapeDtypeStruct((depth, rows, n_col), f32),
        compiler_params=pltpu.CompilerParams(
            dimension_semantics=("parallel", "parallel"), vmem_limit_bytes=32 * MIB),
        name="modulation",
    )(cs, ada_w, ada_b.reshape(depth, 1, n_col))


def _who_flat(tile, tm):
    n_ctx = R_CTX // tm
    per_b = SEQ // tm
    return jnp.where(tile < n_ctx, 0, 1 + jnp.maximum(tile - n_ctx, 0) // per_b)


W_CHUNKS = 16
W_SLOTS = 4


def _fetch_cast(src, dst, stage, sem, place=None):
    rows = dst.shape[0] // W_CHUNKS

    def piece(c):
        slot = c % W_SLOTS
        return pltpu.make_async_copy(src.at[pl.ds(c * rows, rows), :], stage.at[slot], sem.at[slot])

    for c in range(W_SLOTS - 1):
        piece(c).start()
    for c in range(W_CHUNKS):
        if c + W_SLOTS - 1 < W_CHUNKS:
            piece(c + W_SLOTS - 1).start()
        piece(c).wait()
        if place is None:
            dst[c * rows:(c + 1) * rows, :] = stage[c % W_SLOTS].astype(bf16)
        else:
            place(dst, slice(c * rows, (c + 1) * rows), stage[c % W_SLOTS])


def _gelu_tanh(x):
    return x * (0.5 * (1.0 + jnp.tanh(0.7978845608028654 * (x + 0.044715 * (x * x * x)))))


def _even_mix(ha, uv_ref, sg_ref, ws_ref, sb_ref, wm_ref, hb_scr):
    u = uv_ref[:, 0:W_A].astype(f32)
    v = uv_ref[:, W_A:2 * W_A].astype(f32)
    vn = (v * lax.rsqrt(jnp.mean(v * v, axis=-1, keepdims=True) + EPS) * sg_ref[...]).astype(bf16)
    n_chunk = hb_scr.shape[0] // CHUNK
    for g in range(W_A // LANES):
        cs = slice(g * LANES, (g + 1) * LANES)
        rhs = jnp.concatenate([vn[n * CHUNK:(n + 1) * CHUNK, cs] for n in range(n_chunk)], axis=1)
        mixed = _dot(ws_ref[g], rhs)
        for n in range(n_chunk):
            r = slice(n * CHUNK, (n + 1) * CHUNK)
            hb_scr[r, cs] = (u[r, cs] * (mixed[:, n * LANES:(n + 1) * LANES] + sb_ref[:, cs])).astype(bf16)
    return _dot(ha, wm_ref[0:W_A, :]) + _dot(hb_scr[...], wm_ref[W_A:2 * W_A, :])


def _ffn_kernel(*refs, tm, mi, mixer, final, split, sel):
    refs = list(refs)
    is_ctx = pl.program_id(0) < R_CTX // tm
    if split:
        c_ref, x_ref = refs[0:2]
        refs = refs[2:]
        read_h = lambda: jnp.where(is_ctx, c_ref[...], x_ref[...])
    else:
        h_ref = refs.pop(0)
        read_h = lambda: h_ref[...]
    wi_ref, wo_ref, wi_stage, wo_stage, wi_sem, wo_sem = refs[-6:]
    refs = refs[:-6]
    if mixer == "attn":
        a_ref, wm_ref = refs[0:2]
        refs = refs[2:]
    elif mixer == "even":
        hac_ref, hax_ref, uv_ref, sg_ref, ws_ref, sb_ref, wm_ref = refs[0:7]
        refs = refs[7:]
    mod_ref, wi_hbm, wo_hbm = refs[0:3]
    refs = refs[3:]
    if final:
        fn_ref = refs.pop(0)
    o_ref, n_scr, acc_scr = refs[0:3]
    refs = refs[3:]

    @pl.when(pl.program_id(0) == 0)
    def _():
        _fetch_cast(wi_hbm.at[sel[0], sel[1]], wi_ref, wi_stage, wi_sem)
        _fetch_cast(wo_hbm.at[sel[0], sel[1]], wo_ref, wo_stage, wo_sem)

    if mixer is not None:
        h_scr = refs.pop(0)
        if mixer == "attn":
            y = _dot(a_ref[...], wm_ref[...])
        else:
            ha = jnp.where(is_ctx, hac_ref[...], hax_ref[...])
            y = _even_mix(ha, uv_ref, sg_ref, ws_ref, sb_ref, wm_ref, refs.pop(0))
        h_scr[...] = read_h() + mod_ref[5:6, :] * y
        read_h = lambda: h_scr[...]
    n_scr[...] = _modulated(read_h(), mod_ref[mi:mi + 1, :], mod_ref[mi + 1:mi + 2, :]).astype(bf16)
    for j in range(N_FC):
        n = n_scr[...]
        g = _dot(n, wi_ref[:, j * FC:(j + 1) * FC])
        u = _dot(n, wi_ref[:, D_FF + j * FC:D_FF + (j + 1) * FC])
        a = (g * _sigmoid(g) * u).astype(bf16)
        y = _dot(a, wo_ref[j * FC:(j + 1) * FC, :])
        if j == 0:
            acc_scr[...] = y
        else:
            acc_scr[...] += y
    out = read_h() + (0.5 * mod_ref[mi + 2:mi + 3, :]) * acc_scr[...]
    if final:
        ms = jnp.mean(out * out, axis=-1, keepdims=True)
        out = out * lax.rsqrt(ms + EPS) * fn_ref[...]
    o_ref[...] = out


def _ffn(h, mods, w_in, w_out, *, sel, mi, even=None, last=None, tm=TM_FFN):
    n_ctx = R_CTX // tm
    tile0 = n_ctx if last is not None else 0
    split = isinstance(h, tuple)
    const2 = lambda i: (0, 0)
    ctx_map = lambda i: (jnp.minimum(i, n_ctx - 1), 0)
    lat_map = lambda i: (jnp.maximum(i - n_ctx, 0), 0)
    if split:
        rows_out = R_ALL
        in_specs = [pl.BlockSpec((tm, D), ctx_map), pl.BlockSpec((tm, D), lat_map)]
        args = list(h)
    else:
        rows_out = h.shape[0] - tile0 * tm
        in_specs = [pl.BlockSpec((tm, D), lambda i: (i + tile0, 0))]
        args = [h]
    scratch = [pltpu.VMEM((tm, D), bf16), pltpu.VMEM((tm, D), f32)]
    mixer = None
    if last is not None:
        mixer = "attn"
        attn, w_attn, final_norm = last
        in_specs += [
            pl.BlockSpec((tm, D), lambda i: (i, 0)),
            pl.BlockSpec((D, D), const2, pipeline_mode=pl.Buffered(1)),
        ]
        args += [attn, w_attn]
        scratch.append(pltpu.VMEM((tm, D), f32))
    elif even is not None:
        mixer = "even"
        in_specs += [
            pl.BlockSpec((tm, W_A), ctx_map),
            pl.BlockSpec((tm, W_A), lat_map),
            pl.BlockSpec((tm, 2 * W_A), lambda i: (i, 0)),
            pl.BlockSpec((1, W_A), const2),
            pl.BlockSpec((W_A // LANES, CHUNK, CHUNK), lambda i: (0, 0, 0)),
            pl.BlockSpec((CHUNK, W_A), const2),
            pl.BlockSpec((2 * W_A, D), const2, pipeline_mode=pl.Buffered(1)),
        ]
        args += list(even)
        scratch += [pltpu.VMEM((tm, D), f32), pltpu.VMEM((tm, W_A), bf16)]
    in_specs += [
        pl.BlockSpec((None, N_MOD, D), lambda i: (_who_flat(i + tile0, tm), 0, 0)),
        pl.BlockSpec(memory_space=pl.ANY),
        pl.BlockSpec(memory_space=pl.ANY),
    ]
    args += [mods, w_in, w_out]
    if last is not None:
        in_specs.append(pl.BlockSpec((1, D), const2))
        args.append(final_norm.reshape(1, D))
    scratch += [
        pltpu.VMEM((D, 2 * D_FF), bf16),
        pltpu.VMEM((D_FF, D), bf16),
        pltpu.VMEM((W_SLOTS, D // W_CHUNKS, 2 * D_FF), f32),
        pltpu.VMEM((W_SLOTS, D_FF // W_CHUNKS, D), f32),
        pltpu.SemaphoreType.DMA((W_SLOTS,)),
        pltpu.SemaphoreType.DMA((W_SLOTS,)),
    ]
    return pl.pallas_call(
        functools.partial(_ffn_kernel, tm=tm, mi=mi, mixer=mixer, final=last is not None, split=split, sel=sel),
        grid=(rows_out // tm,),
        in_specs=in_specs,
        out_specs=pl.BlockSpec((tm, D), lambda i: (i, 0)),
        out_shape=jax.ShapeDtypeStruct((rows_out, D), f32),
        scratch_shapes=scratch,
        compiler_params=pltpu.CompilerParams(
            dimension_semantics=("arbitrary",), vmem_limit_bytes=56 * MIB),
        name={None: "ffn", "even": "ffn_even", "attn": "ffn_final"}[mixer],
    )(*args)


HALO = 8


def _place_even_w(dst, rows, piece):
    g0 = 4 * W_A
    g1 = g0 + 4 * HEADS_A
    dst[rows, 0:g0] = piece[:, 0:g0].astype(bf16)
    dst[rows, g0:g0 + 2 * W_A] = piece[:, g1:g1 + 2 * W_A].astype(bf16)
    lane = lax.broadcasted_iota(jnp.int32, (piece.shape[0], LANES), 1)
    dst[rows, g0 + 2 * W_A:EVEN_COLS] = jnp.where(lane < 4 * HEADS_A, piece[:, g0:g0 + LANES], 0.0).astype(bf16)


def _even_in_kernel(h_ref, hp_ref, hn_ref, mod_ref, w_hbm, cw_ref, qk_ref, vo_ref, uv_ref, g_ref,
                    w_ref, w_stage, w_sem):
    tile = pl.program_id(0)

    @pl.when(tile == 0)
    def _():
        _fetch_cast(w_hbm.at[0], w_ref, w_stage, w_sem, place=_place_even_w)

    tm = TM_PROJ
    n_chunk = tm // CHUNK
    shift, scale = mod_ref[3:4, :], mod_ref[4:5, :]
    n = _modulated(h_ref[...], shift, scale)
    halo = _modulated(jnp.concatenate([hp_ref[...], hn_ref[...]], axis=0), shift, scale)
    n_ext = jnp.concatenate([halo[0:HALO], n, halo[HALO:2 * HALO]], axis=0).astype(bf16)
    n = n.astype(bf16)

    row = lax.broadcasted_iota(jnp.int32, (tm, LANES), 0)
    is_ctx = tile < R_CTX // tm
    pos = jnp.where(is_ctx, row % CTX, row + (jnp.maximum(tile - R_CTX // tm, 0) % (SEQ // tm)) * tm)
    seq_start = pos == 0
    seq_end = pos == jnp.where(is_ctx, CTX - 1, SEQ - 1)

    p = _dot(n_ext, w_ref[:, 0:2 * W_A])
    cur = p[HALO:HALO + tm]
    prv = pltpu.roll(p, 1, 0)[HALO:HALO + tm]
    nxt = pltpu.roll(p, tm + 2 * HALO - 1, 0)[HALO:HALO + tm]
    for cb in range(2 * HEADS_A):
        cs = slice(cb * LANES, (cb + 1) * LANES)
        y = (cw_ref[0:1, cs] * jnp.where(seq_start, 0.0, prv[:, cs]) + cw_ref[1:2, cs] * cur[:, cs]
             + cw_ref[2:3, cs] * jnp.where(seq_end, 0.0, nxt[:, cs]))
        y = y * _sigmoid(y)
        if cb < HEADS_A:
            for c in range(n_chunk):
                r = slice(c * CHUNK, (c + 1) * CHUNK)
                qk_ref[r, cs] = y[r, :].T.astype(bf16)
        else:
            qk_ref[:, cs] = (y * CHUNK ** -0.5).astype(bf16)

    v = _dot(n, w_ref[:, 2 * W_A:3 * W_A])
    for hd in range(HEADS_A):
        cs = slice(hd * LANES, (hd + 1) * LANES)
        for c in range(n_chunk):
            r = slice(c * CHUNK, (c + 1) * CHUNK)
            vo_ref[r, cs] = v[r, cs].T.astype(bf16)
    vo_ref[:, W_A:2 * W_A] = _sigmoid(_dot(n, w_ref[:, 3 * W_A:4 * W_A])).astype(bf16)
    uv_ref[...] = _gelu_tanh(_dot(n, w_ref[:, 2048:3072])).astype(bf16)
    g_ref[...] = _dot(n, w_ref[:, 3072:3200])


def _even_in(h, mods, w, conv_w):
    tm = TM_PROJ
    out_map = lambda i: (i, 0)
    halo_blocks = tm // HALO
    last_halo = R_ALL // HALO - 1
    return pl.pallas_call(
        _even_in_kernel,
        grid=(R_ALL // tm,),
        in_specs=[
            pl.BlockSpec((tm, D), lambda i: (i, 0)),
            pl.BlockSpec((HALO, D), lambda i: (jnp.maximum(i * halo_blocks - 1, 0), 0)),
            pl.BlockSpec((HALO, D), lambda i: (jnp.minimum((i + 1) * halo_blocks, last_halo), 0)),
            pl.BlockSpec((None, N_MOD, D), lambda i: (_who_flat(i, tm), 0, 0)),
            pl.BlockSpec(memory_space=pl.ANY),
            pl.BlockSpec((3, 2 * W_A), lambda i: (0, 0)),
        ],
        out_specs=[
            pl.BlockSpec((tm, 1024), out_map),
            pl.BlockSpec((tm, 1024), out_map),
            pl.BlockSpec((tm, 1024), out_map),
            pl.BlockSpec((tm, LANES), out_map),
        ],
        out_shape=[
            jax.ShapeDtypeStruct((R_ALL, 1024), bf16),
            jax.ShapeDtypeStruct((R_ALL, 1024), bf16),
            jax.ShapeDtypeStruct((R_ALL, 1024), bf16),
            jax.ShapeDtypeStruct((R_ALL, LANES), f32),
        ],
        scratch_shapes=[
            pltpu.VMEM((D, EVEN_COLS), bf16),
            pltpu.VMEM((W_SLOTS, D // W_CHUNKS, w.shape[-1]), f32),
            pltpu.SemaphoreType.DMA((W_SLOTS,)),
        ],
        compiler_params=pltpu.CompilerParams(
            dimension_semantics=("arbitrary",), vmem_limit_bytes=40 * MIB),
        name="even_in",
    )(h, h, h, mods, w, conv_w)


N_AUG = CHUNK + 16
CHUNKS_PER_ITER = 18


def _chunk_loop(body):
    def group(i, carry):
        for u in range(CHUNKS_PER_ITER):
            carry = body(i * CHUNKS_PER_ITER + u, carry)
        return carry
    lax.fori_loop(0, N_CHUNK // CHUNKS_PER_ITER, group, 0)


def _mlstm_kernel(qc_ref, ql_ref, kc_ref, kl_ref, vc_ref, vl_ref, oc_ref, ol_ref, gc_ref, gl_ref,
                  gb_ref, mn_ref, outc_ref, outl_ref,
                  ks, qts, vts, og, hg, xc, xr, cl, st, cst, mst):
    head = pl.program_id(1)
    rowi = lax.broadcasted_iota(jnp.int32, (CHUNK, CHUNK), 0)
    coli = lax.broadcasted_iota(jnp.int32, (CHUNK, CHUNK), 1)
    lower = coli <= rowi
    upper = coli >= rowi
    tri = jnp.where(lower, 1.0, 0.0).astype(bf16)

    def part(ctx_ref, lat_ref, c):
        return (ctx_ref, c * CHUNK) if c < N_CTX_CHUNK else (lat_ref, (c - N_CTX_CHUNK) * CHUNK)

    for c in range(N_CHUNK):
        lo = c * CHUNK
        for dst, refs in ((qts, (qc_ref, ql_ref)), (ks, (kc_ref, kl_ref)), (vts, (vc_ref, vl_ref)),
                          (og, (oc_ref, ol_ref))):
            src, at = part(*refs, c)
            dst[lo:lo + CHUNK, :] = src[at:at + CHUNK, :]

    @pl.when(head == 0)
    def _():
        kind = (coli // HEADS_A) % 4
        for c in range(N_CHUNK):
            lo = c * CHUNK
            src, at = part(gc_ref, gl_ref, c)
            gt = src[at:at + CHUNK, :] + gb_ref[...]
            lf = jnp.minimum(gt, 0.0) - jnp.log1p(jnp.exp(-jnp.abs(gt)))
            hi, mid, lw = _split3(lf)
            pre = _dot(tri, hi) + _dot(tri, mid) + _dot(tri, lw)
            suf = pre[CHUNK - 1:CHUNK, :] - pre + lf
            x = jnp.where(kind == 1, pre, jnp.where(kind == 3, suf, gt))
            xr[16 * c:16 * c + 16, :] = x.T[0:16, :]
            xc[lo:lo + CHUNK, :] = x

    lane_shift = (LANES - head) % LANES

    def gate_rows(c, d):
        row = c * 16 + 8 * d + head
        return xr[pl.ds(row, 1), :], xr[pl.ds(row + HEADS_A, 1), :]

    def local_state(c, carry):
        lo = pl.multiple_of(c * CHUNK, CHUNK)
        k = ks[pl.ds(lo, CHUNK), :]
        v_t = vts[pl.ds(lo, CHUNK), :].astype(f32)
        lhs = []
        for d in (0, 1):
            ig_r, b_r = gate_rows(c, d)
            g = b_r[:, CHUNK - 1:CHUNK] if d == 0 else b_r[:, 0:1]
            a_r = g - b_r + ig_r
            m_loc = jnp.max(a_r, axis=1, keepdims=True)
            w_r = jnp.exp(a_r - m_loc)
            lhs += [v_t * w_r, jnp.broadcast_to(w_r, (16, LANES))]
            st[d * N_CHUNK + c, 0:1, :] = jnp.broadcast_to(m_loc, (1, LANES))
            st[d * N_CHUNK + c, 1:2, :] = jnp.broadcast_to(g, (1, LANES))
        both = _dot(jnp.concatenate(lhs, axis=0).astype(bf16), k)
        cl[c] = both[0:N_AUG, :]
        cl[N_CHUNK + c] = both[N_AUG:2 * N_AUG, :]
        return carry

    _chunk_loop(local_state)

    cst[...] = jnp.zeros(cst.shape, f32)
    mst[...] = jnp.zeros(mst.shape, f32)

    def scan_step(i, carry):
        for d in (0, 1):
            c = i if d == 0 else jnp.where(i < N_CTX_CHUNK, N_CTX_CHUNK - 1 - i, N_CHUNK + N_CTX_CHUNK - 1 - i)
            idx = d * N_CHUNK + c
            c_loc = cl[idx]
            m_loc = st[idx, 0:1, :]
            g = st[idx, 1:2, :]
            c_prev = cst[d]
            m_prev = mst[d, 0:1, :]
            m_new = jnp.maximum(g + m_prev, m_loc)
            dec = jnp.exp(g + m_prev - m_new)
            add = jnp.exp(m_loc - m_new)
            cl[idx] = c_prev
            st[idx, 2:3, :] = m_prev
            cst[d] = dec * c_prev + add * c_loc
            mst[d, 0:1, :] = m_new
        return carry

    lax.fori_loop(0, N_CHUNK, scan_step, 0)

    def outputs(c, carry):
        lo = pl.multiple_of(c * CHUNK, CHUNK)
        k = ks[pl.ds(lo, CHUNK), :]
        q_t = qts[pl.ds(lo, CHUNK), :]
        v_aug = jnp.concatenate([vts[pl.ds(lo, CHUNK), :], jnp.ones((16, LANES), bf16)], axis=0)
        q_f = q_t.astype(f32)
        s_t = _dot(k, q_t)
        x_c = pltpu.roll(xc[pl.ds(lo, CHUNK), :], lane_shift, 1)
        hs = None
        for d in (0, 1):
            _, b_r = gate_rows(c, d)
            idx = d * N_CHUNK + c
            r_c = x_c[:, 8 * d:8 * d + 1] - x_c[:, 8 * d + 4:8 * d + 5]
            dm = jnp.where(upper if d == 0 else lower, b_r + r_c, -jnp.inf)
            e_r = b_r + st[idx, 2:3, :]
            m_t = jnp.maximum(e_r, jnp.max(dm, axis=0, keepdims=True))
            p_t = s_t * jnp.exp(dm - m_t)
            inter = jnp.exp(e_r - m_t)
            lhs = jnp.concatenate([v_aug, cl[idx].astype(bf16)], axis=1)
            rhs = jnp.concatenate([p_t, q_f * inter], axis=0).astype(bf16)
            nd = _dot(lhs, rhs)
            den = nd[CHUNK:CHUNK + 1, :]
            h_d = nd[0:CHUNK, :] * (1.0 / jnp.maximum(jnp.abs(den), jnp.exp(-m_t)))
            hs = h_d if hs is None else hs + h_d
        hn = hs * lax.rsqrt(jnp.mean(hs * hs, axis=0, keepdims=True) + EPS) * mn_ref[...]
        hg[pl.ds(lo, CHUNK), :] = (og[pl.ds(lo, CHUNK), :].astype(f32) * hn.T).astype(bf16)
        return carry

    _chunk_loop(outputs)
    outc_ref[...] = hg[0:CTX, :]
    outl_ref[...] = hg[CTX:TOK, :]


def _mlstm(qk, vo, gates, gate_b, mnorm):
    lat0 = R_CTX // SEQ
    ctx = lambda col0: pl.BlockSpec((CTX, LANES), lambda b, h: (b, col0 + h))
    lat = lambda col0: pl.BlockSpec((SEQ, LANES), lambda b, h: (lat0 + b, col0 + h))
    return pl.pallas_call(
        _mlstm_kernel,
        grid=(B, HEADS_A),
        in_specs=[
            ctx(0), lat(0), ctx(HEADS_A), lat(HEADS_A),
            ctx(0), lat(0), ctx(HEADS_A), lat(HEADS_A),
            pl.BlockSpec((CTX, LANES), lambda b, h: (b, 0)),
            pl.BlockSpec((SEQ, LANES), lambda b, h: (lat0 + b, 0)),
            pl.BlockSpec((1, LANES), lambda b, h: (0, 0)),
            pl.BlockSpec((None, CHUNK, LANES), lambda b, h: (h, 0, 0)),
        ],
        out_specs=[
            pl.BlockSpec((CTX, LANES), lambda b, h: (b, h)),
            pl.BlockSpec((SEQ, LANES), lambda b, h: (b, h)),
        ],
        out_shape=[
            jax.ShapeDtypeStruct((R_CTX, W_A), bf16),
            jax.ShapeDtypeStruct((R_LAT, W_A), bf16),
        ],
        scratch_shapes=[
            pltpu.VMEM((TOK, LANES), bf16),
            pltpu.VMEM((TOK, LANES), bf16),
            pltpu.VMEM((TOK, LANES), bf16),
            pltpu.VMEM((TOK, LANES), bf16),
            pltpu.VMEM((TOK, LANES), bf16),
            pltpu.VMEM((TOK, LANES), f32),
            pltpu.VMEM((16 * N_CHUNK, LANES), f32),
            pltpu.VMEM((2 * N_CHUNK, CHUNK + 16, LANES), f32),
            pltpu.VMEM((2 * N_CHUNK, 8, LANES), f32),
            pltpu.VMEM((2, CHUNK + 16, LANES), f32),
            pltpu.VMEM((2, 8, LANES), f32),
        ],
        compiler_params=pltpu.CompilerParams(
            dimension_semantics=("parallel", "arbitrary"), vmem_limit_bytes=40 * MIB),
        name="mlstm",
    )(qk, qk, qk, qk, vo, vo, vo, vo, gates, gates, gate_b, mnorm)


def _odd_in_kernel(h_ref, mod_ref, w_ref, cos_ref, sin_ref, o_ref):
    n = _modulated(h_ref[...], mod_ref[3:4, :], mod_ref[4:5, :]).astype(bf16)
    cos = cos_ref[...]
    sin = sin_ref[...]
    n_rot = (HEADS_C + KV_HEADS) * DH // LANES
    y = _dot(n, w_ref[:, 0:n_rot * LANES])
    for c in range(0, n_rot, 2):
        x1 = y[:, c * LANES:(c + 1) * LANES]
        x2 = y[:, (c + 1) * LANES:(c + 2) * LANES]
        r1 = x1 * cos - x2 * sin
        r2 = x1 * sin + x2 * cos
        if c < HEADS_C * DH // LANES:
            r1 = r1 * (DH ** -0.5 * LOG2E)
            r2 = r2 * (DH ** -0.5 * LOG2E)
        o_ref[:, c * LANES:(c + 1) * LANES] = r1.astype(bf16)
        o_ref[:, (c + 1) * LANES:(c + 2) * LANES] = r2.astype(bf16)
    v0 = n_rot * LANES
    o_ref[:, v0:QKV] = _dot(n, w_ref[:, v0:QKV]).astype(bf16)


def _odd_in(h, mods, w, cos_t, sin_t):
    tm = TM_PROJ
    n_ctx = R_CTX // tm
    per_b = SEQ // tm
    rope_map = lambda i: (jnp.where(i < n_ctx, 0, 1 + jnp.maximum(i - n_ctx, 0) % per_b), 0)
    return pl.pallas_call(
        _odd_in_kernel,
        grid=(R_ALL // tm,),
        in_specs=[
            pl.BlockSpec((tm, D), lambda i: (i, 0)),
            pl.BlockSpec((None, N_MOD, D), lambda i: (_who_flat(i, tm), 0, 0)),
            pl.BlockSpec((D, QKV), lambda i: (0, 0), pipeline_mode=pl.Buffered(1)),
            pl.BlockSpec((tm, LANES), rope_map),
            pl.BlockSpec((tm, LANES), rope_map),
        ],
        out_specs=pl.BlockSpec((tm, QKV), lambda i: (i, 0)),
        out_shape=jax.ShapeDtypeStruct((R_ALL, QKV), bf16),
        compiler_params=pltpu.CompilerParams(
            dimension_semantics=("parallel",), vmem_limit_bytes=32 * MIB),
        name="odd_in",
    )(h, mods, w, cos_t, sin_t)


BLOCKS_PER_ITER = 2


def _attn_kernel(sink_ref, q_ref, kc_ref, kl_ref, vc_ref, vl_ref, o_ref,
                 k_all, vt_all, q_all, s_all, p_all, ot_all):
    kv_w = KV_HEADS * DH
    key = lax.broadcasted_iota(jnp.int32, (CHUNK, CHUNK), 0)
    qry = lax.broadcasted_iota(jnp.int32, (CHUNK, CHUNK), 1)
    far = 1 << 20
    lane_head = lax.broadcasted_iota(jnp.int32, (CHUNK, kv_w), 1) % LANES // (DH // 2)
    neg = -1e30
    n_key = CTX + 3 * CHUNK
    n_slab = n_key // CHUNK

    def stage_keys(k_scr, vt_scr, k_rows, v_rows, row0):
        rows = k_rows.shape[0]
        k_scr[row0:row0 + rows, :] = k_rows
        v_t = v_rows.astype(f32).T.astype(bf16)
        for j in range(KV_HEADS):
            vt_scr[j, 0:DH, row0:row0 + rows] = v_t[j * DH:(j + 1) * DH, :]

    for u in range(BLOCKS_PER_ITER):
        stage_keys(k_all.at[u], vt_all.at[u], kc_ref[...], vc_ref[...], 0)
        for j in range(KV_HEADS):
            vt_all[u, j, DH:DH + 16, :] = jnp.ones((16, n_key), bf16)

    def block(blk, u):
        k_scr, vt_scr, q_scr, s_scr, p_scr, ot_scr = (r.at[u] for r in (k_all, vt_all, q_all, s_all, p_all, ot_all))
        q0 = pl.multiple_of(blk * CHUNK, CHUNK)
        prev_ok = key >= qry + jnp.where(blk > 0, 0, far)
        next_ok = key <= qry - jnp.where(blk < N_BLK - 1, 0, far)
        for n, off in enumerate((-1, 0, 1)):
            src = pl.multiple_of(jnp.clip(blk + off, 0, N_BLK - 1) * CHUNK, CHUNK)
            stage_keys(k_scr, vt_scr, kl_ref[pl.ds(src, CHUNK), :], vl_ref[pl.ds(src, CHUNK), :], CTX + n * CHUNK)

        for j in range(KV_HEADS):
            keep = jnp.where(lane_head == j, 1.0, 0.0).astype(bf16)
            for g in range(GROUP):
                q_scr[j, g * CHUNK:(g + 1) * CHUNK, :] = q_ref[pl.ds(q0, CHUNK), g * kv_w:(g + 1) * kv_w] * keep
            s_scr[j] = _dot_nt(k_scr[...], q_scr[j])

        def scores(j, g, slab):
            s = s_scr[j, slab * CHUNK:(slab + 1) * CHUNK, g * CHUNK:(g + 1) * CHUNK]
            if slab == 2:
                s = jnp.where(prev_ok, s, neg)
            if slab == 4:
                s = jnp.where(next_ok, s, neg)
            return s

        for j in range(KV_HEADS):
            sink_terms = []
            for g in range(GROUP):
                sink = jnp.full((1, CHUNK), sink_ref[j * GROUP + g] * LOG2E, f32)
                m8 = None
                for slab in range(n_slab):
                    part = jnp.max(scores(j, g, slab).reshape(CHUNK // 8, 8, CHUNK), axis=0)
                    m8 = part if m8 is None else jnp.maximum(m8, part)
                m = jnp.maximum(sink, jnp.max(m8, axis=0, keepdims=True))
                for slab in range(n_slab):
                    p = jnp.exp2(scores(j, g, slab) - m)
                    p_scr[j, slab * CHUNK:(slab + 1) * CHUNK, g * CHUNK:(g + 1) * CHUNK] = p.astype(bf16)
                sink_terms.append(jnp.exp2(sink - m))
            acc = _dot(vt_scr[j], p_scr[j])
            den = acc[DH:DH + 1, :] + jnp.concatenate(sink_terms, axis=1)
            ot_scr[j * DH:(j + 1) * DH, :] = acc[0:DH, :] * (1.0 / den)
        for g in range(GROUP):
            o_ref[pl.ds(q0, CHUNK), g * kv_w:(g + 1) * kv_w] = (
                ot_scr[:, g * CHUNK:(g + 1) * CHUNK].T.astype(bf16))

    def blocks(i, carry):
        for u in range(BLOCKS_PER_ITER):
            block(i * BLOCKS_PER_ITER + u, u)
        return carry

    lax.fori_loop(0, N_BLK // BLOCKS_PER_ITER, blocks, 0)


def _attention(qkv, sink):
    kv_w = KV_HEADS * DH
    n_key = CTX + 3 * CHUNK
    k_col = HEADS_C * DH // kv_w
    v_col = k_col + 1
    lat0 = R_CTX // SEQ
    ctx = lambda col: pl.BlockSpec((CTX, kv_w), lambda b: (b, col))
    lat = lambda col: pl.BlockSpec((SEQ, kv_w), lambda b: (lat0 + b, col))
    return pl.pallas_call(
        _attn_kernel,
        grid=(B,),
        in_specs=[
            pl.BlockSpec(memory_space=pltpu.SMEM),
            pl.BlockSpec((SEQ, HEADS_C * DH), lambda b: (lat0 + b, 0)),
            ctx(k_col), lat(k_col), ctx(v_col), lat(v_col),
        ],
        out_specs=pl.BlockSpec((SEQ, HEADS_C * DH), lambda b: (b, 0)),
        out_shape=jax.ShapeDtypeStruct((R_LAT, HEADS_C * DH), bf16),
        scratch_shapes=[
            pltpu.VMEM((BLOCKS_PER_ITER, n_key, kv_w), bf16),
            pltpu.VMEM((BLOCKS_PER_ITER, KV_HEADS, DH + 16, n_key), bf16),
            pltpu.VMEM((BLOCKS_PER_ITER, KV_HEADS, GROUP * CHUNK, kv_w), bf16),
            pltpu.VMEM((BLOCKS_PER_ITER, KV_HEADS, n_key, GROUP * CHUNK), f32),
            pltpu.VMEM((BLOCKS_PER_ITER, KV_HEADS, n_key, GROUP * CHUNK), bf16),
            pltpu.VMEM((BLOCKS_PER_ITER, kv_w, GROUP * CHUNK), f32),
        ],
        compiler_params=pltpu.CompilerParams(
            dimension_semantics=("parallel",), vmem_limit_bytes=48 * MIB),
        name="window_attention",
    )(sink, qkv, qkv, qkv, qkv, qkv)


def _rope_tables():
    rows = SEQ // GRID_W
    row, col = np.meshgrid(np.arange(rows), np.arange(GRID_W), indexing='ij')
    n_freq = DH // 4
    inv = (np.float32(ROPE_BASE) ** (-np.arange(n_freq, dtype=np.float32) / np.float32(n_freq))).astype(np.float32)
    ang = np.concatenate([row.reshape(-1, 1).astype(np.float32) * inv,
                          col.reshape(-1, 1).astype(np.float32) * inv], axis=-1)
    reps = 2 * LANES // DH
    cos = np.tile(np.cos(ang).astype(np.float32), (1, reps))
    sin = np.tile(np.sin(ang).astype(np.float32), (1, reps))
    cos = np.concatenate([np.ones((TM_PROJ, LANES), np.float32), cos], axis=0)
    sin = np.concatenate([np.zeros((TM_PROJ, LANES), np.float32), sin], axis=0)
    return jnp.asarray(cos), jnp.asarray(sin)


def kernel(x, c, ctx, c_ctx, ada_w, ada_b, ffn_w_in, ffn_w_out, even_w_in, even_w_out, mlstm_conv,
           mlstm_gate_b, mlstm_norm, sgu_norm, sgu_ws, sgu_b, odd_w_qkv, odd_w_out, attn_sink, final_norm):
    cs = jnp.concatenate([c_ctx[None, :], c, jnp.zeros((16 - 1 - B, D), f32)], axis=0)
    mods = _modulation(cs, ada_w, ada_b)[:, :1 + B, :].reshape(2, 1 + B, N_MOD, D)

    fw_in = ffn_w_in
    fw_out = ffn_w_out

    m0 = mods[0]
    h = _ffn((ctx.reshape(R_CTX, D), x.reshape(R_LAT, D)), m0, fw_in, fw_out, sel=(0, 0), mi=0)
    qk, vo, uv, gates = _even_in(h, m0, even_w_in, mlstm_conv[0])
    gate_b = jnp.pad(mlstm_gate_b[0].reshape(1, 4 * HEADS_A), ((0, 0), (0, LANES - 4 * HEADS_A)))
    mnorm_t = jnp.broadcast_to(mlstm_norm[0][:, :, None], (HEADS_A, CHUNK, LANES))
    ha_ctx, ha_lat = _mlstm(qk, vo, gates, gate_b, mnorm_t)
    sbx = jnp.repeat(sgu_b[0].T, LANES, axis=1)
    h = _ffn(h, m0, fw_in, fw_out, sel=(0, 1), mi=6,
             even=(ha_ctx, ha_lat, uv, sgu_norm[0].reshape(1, W_A), sgu_ws[0].astype(bf16), sbx,
                   even_w_out[0].astype(bf16)))

    m1 = mods[1]
    h = _ffn(h, m1, fw_in, fw_out, sel=(1, 0), mi=0, tm=TM_FFN_WIDE)
    cos_t, sin_t = _rope_tables()
    qdim = HEADS_C * DH
    kdim = KV_HEADS * DH
    w_q = odd_w_qkv[0][:, :qdim].reshape(D, KV_HEADS, GROUP, DH // 2, 2).transpose(0, 2, 4, 1, 3).reshape(D, qdim)
    w_k = odd_w_qkv[0][:, qdim:qdim + kdim].reshape(D, KV_HEADS, DH // 2, 2).transpose(0, 3, 1, 2).reshape(D, kdim)
    w_qkv = jnp.concatenate([w_q, w_k, odd_w_qkv[0][:, qdim + kdim:]], axis=1).astype(bf16)
    w_o = odd_w_out[0].reshape(KV_HEADS, GROUP, DH, D).transpose(1, 0, 2, 3).reshape(qdim, D).astype(bf16)
    qkv = _odd_in(h, m1, w_qkv, cos_t, sin_t)
    attn = _attention(qkv, attn_sink[0])
    out = _ffn(h, m1, fw_in, fw_out, sel=(1, 1), mi=6, last=(attn, w_o, final_norm))
    return out.reshape(B, SEQ, D)
```

```python
import functools

import numpy as np
import jax
import jax.numpy as jnp
from jax import lax
from jax.experimental import pallas as pl
from jax.experimental.pallas import tpu as pltpu

f32 = jnp.float32
bf16 = jnp.bfloat16

D = 1024
B = 8
SEQ = 2048
CTX = 256
TOK = CTX + SEQ
GRID_W = 64
N_MOD = 9
D_FF = 2816
EPS = 1e-6
HEADS_A = 4
CHUNK = 128
N_CHUNK = TOK // CHUNK
N_CTX_CHUNK = CTX // CHUNK
W_A = 512
EVEN_COLS = 3200
HEADS_C = 16
KV_HEADS = 4
GROUP = HEADS_C // KV_HEADS
DH = 64
QKV = (HEADS_C + 2 * KV_HEADS) * DH
N_BLK = SEQ // CHUNK
ROPE_BASE = 10000.0
LOG2E = 1.4426950408889634

R_CTX = B * CTX
R_LAT = B * SEQ
R_ALL = R_CTX + R_LAT

LANES = 128
TM_FFN = 512
TM_FFN_WIDE = 1024
TM_PROJ = 512
TM_ODD_IN = 1024
FC = 256
N_FC = D_FF // FC
MIB = 1024 * 1024


def _dot(a, b):
    return jnp.dot(a, b, preferred_element_type=f32)


def _dot_nt(a, b):
    return lax.dot_general(a, b, (((1,), (1,)), ((), ())), preferred_element_type=f32)


def _dot_tn(a, b):
    return lax.dot_general(a, b, (((0,), (0,)), ((), ())), preferred_element_type=f32)


def _sigmoid(x):
    return 1.0 / (1.0 + jnp.exp(-x))


def _split3(x):
    hi = x.astype(bf16)
    r1 = x - hi.astype(f32)
    mid = r1.astype(bf16)
    lo = (r1 - mid.astype(f32)).astype(bf16)
    return hi, mid, lo


def _modulated(h, shift, scale):
    ms = jnp.mean(h * h, axis=-1, keepdims=True)
    return h * lax.rsqrt(ms + EPS) * (1.0 + scale) + shift


def _mod_kernel(c_ref, w_ref, b_ref, o_ref):
    x = c_ref[...]
    s = x * _sigmoid(x)
    w = w_ref[...]
    s_hi = s.astype(bf16)
    s_lo = (s - s_hi.astype(f32)).astype(bf16)
    w_hi = w.astype(bf16)
    w_lo = (w - w_hi.astype(f32)).astype(bf16)
    o_ref[...] = _dot(s_hi, w_hi) + _dot(s_hi, w_lo) + _dot(s_lo, w_hi) + b_ref[...]


def _modulation(cs, ada_w, ada_b):
    depth = ada_w.shape[0]
    rows = cs.shape[0]
    n_col = N_MOD * D
    tn = 1024
    return pl.pallas_call(
        _mod_kernel,
        grid=(depth, n_col // tn),
        in_specs=[
            pl.BlockSpec((rows, D), lambda l, j: (0, 0)),
            pl.BlockSpec((None, D, tn), lambda l, j: (l, 0, j)),
            pl.BlockSpec((None, 1, tn), lambda l, j: (l, 0, j)),
        ],
        out_specs=pl.BlockSpec((None, rows, tn), lambda l, j: (l, 0, j)),
        out_shape=jax.ShapeDtypeStruct((depth, rows, n_col), f32),
        compiler_params=pltpu.CompilerParams(
            dimension_semantics=("parallel", "parallel"), vmem_limit_bytes=32 * MIB),
        name="modulation",
    )(cs, ada_w, ada_b.reshape(depth, 1, n_col))


def _who_flat(tile, tm):
    n_ctx = R_CTX // tm
    per_b = SEQ // tm
    return jnp.where(tile < n_ctx, 0, 1 + jnp.maximum(tile - n_ctx, 0) // per_b)


W_CHUNKS = 16
W_SLOTS = 4
FFN_SLOTS = 3


def _fetch_cast(src, dst, stage, sem, place=None):
    rows = dst.shape[0] // W_CHUNKS

    def piece(c):
        slot = c % W_SLOTS
        return pltpu.make_async_copy(src.at[pl.ds(c * rows, rows), :], stage.at[slot], sem.at[slot])

    for c in range(W_SLOTS - 1):
        piece(c).start()
    for c in range(W_CHUNKS):
        if c + W_SLOTS - 1 < W_CHUNKS:
            piece(c + W_SLOTS - 1).start()
        piece(c).wait()
        if place is None:
            dst[c * rows:(c + 1) * rows, :] = stage[c % W_SLOTS].astype(bf16)
        else:
            place(dst, slice(c * rows, (c + 1) * rows), stage[c % W_SLOTS])


def _gelu_tanh(x):
    return x * (0.5 * (1.0 + jnp.tanh(0.7978845608028654 * (x + 0.044715 * (x * x * x)))))


def _even_mix(ha, uv_ref, sg_ref, ws_ref, sb_ref, wm_ref, hb_scr):
    u = uv_ref[:, 0:W_A].astype(f32)
    v = uv_ref[:, W_A:2 * W_A].astype(f32)
    vn = (v * lax.rsqrt(jnp.mean(v * v, axis=-1, keepdims=True) + EPS) * sg_ref[...]).astype(bf16)
    n_chunk = hb_scr.shape[0] // CHUNK
    for g in range(W_A // LANES):
        cs = slice(g * LANES, (g + 1) * LANES)
        rhs = jnp.concatenate([vn[n * CHUNK:(n + 1) * CHUNK, cs] for n in range(n_chunk)], axis=1)
        mixed = _dot(ws_ref[g], rhs)
        for n in range(n_chunk):
            r = slice(n * CHUNK, (n + 1) * CHUNK)
            hb_scr[r, cs] = (u[r, cs] * (mixed[:, n * LANES:(n + 1) * LANES] + sb_ref[:, cs])).astype(bf16)
    return _dot(ha, wm_ref[0:W_A, :]) + _dot(hb_scr[...], wm_ref[W_A:2 * W_A, :])


def _ffn_kernel(*refs, tm, mi, mixer, final, split, sel):
    refs = list(refs)
    is_ctx = pl.program_id(0) < R_CTX // tm
    if split:
        c_ref, x_ref = refs[0:2]
        refs = refs[2:]
        read_h = lambda: jnp.where(is_ctx, c_ref[...], x_ref[...])
    else:
        h_ref = refs.pop(0)
        read_h = lambda: h_ref[...]
    wi_ref, wo_ref, wi_stage, wo_stage, w_sem = refs[-5:]
    refs = refs[:-5]
    if mixer == "attn":
        a_ref, wm_ref = refs[0:2]
        refs = refs[2:]
    elif mixer == "even":
        hac_ref, hax_ref, uv_ref, sg_ref, ws_ref, sb_ref, wm_ref = refs[0:7]
        refs = refs[7:]
    mod_ref, wi_hbm, wo_hbm = refs[0:3]
    refs = refs[3:]
    if final:
        fn_ref = refs.pop(0)
    o_ref, n_scr, acc_scr = refs[0:3]
    refs = refs[3:]

    first_step = pl.program_id(0) == 0
    wi_src = wi_hbm.at[sel[0], sel[1]]
    wo_src = wo_hbm.at[sel[0], sel[1]]

    def chunk_copies(j):
        slot = j % FFN_SLOTS
        return (
            pltpu.make_async_copy(wi_src.at[:, pl.ds(j * FC, FC)], wi_stage.at[slot, 0], w_sem.at[3 * slot]),
            pltpu.make_async_copy(wi_src.at[:, pl.ds(D_FF + j * FC, FC)], wi_stage.at[slot, 1],
                                  w_sem.at[3 * slot + 1]),
            pltpu.make_async_copy(wo_src.at[pl.ds(j * FC, FC), :], wo_stage.at[slot], w_sem.at[3 * slot + 2]),
        )

    def land_chunk(j):
        slot = j % FFN_SLOTS
        for copy in chunk_copies(j):
            copy.wait()
        wi_ref[:, j * FC:(j + 1) * FC] = wi_stage[slot, 0].astype(bf16)
        wi_ref[:, D_FF + j * FC:D_FF + (j + 1) * FC] = wi_stage[slot, 1].astype(bf16)
        wo_ref[j * FC:(j + 1) * FC, :] = wo_stage[slot].astype(bf16)
        if j + FFN_SLOTS < N_FC:
            for copy in chunk_copies(j + FFN_SLOTS):
                copy.start()

    @pl.when(first_step)
    def _():
        for j in range(FFN_SLOTS):
            for copy in chunk_copies(j):
                copy.start()

    if mixer is not None:
        h_scr = refs.pop(0)
        if mixer == "attn":
            y = _dot(a_ref[...], wm_ref[...])
        else:
            ha = jnp.where(is_ctx, hac_ref[...], hax_ref[...])
            y = _even_mix(ha, uv_ref, sg_ref, ws_ref, sb_ref, wm_ref, refs.pop(0))
        h_scr[...] = read_h() + mod_ref[5:6, :] * y
        read_h = lambda: h_scr[...]
    n_scr[...] = _modulated(read_h(), mod_ref[mi:mi + 1, :], mod_ref[mi + 1:mi + 2, :]).astype(bf16)

    def swiglu(before_chunk):
        for j in range(N_FC):
            before_chunk(j)
            n = n_scr[...]
            g = _dot(n, wi_ref[:, j * FC:(j + 1) * FC])
            u = _dot(n, wi_ref[:, D_FF + j * FC:D_FF + (j + 1) * FC])
            a = (g * _sigmoid(g) * u).astype(bf16)
            y = _dot(a, wo_ref[j * FC:(j + 1) * FC, :])
            if j == 0:
                acc_scr[...] = y
            else:
                acc_scr[...] += y

    @pl.when(first_step)
    def _():
        swiglu(land_chunk)

    @pl.when(jnp.logical_not(first_step))
    def _():
        swiglu(lambda j: None)

    out = read_h() + (0.5 * mod_ref[mi + 2:mi + 3, :]) * acc_scr[...]
    if final:
        ms = jnp.mean(out * out, axis=-1, keepdims=True)
        out = out * lax.rsqrt(ms + EPS) * fn_ref[...]
    o_ref[...] = out


def _ffn(h, mods, w_in, w_out, *, sel, mi, even=None, last=None, tm=TM_FFN):
    n_ctx = R_CTX // tm
    tile0 = n_ctx if last is not None else 0
    split = isinstance(h, tuple)
    const2 = lambda i: (0, 0)
    ctx_map = lambda i: (jnp.minimum(i, n_ctx - 1), 0)
    lat_map = lambda i: (jnp.maximum(i - n_ctx, 0), 0)
    if split:
        rows_out = R_ALL
        in_specs = [pl.BlockSpec((tm, D), ctx_map), pl.BlockSpec((tm, D), lat_map)]
        args = list(h)
    else:
        rows_out = h.shape[0] - tile0 * tm
        in_specs = [pl.BlockSpec((tm, D), lambda i: (i + tile0, 0))]
        args = [h]
    scratch = [pltpu.VMEM((tm, D), bf16), pltpu.VMEM((tm, D), f32)]
    mixer = None
    if last is not None:
        mixer = "attn"
        attn, w_attn, final_norm = last
        in_specs += [
            pl.BlockSpec((tm, D), lambda i: (i, 0)),
            pl.BlockSpec((D, D), const2, pipeline_mode=pl.Buffered(1)),
        ]
        args += [attn, w_attn]
        scratch.append(pltpu.VMEM((tm, D), f32))
    elif even is not None:
        mixer = "even"
        in_specs += [
            pl.BlockSpec((tm, W_A), ctx_map),
            pl.BlockSpec((tm, W_A), lat_map),
            pl.BlockSpec((tm, 2 * W_A), lambda i: (i, 0)),
            pl.BlockSpec((1, W_A), const2),
            pl.BlockSpec((W_A // LANES, CHUNK, CHUNK), lambda i: (0, 0, 0)),
            pl.BlockSpec((CHUNK, W_A), const2),
            pl.BlockSpec((2 * W_A, D), const2, pipeline_mode=pl.Buffered(1)),
        ]
        args += list(even)
        scratch += [pltpu.VMEM((tm, D), f32), pltpu.VMEM((tm, W_A), bf16)]
    in_specs += [
        pl.BlockSpec((None, N_MOD, D), lambda i: (_who_flat(i + tile0, tm), 0, 0)),
        pl.BlockSpec(memory_space=pl.ANY),
        pl.BlockSpec(memory_space=pl.ANY),
    ]
    args += [mods, w_in, w_out]
    if last is not None:
        in_specs.append(pl.BlockSpec((1, D), const2))
        args.append(final_norm.reshape(1, D))
    scratch += [
        pltpu.VMEM((D, 2 * D_FF), bf16),
        pltpu.VMEM((D_FF, D), bf16),
        pltpu.VMEM((FFN_SLOTS, 2, D, FC), f32),
        pltpu.VMEM((FFN_SLOTS, FC, D), f32),
        pltpu.SemaphoreType.DMA((3 * FFN_SLOTS,)),
    ]
    return pl.pallas_call(
        functools.partial(_ffn_kernel, tm=tm, mi=mi, mixer=mixer, final=last is not None, split=split, sel=sel),
        grid=(rows_out // tm,),
        in_specs=in_specs,
        out_specs=pl.BlockSpec((tm, D), lambda i: (i, 0)),
        out_shape=jax.ShapeDtypeStruct((rows_out, D), f32),
        scratch_shapes=scratch,
        compiler_params=pltpu.CompilerParams(
            dimension_semantics=("arbitrary",), vmem_limit_bytes=56 * MIB),
        name={None: "ffn", "even": "ffn_even", "attn": "ffn_final"}[mixer],
    )(*args)


HALO = 8


def _place_even_w(dst, rows, piece):
    g0 = 4 * W_A
    g1 = g0 + 4 * HEADS_A
    dst[rows, 0:g0] = piece[:, 0:g0].astype(bf16)
    dst[rows, g0:g0 + 2 * W_A] = piece[:, g1:g1 + 2 * W_A].astype(bf16)
    lane = lax.broadcasted_iota(jnp.int32, (piece.shape[0], LANES), 1)
    dst[rows, g0 + 2 * W_A:EVEN_COLS] = jnp.where(lane < 4 * HEADS_A, piece[:, g0:g0 + LANES], 0.0).astype(bf16)


def _even_in_kernel(h_ref, hp_ref, hn_ref, mod_ref, w_hbm, cw_ref, qk_ref, vo_ref, uv_ref, g_ref,
                    w_ref, w_stage, w_sem):
    tile = pl.program_id(0)

    @pl.when(tile == 0)
    def _():
        _fetch_cast(w_hbm.at[0], w_ref, w_stage, w_sem, place=_place_even_w)

    tm = TM_PROJ
    n_chunk = tm // CHUNK
    shift, scale = mod_ref[3:4, :], mod_ref[4:5, :]
    n = _modulated(h_ref[...], shift, scale)
    halo = _modulated(jnp.concatenate([hp_ref[...], hn_ref[...]], axis=0), shift, scale)
    n_ext = jnp.concatenate([halo[0:HALO], n, halo[HALO:2 * HALO]], axis=0).astype(bf16)
    n = n.astype(bf16)

    row = lax.broadcasted_iota(jnp.int32, (tm, LANES), 0)
    is_ctx = tile < R_CTX // tm
    pos = jnp.where(is_ctx, row % CTX, row + (jnp.maximum(tile - R_CTX // tm, 0) % (SEQ // tm)) * tm)
    seq_start = pos == 0
    seq_end = pos == jnp.where(is_ctx, CTX - 1, SEQ - 1)

    p = _dot(n_ext, w_ref[:, 0:2 * W_A])
    cur = p[HALO:HALO + tm]
    prv = pltpu.roll(p, 1, 0)[HALO:HALO + tm]
    nxt = pltpu.roll(p, tm + 2 * HALO - 1, 0)[HALO:HALO + tm]
    for cb in range(2 * HEADS_A):
        cs = slice(cb * LANES, (cb + 1) * LANES)
        y = (cw_ref[0:1, cs] * jnp.where(seq_start, 0.0, prv[:, cs]) + cw_ref[1:2, cs] * cur[:, cs]
             + cw_ref[2:3, cs] * jnp.where(seq_end, 0.0, nxt[:, cs]))
        y = y * _sigmoid(y)
        if cb < HEADS_A:
            for c in range(n_chunk):
                r = slice(c * CHUNK, (c + 1) * CHUNK)
                qk_ref[r, cs] = y[r, :].T.astype(bf16)
        else:
            qk_ref[:, cs] = (y * CHUNK ** -0.5).astype(bf16)

    v = _dot(n, w_ref[:, 2 * W_A:3 * W_A])
    for hd in range(HEADS_A):
        cs = slice(hd * LANES, (hd + 1) * LANES)
        for c in range(n_chunk):
            r = slice(c * CHUNK, (c + 1) * CHUNK)
            vo_ref[r, cs] = v[r, cs].T.astype(bf16)
    vo_ref[:, W_A:2 * W_A] = _sigmoid(_dot(n, w_ref[:, 3 * W_A:4 * W_A])).astype(bf16)
    uv_ref[...] = _gelu_tanh(_dot(n, w_ref[:, 2048:3072])).astype(bf16)
    g_ref[...] = _dot(n, w_ref[:, 3072:3200])


def _even_in(h, mods, w, conv_w):
    tm = TM_PROJ
    out_map = lambda i: (i, 0)
    halo_blocks = tm // HALO
    last_halo = R_ALL // HALO - 1
    return pl.pallas_call(
        _even_in_kernel,
        grid=(R_ALL // tm,),
        in_specs=[
            pl.BlockSpec((tm, D), lambda i: (i, 0)),
            pl.BlockSpec((HALO, D), lambda i: (jnp.maximum(i * halo_blocks - 1, 0), 0)),
            pl.BlockSpec((HALO, D), lambda i: (jnp.minimum((i + 1) * halo_blocks, last_halo), 0)),
            pl.BlockSpec((None, N_MOD, D), lambda i: (_who_flat(i, tm), 0, 0)),
            pl.BlockSpec(memory_space=pl.ANY),
            pl.BlockSpec((3, 2 * W_A), lambda i: (0, 0)),
        ],
        out_specs=[
            pl.BlockSpec((tm, 1024), out_map),
            pl.BlockSpec((tm, 1024), out_map),
            pl.BlockSpec((tm, 1024), out_map),
            pl.BlockSpec((tm, LANES), out_map),
        ],
        out_shape=[
            jax.ShapeDtypeStruct((R_ALL, 1024), bf16),
            jax.ShapeDtypeStruct((R_ALL, 1024), bf16),
            jax.ShapeDtypeStruct((R_ALL, 1024), bf16),
            jax.ShapeDtypeStruct((R_ALL, LANES), f32),
        ],
        scratch_shapes=[
            pltpu.VMEM((D, EVEN_COLS), bf16),
            pltpu.VMEM((W_SLOTS, D // W_CHUNKS, w.shape[-1]), f32),
            pltpu.SemaphoreType.DMA((W_SLOTS,)),
        ],
        compiler_params=pltpu.CompilerParams(
            dimension_semantics=("arbitrary",), vmem_limit_bytes=40 * MIB),
        name="even_in",
    )(h, h, h, mods, w, conv_w)


N_AUG = CHUNK + 16
CHUNKS_PER_ITER = 18


def _chunk_loop(body):
    def group(i, carry):
        for u in range(CHUNKS_PER_ITER):
            carry = body(i * CHUNKS_PER_ITER + u, carry)
        return carry
    lax.fori_loop(0, N_CHUNK // CHUNKS_PER_ITER, group, 0)


def _mlstm_kernel(qc_ref, ql_ref, kc_ref, kl_ref, vc_ref, vl_ref, oc_ref, ol_ref, gc_ref, gl_ref,
                  gb_ref, mn_ref, outc_ref, outl_ref,
                  ks, qts, vts, og, hg, xc, xr, cl, st, cst, mst):
    head = pl.program_id(1)
    rowi = lax.broadcasted_iota(jnp.int32, (CHUNK, CHUNK), 0)
    coli = lax.broadcasted_iota(jnp.int32, (CHUNK, CHUNK), 1)
    lower = coli <= rowi
    upper = coli >= rowi
    tri = jnp.where(lower, 1.0, 0.0).astype(bf16)

    def part(ctx_ref, lat_ref, c):
        return (ctx_ref, c * CHUNK) if c < N_CTX_CHUNK else (lat_ref, (c - N_CTX_CHUNK) * CHUNK)

    for c in range(N_CHUNK):
        lo = c * CHUNK
        for dst, refs in ((qts, (qc_ref, ql_ref)), (ks, (kc_ref, kl_ref)), (vts, (vc_ref, vl_ref)),
                          (og, (oc_ref, ol_ref))):
            src, at = part(*refs, c)
            dst[lo:lo + CHUNK, :] = src[at:at + CHUNK, :]

    @pl.when(head == 0)
    def _():
        kind = (coli // HEADS_A) % 4
        for c in range(N_CHUNK):
            lo = c * CHUNK
            src, at = part(gc_ref, gl_ref, c)
            gt = src[at:at + CHUNK, :] + gb_ref[...]
            lf = jnp.minimum(gt, 0.0) - jnp.log1p(jnp.exp(-jnp.abs(gt)))
            hi, mid, lw = _split3(lf)
            pre = _dot(tri, hi) + _dot(tri, mid) + _dot(tri, lw)
            suf = pre[CHUNK - 1:CHUNK, :] - pre + lf
            x = jnp.where(kind == 1, pre, jnp.where(kind == 3, suf, gt))
            xr[16 * c:16 * c + 16, :] = x.T[0:16, :]
            xc[lo:lo + CHUNK, :] = x

    lane_shift = (LANES - head) % LANES

    def gate_rows(c, d):
        row = c * 16 + 8 * d + head
        return xr[pl.ds(row, 1), :], xr[pl.ds(row + HEADS_A, 1), :]

    def local_state(c, carry):
        lo = pl.multiple_of(c * CHUNK, CHUNK)
        k = ks[pl.ds(lo, CHUNK), :]
        v_t = vts[pl.ds(lo, CHUNK), :].astype(f32)
        lhs = []
        for d in (0, 1):
            ig_r, b_r = gate_rows(c, d)
            g = b_r[:, CHUNK - 1:CHUNK] if d == 0 else b_r[:, 0:1]
            a_r = g - b_r + ig_r
            m_loc = jnp.max(a_r, axis=1, keepdims=True)
            w_r = jnp.exp(a_r - m_loc)
            lhs += [v_t * w_r, jnp.broadcast_to(w_r, (16, LANES))]
            st[d * N_CHUNK + c, 0:1, :] = jnp.broadcast_to(m_loc, (1, LANES))
            st[d * N_CHUNK + c, 1:2, :] = jnp.broadcast_to(g, (1, LANES))
        both = _dot(jnp.concatenate(lhs, axis=0).astype(bf16), k)
        cl[c] = both[0:N_AUG, :]
        cl[N_CHUNK + c] = both[N_AUG:2 * N_AUG, :]
        return carry

    _chunk_loop(local_state)

    cst[...] = jnp.zeros(cst.shape, f32)
    mst[...] = jnp.zeros(mst.shape, f32)

    def scan_step(i, carry):
        for d in (0, 1):
            c = i if d == 0 else jnp.where(i < N_CTX_CHUNK, N_CTX_CHUNK - 1 - i, N_CHUNK + N_CTX_CHUNK - 1 - i)
            idx = d * N_CHUNK + c
            c_loc = cl[idx]
            m_loc = st[idx, 0:1, :]
            g = st[idx, 1:2, :]
            c_prev = cst[d]
            m_prev = mst[d, 0:1, :]
            m_new = jnp.maximum(g + m_prev, m_loc)
            dec = jnp.exp(g + m_prev - m_new)
            add = jnp.exp(m_loc - m_new)
            cl[idx] = c_prev
            st[idx, 2:3, :] = m_prev
            cst[d] = dec * c_prev + add * c_loc
            mst[d, 0:1, :] = m_new
        return carry

    lax.fori_loop(0, N_CHUNK, scan_step, 0)

    def outputs(c, carry):
        lo = pl.multiple_of(c * CHUNK, CHUNK)
        k = ks[pl.ds(lo, CHUNK), :]
        q_t = qts[pl.ds(lo, CHUNK), :]
        v_aug = jnp.concatenate([vts[pl.ds(lo, CHUNK), :], jnp.ones((16, LANES), bf16)], axis=0)
        q_f = q_t.astype(f32)
        s_t = _dot(k, q_t)
        x_c = pltpu.roll(xc[pl.ds(lo, CHUNK), :], lane_shift, 1)
        hs = None
        for d in (0, 1):
            _, b_r = gate_rows(c, d)
            idx = d * N_CHUNK + c
            r_c = x_c[:, 8 * d:8 * d + 1] - x_c[:, 8 * d + 4:8 * d + 5]
            dm = jnp.where(upper if d == 0 else lower, b_r + r_c, -jnp.inf)
            e_r = b_r + st[idx, 2:3, :]
            m_t = jnp.maximum(e_r, jnp.max(dm, axis=0, keepdims=True))
            p_t = s_t * jnp.exp(dm - m_t)
            inter = jnp.exp(e_r - m_t)
            lhs = jnp.concatenate([v_aug, cl[idx].astype(bf16)], axis=1)
            rhs = jnp.concatenate([p_t, q_f * inter], axis=0).astype(bf16)
            nd = _dot(lhs, rhs)
            den = nd[CHUNK:CHUNK + 1, :]
            h_d = nd[0:CHUNK, :] * (1.0 / jnp.maximum(jnp.abs(den), jnp.exp(-m_t)))
            hs = h_d if hs is None else hs + h_d
        hn = hs * lax.rsqrt(jnp.mean(hs * hs, axis=0, keepdims=True) + EPS) * mn_ref[...]
        hg[pl.ds(lo, CHUNK), :] = (og[pl.ds(lo, CHUNK), :].astype(f32) * hn.T).astype(bf16)
        return carry

    _chunk_loop(outputs)
    outc_ref[...] = hg[0:CTX, :]
    outl_ref[...] = hg[CTX:TOK, :]


def _mlstm(qk, vo, gates, gate_b, mnorm):
    lat0 = R_CTX // SEQ
    ctx = lambda col0: pl.BlockSpec((CTX, LANES), lambda b, h: (b, col0 + h))
    lat = lambda col0: pl.BlockSpec((SEQ, LANES), lambda b, h: (lat0 + b, col0 + h))
    return pl.pallas_call(
        _mlstm_kernel,
        grid=(B, HEADS_A),
        in_specs=[
            ctx(0), lat(0), ctx(HEADS_A), lat(HEADS_A),
            ctx(0), lat(0), ctx(HEADS_A), lat(HEADS_A),
            pl.BlockSpec((CTX, LANES), lambda b, h: (b, 0)),
            pl.BlockSpec((SEQ, LANES), lambda b, h: (lat0 + b, 0)),
            pl.BlockSpec((1, LANES), lambda b, h: (0, 0)),
            pl.BlockSpec((None, CHUNK, LANES), lambda b, h: (h, 0, 0)),
        ],
        out_specs=[
            pl.BlockSpec((CTX, LANES), lambda b, h: (b, h)),
            pl.BlockSpec((SEQ, LANES), lambda b, h: (b, h)),
        ],
        out_shape=[
            jax.ShapeDtypeStruct((R_CTX, W_A), bf16),
            jax.ShapeDtypeStruct((R_LAT, W_A), bf16),
        ],
        scratch_shapes=[
            pltpu.VMEM((TOK, LANES), bf16),
            pltpu.VMEM((TOK, LANES), bf16),
            pltpu.VMEM((TOK, LANES), bf16),
            pltpu.VMEM((TOK, LANES), bf16),
            pltpu.VMEM((TOK, LANES), bf16),
            pltpu.VMEM((TOK, LANES), f32),
            pltpu.VMEM((16 * N_CHUNK, LANES), f32),
            pltpu.VMEM((2 * N_CHUNK, CHUNK + 16, LANES), f32),
            pltpu.VMEM((2 * N_CHUNK, 8, LANES), f32),
            pltpu.VMEM((2, CHUNK + 16, LANES), f32),
            pltpu.VMEM((2, 8, LANES), f32),
        ],
        compiler_params=pltpu.CompilerParams(
            dimension_semantics=("parallel", "arbitrary"), vmem_limit_bytes=40 * MIB),
        name="mlstm",
    )(qk, qk, qk, qk, vo, vo, vo, vo, gates, gates, gate_b, mnorm)


def _odd_in_kernel(h_ref, mod_ref, w_ref, cos_ref, sin_ref, o_ref):
    n = _modulated(h_ref[...], mod_ref[3:4, :], mod_ref[4:5, :]).astype(bf16)
    cos = cos_ref[...]
    sin = sin_ref[...]
    n_rot = (HEADS_C + KV_HEADS) * DH // LANES
    y = _dot(n, w_ref[:, 0:n_rot * LANES])
    for c in range(0, n_rot, 2):
        x1 = y[:, c * LANES:(c + 1) * LANES]
        x2 = y[:, (c + 1) * LANES:(c + 2) * LANES]
        r1 = x1 * cos - x2 * sin
        r2 = x1 * sin + x2 * cos
        if c < HEADS_C * DH // LANES:
            r1 = r1 * (DH ** -0.5 * LOG2E)
            r2 = r2 * (DH ** -0.5 * LOG2E)
        o_ref[:, c * LANES:(c + 1) * LANES] = r1.astype(bf16)
        o_ref[:, (c + 1) * LANES:(c + 2) * LANES] = r2.astype(bf16)
    v0 = n_rot * LANES
    o_ref[:, v0:QKV] = _dot(n, w_ref[:, v0:QKV]).astype(bf16)


def _odd_in(h, mods, w, cos_t, sin_t):
    tm = TM_ODD_IN
    n_ctx = R_CTX // tm
    per_b = SEQ // tm
    rope_map = lambda i: (jnp.where(i < n_ctx, 0, 1 + jnp.maximum(i - n_ctx, 0) % per_b), 0)
    return pl.pallas_call(
        _odd_in_kernel,
        grid=(R_ALL // tm,),
        in_specs=[
            pl.BlockSpec((tm, D), lambda i: (i, 0)),
            pl.BlockSpec((None, N_MOD, D), lambda i: (_who_flat(i, tm), 0, 0)),
            pl.BlockSpec((D, QKV), lambda i: (0, 0), pipeline_mode=pl.Buffered(1)),
            pl.BlockSpec((tm, LANES), rope_map),
            pl.BlockSpec((tm, LANES), rope_map),
        ],
        out_specs=pl.BlockSpec((tm, QKV), lambda i: (i, 0)),
        out_shape=jax.ShapeDtypeStruct((R_ALL, QKV), bf16),
        compiler_params=pltpu.CompilerParams(
            dimension_semantics=("parallel",), vmem_limit_bytes=32 * MIB),
        name="odd_in",
    )(h, mods, w, cos_t, sin_t)


BLOCKS_PER_ITER = 2


def _attn_kernel(sink_ref, q_ref, kc_ref, kl_ref, vc_ref, vl_ref, o_ref,
                 k_all, vt_all, q_all, s_all, p_all, ot_all):
    kv_w = KV_HEADS * DH
    key = lax.broadcasted_iota(jnp.int32, (CHUNK, CHUNK), 0)
    qry = lax.broadcasted_iota(jnp.int32, (CHUNK, CHUNK), 1)
    far = 1 << 20
    lane_head = lax.broadcasted_iota(jnp.int32, (CHUNK, kv_w), 1) % LANES // (DH // 2)
    neg = -1e30
    n_key = CTX + 3 * CHUNK
    n_slab = n_key // CHUNK

    def stage_keys(k_scr, vt_scr, k_rows, v_rows, row0):
        rows = k_rows.shape[0]
        k_scr[row0:row0 + rows, :] = k_rows
        v_t = v_rows.astype(f32).T.astype(bf16)
        for j in range(KV_HEADS):
            vt_scr[j, 0:DH, row0:row0 + rows] = v_t[j * DH:(j + 1) * DH, :]

    for u in range(BLOCKS_PER_ITER):
        stage_keys(k_all.at[u], vt_all.at[u], kc_ref[...], vc_ref[...], 0)
        for j in range(KV_HEADS):
            vt_all[u, j, DH:DH + 16, :] = jnp.ones((16, n_key), bf16)

    def block(blk, u):
        k_scr, vt_scr, q_scr, s_scr, p_scr, ot_scr = (r.at[u] for r in (k_all, vt_all, q_all, s_all, p_all, ot_all))
        q0 = pl.multiple_of(blk * CHUNK, CHUNK)
        prev_ok = key >= qry + jnp.where(blk > 0, 0, far)
        next_ok = key <= qry - jnp.where(blk < N_BLK - 1, 0, far)
        for n, off in enumerate((-1, 0, 1)):
            src = pl.multiple_of(jnp.clip(blk + off, 0, N_BLK - 1) * CHUNK, CHUNK)
            stage_keys(k_scr, vt_scr, kl_ref[pl.ds(src, CHUNK), :], vl_ref[pl.ds(src, CHUNK), :], CTX + n * CHUNK)

        for j in range(KV_HEADS):
            keep = jnp.where(lane_head == j, 1.0, 0.0).astype(bf16)
            for g in range(GROUP):
                q_scr[j, g * CHUNK:(g + 1) * CHUNK, :] = q_ref[pl.ds(q0, CHUNK), g * kv_w:(g + 1) * kv_w] * keep
            s_scr[j] = _dot_nt(k_scr[...], q_scr[j])

        def scores(j, g, slab):
            s = s_scr[j, slab * CHUNK:(slab + 1) * CHUNK, g * CHUNK:(g + 1) * CHUNK]
            if slab == 2:
                s = jnp.where(prev_ok, s, neg)
            if slab == 4:
                s = jnp.where(next_ok, s, neg)
            return s

        for j in range(KV_HEADS):
            sink_terms = []
            for g in range(GROUP):
                sink = jnp.full((1, CHUNK), sink_ref[j * GROUP + g] * LOG2E, f32)
                m8 = None
                for slab in range(n_slab):
                    part = jnp.max(scores(j, g, slab).reshape(CHUNK // 8, 8, CHUNK), axis=0)
                    m8 = part if m8 is None else jnp.maximum(m8, part)
                m = jnp.maximum(sink, jnp.max(m8, axis=0, keepdims=True))
                for slab in range(n_slab):
                    p = jnp.exp2(scores(j, g, slab) - m)
                    p_scr[j, slab * CHUNK:(slab + 1) * CHUNK, g * CHUNK:(g + 1) * CHUNK] = p.astype(bf16)
                sink_terms.append(jnp.exp2(sink - m))
            acc = _dot(vt_scr[j], p_scr[j])
            den = acc[DH:DH + 1, :] + jnp.concatenate(sink_terms, axis=1)
            ot_scr[j * DH:(j + 1) * DH, :] = acc[0:DH, :] * (1.0 / den)
        for g in range(GROUP):
            o_ref[pl.ds(q0, CHUNK), g * kv_w:(g + 1) * kv_w] = (
                ot_scr[:, g * CHUNK:(g + 1) * CHUNK].T.astype(bf16))

    def blocks(i, carry):
        for u in range(BLOCKS_PER_ITER):
            block(i * BLOCKS_PER_ITER + u, u)
        return carry

    lax.fori_loop(0, N_BLK // BLOCKS_PER_ITER, blocks, 0)


def _attention(qkv, sink):
    kv_w = KV_HEADS * DH
    n_key = CTX + 3 * CHUNK
    k_col = HEADS_C * DH // kv_w
    v_col = k_col + 1
    lat0 = R_CTX // SEQ
    ctx = lambda col: pl.BlockSpec((CTX, kv_w), lambda b: (b, col))
    lat = lambda col: pl.BlockSpec((SEQ, kv_w), lambda b: (lat0 + b, col))
    return pl.pallas_call(
        _attn_kernel,
        grid=(B,),
        in_specs=[
            pl.BlockSpec(memory_space=pltpu.SMEM),
            pl.BlockSpec((SEQ, HEADS_C * DH), lambda b: (lat0 + b, 0)),
            ctx(k_col), lat(k_col), ctx(v_col), lat(v_col),
        ],
        out_specs=pl.BlockSpec((SEQ, HEADS_C * DH), lambda b: (b, 0)),
        out_shape=jax.ShapeDtypeStruct((R_LAT, HEADS_C * DH), bf16),
        scratch_shapes=[
            pltpu.VMEM((BLOCKS_PER_ITER, n_key, kv_w), bf16),
            pltpu.VMEM((BLOCKS_PER_ITER, KV_HEADS, DH + 16, n_key), bf16),
            pltpu.VMEM((BLOCKS_PER_ITER, KV_HEADS, GROUP * CHUNK, kv_w), bf16),
            pltpu.VMEM((BLOCKS_PER_ITER, KV_HEADS, n_key, GROUP * CHUNK), f32),
            pltpu.VMEM((BLOCKS_PER_ITER, KV_HEADS, n_key, GROUP * CHUNK), bf16),
            pltpu.VMEM((BLOCKS_PER_ITER, kv_w, GROUP * CHUNK), f32),
        ],
        compiler_params=pltpu.CompilerParams(
            dimension_semantics=("parallel",), vmem_limit_bytes=48 * MIB),
        name="window_attention",
    )(sink, qkv, qkv, qkv, qkv, qkv)


def _rope_tables():
    rows = SEQ // GRID_W
    row, col = np.meshgrid(np.arange(rows), np.arange(GRID_W), indexing='ij')
    n_freq = DH // 4
    inv = (np.float32(ROPE_BASE) ** (-np.arange(n_freq, dtype=np.float32) / np.float32(n_freq))).astype(np.float32)
    ang = np.concatenate([row.reshape(-1, 1).astype(np.float32) * inv,
                          col.reshape(-1, 1).astype(np.float32) * inv], axis=-1)
    reps = 2 * LANES // DH
    cos = np.tile(np.cos(ang).astype(np.float32), (1, reps))
    sin = np.tile(np.sin(ang).astype(np.float32), (1, reps))
    cos = np.concatenate([np.ones((TM_ODD_IN, LANES), np.float32), cos], axis=0)
    sin = np.concatenate([np.zeros((TM_ODD_IN, LANES), np.float32), sin], axis=0)
    return jnp.asarray(cos), jnp.asarray(sin)


def kernel(x, c, ctx, c_ctx, ada_w, ada_b, ffn_w_in, ffn_w_out, even_w_in, even_w_out, mlstm_conv,
           mlstm_gate_b, mlstm_norm, sgu_norm, sgu_ws, sgu_b, odd_w_qkv, odd_w_out, attn_sink, final_norm):
    cs = jnp.concatenate([c_ctx[None, :], c, jnp.zeros((16 - 1 - B, D), f32)], axis=0)
    mods = _modulation(cs, ada_w, ada_b)[:, :1 + B, :].reshape(2, 1 + B, N_MOD, D)

    fw_in = ffn_w_in
    fw_out = ffn_w_out

    m0 = mods[0]
    h = _ffn((ctx.reshape(R_CTX, D), x.reshape(R_LAT, D)), m0, fw_in, fw_out, sel=(0, 0), mi=0)
    qk, vo, uv, gates = _even_in(h, m0, even_w_in, mlstm_conv[0])
    gate_b = jnp.pad(mlstm_gate_b[0].reshape(1, 4 * HEADS_A), ((0, 0), (0, LANES - 4 * HEADS_A)))
    mnorm_t = jnp.broadcast_to(mlstm_norm[0][:, :, None], (HEADS_A, CHUNK, LANES))
    ha_ctx, ha_lat = _mlstm(qk, vo, gates, gate_b, mnorm_t)
    sbx = jnp.repeat(sgu_b[0].T, LANES, axis=1)
    h = _ffn(h, m0, fw_in, fw_out, sel=(0, 1), mi=6,
             even=(ha_ctx, ha_lat, uv, sgu_norm[0].reshape(1, W_A), sgu_ws[0].astype(bf16), sbx,
                   even_w_out[0].astype(bf16)))

    m1 = mods[1]
    h = _ffn(h, m1, fw_in, fw_out, sel=(1, 0), mi=0, tm=TM_FFN_WIDE)
    cos_t, sin_t = _rope_tables()
    qdim = HEADS_C * DH
    kdim = KV_HEADS * DH
    w_q = odd_w_qkv[0][:, :qdim].reshape(D, KV_HEADS, GROUP, DH // 2, 2).transpose(0, 2, 4, 1, 3).reshape(D, qdim)
    w_k = odd_w_qkv[0][:, qdim:qdim + kdim].reshape(D, KV_HEADS, DH // 2, 2).transpose(0, 3, 1, 2).reshape(D, kdim)
    w_qkv = jnp.concatenate([w_q, w_k, odd_w_qkv[0][:, qdim + kdim:]], axis=1).astype(bf16)
    w_o = odd_w_out[0].reshape(KV_HEADS, GROUP, DH, D).transpose(1, 0, 2, 3).reshape(qdim, D).astype(bf16)
    qkv = _odd_in(h, m1, w_qkv, cos_t, sin_t)
    attn = _attention(qkv, attn_sink[0])
    out = _ffn(h, m1, fw_in, fw_out, sel=(1, 1), mi=6, last=(attn, w_o, final_norm))
    return out.reshape(B, SEQ, D)
```

```python
import functools

import numpy as np
import jax
import jax.numpy as jnp
from jax import lax
from jax.experimental import pallas as pl
from jax.experimental.pallas import tpu as pltpu

f32 = jnp.float32
bf16 = jnp.bfloat16

D = 1024
B = 8
SEQ = 2048
CTX = 256
TOK = CTX + SEQ
GRID_W = 64
N_MOD = 9
D_FF = 2816
EPS = 1e-6
HEADS_A = 4
CHUNK = 128
N_CHUNK = TOK // CHUNK
N_CTX_CHUNK = CTX // CHUNK
W_A = 512
EVEN_COLS = 3200
HEADS_C = 16
KV_HEADS = 4
GROUP = HEADS_C // KV_HEADS
DH = 64
QKV = (HEADS_C + 2 * KV_HEADS) * DH
N_BLK = SEQ // CHUNK
ROPE_BASE = 10000.0
LOG2E = 1.4426950408889634

R_CTX = B * CTX
R_LAT = B * SEQ
R_ALL = R_CTX + R_LAT

LANES = 128
TM_FFN = 512
TM_FFN_WIDE = 1024
TM_PROJ = 512
TM_ODD_IN = 1024
FC = 256
N_FC = D_FF // FC
MIB = 1024 * 1024


def _dot(a, b):
    return jnp.dot(a, b, preferred_element_type=f32)


def _dot_nt(a, b):
    return lax.dot_general(a, b, (((1,), (1,)), ((), ())), preferred_element_type=f32)


def _dot_tn(a, b):
    return lax.dot_general(a, b, (((0,), (0,)), ((), ())), preferred_element_type=f32)


def _sigmoid(x):
    return 1.0 / (1.0 + jnp.exp(-x))


def _split3(x):
    hi = x.astype(bf16)
    r1 = x - hi.astype(f32)
    mid = r1.astype(bf16)
    lo = (r1 - mid.astype(f32)).astype(bf16)
    return hi, mid, lo


def _modulated(h, shift, scale):
    ms = jnp.mean(h * h, axis=-1, keepdims=True)
    return h * lax.rsqrt(ms + EPS) * (1.0 + scale) + shift


def _mod_kernel(c_ref, w_ref, b_ref, o_ref):
    x = c_ref[...]
    s = x * _sigmoid(x)
    w = w_ref[...]
    s_hi = s.astype(bf16)
    s_lo = (s - s_hi.astype(f32)).astype(bf16)
    w_hi = w.astype(bf16)
    w_lo = (w - w_hi.astype(f32)).astype(bf16)
    o_ref[...] = _dot(s_hi, w_hi) + _dot(s_hi, w_lo) + _dot(s_lo, w_hi) + b_ref[...]


def _modulation(cs, ada_w, ada_b):
    depth = ada_w.shape[0]
    rows = cs.shape[0]
    n_col = N_MOD * D
    tn = 1024
    return pl.pallas_call(
        _mod_kernel,
        grid=(depth, n_col // tn),
        in_specs=[
            pl.BlockSpec((rows, D), lambda l, j: (0, 0)),
            pl.BlockSpec((None, D, tn), lambda l, j: (l, 0, j)),
            pl.BlockSpec((None, 1, tn), lambda l, j: (l, 0, j)),
        ],
        out_specs=pl.BlockSpec((None, rows, tn), lambda l, j: (l, 0, j)),
        out_shape=jax.ShapeDtypeStruct((depth, rows, n_col), f32),
        compiler_params=pltpu.CompilerParams(
            dimension_semantics=("parallel", "parallel"), vmem_limit_bytes=32 * MIB),
        name="modulation",
    )(cs, ada_w, ada_b.reshape(depth, 1, n_col))


def _who_flat(tile, tm):
    n_ctx = R_CTX // tm
    per_b = SEQ // tm
    return jnp.where(tile < n_ctx, 0, 1 + jnp.maximum(tile - n_ctx, 0) // per_b)


W_CHUNKS = 16
W_SLOTS = 4


def _fetch_cast(src, dst, stage, sem, place=None):
    rows = dst.shape[0] // W_CHUNKS

    def piece(c):
        slot = c % W_SLOTS
        return pltpu.make_async_copy(src.at[pl.ds(c * rows, rows), :], stage.at[slot], sem.at[slot])

    for c in range(W_SLOTS - 1):
        piece(c).start()
    for c in range(W_CHUNKS):
        if c + W_SLOTS - 1 < W_CHUNKS:
            piece(c + W_SLOTS - 1).start()
        piece(c).wait()
        if place is None:
            dst[c * rows:(c + 1) * rows, :] = stage[c % W_SLOTS].astype(bf16)
        else:
            place(dst, slice(c * rows, (c + 1) * rows), stage[c % W_SLOTS])


def _gelu_tanh(x):
    return x * (0.5 * (1.0 + jnp.tanh(0.7978845608028654 * (x + 0.044715 * (x * x * x)))))


def _even_mix(ha, uv_ref, sg_ref, ws_ref, sb_ref, wm_ref, hb_scr):
    u = uv_ref[:, 0:W_A].astype(f32)
    v = uv_ref[:, W_A:2 * W_A].astype(f32)
    vn = (v * lax.rsqrt(jnp.mean(v * v, axis=-1, keepdims=True) + EPS) * sg_ref[...]).astype(bf16)
    n_chunk = hb_scr.shape[0] // CHUNK
    for g in range(W_A // LANES):
        cs = slice(g * LANES, (g + 1) * LANES)
        rhs = jnp.concatenate([vn[n * CHUNK:(n + 1) * CHUNK, cs] for n in range(n_chunk)], axis=1)
        mixed = _dot(ws_ref[g], rhs)
        for n in range(n_chunk):
            r = slice(n * CHUNK, (n + 1) * CHUNK)
            hb_scr[r, cs] = (u[r, cs] * (mixed[:, n * LANES:(n + 1) * LANES] + sb_ref[:, cs])).astype(bf16)
    return _dot(ha, wm_ref[0:W_A, :]) + _dot(hb_scr[...], wm_ref[W_A:2 * W_A, :])


def _ffn_kernel(*refs, tm, mi, mixer, final, split, sel):
    refs = list(refs)
    is_ctx = pl.program_id(0) < R_CTX // tm
    if split:
        c_ref, x_ref = refs[0:2]
        refs = refs[2:]
        read_h = lambda: jnp.where(is_ctx, c_ref[...], x_ref[...])
    else:
        h_ref = refs.pop(0)
        read_h = lambda: h_ref[...]
    wi_ref, wo_ref, wi_stage, wo_stage, wi_sem, wo_sem = refs[-6:]
    refs = refs[:-6]
    if mixer == "attn":
        a_ref, wm_ref = refs[0:2]
        refs = refs[2:]
    elif mixer == "even":
        hac_ref, hax_ref, uv_ref, sg_ref, ws_ref, sb_ref, wm_ref = refs[0:7]
        refs = refs[7:]
    mod_ref, wi_hbm, wo_hbm = refs[0:3]
    refs = refs[3:]
    if final:
        fn_ref = refs.pop(0)
    o_ref, n_scr, acc_scr = refs[0:3]
    refs = refs[3:]

    @pl.when(pl.program_id(0) == 0)
    def _():
        _fetch_cast(wi_hbm.at[sel[0], sel[1]], wi_ref, wi_stage, wi_sem)
        _fetch_cast(wo_hbm.at[sel[0], sel[1]], wo_ref, wo_stage, wo_sem)

    if mixer is not None:
        h_scr = refs.pop(0)
        if mixer == "attn":
            y = _dot(a_ref[...], wm_ref[...])
        else:
            ha = jnp.where(is_ctx, hac_ref[...], hax_ref[...])
            y = _even_mix(ha, uv_ref, sg_ref, ws_ref, sb_ref, wm_ref, refs.pop(0))
        h_scr[...] = read_h() + mod_ref[5:6, :] * y
        read_h = lambda: h_scr[...]
    n_scr[...] = _modulated(read_h(), mod_ref[mi:mi + 1, :], mod_ref[mi + 1:mi + 2, :]).astype(bf16)

    for j in range(N_FC):
        n = n_scr[...]
        g = _dot(n, wi_ref[:, j * FC:(j + 1) * FC])
        u = _dot(n, wi_ref[:, D_FF + j * FC:D_FF + (j + 1) * FC])
        a = (g * _sigmoid(g) * u).astype(bf16)
        y = _dot(a, wo_ref[j * FC:(j + 1) * FC, :])
        if j == 0:
            acc_scr[...] = y
        else:
            acc_scr[...] += y
    out = read_h() + (0.5 * mod_ref[mi + 2:mi + 3, :]) * acc_scr[...]
    if final:
        ms = jnp.mean(out * out, axis=-1, keepdims=True)
        out = out * lax.rsqrt(ms + EPS) * fn_ref[...]
    o_ref[...] = out


def _ffn(h, mods, w_in, w_out, *, sel, mi, even=None, last=None, tm=TM_FFN):
    n_ctx = R_CTX // tm
    tile0 = n_ctx if last is not None else 0
    split = isinstance(h, tuple)
    const2 = lambda i: (0, 0)
    ctx_map = lambda i: (jnp.minimum(i, n_ctx - 1), 0)
    lat_map = lambda i: (jnp.maximum(i - n_ctx, 0), 0)
    if split:
        rows_out = R_ALL
        in_specs = [pl.BlockSpec((tm, D), ctx_map), pl.BlockSpec((tm, D), lat_map)]
        args = list(h)
    else:
        rows_out = h.shape[0] - tile0 * tm
        in_specs = [pl.BlockSpec((tm, D), lambda i: (i + tile0, 0))]
        args = [h]
    scratch = [pltpu.VMEM((tm, D), bf16), pltpu.VMEM((tm, D), f32)]
    mixer = None
    if last is not None:
        mixer = "attn"
        attn, w_attn, final_norm = last
        in_specs += [
            pl.BlockSpec((tm, D), lambda i: (i, 0)),
            pl.BlockSpec((D, D), const2, pipeline_mode=pl.Buffered(1)),
        ]
        args += [attn, w_attn]
        scratch.append(pltpu.VMEM((tm, D), f32))
    elif even is not None:
        mixer = "even"
        in_specs += [
            pl.BlockSpec((tm, W_A), ctx_map),
            pl.BlockSpec((tm, W_A), lat_map),
            pl.BlockSpec((tm, 2 * W_A), lambda i: (i, 0)),
            pl.BlockSpec((1, W_A), const2),
            pl.BlockSpec((W_A // LANES, CHUNK, CHUNK), lambda i: (0, 0, 0)),
            pl.BlockSpec((CHUNK, W_A), const2),
            pl.BlockSpec((2 * W_A, D), const2, pipeline_mode=pl.Buffered(1)),
        ]
        args += list(even)
        scratch += [pltpu.VMEM((tm, D), f32), pltpu.VMEM((tm, W_A), bf16)]
    in_specs += [
        pl.BlockSpec((None, N_MOD, D), lambda i: (_who_flat(i + tile0, tm), 0, 0)),
        pl.BlockSpec(memory_space=pl.ANY),
        pl.BlockSpec(memory_space=pl.ANY),
    ]
    args += [mods, w_in, w_out]
    if last is not None:
        in_specs.append(pl.BlockSpec((1, D), const2))
        args.append(final_norm.reshape(1, D))
    scratch += [
        pltpu.VMEM((D, 2 * D_FF), bf16),
        pltpu.VMEM((D_FF, D), bf16),
        pltpu.VMEM((W_SLOTS, D // W_CHUNKS, 2 * D_FF), f32),
        pltpu.VMEM((W_SLOTS, D_FF // W_CHUNKS, D), f32),
        pltpu.SemaphoreType.DMA((W_SLOTS,)),
        pltpu.SemaphoreType.DMA((W_SLOTS,)),
    ]
    return pl.pallas_call(
        functools.partial(_ffn_kernel, tm=tm, mi=mi, mixer=mixer, final=last is not None, split=split, sel=sel),
        grid=(rows_out // tm,),
        in_specs=in_specs,
        out_specs=pl.BlockSpec((tm, D), lambda i: (i, 0)),
        out_shape=jax.ShapeDtypeStruct((rows_out, D), f32),
        scratch_shapes=scratch,
        compiler_params=pltpu.CompilerParams(
            dimension_semantics=("arbitrary",), vmem_limit_bytes=56 * MIB),
        name={None: "ffn", "even": "ffn_even", "attn": "ffn_final"}[mixer],
    )(*args)


HALO = 8


def _place_even_w(dst, rows, piece):
    g0 = 4 * W_A
    g1 = g0 + 4 * HEADS_A
    dst[rows, 0:g0] = piece[:, 0:g0].astype(bf16)
    dst[rows, g0:g0 + 2 * W_A] = piece[:, g1:g1 + 2 * W_A].astype(bf16)
    lane = lax.broadcasted_iota(jnp.int32, (piece.shape[0], LANES), 1)
    dst[rows, g0 + 2 * W_A:EVEN_COLS] = jnp.where(lane < 4 * HEADS_A, piece[:, g0:g0 + LANES], 0.0).astype(bf16)


def _even_in_kernel(h_ref, hp_ref, hn_ref, mod_ref, w_hbm, cw_ref, qk_ref, vo_ref, uv_ref, g_ref,
                    w_ref, w_stage, w_sem):
    tile = pl.program_id(0)

    @pl.when(tile == 0)
    def _():
        _fetch_cast(w_hbm.at[0], w_ref, w_stage, w_sem, place=_place_even_w)

    tm = TM_PROJ
    n_chunk = tm // CHUNK
    shift, scale = mod_ref[3:4, :], mod_ref[4:5, :]
    n = _modulated(h_ref[...], shift, scale)
    halo = _modulated(jnp.concatenate([hp_ref[...], hn_ref[...]], axis=0), shift, scale)
    n_ext = jnp.concatenate([halo[0:HALO], n, halo[HALO:2 * HALO]], axis=0).astype(bf16)
    n = n.astype(bf16)

    row = lax.broadcasted_iota(jnp.int32, (tm, LANES), 0)
    is_ctx = tile < R_CTX // tm
    pos = jnp.where(is_ctx, row % CTX, row + (jnp.maximum(tile - R_CTX // tm, 0) % (SEQ // tm)) * tm)
    seq_start = pos == 0
    seq_end = pos == jnp.where(is_ctx, CTX - 1, SEQ - 1)

    p = _dot(n_ext, w_ref[:, 0:2 * W_A])
    cur = p[HALO:HALO + tm]
    prv = pltpu.roll(p, 1, 0)[HALO:HALO + tm]
    nxt = pltpu.roll(p, tm + 2 * HALO - 1, 0)[HALO:HALO + tm]
    for cb in range(2 * HEADS_A):
        cs = slice(cb * LANES, (cb + 1) * LANES)
        y = (cw_ref[0:1, cs] * jnp.where(seq_start, 0.0, prv[:, cs]) + cw_ref[1:2, cs] * cur[:, cs]
             + cw_ref[2:3, cs] * jnp.where(seq_end, 0.0, nxt[:, cs]))
        y = y * _sigmoid(y)
        if cb < HEADS_A:
            for c in range(n_chunk):
                r = slice(c * CHUNK, (c + 1) * CHUNK)
                qk_ref[r, cs] = y[r, :].T.astype(bf16)
        else:
            qk_ref[:, cs] = (y * CHUNK ** -0.5).astype(bf16)

    v = _dot(n, w_ref[:, 2 * W_A:3 * W_A])
    for hd in range(HEADS_A):
        cs = slice(hd * LANES, (hd + 1) * LANES)
        for c in range(n_chunk):
            r = slice(c * CHUNK, (c + 1) * CHUNK)
            vo_ref[r, cs] = v[r, cs].T.astype(bf16)
    vo_ref[:, W_A:2 * W_A] = _sigmoid(_dot(n, w_ref[:, 3 * W_A:4 * W_A])).astype(bf16)
    uv_ref[...] = _gelu_tanh(_dot(n, w_ref[:, 2048:3072])).astype(bf16)
    g_ref[...] = _dot(n, w_ref[:, 3072:3200])


def _even_in(h, mods, w, conv_w):
    tm = TM_PROJ
    out_map = lambda i: (i, 0)
    halo_blocks = tm // HALO
    last_halo = R_ALL // HALO - 1
    return pl.pallas_call(
        _even_in_kernel,
        grid=(R_ALL // tm,),
        in_specs=[
            pl.BlockSpec((tm, D), lambda i: (i, 0)),
            pl.BlockSpec((HALO, D), lambda i: (jnp.maximum(i * halo_blocks - 1, 0), 0)),
            pl.BlockSpec((HALO, D), lambda i: (jnp.minimum((i + 1) * halo_blocks, last_halo), 0)),
            pl.BlockSpec((None, N_MOD, D), lambda i: (_who_flat(i, tm), 0, 0)),
            pl.BlockSpec(memory_space=pl.ANY),
            pl.BlockSpec((3, 2 * W_A), lambda i: (0, 0)),
        ],
        out_specs=[
            pl.BlockSpec((tm, 1024), out_map),
            pl.BlockSpec((tm, 1024), out_map),
            pl.BlockSpec((tm, 1024), out_map),
            pl.BlockSpec((tm, LANES), out_map),
        ],
        out_shape=[
            jax.ShapeDtypeStruct((R_ALL, 1024), bf16),
            jax.ShapeDtypeStruct((R_ALL, 1024), bf16),
            jax.ShapeDtypeStruct((R_ALL, 1024), bf16),
            jax.ShapeDtypeStruct((R_ALL, LANES), f32),
        ],
        scratch_shapes=[
            pltpu.VMEM((D, EVEN_COLS), bf16),
            pltpu.VMEM((W_SLOTS, D // W_CHUNKS, w.shape[-1]), f32),
            pltpu.SemaphoreType.DMA((W_SLOTS,)),
        ],
        compiler_params=pltpu.CompilerParams(
            dimension_semantics=("arbitrary",), vmem_limit_bytes=40 * MIB),
        name="even_in",
    )(h, h, h, mods, w, conv_w)


N_AUG = CHUNK + 16
CHUNKS_PER_ITER = 18


def _chunk_loop(body):
    def group(i, carry):
        for u in range(CHUNKS_PER_ITER):
            carry = body(i * CHUNKS_PER_ITER + u, carry)
        return carry
    lax.fori_loop(0, N_CHUNK // CHUNKS_PER_ITER, group, 0)


def _mlstm_kernel(qc_ref, ql_ref, kc_ref, kl_ref, vc_ref, vl_ref, oc_ref, ol_ref, gc_ref, gl_ref,
                  gb_ref, mn_ref, outc_ref, outl_ref,
                  ks, qts, vts, og, hg, xc, xr, cl, st, cst, mst):
    head = pl.program_id(1)
    rowi = lax.broadcasted_iota(jnp.int32, (CHUNK, CHUNK), 0)
    coli = lax.broadcasted_iota(jnp.int32, (CHUNK, CHUNK), 1)
    lower = coli <= rowi
    upper = coli >= rowi
    tri = jnp.where(lower, 1.0, 0.0).astype(bf16)

    def part(ctx_ref, lat_ref, c):
        return (ctx_ref, c * CHUNK) if c < N_CTX_CHUNK else (lat_ref, (c - N_CTX_CHUNK) * CHUNK)

    for c in range(N_CHUNK):
        lo = c * CHUNK
        for dst, refs in ((qts, (qc_ref, ql_ref)), (ks, (kc_ref, kl_ref)), (vts, (vc_ref, vl_ref)),
                          (og, (oc_ref, ol_ref))):
            src, at = part(*refs, c)
            dst[lo:lo + CHUNK, :] = src[at:at + CHUNK, :]

    @pl.when(head == 0)
    def _():
        kind = (coli // HEADS_A) % 4
        for c in range(N_CHUNK):
            lo = c * CHUNK
            src, at = part(gc_ref, gl_ref, c)
            gt = src[at:at + CHUNK, :] + gb_ref[...]
            lf = jnp.minimum(gt, 0.0) - jnp.log1p(jnp.exp(-jnp.abs(gt)))
            hi, mid, lw = _split3(lf)
            pre = _dot(tri, hi) + _dot(tri, mid) + _dot(tri, lw)
            suf = pre[CHUNK - 1:CHUNK, :] - pre + lf
            x = jnp.where(kind == 1, pre, jnp.where(kind == 3, suf, gt))
            xr[16 * c:16 * c + 16, :] = x.T[0:16, :]
            xc[lo:lo + CHUNK, :] = x

    lane_shift = (LANES - head) % LANES

    def gate_rows(c, d):
        row = c * 16 + 8 * d + head
        return xr[pl.ds(row, 1), :], xr[pl.ds(row + HEADS_A, 1), :]

    def local_state(c, carry):
        lo = pl.multiple_of(c * CHUNK, CHUNK)
        k = ks[pl.ds(lo, CHUNK), :]
        v_t = vts[pl.ds(lo, CHUNK), :].astype(f32)
        lhs = []
        for d in (0, 1):
            ig_r, b_r = gate_rows(c, d)
            g = b_r[:, CHUNK - 1:CHUNK] if d == 0 else b_r[:, 0:1]
            a_r = g - b_r + ig_r
            m_loc = jnp.max(a_r, axis=1, keepdims=True)
            w_r = jnp.exp(a_r - m_loc)
            lhs += [v_t * w_r, jnp.broadcast_to(w_r, (16, LANES))]
            st[d * N_CHUNK + c, 0:1, :] = jnp.broadcast_to(m_loc, (1, LANES))
            st[d * N_CHUNK + c, 1:2, :] = jnp.broadcast_to(g, (1, LANES))
        both = _dot(jnp.concatenate(lhs, axis=0).astype(bf16), k)
        cl[c] = both[0:N_AUG, :]
        cl[N_CHUNK + c] = both[N_AUG:2 * N_AUG, :]
        return carry

    _chunk_loop(local_state)

    cst[...] = jnp.zeros(cst.shape, f32)
    mst[...] = jnp.zeros(mst.shape, f32)

    def scan_step(i, carry):
        for d in (0, 1):
            c = i if d == 0 else jnp.where(i < N_CTX_CHUNK, N_CTX_CHUNK - 1 - i, N_CHUNK + N_CTX_CHUNK - 1 - i)
            idx = d * N_CHUNK + c
            c_loc = cl[idx]
            m_loc = st[idx, 0:1, :]
            g = st[idx, 1:2, :]
            c_prev = cst[d]
            m_prev = mst[d, 0:1, :]
            m_new = jnp.maximum(g + m_prev, m_loc)
            dec = jnp.exp(g + m_prev - m_new)
            add = jnp.exp(m_loc - m_new)
            cl[idx] = c_prev
            st[idx, 2:3, :] = m_prev
            cst[d] = dec * c_prev + add * c_loc
            mst[d, 0:1, :] = m_new
        return carry

    lax.fori_loop(0, N_CHUNK, scan_step, 0)

    def outputs(c, carry):
        lo = pl.multiple_of(c * CHUNK, CHUNK)
        k = ks[pl.ds(lo, CHUNK), :]
        q_t = qts[pl.ds(lo, CHUNK), :]
        v_aug = jnp.concatenate([vts[pl.ds(lo, CHUNK), :], jnp.ones((16, LANES), bf16)], axis=0)
        q_f = q_t.astype(f32)
        s_t = _dot(k, q_t)
        x_c = pltpu.roll(xc[pl.ds(lo, CHUNK), :], lane_shift, 1)
        hs = None
        for d in (0, 1):
            _, b_r = gate_rows(c, d)
            idx = d * N_CHUNK + c
            r_c = x_c[:, 8 * d:8 * d + 1] - x_c[:, 8 * d + 4:8 * d + 5]
            dm = jnp.where(upper if d == 0 else lower, b_r + r_c, -jnp.inf)
            e_r = b_r + st[idx, 2:3, :]
            m_t = jnp.maximum(e_r, jnp.max(dm, axis=0, keepdims=True))
            p_t = s_t * jnp.exp(dm - m_t)
            inter = jnp.exp(e_r - m_t)
            lhs = jnp.concatenate([v_aug, cl[idx].astype(bf16)], axis=1)
            rhs = jnp.concatenate([p_t, q_f * inter], axis=0).astype(bf16)
            nd = _dot(lhs, rhs)
            den = nd[CHUNK:CHUNK + 1, :]
            h_d = nd[0:CHUNK, :] * (1.0 / jnp.maximum(jnp.abs(den), jnp.exp(-m_t)))
            hs = h_d if hs is None else hs + h_d
        hn = hs * lax.rsqrt(jnp.mean(hs * hs, axis=0, keepdims=True) + EPS) * mn_ref[...]
        hg[pl.ds(lo, CHUNK), :] = (og[pl.ds(lo, CHUNK), :].astype(f32) * hn.T).astype(bf16)
        return carry

    _chunk_loop(outputs)
    outc_ref[...] = hg[0:CTX, :]
    outl_ref[...] = hg[CTX:TOK, :]


def _mlstm(qk, vo, gates, gate_b, mnorm):
    lat0 = R_CTX // SEQ
    ctx = lambda col0: pl.BlockSpec((CTX, LANES), lambda b, h: (b, col0 + h))
    lat = lambda col0: pl.BlockSpec((SEQ, LANES), lambda b, h: (lat0 + b, col0 + h))
    return pl.pallas_call(
        _mlstm_kernel,
        grid=(B, HEADS_A),
        in_specs=[
            ctx(0), lat(0), ctx(HEADS_A), lat(HEADS_A),
            ctx(0), lat(0), ctx(HEADS_A), lat(HEADS_A),
            pl.BlockSpec((CTX, LANES), lambda b, h: (b, 0)),
            pl.BlockSpec((SEQ, LANES), lambda b, h: (lat0 + b, 0)),
            pl.BlockSpec((1, LANES), lambda b, h: (0, 0)),
            pl.BlockSpec((None, CHUNK, LANES), lambda b, h: (h, 0, 0)),
        ],
        out_specs=[
            pl.BlockSpec((CTX, LANES), lambda b, h: (b, h)),
            pl.BlockSpec((SEQ, LANES), lambda b, h: (b, h)),
        ],
        out_shape=[
            jax.ShapeDtypeStruct((R_CTX, W_A), bf16),
            jax.ShapeDtypeStruct((R_LAT, W_A), bf16),
        ],
        scratch_shapes=[
            pltpu.VMEM((TOK, LANES), bf16),
            pltpu.VMEM((TOK, LANES), bf16),
            pltpu.VMEM((TOK, LANES), bf16),
            pltpu.VMEM((TOK, LANES), bf16),
            pltpu.VMEM((TOK, LANES), bf16),
            pltpu.VMEM((TOK, LANES), f32),
            pltpu.VMEM((16 * N_CHUNK, LANES), f32),
            pltpu.VMEM((2 * N_CHUNK, CHUNK + 16, LANES), f32),
            pltpu.VMEM((2 * N_CHUNK, 8, LANES), f32),
            pltpu.VMEM((2, CHUNK + 16, LANES), f32),
            pltpu.VMEM((2, 8, LANES), f32),
        ],
        compiler_params=pltpu.CompilerParams(
            dimension_semantics=("parallel", "arbitrary"), vmem_limit_bytes=40 * MIB),
        name="mlstm",
    )(qk, qk, qk, qk, vo, vo, vo, vo, gates, gates, gate_b, mnorm)


def _odd_in_kernel(h_ref, mod_ref, w_ref, cos_ref, sin_ref, o_ref):
    n = _modulated(h_ref[...], mod_ref[3:4, :], mod_ref[4:5, :]).astype(bf16)
    cos = cos_ref[...]
    sin = sin_ref[...]
    n_rot = (HEADS_C + KV_HEADS) * DH // LANES
    y = _dot(n, w_ref[:, 0:n_rot * LANES])
    for c in range(0, n_rot, 2):
        x1 = y[:, c * LANES:(c + 1) * LANES]
        x2 = y[:, (c + 1) * LANES:(c + 2) * LANES]
        r1 = x1 * cos - x2 * sin
        r2 = x1 * sin + x2 * cos
        if c < HEADS_C * DH // LANES:
            r1 = r1 * (DH ** -0.5 * LOG2E)
            r2 = r2 * (DH ** -0.5 * LOG2E)
        o_ref[:, c * LANES:(c + 1) * LANES] = r1.astype(bf16)
        o_ref[:, (c + 1) * LANES:(c + 2) * LANES] = r2.astype(bf16)
    v0 = n_rot * LANES
    o_ref[:, v0:QKV] = _dot(n, w_ref[:, v0:QKV]).astype(bf16)


def _odd_in(h, mods, w, cos_t, sin_t):
    tm = TM_ODD_IN
    n_ctx = R_CTX // tm
    per_b = SEQ // tm
    rope_map = lambda i: (jnp.where(i < n_ctx, 0, 1 + jnp.maximum(i - n_ctx, 0) % per_b), 0)
    return pl.pallas_call(
        _odd_in_kernel,
        grid=(R_ALL // tm,),
        in_specs=[
            pl.BlockSpec((tm, D), lambda i: (i, 0)),
            pl.BlockSpec((None, N_MOD, D), lambda i: (_who_flat(i, tm), 0, 0)),
            pl.BlockSpec((D, QKV), lambda i: (0, 0), pipeline_mode=pl.Buffered(1)),
            pl.BlockSpec((tm, LANES), rope_map),
            pl.BlockSpec((tm, LANES), rope_map),
        ],
        out_specs=pl.BlockSpec((tm, QKV), lambda i: (i, 0)),
        out_shape=jax.ShapeDtypeStruct((R_ALL, QKV), bf16),
        compiler_params=pltpu.CompilerParams(
            dimension_semantics=("parallel",), vmem_limit_bytes=32 * MIB),
        name="odd_in",
    )(h, mods, w, cos_t, sin_t)


BLOCKS_PER_ITER = 4


def _attn_kernel(sink_ref, q_ref, kc_ref, kl_ref, vc_ref, vl_ref, o_ref,
                 k_all, vt_all, q_all, s_all, p_all, ot_all):
    kv_w = KV_HEADS * DH
    key = lax.broadcasted_iota(jnp.int32, (CHUNK, CHUNK), 0)
    qry = lax.broadcasted_iota(jnp.int32, (CHUNK, CHUNK), 1)
    far = 1 << 20
    lane_head = lax.broadcasted_iota(jnp.int32, (CHUNK, kv_w), 1) % LANES // (DH // 2)
    neg = -1e30
    n_key = CTX + 3 * CHUNK
    n_slab = n_key // CHUNK

    def stage_keys(k_scr, vt_scr, k_rows, v_rows, row0):
        rows = k_rows.shape[0]
        k_scr[row0:row0 + rows, :] = k_rows
        v_t = v_rows.astype(f32).T.astype(bf16)
        for j in range(KV_HEADS):
            vt_scr[j, 0:DH, row0:row0 + rows] = v_t[j * DH:(j + 1) * DH, :]

    for u in range(BLOCKS_PER_ITER):
        stage_keys(k_all.at[u], vt_all.at[u], kc_ref[...], vc_ref[...], 0)
        for j in range(KV_HEADS):
            vt_all[u, j, DH:DH + 16, :] = jnp.ones((16, n_key), bf16)

    def block(blk, u):
        k_scr, vt_scr, q_scr, s_scr, p_scr, ot_scr = (r.at[u] for r in (k_all, vt_all, q_all, s_all, p_all, ot_all))
        q0 = pl.multiple_of(blk * CHUNK, CHUNK)
        prev_ok = key >= qry + jnp.where(blk > 0, 0, far)
        next_ok = key <= qry - jnp.where(blk < N_BLK - 1, 0, far)
        for n, off in enumerate((-1, 0, 1)):
            src = pl.multiple_of(jnp.clip(blk + off, 0, N_BLK - 1) * CHUNK, CHUNK)
            stage_keys(k_scr, vt_scr, kl_ref[pl.ds(src, CHUNK), :], vl_ref[pl.ds(src, CHUNK), :], CTX + n * CHUNK)

        for j in range(KV_HEADS):
            keep = jnp.where(lane_head == j, 1.0, 0.0).astype(bf16)
            for g in range(GROUP):
                q_scr[j, g * CHUNK:(g + 1) * CHUNK, :] = q_ref[pl.ds(q0, CHUNK), g * kv_w:(g + 1) * kv_w] * keep
            s_scr[j] = _dot_nt(k_scr[...], q_scr[j])

        def scores(j, g, slab):
            s = s_scr[j, slab * CHUNK:(slab + 1) * CHUNK, g * CHUNK:(g + 1) * CHUNK]
            if slab == 2:
                s = jnp.where(prev_ok, s, neg)
            if slab == 4:
                s = jnp.where(next_ok, s, neg)
            return s

        for j in range(KV_HEADS):
            sink_terms = []
            for g in range(GROUP):
                sink = jnp.full((1, CHUNK), sink_ref[j * GROUP + g] * LOG2E, f32)
                m8 = None
                for slab in range(n_slab):
                    part = jnp.max(scores(j, g, slab).reshape(CHUNK // 8, 8, CHUNK), axis=0)
                    m8 = part if m8 is None else jnp.maximum(m8, part)
                m = jnp.maximum(sink, jnp.max(m8, axis=0, keepdims=True))
                for slab in range(n_slab):
                    p = jnp.exp2(scores(j, g, slab) - m)
                    p_scr[j, slab * CHUNK:(slab + 1) * CHUNK, g * CHUNK:(g + 1) * CHUNK] = p.astype(bf16)
                sink_terms.append(jnp.exp2(sink - m))
            acc = _dot(vt_scr[j], p_scr[j])
            den = acc[DH:DH + 1, :] + jnp.concatenate(sink_terms, axis=1)
            ot_scr[j * DH:(j + 1) * DH, :] = acc[0:DH, :] * (1.0 / den)
        for g in range(GROUP):
            o_ref[pl.ds(q0, CHUNK), g * kv_w:(g + 1) * kv_w] = (
                ot_scr[:, g * CHUNK:(g + 1) * CHUNK].T.astype(bf16))

    def blocks(i, carry):
        for u in range(BLOCKS_PER_ITER):
            block(i * BLOCKS_PER_ITER + u, u)
        return carry

    lax.fori_loop(0, N_BLK // BLOCKS_PER_ITER, blocks, 0)


def _attention(qkv, sink):
    kv_w = KV_HEADS * DH
    n_key = CTX + 3 * CHUNK
    k_col = HEADS_C * DH // kv_w
    v_col = k_col + 1
    lat0 = R_CTX // SEQ
    ctx = lambda col: pl.BlockSpec((CTX, kv_w), lambda b: (b, col))
    lat = lambda col: pl.BlockSpec((SEQ, kv_w), lambda b: (lat0 + b, col))
    return pl.pallas_call(
        _attn_kernel,
        grid=(B,),
        in_specs=[
            pl.BlockSpec(memory_space=pltpu.SMEM),
            pl.BlockSpec((SEQ, HEADS_C * DH), lambda b: (lat0 + b, 0)),
            ctx(k_col), lat(k_col), ctx(v_col), lat(v_col),
        ],
        out_specs=pl.BlockSpec((SEQ, HEADS_C * DH), lambda b: (b, 0)),
        out_shape=jax.ShapeDtypeStruct((R_LAT, HEADS_C * DH), bf16),
        scratch_shapes=[
            pltpu.VMEM((BLOCKS_PER_ITER, n_key, kv_w), bf16),
            pltpu.VMEM((BLOCKS_PER_ITER, KV_HEADS, DH + 16, n_key), bf16),
            pltpu.VMEM((BLOCKS_PER_ITER, KV_HEADS, GROUP * CHUNK, kv_w), bf16),
            pltpu.VMEM((BLOCKS_PER_ITER, KV_HEADS, n_key, GROUP * CHUNK), f32),
            pltpu.VMEM((BLOCKS_PER_ITER, KV_HEADS, n_key, GROUP * CHUNK), bf16),
            pltpu.VMEM((BLOCKS_PER_ITER, kv_w, GROUP * CHUNK), f32),
        ],
        compiler_params=pltpu.CompilerParams(
            dimension_semantics=("parallel",), vmem_limit_bytes=56 * MIB),
        name="window_attention",
    )(sink, qkv, qkv, qkv, qkv, qkv)


def _rope_tables():
    rows = SEQ // GRID_W
    row, col = np.meshgrid(np.arange(rows), np.arange(GRID_W), indexing='ij')
    n_freq = DH // 4
    inv = (np.float32(ROPE_BASE) ** (-np.arange(n_freq, dtype=np.float32) / np.float32(n_freq))).astype(np.float32)
    ang = np.concatenate([row.reshape(-1, 1).astype(np.float32) * inv,
                          col.reshape(-1, 1).astype(np.float32) * inv], axis=-1)
    reps = 2 * LANES // DH
    cos = np.tile(np.cos(ang).astype(np.float32), (1, reps))
    sin = np.tile(np.sin(ang).astype(np.float32), (1, reps))
    cos = np.concatenate([np.ones((TM_ODD_IN, LANES), np.float32), cos], axis=0)
    sin = np.concatenate([np.zeros((TM_ODD_IN, LANES), np.float32), sin], axis=0)
    return jnp.asarray(cos), jnp.asarray(sin)


def kernel(x, c, ctx, c_ctx, ada_w, ada_b, ffn_w_in, ffn_w_out, even_w_in, even_w_out, mlstm_conv,
           mlstm_gate_b, mlstm_norm, sgu_norm, sgu_ws, sgu_b, odd_w_qkv, odd_w_out, attn_sink, final_norm):
    cs = jnp.concatenate([c_ctx[None, :], c, jnp.zeros((16 - 1 - B, D), f32)], axis=0)
    mods = _modulation(cs, ada_w, ada_b)[:, :1 + B, :].reshape(2, 1 + B, N_MOD, D)

    fw_in = ffn_w_in
    fw_out = ffn_w_out

    m0 = mods[0]
    h = _ffn((ctx.reshape(R_CTX, D), x.reshape(R_LAT, D)), m0, fw_in, fw_out, sel=(0, 0), mi=0)
    qk, vo, uv, gates = _even_in(h, m0, even_w_in, mlstm_conv[0])
    gate_b = jnp.pad(mlstm_gate_b[0].reshape(1, 4 * HEADS_A), ((0, 0), (0, LANES - 4 * HEADS_A)))
    mnorm_t = jnp.broadcast_to(mlstm_norm[0][:, :, None], (HEADS_A, CHUNK, LANES))
    ha_ctx, ha_lat = _mlstm(qk, vo, gates, gate_b, mnorm_t)
    sbx = jnp.repeat(sgu_b[0].T, LANES, axis=1)
    h = _ffn(h, m0, fw_in, fw_out, sel=(0, 1), mi=6,
             even=(ha_ctx, ha_lat, uv, sgu_norm[0].reshape(1, W_A), sgu_ws[0].astype(bf16), sbx,
                   even_w_out[0].astype(bf16)))

    m1 = mods[1]
    h = _ffn(h, m1, fw_in, fw_out, sel=(1, 0), mi=0, tm=TM_FFN_WIDE)
    cos_t, sin_t = _rope_tables()
    qdim = HEADS_C * DH
    kdim = KV_HEADS * DH
    w_q = odd_w_qkv[0][:, :qdim].reshape(D, KV_HEADS, GROUP, DH // 2, 2).transpose(0, 2, 4, 1, 3).reshape(D, qdim)
    w_k = odd_w_qkv[0][:, qdim:qdim + kdim].reshape(D, KV_HEADS, DH // 2, 2).transpose(0, 3, 1, 2).reshape(D, kdim)
    w_qkv = jnp.concatenate([w_q, w_k, odd_w_qkv[0][:, qdim + kdim:]], axis=1).astype(bf16)
    w_o = odd_w_out[0].reshape(KV_HEADS, GROUP, DH, D).transpose(1, 0, 2, 3).reshape(qdim, D).astype(bf16)
    qkv = _odd_in(h, m1, w_qkv, cos_t, sin_t)
    attn = _attention(qkv, attn_sink[0])
    out = _ffn(h, m1, fw_in, fw_out, sel=(1, 1), mi=6, last=(attn, w_o, final_norm))
    return out.reshape(B, SEQ, D)
```

```python
import functools

import numpy as np
import jax
import jax.numpy as jnp
from jax import lax
from jax.experimental import pallas as pl
from jax.experimental.pallas import tpu as pltpu

f32 = jnp.float32
bf16 = jnp.bfloat16

D = 1024
B = 8
SEQ = 2048
CTX = 256
TOK = CTX + SEQ
GRID_W = 64
N_MOD = 9
D_FF = 2816
EPS = 1e-6
HEADS_A = 4
CHUNK = 128
N_CHUNK = TOK // CHUNK
N_CTX_CHUNK = CTX // CHUNK
W_A = 512
EVEN_COLS = 3200
HEADS_C = 16
KV_HEADS = 4
GROUP = HEADS_C // KV_HEADS
DH = 64
QKV = (HEADS_C + 2 * KV_HEADS) * DH
N_BLK = SEQ // CHUNK
ROPE_BASE = 10000.0
LOG2E = 1.4426950408889634

R_CTX = B * CTX
R_LAT = B * SEQ
R_ALL = R_CTX + R_LAT

LANES = 128
TM_FFN = 512
TM_FFN_WIDE = 1024
TM_PROJ = 512
TM_ODD_IN = 1024
FC = 256
N_FC = D_FF // FC
MIB = 1024 * 1024


def _dot(a, b):
    return jnp.dot(a, b, preferred_element_type=f32)


def _dot_nt(a, b):
    return lax.dot_general(a, b, (((1,), (1,)), ((), ())), preferred_element_type=f32)


def _sigmoid(x):
    return 1.0 / (1.0 + jnp.exp(-x))


def _split3(x):
    hi = x.astype(bf16)
    r1 = x - hi.astype(f32)
    mid = r1.astype(bf16)
    lo = (r1 - mid.astype(f32)).astype(bf16)
    return hi, mid, lo


def _modulated(h, shift, scale):
    ms = jnp.mean(h * h, axis=-1, keepdims=True)
    return h * lax.rsqrt(ms + EPS) * (1.0 + scale) + shift


def _mod_kernel(c_ref, w_ref, b_ref, o_ref):
    x = c_ref[...]
    s = x * _sigmoid(x)
    w = w_ref[...]
    s_hi = s.astype(bf16)
    s_lo = (s - s_hi.astype(f32)).astype(bf16)
    w_hi = w.astype(bf16)
    w_lo = (w - w_hi.astype(f32)).astype(bf16)
    o_ref[...] = _dot(s_hi, w_hi) + _dot(s_hi, w_lo) + _dot(s_lo, w_hi) + b_ref[...]


def _modulation(cs, ada_w, ada_b):
    depth = ada_w.shape[0]
    rows = cs.shape[0]
    n_col = N_MOD * D
    tn = 1024
    return pl.pallas_call(
        _mod_kernel,
        grid=(depth, n_col // tn),
        in_specs=[
            pl.BlockSpec((rows, D), lambda l, j: (0, 0)),
            pl.BlockSpec((None, D, tn), lambda l, j: (l, 0, j)),
            pl.BlockSpec((None, 1, tn), lambda l, j: (l, 0, j)),
        ],
        out_specs=pl.BlockSpec((None, rows, tn), lambda l, j: (l, 0, j)),
        out_shape=jax.ShapeDtypeStruct((depth, rows, n_col), f32),
        compiler_params=pltpu.CompilerParams(
            dimension_semantics=("parallel", "parallel"), vmem_limit_bytes=32 * MIB),
        name="modulation",
    )(cs, ada_w, ada_b.reshape(depth, 1, n_col))


def _who_flat(tile, tm):
    n_ctx = R_CTX // tm
    per_b = SEQ // tm
    return jnp.where(tile < n_ctx, 0, 1 + jnp.maximum(tile - n_ctx, 0) // per_b)


W_CHUNKS = 16
W_SLOTS = 4


def _fetch_cast(src, dst, stage, sem, place=None):
    rows = dst.shape[0] // W_CHUNKS

    def piece(c):
        slot = c % W_SLOTS
        return pltpu.make_async_copy(src.at[pl.ds(c * rows, rows), :], stage.at[slot], sem.at[slot])

    for c in range(W_SLOTS - 1):
        piece(c).start()
    for c in range(W_CHUNKS):
        if c + W_SLOTS - 1 < W_CHUNKS:
            piece(c + W_SLOTS - 1).start()
        piece(c).wait()
        if place is None:
            dst[c * rows:(c + 1) * rows, :] = stage[c % W_SLOTS].astype(bf16)
        else:
            place(dst, slice(c * rows, (c + 1) * rows), stage[c % W_SLOTS])


def _gelu_tanh(x):
    return x * (0.5 * (1.0 + jnp.tanh(0.7978845608028654 * (x + 0.044715 * (x * x * x)))))


def _even_mix(ha, uv_ref, sg_ref, ws_ref, sb_ref, wm_ref, hb_scr):
    u = uv_ref[:, 0:W_A].astype(f32)
    v = uv_ref[:, W_A:2 * W_A].astype(f32)
    vn = (v * lax.rsqrt(jnp.mean(v * v, axis=-1, keepdims=True) + EPS) * sg_ref[...]).astype(bf16)
    n_chunk = hb_scr.shape[0] // CHUNK
    for g in range(W_A // LANES):
        cs = slice(g * LANES, (g + 1) * LANES)
        rhs = jnp.concatenate([vn[n * CHUNK:(n + 1) * CHUNK, cs] for n in range(n_chunk)], axis=1)
        mixed = _dot(ws_ref[g], rhs)
        for n in range(n_chunk):
            r = slice(n * CHUNK, (n + 1) * CHUNK)
            hb_scr[r, cs] = (u[r, cs] * (mixed[:, n * LANES:(n + 1) * LANES] + sb_ref[:, cs])).astype(bf16)
    return _dot(ha, wm_ref[0:W_A, :]) + _dot(hb_scr[...], wm_ref[W_A:2 * W_A, :])


def _ffn_kernel(*refs, tm, mi, mixer, final, split, sel):
    refs = list(refs)
    is_ctx = pl.program_id(0) < R_CTX // tm
    if split:
        c_ref, x_ref = refs[0:2]
        refs = refs[2:]
        read_h = lambda: jnp.where(is_ctx, c_ref[...], x_ref[...])
    else:
        h_ref = refs.pop(0)
        read_h = lambda: h_ref[...]
    wi_ref, wo_ref, wi_stage, wo_stage, wi_sem, wo_sem = refs[-6:]
    refs = refs[:-6]
    if mixer == "attn":
        a_ref, wm_ref = refs[0:2]
        refs = refs[2:]
    elif mixer == "even":
        hac_ref, hax_ref, uv_ref, sg_ref, ws_ref, sb_ref, wm_ref = refs[0:7]
        refs = refs[7:]
    mod_ref, wi_hbm, wo_hbm = refs[0:3]
    refs = refs[3:]
    if final:
        fn_ref = refs.pop(0)
    o_ref, n_scr, acc_scr = refs[0:3]
    refs = refs[3:]

    @pl.when(pl.program_id(0) == 0)
    def _():
        _fetch_cast(wi_hbm.at[sel[0], sel[1]], wi_ref, wi_stage, wi_sem)
        _fetch_cast(wo_hbm.at[sel[0], sel[1]], wo_ref, wo_stage, wo_sem)

    if mixer is not None:
        h_scr = refs.pop(0)
        if mixer == "attn":
            y = _dot(a_ref[...], wm_ref[...])
        else:
            ha = jnp.where(is_ctx, hac_ref[...], hax_ref[...])
            y = _even_mix(ha, uv_ref, sg_ref, ws_ref, sb_ref, wm_ref, refs.pop(0))
        h_scr[...] = read_h() + mod_ref[5:6, :] * y
        read_h = lambda: h_scr[...]
    n_scr[...] = _modulated(read_h(), mod_ref[mi:mi + 1, :], mod_ref[mi + 1:mi + 2, :]).astype(bf16)

    for j in range(N_FC):
        n = n_scr[...]
        g = _dot(n, wi_ref[:, j * FC:(j + 1) * FC])
        u = _dot(n, wi_ref[:, D_FF + j * FC:D_FF + (j + 1) * FC])
        a = (g * _sigmoid(g) * u).astype(bf16)
        y = _dot(a, wo_ref[j * FC:(j + 1) * FC, :])
        if j == 0:
            acc_scr[...] = y
        else:
            acc_scr[...] += y
    out = read_h() + (0.5 * mod_ref[mi + 2:mi + 3, :]) * acc_scr[...]
    if final:
        ms = jnp.mean(out * out, axis=-1, keepdims=True)
        out = out * lax.rsqrt(ms + EPS) * fn_ref[...]
    o_ref[...] = out


def _ffn(h, mods, w_in, w_out, *, sel, mi, even=None, last=None, tm=TM_FFN):
    n_ctx = R_CTX // tm
    tile0 = n_ctx if last is not None else 0
    split = isinstance(h, tuple)
    const2 = lambda i: (0, 0)
    ctx_map = lambda i: (jnp.minimum(i, n_ctx - 1), 0)
    lat_map = lambda i: (jnp.maximum(i - n_ctx, 0), 0)
    if split:
        rows_out = R_ALL
        in_specs = [pl.BlockSpec((tm, D), ctx_map), pl.BlockSpec((tm, D), lat_map)]
        args = list(h)
    else:
        rows_out = h.shape[0] - tile0 * tm
        in_specs = [pl.BlockSpec((tm, D), lambda i: (i + tile0, 0))]
        args = [h]
    scratch = [pltpu.VMEM((tm, D), bf16), pltpu.VMEM((tm, D), f32)]
    mixer = None
    if last is not None:
        mixer = "attn"
        attn, w_attn, final_norm = last
        in_specs += [
            pl.BlockSpec((tm, D), lambda i: (i, 0)),
            pl.BlockSpec((D, D), const2, pipeline_mode=pl.Buffered(1)),
        ]
        args += [attn, w_attn]
        scratch.append(pltpu.VMEM((tm, D), f32))
    elif even is not None:
        mixer = "even"
        in_specs += [
            pl.BlockSpec((tm, W_A), ctx_map),
            pl.BlockSpec((tm, W_A), lat_map),
            pl.BlockSpec((tm, 2 * W_A), lambda i: (i, 0)),
            pl.BlockSpec((1, W_A), const2),
            pl.BlockSpec((W_A // LANES, CHUNK, CHUNK), lambda i: (0, 0, 0)),
            pl.BlockSpec((CHUNK, W_A), const2),
            pl.BlockSpec((2 * W_A, D), const2, pipeline_mode=pl.Buffered(1)),
        ]
        args += list(even)
        scratch += [pltpu.VMEM((tm, D), f32), pltpu.VMEM((tm, W_A), bf16)]
    in_specs += [
        pl.BlockSpec((None, N_MOD, D), lambda i: (_who_flat(i + tile0, tm), 0, 0)),
        pl.BlockSpec(memory_space=pl.ANY),
        pl.BlockSpec(memory_space=pl.ANY),
    ]
    args += [mods, w_in, w_out]
    if last is not None:
        in_specs.append(pl.BlockSpec((1, D), const2))
        args.append(final_norm.reshape(1, D))
    scratch += [
        pltpu.VMEM((D, 2 * D_FF), bf16),
        pltpu.VMEM((D_FF, D), bf16),
        pltpu.VMEM((W_SLOTS, D // W_CHUNKS, 2 * D_FF), f32),
        pltpu.VMEM((W_SLOTS, D_FF // W_CHUNKS, D), f32),
        pltpu.SemaphoreType.DMA((W_SLOTS,)),
        pltpu.SemaphoreType.DMA((W_SLOTS,)),
    ]
    return pl.pallas_call(
        functools.partial(_ffn_kernel, tm=tm, mi=mi, mixer=mixer, final=last is not None, split=split, sel=sel),
        grid=(rows_out // tm,),
        in_specs=in_specs,
        out_specs=pl.BlockSpec((tm, D), lambda i: (i, 0)),
        out_shape=jax.ShapeDtypeStruct((rows_out, D), f32),
        scratch_shapes=scratch,
        compiler_params=pltpu.CompilerParams(
            dimension_semantics=("arbitrary",), vmem_limit_bytes=56 * MIB),
        name={None: "ffn", "even": "ffn_even", "attn": "ffn_final"}[mixer],
    )(*args)


HALO = 8


def _place_even_w(dst, rows, piece):
    g0 = 4 * W_A
    g1 = g0 + 4 * HEADS_A
    dst[rows, 0:g0] = piece[:, 0:g0].astype(bf16)
    dst[rows, g0:g0 + 2 * W_A] = piece[:, g1:g1 + 2 * W_A].astype(bf16)
    lane = lax.broadcasted_iota(jnp.int32, (piece.shape[0], LANES), 1)
    dst[rows, g0 + 2 * W_A:EVEN_COLS] = jnp.where(lane < 4 * HEADS_A, piece[:, g0:g0 + LANES], 0.0).astype(bf16)


def _even_in_kernel(h_ref, hp_ref, hn_ref, mod_ref, w_hbm, cw_ref, qk_ref, vo_ref, uv_ref, g_ref,
                    w_ref, w_stage, w_sem):
    tile = pl.program_id(0)

    @pl.when(tile == 0)
    def _():
        _fetch_cast(w_hbm.at[0], w_ref, w_stage, w_sem, place=_place_even_w)

    tm = TM_PROJ
    n_chunk = tm // CHUNK
    shift, scale = mod_ref[3:4, :], mod_ref[4:5, :]
    n = _modulated(h_ref[...], shift, scale)
    halo = _modulated(jnp.concatenate([hp_ref[...], hn_ref[...]], axis=0), shift, scale)
    n_ext = jnp.concatenate([halo[0:HALO], n, halo[HALO:2 * HALO]], axis=0).astype(bf16)
    n = n.astype(bf16)

    row = lax.broadcasted_iota(jnp.int32, (tm, LANES), 0)
    is_ctx = tile < R_CTX // tm
    pos = jnp.where(is_ctx, row % CTX, row + (jnp.maximum(tile - R_CTX // tm, 0) % (SEQ // tm)) * tm)
    seq_start = pos == 0
    seq_end = pos == jnp.where(is_ctx, CTX - 1, SEQ - 1)

    p = _dot(n_ext, w_ref[:, 0:2 * W_A])
    cur = p[HALO:HALO + tm]
    prv = pltpu.roll(p, 1, 0)[HALO:HALO + tm]
    nxt = pltpu.roll(p, tm + 2 * HALO - 1, 0)[HALO:HALO + tm]
    for cb in range(2 * HEADS_A):
        cs = slice(cb * LANES, (cb + 1) * LANES)
        y = (cw_ref[0:1, cs] * jnp.where(seq_start, 0.0, prv[:, cs]) + cw_ref[1:2, cs] * cur[:, cs]
             + cw_ref[2:3, cs] * jnp.where(seq_end, 0.0, nxt[:, cs]))
        y = y * _sigmoid(y)
        if cb < HEADS_A:
            for c in range(n_chunk):
                r = slice(c * CHUNK, (c + 1) * CHUNK)
                qk_ref[r, cs] = y[r, :].T.astype(bf16)
        else:
            qk_ref[:, cs] = (y * CHUNK ** -0.5).astype(bf16)

    v = _dot(n, w_ref[:, 2 * W_A:3 * W_A])
    for hd in range(HEADS_A):
        cs = slice(hd * LANES, (hd + 1) * LANES)
        for c in range(n_chunk):
            r = slice(c * CHUNK, (c + 1) * CHUNK)
            vo_ref[r, cs] = v[r, cs].T.astype(bf16)
    vo_ref[:, W_A:2 * W_A] = _sigmoid(_dot(n, w_ref[:, 3 * W_A:4 * W_A])).astype(bf16)
    uv_ref[...] = _gelu_tanh(_dot(n, w_ref[:, 4 * W_A:6 * W_A])).astype(bf16)
    g_ref[...] = _dot(n, w_ref[:, 6 * W_A:EVEN_COLS])


def _even_in(h, mods, w, conv_w):
    tm = TM_PROJ
    out_map = lambda i: (i, 0)
    halo_blocks = tm // HALO
    last_halo = R_ALL // HALO - 1
    return pl.pallas_call(
        _even_in_kernel,
        grid=(R_ALL // tm,),
        in_specs=[
            pl.BlockSpec((tm, D), lambda i: (i, 0)),
            pl.BlockSpec((HALO, D), lambda i: (jnp.maximum(i * halo_blocks - 1, 0), 0)),
            pl.BlockSpec((HALO, D), lambda i: (jnp.minimum((i + 1) * halo_blocks, last_halo), 0)),
            pl.BlockSpec((None, N_MOD, D), lambda i: (_who_flat(i, tm), 0, 0)),
            pl.BlockSpec(memory_space=pl.ANY),
            pl.BlockSpec((3, 2 * W_A), lambda i: (0, 0)),
        ],
        out_specs=[
            pl.BlockSpec((tm, 1024), out_map),
            pl.BlockSpec((tm, 1024), out_map),
            pl.BlockSpec((tm, 1024), out_map),
            pl.BlockSpec((tm, LANES), out_map),
        ],
        out_shape=[
            jax.ShapeDtypeStruct((R_ALL, 1024), bf16),
            jax.ShapeDtypeStruct((R_ALL, 1024), bf16),
            jax.ShapeDtypeStruct((R_ALL, 1024), bf16),
            jax.ShapeDtypeStruct((R_ALL, LANES), f32),
        ],
        scratch_shapes=[
            pltpu.VMEM((D, EVEN_COLS), bf16),
            pltpu.VMEM((W_SLOTS, D // W_CHUNKS, w.shape[-1]), f32),
            pltpu.SemaphoreType.DMA((W_SLOTS,)),
        ],
        compiler_params=pltpu.CompilerParams(
            dimension_semantics=("arbitrary",), vmem_limit_bytes=40 * MIB),
        name="even_in",
    )(h, h, h, mods, w, conv_w)


N_AUG = CHUNK + 16
CHUNKS_PER_ITER = 18


def _chunk_loop(body):
    def group(i, carry):
        for u in range(CHUNKS_PER_ITER):
            carry = body(i * CHUNKS_PER_ITER + u, carry)
        return carry
    lax.fori_loop(0, N_CHUNK // CHUNKS_PER_ITER, group, 0)


def _mlstm_kernel(qc_ref, ql_ref, kc_ref, kl_ref, vc_ref, vl_ref, oc_ref, ol_ref, gc_ref, gl_ref,
                  gb_ref, mn_ref, outc_ref, outl_ref,
                  ks, qts, vts, og, hg, xc, xr, cl, st, cst, mst):
    head = pl.program_id(1)
    rowi = lax.broadcasted_iota(jnp.int32, (CHUNK, CHUNK), 0)
    coli = lax.broadcasted_iota(jnp.int32, (CHUNK, CHUNK), 1)
    lower = coli <= rowi
    upper = coli >= rowi
    tri = jnp.where(lower, 1.0, 0.0).astype(bf16)

    def part(ctx_ref, lat_ref, c):
        return (ctx_ref, c * CHUNK) if c < N_CTX_CHUNK else (lat_ref, (c - N_CTX_CHUNK) * CHUNK)

    for c in range(N_CHUNK):
        lo = c * CHUNK
        for dst, refs in ((qts, (qc_ref, ql_ref)), (ks, (kc_ref, kl_ref)), (vts, (vc_ref, vl_ref)),
                          (og, (oc_ref, ol_ref))):
            src, at = part(*refs, c)
            dst[lo:lo + CHUNK, :] = src[at:at + CHUNK, :]

    @pl.when(head == 0)
    def _():
        kind = (coli // HEADS_A) % 4
        for c in range(N_CHUNK):
            lo = c * CHUNK
            src, at = part(gc_ref, gl_ref, c)
            gt = src[at:at + CHUNK, :] + gb_ref[...]
            lf = jnp.minimum(gt, 0.0) - jnp.log1p(jnp.exp(-jnp.abs(gt)))
            hi, mid, lw = _split3(lf)
            pre = _dot(tri, hi) + _dot(tri, mid) + _dot(tri, lw)
            suf = pre[CHUNK - 1:CHUNK, :] - pre + lf
            x = jnp.where(kind == 1, pre, jnp.where(kind == 3, suf, gt))
            xr[16 * c:16 * c + 16, :] = x.T[0:16, :]
            xc[lo:lo + CHUNK, :] = x

    lane_shift = (LANES - head) % LANES

    def gate_rows(c, d):
        row = c * 16 + 8 * d + head
        return xr[pl.ds(row, 1), :], xr[pl.ds(row + HEADS_A, 1), :]

    def local_state(c, carry):
        lo = pl.multiple_of(c * CHUNK, CHUNK)
        k = ks[pl.ds(lo, CHUNK), :]
        v_t = vts[pl.ds(lo, CHUNK), :].astype(f32)
        lhs = []
        for d in (0, 1):
            ig_r, b_r = gate_rows(c, d)
            g = b_r[:, CHUNK - 1:CHUNK] if d == 0 else b_r[:, 0:1]
            a_r = g - b_r + ig_r
            m_loc = jnp.max(a_r, axis=1, keepdims=True)
            w_r = jnp.exp(a_r - m_loc)
            lhs += [v_t * w_r, jnp.broadcast_to(w_r, (16, LANES))]
            st[d * N_CHUNK + c, 0:1, :] = jnp.broadcast_to(m_loc, (1, LANES))
            st[d * N_CHUNK + c, 1:2, :] = jnp.broadcast_to(g, (1, LANES))
        both = _dot(jnp.concatenate(lhs, axis=0).astype(bf16), k)
        cl[c] = both[0:N_AUG, :]
        cl[N_CHUNK + c] = both[N_AUG:2 * N_AUG, :]
        return carry

    _chunk_loop(local_state)

    cst[...] = jnp.zeros(cst.shape, f32)
    mst[...] = jnp.zeros(mst.shape, f32)

    def scan_step(i, carry):
        for d in (0, 1):
            c = i if d == 0 else jnp.where(i < N_CTX_CHUNK, N_CTX_CHUNK - 1 - i, N_CHUNK + N_CTX_CHUNK - 1 - i)
            idx = d * N_CHUNK + c
            c_loc = cl[idx]
            m_loc = st[idx, 0:1, :]
            g = st[idx, 1:2, :]
            c_prev = cst[d]
            m_prev = mst[d, 0:1, :]
            m_new = jnp.maximum(g + m_prev, m_loc)
            dec = jnp.exp(g + m_prev - m_new)
            add = jnp.exp(m_loc - m_new)
            cl[idx] = c_prev
            st[idx, 2:3, :] = m_prev
            cst[d] = dec * c_prev + add * c_loc
            mst[d, 0:1, :] = m_new
        return carry

    lax.fori_loop(0, N_CHUNK, scan_step, 0)

    def outputs(c, carry):
        lo = pl.multiple_of(c * CHUNK, CHUNK)
        k = ks[pl.ds(lo, CHUNK), :]
        q_t = qts[pl.ds(lo, CHUNK), :]
        v_aug = jnp.concatenate([vts[pl.ds(lo, CHUNK), :], jnp.ones((16, LANES), bf16)], axis=0)
        q_f = q_t.astype(f32)
        s_t = _dot(k, q_t)
        x_c = pltpu.roll(xc[pl.ds(lo, CHUNK), :], lane_shift, 1)
        hs = None
        for d in (0, 1):
            _, b_r = gate_rows(c, d)
            idx = d * N_CHUNK + c
            r_c = x_c[:, 8 * d:8 * d + 1] - x_c[:, 8 * d + 4:8 * d + 5]
            dm = jnp.where(upper if d == 0 else lower, b_r + r_c, -jnp.inf)
            e_r = b_r + st[idx, 2:3, :]
            m_t = jnp.maximum(e_r, jnp.max(dm, axis=0, keepdims=True))
            p_t = s_t * jnp.exp(dm - m_t)
            inter = jnp.exp(e_r - m_t)
            lhs = jnp.concatenate([v_aug, cl[idx].astype(bf16)], axis=1)
            rhs = jnp.concatenate([p_t, q_f * inter], axis=0).astype(bf16)
            nd = _dot(lhs, rhs)
            den = nd[CHUNK:CHUNK + 1, :]
            h_d = nd[0:CHUNK, :] * (1.0 / jnp.maximum(jnp.abs(den), jnp.exp(-m_t)))
            hs = h_d if hs is None else hs + h_d
        hn = hs * lax.rsqrt(jnp.mean(hs * hs, axis=0, keepdims=True) + EPS) * mn_ref[...]
        hg[pl.ds(lo, CHUNK), :] = (og[pl.ds(lo, CHUNK), :].astype(f32) * hn.T).astype(bf16)
        return carry

    _chunk_loop(outputs)
    outc_ref[...] = hg[0:CTX, :]
    outl_ref[...] = hg[CTX:TOK, :]


def _mlstm(qk, vo, gates, gate_b, mnorm):
    lat0 = R_CTX // SEQ
    ctx = lambda col0: pl.BlockSpec((CTX, LANES), lambda b, h: (b, col0 + h))
    lat = lambda col0: pl.BlockSpec((SEQ, LANES), lambda b, h: (lat0 + b, col0 + h))
    return pl.pallas_call(
        _mlstm_kernel,
        grid=(B, HEADS_A),
        in_specs=[
            ctx(0), lat(0), ctx(HEADS_A), lat(HEADS_A),
            ctx(0), lat(0), ctx(HEADS_A), lat(HEADS_A),
            pl.BlockSpec((CTX, LANES), lambda b, h: (b, 0)),
            pl.BlockSpec((SEQ, LANES), lambda b, h: (lat0 + b, 0)),
            pl.BlockSpec((1, LANES), lambda b, h: (0, 0)),
            pl.BlockSpec((None, CHUNK, LANES), lambda b, h: (h, 0, 0)),
        ],
        out_specs=[
            pl.BlockSpec((CTX, LANES), lambda b, h: (b, h)),
            pl.BlockSpec((SEQ, LANES), lambda b, h: (b, h)),
        ],
        out_shape=[
            jax.ShapeDtypeStruct((R_CTX, W_A), bf16),
            jax.ShapeDtypeStruct((R_LAT, W_A), bf16),
        ],
        scratch_shapes=[
            pltpu.VMEM((TOK, LANES), bf16),
            pltpu.VMEM((TOK, LANES), bf16),
            pltpu.VMEM((TOK, LANES), bf16),
            pltpu.VMEM((TOK, LANES), bf16),
            pltpu.VMEM((TOK, LANES), bf16),
            pltpu.VMEM((TOK, LANES), f32),
            pltpu.VMEM((16 * N_CHUNK, LANES), f32),
            pltpu.VMEM((2 * N_CHUNK, CHUNK + 16, LANES), f32),
            pltpu.VMEM((2 * N_CHUNK, 8, LANES), f32),
            pltpu.VMEM((2, CHUNK + 16, LANES), f32),
            pltpu.VMEM((2, 8, LANES), f32),
        ],
        compiler_params=pltpu.CompilerParams(
            dimension_semantics=("parallel", "arbitrary"), vmem_limit_bytes=40 * MIB),
        name="mlstm",
    )(qk, qk, qk, qk, vo, vo, vo, vo, gates, gates, gate_b, mnorm)


def _odd_in_kernel(h_ref, mod_ref, w_ref, cos_ref, sin_ref, o_ref):
    n = _modulated(h_ref[...], mod_ref[3:4, :], mod_ref[4:5, :]).astype(bf16)
    cos = cos_ref[...]
    sin = sin_ref[...]
    n_rot = (HEADS_C + KV_HEADS) * DH // LANES
    y = _dot(n, w_ref[:, 0:n_rot * LANES])
    for c in range(0, n_rot, 2):
        x1 = y[:, c * LANES:(c + 1) * LANES]
        x2 = y[:, (c + 1) * LANES:(c + 2) * LANES]
        r1 = x1 * cos - x2 * sin
        r2 = x1 * sin + x2 * cos
        if c < HEADS_C * DH // LANES:
            r1 = r1 * (DH ** -0.5 * LOG2E)
            r2 = r2 * (DH ** -0.5 * LOG2E)
        o_ref[:, c * LANES:(c + 1) * LANES] = r1.astype(bf16)
        o_ref[:, (c + 1) * LANES:(c + 2) * LANES] = r2.astype(bf16)
    v0 = n_rot * LANES
    o_ref[:, v0:QKV] = _dot(n, w_ref[:, v0:QKV]).astype(bf16)


def _odd_in(h, mods, w, cos_t, sin_t):
    tm = TM_ODD_IN
    n_ctx = R_CTX // tm
    per_b = SEQ // tm
    rope_map = lambda i: (jnp.where(i < n_ctx, 0, 1 + jnp.maximum(i - n_ctx, 0) % per_b), 0)
    return pl.pallas_call(
        _odd_in_kernel,
        grid=(R_ALL // tm,),
        in_specs=[
            pl.BlockSpec((tm, D), lambda i: (i, 0)),
            pl.BlockSpec((None, N_MOD, D), lambda i: (_who_flat(i, tm), 0, 0)),
            pl.BlockSpec((D, QKV), lambda i: (0, 0), pipeline_mode=pl.Buffered(1)),
            pl.BlockSpec((tm, LANES), rope_map),
            pl.BlockSpec((tm, LANES), rope_map),
        ],
        out_specs=pl.BlockSpec((tm, QKV), lambda i: (i, 0)),
        out_shape=jax.ShapeDtypeStruct((R_ALL, QKV), bf16),
        compiler_params=pltpu.CompilerParams(
            dimension_semantics=("parallel",), vmem_limit_bytes=32 * MIB),
        name="odd_in",
    )(h, mods, w, cos_t, sin_t)


BLOCKS_PER_ITER = 4


def _attn_kernel(sink_ref, q_ref, kc_ref, kl_ref, vc_ref, vl_ref, o_ref,
                 k_all, vt_all, q_all, s_all, p_all, ot_all):
    kv_w = KV_HEADS * DH
    key = lax.broadcasted_iota(jnp.int32, (CHUNK, CHUNK), 0)
    qry = lax.broadcasted_iota(jnp.int32, (CHUNK, CHUNK), 1)
    far = 1 << 20
    lane_head = lax.broadcasted_iota(jnp.int32, (CHUNK, kv_w), 1) % LANES // (DH // 2)
    neg = -1e30
    n_key = CTX + 3 * CHUNK
    n_slab = n_key // CHUNK

    def stage_keys(k_scr, vt_scr, k_rows, v_rows, row0):
        rows = k_rows.shape[0]
        k_scr[row0:row0 + rows, :] = k_rows
        v_t = v_rows.astype(f32).T.astype(bf16)
        for j in range(KV_HEADS):
            vt_scr[j, 0:DH, row0:row0 + rows] = v_t[j * DH:(j + 1) * DH, :]

    for u in range(BLOCKS_PER_ITER):
        stage_keys(k_all.at[u], vt_all.at[u], kc_ref[...], vc_ref[...], 0)
        for j in range(KV_HEADS):
            vt_all[u, j, DH:DH + 16, :] = jnp.ones((16, n_key), bf16)

    def block(blk, u):
        k_scr, vt_scr, q_scr, s_scr, p_scr, ot_scr = (r.at[u] for r in (k_all, vt_all, q_all, s_all, p_all, ot_all))
        q0 = pl.multiple_of(blk * CHUNK, CHUNK)
        prev_ok = key >= qry + jnp.where(blk > 0, 0, far)
        next_ok = key <= qry - jnp.where(blk < N_BLK - 1, 0, far)
        for n, off in enumerate((-1, 0, 1)):
            src = pl.multiple_of(jnp.clip(blk + off, 0, N_BLK - 1) * CHUNK, CHUNK)
            stage_keys(k_scr, vt_scr, kl_ref[pl.ds(src, CHUNK), :], vl_ref[pl.ds(src, CHUNK), :], CTX + n * CHUNK)

        for j in range(KV_HEADS):
            keep = jnp.where(lane_head == j, 1.0, 0.0).astype(bf16)
            for g in range(GROUP):
                q_scr[j, g * CHUNK:(g + 1) * CHUNK, :] = q_ref[pl.ds(q0, CHUNK), g * kv_w:(g + 1) * kv_w] * keep
            s_scr[j] = _dot_nt(k_scr[...], q_scr[j])

        def scores(j, g, slab):
            s = s_scr[j, slab * CHUNK:(slab + 1) * CHUNK, g * CHUNK:(g + 1) * CHUNK]
            if slab == 2:
                s = jnp.where(prev_ok, s, neg)
            if slab == 4:
                s = jnp.where(next_ok, s, neg)
            return s

        for j in range(KV_HEADS):
            sink_terms = []
            for g in range(GROUP):
                sink = jnp.full((1, CHUNK), sink_ref[j * GROUP + g] * LOG2E, f32)
                m8 = None
                for slab in range(n_slab):
                    part = jnp.max(scores(j, g, slab).reshape(CHUNK // 8, 8, CHUNK), axis=0)
                    m8 = part if m8 is None else jnp.maximum(m8, part)
                m = jnp.maximum(sink, jnp.max(m8, axis=0, keepdims=True))
                for slab in range(n_slab):
                    p = jnp.exp2(scores(j, g, slab) - m)
                    p_scr[j, slab * CHUNK:(slab + 1) * CHUNK, g * CHUNK:(g + 1) * CHUNK] = p.astype(bf16)
                sink_terms.append(jnp.exp2(sink - m))
            acc = _dot(vt_scr[j], p_scr[j])
            den = acc[DH:DH + 1, :] + jnp.concatenate(sink_terms, axis=1)
            ot_scr[j * DH:(j + 1) * DH, :] = acc[0:DH, :] * (1.0 / den)
        for g in range(GROUP):
            o_ref[pl.ds(q0, CHUNK), g * kv_w:(g + 1) * kv_w] = (
                ot_scr[:, g * CHUNK:(g + 1) * CHUNK].T.astype(bf16))

    def blocks(i, carry):
        for u in range(BLOCKS_PER_ITER):
            block(i * BLOCKS_PER_ITER + u, u)
        return carry

    lax.fori_loop(0, N_BLK // BLOCKS_PER_ITER, blocks, 0)


def _attention(qkv, sink):
    kv_w = KV_HEADS * DH
    n_key = CTX + 3 * CHUNK
    k_col = HEADS_C * DH // kv_w
    v_col = k_col + 1
    lat0 = R_CTX // SEQ
    ctx = lambda col: pl.BlockSpec((CTX, kv_w), lambda b: (b, col))
    lat = lambda col: pl.BlockSpec((SEQ, kv_w), lambda b: (lat0 + b, col))
    return pl.pallas_call(
        _attn_kernel,
        grid=(B,),
        in_specs=[
            pl.BlockSpec(memory_space=pltpu.SMEM),
            pl.BlockSpec((SEQ, HEADS_C * DH), lambda b: (lat0 + b, 0)),
            ctx(k_col), lat(k_col), ctx(v_col), lat(v_col),
        ],
        out_specs=pl.BlockSpec((SEQ, HEADS_C * DH), lambda b: (b, 0)),
        out_shape=jax.ShapeDtypeStruct((R_LAT, HEADS_C * DH), bf16),
        scratch_shapes=[
            pltpu.VMEM((BLOCKS_PER_ITER, n_key, kv_w), bf16),
            pltpu.VMEM((BLOCKS_PER_ITER, KV_HEADS, DH + 16, n_key), bf16),
            pltpu.VMEM((BLOCKS_PER_ITER, KV_HEADS, GROUP * CHUNK, kv_w), bf16),
            pltpu.VMEM((BLOCKS_PER_ITER, KV_HEADS, n_key, GROUP * CHUNK), f32),
            pltpu.VMEM((BLOCKS_PER_ITER, KV_HEADS, n_key, GROUP * CHUNK), bf16),
            pltpu.VMEM((BLOCKS_PER_ITER, kv_w, GROUP * CHUNK), f32),
        ],
        compiler_params=pltpu.CompilerParams(
            dimension_semantics=("parallel",), vmem_limit_bytes=56 * MIB),
        name="window_attention",
    )(sink, qkv, qkv, qkv, qkv, qkv)


def _rope_tables():
    rows = SEQ // GRID_W
    row, col = np.meshgrid(np.arange(rows), np.arange(GRID_W), indexing='ij')
    n_freq = DH // 4
    inv = (np.float32(ROPE_BASE) ** (-np.arange(n_freq, dtype=np.float32) / np.float32(n_freq))).astype(np.float32)
    ang = np.concatenate([row.reshape(-1, 1).astype(np.float32) * inv,
                          col.reshape(-1, 1).astype(np.float32) * inv], axis=-1)
    reps = 2 * LANES // DH
    cos = np.tile(np.cos(ang).astype(np.float32), (1, reps))
    sin = np.tile(np.sin(ang).astype(np.float32), (1, reps))
    cos = np.concatenate([np.ones((TM_ODD_IN, LANES), np.float32), cos], axis=0)
    sin = np.concatenate([np.zeros((TM_ODD_IN, LANES), np.float32), sin], axis=0)
    return jnp.asarray(cos), jnp.asarray(sin)


def kernel(x, c, ctx, c_ctx, ada_w, ada_b, ffn_w_in, ffn_w_out, even_w_in, even_w_out, mlstm_conv,
           mlstm_gate_b, mlstm_norm, sgu_norm, sgu_ws, sgu_b, odd_w_qkv, odd_w_out, attn_sink, final_norm):
    cs = jnp.concatenate([c_ctx[None, :], c, jnp.zeros((16 - 1 - B, D), f32)], axis=0)
    mods = _modulation(cs, ada_w, ada_b)[:, :1 + B, :].reshape(2, 1 + B, N_MOD, D)

    fw_in, fw_out = ffn_w_in, ffn_w_out

    m0 = mods[0]
    h = _ffn((ctx.reshape(R_CTX, D), x.reshape(R_LAT, D)), m0, fw_in, fw_out, sel=(0, 0), mi=0)
    qk, vo, uv, gates = _even_in(h, m0, even_w_in, mlstm_conv[0])
    gate_b = jnp.pad(mlstm_gate_b[0].reshape(1, 4 * HEADS_A), ((0, 0), (0, LANES - 4 * HEADS_A)))
    mnorm_t = jnp.broadcast_to(mlstm_norm[0][:, :, None], (HEADS_A, CHUNK, LANES))
    ha_ctx, ha_lat = _mlstm(qk, vo, gates, gate_b, mnorm_t)
    sbx = jnp.repeat(sgu_b[0].T, LANES, axis=1)
    h = _ffn(h, m0, fw_in, fw_out, sel=(0, 1), mi=6,
             even=(ha_ctx, ha_lat, uv, sgu_norm[0].reshape(1, W_A), sgu_ws[0].astype(bf16), sbx,
                   even_w_out[0].astype(bf16)))

    m1 = mods[1]
    h = _ffn(h, m1, fw_in, fw_out, sel=(1, 0), mi=0, tm=TM_FFN_WIDE)
    cos_t, sin_t = _rope_tables()
    qdim = HEADS_C * DH
    kdim = KV_HEADS * DH
    w_q = odd_w_qkv[0][:, :qdim].reshape(D, KV_HEADS, GROUP, DH // 2, 2).transpose(0, 2, 4, 1, 3).reshape(D, qdim)
    w_k = odd_w_qkv[0][:, qdim:qdim + kdim].reshape(D, KV_HEADS, DH // 2, 2).transpose(0, 3, 1, 2).reshape(D, kdim)
    w_qkv = jnp.concatenate([w_q, w_k, odd_w_qkv[0][:, qdim + kdim:]], axis=1).astype(bf16)
    w_o = odd_w_out[0].reshape(KV_HEADS, GROUP, DH, D).transpose(1, 0, 2, 3).reshape(qdim, D).astype(bf16)
    qkv = _odd_in(h, m1, w_qkv, cos_t, sin_t)
    attn = _attention(qkv, attn_sink[0])
    out = _ffn(h, m1, fw_in, fw_out, sel=(1, 1), mi=6, last=(attn, w_o, final_norm))
    return out.reshape(B, SEQ, D)
```

```python
import functools

import numpy as np
import jax
import jax.numpy as jnp
from jax import lax
from jax.experimental import pallas as pl
from jax.experimental.pallas import tpu as pltpu

f32 = jnp.float32
bf16 = jnp.bfloat16

D = 1024
B = 8
SEQ = 2048
CTX = 256
TOK = CTX + SEQ
GRID_W = 64
N_MOD = 9
D_FF = 2816
EPS = 1e-6
HEADS_A = 4
CHUNK = 128
N_CHUNK = TOK // CHUNK
N_CTX_CHUNK = CTX // CHUNK
W_A = 512
EVEN_COLS = 3200
HEADS_C = 16
KV_HEADS = 4
GROUP = HEADS_C // KV_HEADS
DH = 64
QKV = (HEADS_C + 2 * KV_HEADS) * DH
N_BLK = SEQ // CHUNK
ROPE_BASE = 10000.0
LOG2E = 1.4426950408889634

R_CTX = B * CTX
R_LAT = B * SEQ
R_ALL = R_CTX + R_LAT

LANES = 128
TM_FFN = 512
TM_FFN_WIDE = 1024
TM_PROJ = 512
TM_ODD_IN = 1024
FC = 256
N_FC = D_FF // FC
MIB = 1024 * 1024


def _dot(a, b):
    return jnp.dot(a, b, preferred_element_type=f32)


def _dot_nt(a, b):
    return lax.dot_general(a, b, (((1,), (1,)), ((), ())), preferred_element_type=f32)


def _sigmoid(x):
    return 1.0 / (1.0 + jnp.exp(-x))


def _split3(x):
    hi = x.astype(bf16)
    r1 = x - hi.astype(f32)
    mid = r1.astype(bf16)
    lo = (r1 - mid.astype(f32)).astype(bf16)
    return hi, mid, lo


def _modulated(h, shift, scale):
    ms = jnp.mean(h * h, axis=-1, keepdims=True)
    return h * lax.rsqrt(ms + EPS) * (1.0 + scale) + shift


def _mod_kernel(c_ref, w_ref, b_ref, o_ref):
    x = c_ref[...]
    s = x * _sigmoid(x)
    w = w_ref[...]
    s_hi = s.astype(bf16)
    s_lo = (s - s_hi.astype(f32)).astype(bf16)
    w_hi = w.astype(bf16)
    w_lo = (w - w_hi.astype(f32)).astype(bf16)
    o_ref[...] = _dot(s_hi, w_hi) + _dot(s_hi, w_lo) + _dot(s_lo, w_hi) + b_ref[...]


def _modulation(cs, ada_w, ada_b):
    depth = ada_w.shape[0]
    rows = cs.shape[0]
    n_col = N_MOD * D
    tn = 1024
    return pl.pallas_call(
        _mod_kernel,
        grid=(depth, n_col // tn),
        in_specs=[
            pl.BlockSpec((rows, D), lambda l, j: (0, 0)),
            pl.BlockSpec((None, D, tn), lambda l, j: (l, 0, j)),
            pl.BlockSpec((None, 1, tn), lambda l, j: (l, 0, j)),
        ],
        out_specs=pl.BlockSpec((None, rows, tn), lambda l, j: (l, 0, j)),
        out_shape=jax.ShapeDtypeStruct((depth, rows, n_col), f32),
        compiler_params=pltpu.CompilerParams(
            dimension_semantics=("parallel", "parallel"), vmem_limit_bytes=32 * MIB),
        name="modulation",
    )(cs, ada_w, ada_b.reshape(depth, 1, n_col))


def _who_flat(tile, tm):
    n_ctx = R_CTX // tm
    per_b = SEQ // tm
    return jnp.where(tile < n_ctx, 0, 1 + jnp.maximum(tile - n_ctx, 0) // per_b)


W_CHUNKS = 16
W_SLOTS = 4


def _fetch_cast(src, dst, stage, sem, place=None):
    rows = stage.shape[1]
    chunks = dst.shape[0] // rows

    def piece(c):
        slot = c % W_SLOTS
        return pltpu.make_async_copy(src.at[pl.ds(c * rows, rows), :], stage.at[slot], sem.at[slot])

    for c in range(W_SLOTS - 1):
        piece(c).start()
    for c in range(chunks):
        if c + W_SLOTS - 1 < chunks:
            piece(c + W_SLOTS - 1).start()
        piece(c).wait()
        if place is None:
            dst[c * rows:(c + 1) * rows, :] = stage[c % W_SLOTS].astype(bf16)
        else:
            place(dst, slice(c * rows, (c + 1) * rows), stage[c % W_SLOTS])


def _gelu_tanh(x):
    return x * (0.5 * (1.0 + jnp.tanh(0.7978845608028654 * (x + 0.044715 * (x * x * x)))))


def _even_mix(ha, uv_ref, sg_ref, ws_ref, sb_ref, wm_ref, hb_scr):
    u = uv_ref[:, 0:W_A].astype(f32)
    v = uv_ref[:, W_A:2 * W_A].astype(f32)
    vn = (v * lax.rsqrt(jnp.mean(v * v, axis=-1, keepdims=True) + EPS) * sg_ref[...]).astype(bf16)
    n_chunk = hb_scr.shape[0] // CHUNK
    for g in range(W_A // LANES):
        cs = slice(g * LANES, (g + 1) * LANES)
        rhs = jnp.concatenate([vn[n * CHUNK:(n + 1) * CHUNK, cs] for n in range(n_chunk)], axis=1)
        mixed = _dot(ws_ref[g], rhs)
        for n in range(n_chunk):
            r = slice(n * CHUNK, (n + 1) * CHUNK)
            hb_scr[r, cs] = (u[r, cs] * (mixed[:, n * LANES:(n + 1) * LANES] + sb_ref[:, cs])).astype(bf16)
    return _dot(ha, wm_ref[0:W_A, :]) + _dot(hb_scr[...], wm_ref[W_A:2 * W_A, :])


def _ffn_kernel(*refs, tm, mi, mixer, final, split, sel):
    refs = list(refs)
    is_ctx = pl.program_id(0) < R_CTX // tm
    if split:
        c_ref, x_ref = refs[0:2]
        refs = refs[2:]
        read_h = lambda: jnp.where(is_ctx, c_ref[...], x_ref[...])
    else:
        h_ref = refs.pop(0)
        read_h = lambda: h_ref[...]
    wi_ref, wo_ref, wi_stage, wo_stage, wi_sem, wo_sem = refs[-6:]
    refs = refs[:-6]
    if mixer == "attn":
        a_ref, wm_ref = refs[0:2]
        refs = refs[2:]
    elif mixer == "even":
        hac_ref, hax_ref, uv_ref, sg_ref, ws_ref, sb_ref, wm_ref = refs[0:7]
        refs = refs[7:]
    mod_ref, wi_hbm, wo_hbm = refs[0:3]
    refs = refs[3:]
    if final:
        fn_ref = refs.pop(0)
    o_ref, n_scr, acc_scr = refs[0:3]
    refs = refs[3:]

    @pl.when(pl.program_id(0) == 0)
    def _():
        _fetch_cast(wi_hbm.at[sel[0], sel[1]], wi_ref, wi_stage, wi_sem)
        _fetch_cast(wo_hbm.at[sel[0], sel[1]], wo_ref, wo_stage, wo_sem)

    if mixer is not None:
        if mixer == "attn":
            y = _dot(a_ref[...], wm_ref[...])
        else:
            ha = jnp.where(is_ctx, hac_ref[...], hax_ref[...])
            y = _even_mix(ha, uv_ref, sg_ref, ws_ref, sb_ref, wm_ref, refs.pop(0))
        o_ref[...] = read_h() + mod_ref[5:6, :] * y
        read_h = lambda: o_ref[...]
    n_scr[...] = _modulated(read_h(), mod_ref[mi:mi + 1, :], mod_ref[mi + 1:mi + 2, :]).astype(bf16)

    for j in range(N_FC):
        n = n_scr[...]
        g = _dot(n, wi_ref[:, j * FC:(j + 1) * FC])
        u = _dot(n, wi_ref[:, D_FF + j * FC:D_FF + (j + 1) * FC])
        a = (g * _sigmoid(g) * u).astype(bf16)
        y = _dot(a, wo_ref[j * FC:(j + 1) * FC, :])
        if j == 0:
            acc_scr[...] = y
        else:
            acc_scr[...] += y
    out = read_h() + (0.5 * mod_ref[mi + 2:mi + 3, :]) * acc_scr[...]
    if final:
        ms = jnp.mean(out * out, axis=-1, keepdims=True)
        out = out * lax.rsqrt(ms + EPS) * fn_ref[...]
    o_ref[...] = out


def _ffn(h, mods, w_in, w_out, *, sel, mi, even=None, last=None, tm=TM_FFN):
    n_ctx = R_CTX // tm
    tile0 = n_ctx if last is not None else 0
    split = isinstance(h, tuple)
    const2 = lambda i: (0, 0)
    ctx_map = lambda i: (jnp.minimum(i, n_ctx - 1), 0)
    lat_map = lambda i: (jnp.maximum(i - n_ctx, 0), 0)
    if split:
        rows_out = R_ALL
        in_specs = [pl.BlockSpec((tm, D), ctx_map), pl.BlockSpec((tm, D), lat_map)]
        args = list(h)
    else:
        rows_out = h.shape[0] - tile0 * tm
        in_specs = [pl.BlockSpec((tm, D), lambda i: (i + tile0, 0))]
        args = [h]
    scratch = [pltpu.VMEM((tm, D), bf16), pltpu.VMEM((tm, D), f32)]
    mixer = None
    if last is not None:
        mixer = "attn"
        attn, w_attn, final_norm = last
        in_specs += [
            pl.BlockSpec((tm, D), lambda i: (i, 0)),
            pl.BlockSpec((D, D), const2, pipeline_mode=pl.Buffered(1)),
        ]
        args += [attn, w_attn]
    elif even is not None:
        mixer = "even"
        in_specs += [
            pl.BlockSpec((tm, W_A), ctx_map),
            pl.BlockSpec((tm, W_A), lat_map),
            pl.BlockSpec((tm, 2 * W_A), lambda i: (i, 0)),
            pl.BlockSpec((1, W_A), const2),
            pl.BlockSpec((W_A // LANES, CHUNK, CHUNK), lambda i: (0, 0, 0)),
            pl.BlockSpec((CHUNK, W_A), const2),
            pl.BlockSpec((2 * W_A, D), const2, pipeline_mode=pl.Buffered(1)),
        ]
        args += list(even)
        scratch.append(pltpu.VMEM((tm, W_A), bf16))
    in_specs += [
        pl.BlockSpec((None, N_MOD, D), lambda i: (_who_flat(i + tile0, tm), 0, 0)),
        pl.BlockSpec(memory_space=pl.ANY),
        pl.BlockSpec(memory_space=pl.ANY),
    ]
    args += [mods, w_in, w_out]
    if last is not None:
        in_specs.append(pl.BlockSpec((1, D), const2))
        args.append(final_norm.reshape(1, D))
    scratch += [
        pltpu.VMEM((D, 2 * D_FF), bf16),
        pltpu.VMEM((D_FF, D), bf16),
        pltpu.VMEM((W_SLOTS, D // (2 * W_CHUNKS), 2 * D_FF), f32),
        pltpu.VMEM((W_SLOTS, D_FF // W_CHUNKS, D), f32),
        pltpu.SemaphoreType.DMA((W_SLOTS,)),
        pltpu.SemaphoreType.DMA((W_SLOTS,)),
    ]
    return pl.pallas_call(
        functools.partial(_ffn_kernel, tm=tm, mi=mi, mixer=mixer, final=last is not None, split=split, sel=sel),
        grid=(rows_out // tm,),
        in_specs=in_specs,
        out_specs=pl.BlockSpec((tm, D), lambda i: (i, 0)),
        out_shape=jax.ShapeDtypeStruct((rows_out, D), f32),
        scratch_shapes=scratch,
        compiler_params=pltpu.CompilerParams(
            dimension_semantics=("arbitrary",), vmem_limit_bytes=56 * MIB),
        name={None: "ffn", "even": "ffn_even", "attn": "ffn_final"}[mixer],
    )(*args)


HALO = 8


def _place_even_w(dst, rows, piece):
    g0 = 4 * W_A
    g1 = g0 + 4 * HEADS_A
    dst[rows, 0:g0] = piece[:, 0:g0].astype(bf16)
    dst[rows, g0:g0 + 2 * W_A] = piece[:, g1:g1 + 2 * W_A].astype(bf16)
    lane = lax.broadcasted_iota(jnp.int32, (piece.shape[0], LANES), 1)
    dst[rows, g0 + 2 * W_A:EVEN_COLS] = jnp.where(lane < 4 * HEADS_A, piece[:, g0:g0 + LANES], 0.0).astype(bf16)


def _even_in_kernel(h_ref, hp_ref, hn_ref, mod_ref, w_hbm, cw_ref, qk_ref, vo_ref, uv_ref, g_ref,
                    w_ref, w_stage, w_sem):
    tile = pl.program_id(0)

    @pl.when(tile == 0)
    def _():
        _fetch_cast(w_hbm.at[0], w_ref, w_stage, w_sem, place=_place_even_w)

    tm = TM_PROJ
    n_chunk = tm // CHUNK
    shift, scale = mod_ref[3:4, :], mod_ref[4:5, :]
    n = _modulated(h_ref[...], shift, scale)
    halo = _modulated(jnp.concatenate([hp_ref[...], hn_ref[...]], axis=0), shift, scale)
    n_ext = jnp.concatenate([halo[0:HALO], n, halo[HALO:2 * HALO]], axis=0).astype(bf16)
    n = n.astype(bf16)

    row = lax.broadcasted_iota(jnp.int32, (tm, LANES), 0)
    is_ctx = tile < R_CTX // tm
    pos = jnp.where(is_ctx, row % CTX, row + (jnp.maximum(tile - R_CTX // tm, 0) % (SEQ // tm)) * tm)
    seq_start = pos == 0
    seq_end = pos == jnp.where(is_ctx, CTX - 1, SEQ - 1)

    p = _dot(n_ext, w_ref[:, 0:2 * W_A])
    cur = p[HALO:HALO + tm]
    prv = pltpu.roll(p, 1, 0)[HALO:HALO + tm]
    nxt = pltpu.roll(p, tm + 2 * HALO - 1, 0)[HALO:HALO + tm]
    for cb in range(2 * HEADS_A):
        cs = slice(cb * LANES, (cb + 1) * LANES)
        y = (cw_ref[0:1, cs] * jnp.where(seq_start, 0.0, prv[:, cs]) + cw_ref[1:2, cs] * cur[:, cs]
             + cw_ref[2:3, cs] * jnp.where(seq_end, 0.0, nxt[:, cs]))
        y = y * _sigmoid(y)
        if cb < HEADS_A:
            for c in range(n_chunk):
                r = slice(c * CHUNK, (c + 1) * CHUNK)
                qk_ref[r, cs] = y[r, :].T.astype(bf16)
        else:
            qk_ref[:, cs] = (y * CHUNK ** -0.5).astype(bf16)

    v = _dot(n, w_ref[:, 2 * W_A:3 * W_A])
    for hd in range(HEADS_A):
        cs = slice(hd * LANES, (hd + 1) * LANES)
        for c in range(n_chunk):
            r = slice(c * CHUNK, (c + 1) * CHUNK)
            vo_ref[r, cs] = v[r, cs].T.astype(bf16)
    vo_ref[:, W_A:2 * W_A] = _sigmoid(_dot(n, w_ref[:, 3 * W_A:4 * W_A])).astype(bf16)
    uv_ref[...] = _gelu_tanh(_dot(n, w_ref[:, 4 * W_A:6 * W_A])).astype(bf16)
    g_ref[...] = _dot(n, w_ref[:, 6 * W_A:EVEN_COLS])


def _even_in(h, mods, w, conv_w):
    tm = TM_PROJ
    out_map = lambda i: (i, 0)
    halo_blocks = tm // HALO
    last_halo = R_ALL // HALO - 1
    return pl.pallas_call(
        _even_in_kernel,
        grid=(R_ALL // tm,),
        in_specs=[
            pl.BlockSpec((tm, D), lambda i: (i, 0)),
            pl.BlockSpec((HALO, D), lambda i: (jnp.maximum(i * halo_blocks - 1, 0), 0)),
            pl.BlockSpec((HALO, D), lambda i: (jnp.minimum((i + 1) * halo_blocks, last_halo), 0)),
            pl.BlockSpec((None, N_MOD, D), lambda i: (_who_flat(i, tm), 0, 0)),
            pl.BlockSpec(memory_space=pl.ANY),
            pl.BlockSpec((3, 2 * W_A), lambda i: (0, 0)),
        ],
        out_specs=[
            pl.BlockSpec((tm, 1024), out_map),
            pl.BlockSpec((tm, 1024), out_map),
            pl.BlockSpec((tm, 1024), out_map),
            pl.BlockSpec((tm, LANES), out_map),
        ],
        out_shape=[
            jax.ShapeDtypeStruct((R_ALL, 1024), bf16),
            jax.ShapeDtypeStruct((R_ALL, 1024), bf16),
            jax.ShapeDtypeStruct((R_ALL, 1024), bf16),
            jax.ShapeDtypeStruct((R_ALL, LANES), f32),
        ],
        scratch_shapes=[
            pltpu.VMEM((D, EVEN_COLS), bf16),
            pltpu.VMEM((W_SLOTS, D // W_CHUNKS, w.shape[-1]), f32),
            pltpu.SemaphoreType.DMA((W_SLOTS,)),
        ],
        compiler_params=pltpu.CompilerParams(
            dimension_semantics=("arbitrary",), vmem_limit_bytes=40 * MIB),
        name="even_in",
    )(h, h, h, mods, w, conv_w)


N_AUG = CHUNK + 16
CHUNKS_PER_ITER = 18


def _chunk_loop(body):
    def group(i, carry):
        for u in range(CHUNKS_PER_ITER):
            carry = body(i * CHUNKS_PER_ITER + u, carry)
        return carry
    lax.fori_loop(0, N_CHUNK // CHUNKS_PER_ITER, group, 0)


def _mlstm_kernel(qc_ref, ql_ref, kc_ref, kl_ref, vc_ref, vl_ref, oc_ref, ol_ref, gc_ref, gl_ref,
                  gb_ref, mn_ref, outc_ref, outl_ref,
                  ks, qts, vts, og, hg, xc, xr, cl, st, cst, mst):
    head = pl.program_id(1)
    rowi = lax.broadcasted_iota(jnp.int32, (CHUNK, CHUNK), 0)
    coli = lax.broadcasted_iota(jnp.int32, (CHUNK, CHUNK), 1)
    lower = coli <= rowi
    upper = coli >= rowi
    tri = jnp.where(lower, 1.0, 0.0).astype(bf16)

    def part(ctx_ref, lat_ref, c):
        return (ctx_ref, c * CHUNK) if c < N_CTX_CHUNK else (lat_ref, (c - N_CTX_CHUNK) * CHUNK)

    for c in range(N_CHUNK):
        lo = c * CHUNK
        for dst, refs in ((qts, (qc_ref, ql_ref)), (ks, (kc_ref, kl_ref)), (vts, (vc_ref, vl_ref)),
                          (og, (oc_ref, ol_ref))):
            src, at = part(*refs, c)
            dst[lo:lo + CHUNK, :] = src[at:at + CHUNK, :]

    @pl.when(head == 0)
    def _():
        kind = (coli // HEADS_A) % 4
        for c in range(N_CHUNK):
            lo = c * CHUNK
            src, at = part(gc_ref, gl_ref, c)
            gt = src[at:at + CHUNK, :] + gb_ref[...]
            lf = jnp.minimum(gt, 0.0) - jnp.log1p(jnp.exp(-jnp.abs(gt)))
            hi, mid, lw = _split3(lf)
            pre = _dot(tri, hi) + _dot(tri, mid) + _dot(tri, lw)
            suf = pre[CHUNK - 1:CHUNK, :] - pre + lf
            x = jnp.where(kind == 1, pre, jnp.where(kind == 3, suf, gt))
            xr[16 * c:16 * c + 16, :] = x.T[0:16, :]
            xc[lo:lo + CHUNK, :] = x

    lane_shift = (LANES - head) % LANES

    def gate_rows(c, d):
        row = c * 16 + 8 * d + head
        return xr[pl.ds(row, 1), :], xr[pl.ds(row + HEADS_A, 1), :]

    def local_state(c, carry):
        lo = pl.multiple_of(c * CHUNK, CHUNK)
        k = ks[pl.ds(lo, CHUNK), :]
        v_t = vts[pl.ds(lo, CHUNK), :].astype(f32)
        lhs = []
        for d in (0, 1):
            ig_r, b_r = gate_rows(c, d)
            g = b_r[:, CHUNK - 1:CHUNK] if d == 0 else b_r[:, 0:1]
            a_r = g - b_r + ig_r
            m_loc = jnp.max(a_r, axis=1, keepdims=True)
            w_r = jnp.exp(a_r - m_loc)
            lhs += [v_t * w_r, jnp.broadcast_to(w_r, (16, LANES))]
            st[d * N_CHUNK + c, 0:1, :] = jnp.broadcast_to(m_loc, (1, LANES))
            st[d * N_CHUNK + c, 1:2, :] = jnp.broadcast_to(g, (1, LANES))
        both = _dot(jnp.concatenate(lhs, axis=0).astype(bf16), k)
        cl[c] = both[0:N_AUG, :]
        cl[N_CHUNK + c] = both[N_AUG:2 * N_AUG, :]
        return carry

    _chunk_loop(local_state)

    cst[...] = jnp.zeros(cst.shape, f32)
    mst[...] = jnp.zeros(mst.shape, f32)

    def scan_step(i, carry):
        for d in (0, 1):
            c = i if d == 0 else jnp.where(i < N_CTX_CHUNK, N_CTX_CHUNK - 1 - i, N_CHUNK + N_CTX_CHUNK - 1 - i)
            idx = d * N_CHUNK + c
            c_loc = cl[idx]
            m_loc = st[idx, 0:1, :]
            g = st[idx, 1:2, :]
            c_prev = cst[d]
            m_prev = mst[d, 0:1, :]
            m_new = jnp.maximum(g + m_prev, m_loc)
            dec = jnp.exp(g + m_prev - m_new)
            add = jnp.exp(m_loc - m_new)
            cl[idx] = c_prev
            st[idx, 2:3, :] = m_prev
            cst[d] = dec * c_prev + add * c_loc
            mst[d, 0:1, :] = m_new
        return carry

    lax.fori_loop(0, N_CHUNK, scan_step, 0)

    def outputs(c, carry):
        lo = pl.multiple_of(c * CHUNK, CHUNK)
        k = ks[pl.ds(lo, CHUNK), :]
        q_t = qts[pl.ds(lo, CHUNK), :]
        v_aug = jnp.concatenate([vts[pl.ds(lo, CHUNK), :], jnp.ones((16, LANES), bf16)], axis=0)
        q_f = q_t.astype(f32)
        s_t = _dot(k, q_t)
        x_c = pltpu.roll(xc[pl.ds(lo, CHUNK), :], lane_shift, 1)
        hs = None
        for d in (0, 1):
            _, b_r = gate_rows(c, d)
            idx = d * N_CHUNK + c
            r_c = x_c[:, 8 * d:8 * d + 1] - x_c[:, 8 * d + 4:8 * d + 5]
            dm = jnp.where(upper if d == 0 else lower, b_r + r_c, -jnp.inf)
            e_r = b_r + st[idx, 2:3, :]
            m_t = jnp.maximum(e_r, jnp.max(dm, axis=0, keepdims=True))
            p_t = s_t * jnp.exp(dm - m_t)
            inter = jnp.exp(e_r - m_t)
            lhs = jnp.concatenate([v_aug, cl[idx].astype(bf16)], axis=1)
            rhs = jnp.concatenate([p_t, q_f * inter], axis=0).astype(bf16)
            nd = _dot(lhs, rhs)
            den = nd[CHUNK:CHUNK + 1, :]
            h_d = nd[0:CHUNK, :] * (1.0 / jnp.maximum(jnp.abs(den), jnp.exp(-m_t)))
            hs = h_d if hs is None else hs + h_d
        hn = hs * lax.rsqrt(jnp.mean(hs * hs, axis=0, keepdims=True) + EPS) * mn_ref[...]
        hg[pl.ds(lo, CHUNK), :] = (og[pl.ds(lo, CHUNK), :].astype(f32) * hn.T).astype(bf16)
        return carry

    _chunk_loop(outputs)
    outc_ref[...] = hg[0:CTX, :]
    outl_ref[...] = hg[CTX:TOK, :]


def _mlstm(qk, vo, gates, gate_b, mnorm):
    lat0 = R_CTX // SEQ
    ctx = lambda col0: pl.BlockSpec((CTX, LANES), lambda b, h: (b, col0 + h))
    lat = lambda col0: pl.BlockSpec((SEQ, LANES), lambda b, h: (lat0 + b, col0 + h))
    return pl.pallas_call(
        _mlstm_kernel,
        grid=(B, HEADS_A),
        in_specs=[
            ctx(0), lat(0), ctx(HEADS_A), lat(HEADS_A),
            ctx(0), lat(0), ctx(HEADS_A), lat(HEADS_A),
            pl.BlockSpec((CTX, LANES), lambda b, h: (b, 0)),
            pl.BlockSpec((SEQ, LANES), lambda b, h: (lat0 + b, 0)),
            pl.BlockSpec((1, LANES), lambda b, h: (0, 0)),
            pl.BlockSpec((None, CHUNK, LANES), lambda b, h: (h, 0, 0)),
        ],
        out_specs=[
            pl.BlockSpec((CTX, LANES), lambda b, h: (b, h)),
            pl.BlockSpec((SEQ, LANES), lambda b, h: (b, h)),
        ],
        out_shape=[
            jax.ShapeDtypeStruct((R_CTX, W_A), bf16),
            jax.ShapeDtypeStruct((R_LAT, W_A), bf16),
        ],
        scratch_shapes=[
            pltpu.VMEM((TOK, LANES), bf16),
            pltpu.VMEM((TOK, LANES), bf16),
            pltpu.VMEM((TOK, LANES), bf16),
            pltpu.VMEM((TOK, LANES), bf16),
            pltpu.VMEM((TOK, LANES), bf16),
            pltpu.VMEM((TOK, LANES), f32),
            pltpu.VMEM((16 * N_CHUNK, LANES), f32),
            pltpu.VMEM((2 * N_CHUNK, CHUNK + 16, LANES), f32),
            pltpu.VMEM((2 * N_CHUNK, 8, LANES), f32),
            pltpu.VMEM((2, CHUNK + 16, LANES), f32),
            pltpu.VMEM((2, 8, LANES), f32),
        ],
        compiler_params=pltpu.CompilerParams(
            dimension_semantics=("parallel", "arbitrary"), vmem_limit_bytes=40 * MIB),
        name="mlstm",
    )(qk, qk, qk, qk, vo, vo, vo, vo, gates, gates, gate_b, mnorm)


def _odd_in_kernel(h_ref, mod_ref, w_ref, cos_ref, sin_ref, o_ref):
    n = _modulated(h_ref[...], mod_ref[3:4, :], mod_ref[4:5, :]).astype(bf16)
    cos = cos_ref[...]
    sin = sin_ref[...]
    n_rot = (HEADS_C + KV_HEADS) * DH // LANES
    y = _dot(n, w_ref[:, 0:n_rot * LANES])
    for c in range(0, n_rot, 2):
        x1 = y[:, c * LANES:(c + 1) * LANES]
        x2 = y[:, (c + 1) * LANES:(c + 2) * LANES]
        r1 = x1 * cos - x2 * sin
        r2 = x1 * sin + x2 * cos
        if c < HEADS_C * DH // LANES:
            r1 = r1 * (DH ** -0.5 * LOG2E)
            r2 = r2 * (DH ** -0.5 * LOG2E)
        o_ref[:, c * LANES:(c + 1) * LANES] = r1.astype(bf16)
        o_ref[:, (c + 1) * LANES:(c + 2) * LANES] = r2.astype(bf16)
    v0 = n_rot * LANES
    o_ref[:, v0:QKV] = _dot(n, w_ref[:, v0:QKV]).astype(bf16)


def _odd_in(h, mods, w, cos_t, sin_t):
    tm = TM_ODD_IN
    n_ctx = R_CTX // tm
    per_b = SEQ // tm
    rope_map = lambda i: (jnp.where(i < n_ctx, 0, 1 + jnp.maximum(i - n_ctx, 0) % per_b), 0)
    return pl.pallas_call(
        _odd_in_kernel,
        grid=(R_ALL // tm,),
        in_specs=[
            pl.BlockSpec((tm, D), lambda i: (i, 0)),
            pl.BlockSpec((None, N_MOD, D), lambda i: (_who_flat(i, tm), 0, 0)),
            pl.BlockSpec((D, QKV), lambda i: (0, 0), pipeline_mode=pl.Buffered(1)),
            pl.BlockSpec((tm, LANES), rope_map),
            pl.BlockSpec((tm, LANES), rope_map),
        ],
        out_specs=pl.BlockSpec((tm, QKV), lambda i: (i, 0)),
        out_shape=jax.ShapeDtypeStruct((R_ALL, QKV), bf16),
        compiler_params=pltpu.CompilerParams(
            dimension_semantics=("parallel",), vmem_limit_bytes=32 * MIB),
        name="odd_in",
    )(h, mods, w, cos_t, sin_t)


BLOCKS_PER_ITER = 4


def _attn_kernel(sink_ref, q_ref, kc_ref, kl_ref, vc_ref, vl_ref, o_ref,
                 k_all, vt_all, q_all, s_all, p_all, ot_all):
    kv_w = KV_HEADS * DH
    key = lax.broadcasted_iota(jnp.int32, (CHUNK, CHUNK), 0)
    qry = lax.broadcasted_iota(jnp.int32, (CHUNK, CHUNK), 1)
    far = 1 << 20
    lane_head = lax.broadcasted_iota(jnp.int32, (CHUNK, kv_w), 1) % LANES // (DH // 2)
    neg = -1e30
    n_key = CTX + 3 * CHUNK
    n_slab = n_key // CHUNK

    def stage_keys(k_scr, vt_scr, k_rows, v_rows, row0):
        rows = k_rows.shape[0]
        k_scr[row0:row0 + rows, :] = k_rows
        v_t = v_rows.astype(f32).T.astype(bf16)
        for j in range(KV_HEADS):
            vt_scr[j, 0:DH, row0:row0 + rows] = v_t[j * DH:(j + 1) * DH, :]

    for u in range(BLOCKS_PER_ITER):
        stage_keys(k_all.at[u], vt_all.at[u], kc_ref[...], vc_ref[...], 0)
        for j in range(KV_HEADS):
            vt_all[u, j, DH:DH + 16, :] = jnp.ones((16, n_key), bf16)

    def block(blk, u):
        k_scr, vt_scr, q_scr, s_scr, p_scr, ot_scr = (r.at[u] for r in (k_all, vt_all, q_all, s_all, p_all, ot_all))
        q0 = pl.multiple_of(blk * CHUNK, CHUNK)
        prev_ok = key >= qry + jnp.where(blk > 0, 0, far)
        next_ok = key <= qry - jnp.where(blk < N_BLK - 1, 0, far)
        for n, off in enumerate((-1, 0, 1)):
            src = pl.multiple_of(jnp.clip(blk + off, 0, N_BLK - 1) * CHUNK, CHUNK)
            stage_keys(k_scr, vt_scr, kl_ref[pl.ds(src, CHUNK), :], vl_ref[pl.ds(src, CHUNK), :], CTX + n * CHUNK)

        for j in range(KV_HEADS):
            keep = jnp.where(lane_head == j, 1.0, 0.0).astype(bf16)
            for g in range(GROUP):
                q_scr[j, g * CHUNK:(g + 1) * CHUNK, :] = q_ref[pl.ds(q0, CHUNK), g * kv_w:(g + 1) * kv_w] * keep
            s_scr[j] = _dot_nt(k_scr[...], q_scr[j])

        def scores(j, g, slab):
            s = s_scr[j, slab * CHUNK:(slab + 1) * CHUNK, g * CHUNK:(g + 1) * CHUNK]
            if slab == 2:
                s = jnp.where(prev_ok, s, neg)
            if slab == 4:
                s = jnp.where(next_ok, s, neg)
            return s

        for j in range(KV_HEADS):
            sink_terms = []
            for g in range(GROUP):
                sink = jnp.full((1, CHUNK), sink_ref[j * GROUP + g] * LOG2E, f32)
                m8 = None
                for slab in range(n_slab):
                    part = jnp.max(scores(j, g, slab).reshape(CHUNK // 8, 8, CHUNK), axis=0)
                    m8 = part if m8 is None else jnp.maximum(m8, part)
                m = jnp.maximum(sink, jnp.max(m8, axis=0, keepdims=True))
                for slab in range(n_slab):
                    p = jnp.exp2(scores(j, g, slab) - m)
                    p_scr[j, slab * CHUNK:(slab + 1) * CHUNK, g * CHUNK:(g + 1) * CHUNK] = p.astype(bf16)
                sink_terms.append(jnp.exp2(sink - m))
            acc = _dot(vt_scr[j], p_scr[j])
            den = acc[DH:DH + 1, :] + jnp.concatenate(sink_terms, axis=1)
            ot_scr[j * DH:(j + 1) * DH, :] = acc[0:DH, :] * (1.0 / den)
        for g in range(GROUP):
            o_ref[pl.ds(q0, CHUNK), g * kv_w:(g + 1) * kv_w] = (
                ot_scr[:, g * CHUNK:(g + 1) * CHUNK].T.astype(bf16))

    def blocks(i, carry):
        for u in range(BLOCKS_PER_ITER):
            block(i * BLOCKS_PER_ITER + u, u)
        return carry

    lax.fori_loop(0, N_BLK // BLOCKS_PER_ITER, blocks, 0)


def _attention(qkv, sink):
    kv_w = KV_HEADS * DH
    n_key = CTX + 3 * CHUNK
    k_col = HEADS_C * DH // kv_w
    v_col = k_col + 1
    lat0 = R_CTX // SEQ
    ctx = lambda col: pl.BlockSpec((CTX, kv_w), lambda b: (b, col))
    lat = lambda col: pl.BlockSpec((SEQ, kv_w), lambda b: (lat0 + b, col))
    return pl.pallas_call(
        _attn_kernel,
        grid=(B,),
        in_specs=[
            pl.BlockSpec(memory_space=pltpu.SMEM),
            pl.BlockSpec((SEQ, HEADS_C * DH), lambda b: (lat0 + b, 0)),
            ctx(k_col), lat(k_col), ctx(v_col), lat(v_col),
        ],
        out_specs=pl.BlockSpec((SEQ, HEADS_C * DH), lambda b: (b, 0)),
        out_shape=jax.ShapeDtypeStruct((R_LAT, HEADS_C * DH), bf16),
        scratch_shapes=[
            pltpu.VMEM((BLOCKS_PER_ITER, n_key, kv_w), bf16),
            pltpu.VMEM((BLOCKS_PER_ITER, KV_HEADS, DH + 16, n_key), bf16),
            pltpu.VMEM((BLOCKS_PER_ITER, KV_HEADS, GROUP * CHUNK, kv_w), bf16),
            pltpu.VMEM((BLOCKS_PER_ITER, KV_HEADS, n_key, GROUP * CHUNK), f32),
            pltpu.VMEM((BLOCKS_PER_ITER, KV_HEADS, n_key, GROUP * CHUNK), bf16),
            pltpu.VMEM((BLOCKS_PER_ITER, kv_w, GROUP * CHUNK), f32),
        ],
        compiler_params=pltpu.CompilerParams(
            dimension_semantics=("parallel",), vmem_limit_bytes=56 * MIB),
        name="window_attention",
    )(sink, qkv, qkv, qkv, qkv, qkv)


def _rope_tables():
    rows = SEQ // GRID_W
    row, col = np.meshgrid(np.arange(rows), np.arange(GRID_W), indexing='ij')
    n_freq = DH // 4
    inv = (np.float32(ROPE_BASE) ** (-np.arange(n_freq, dtype=np.float32) / np.float32(n_freq))).astype(np.float32)
    ang = np.concatenate([row.reshape(-1, 1).astype(np.float32) * inv,
                          col.reshape(-1, 1).astype(np.float32) * inv], axis=-1)
    reps = 2 * LANES // DH
    cos = np.tile(np.cos(ang).astype(np.float32), (1, reps))
    sin = np.tile(np.sin(ang).astype(np.float32), (1, reps))
    cos = np.concatenate([np.ones((TM_ODD_IN, LANES), np.float32), cos], axis=0)
    sin = np.concatenate([np.zeros((TM_ODD_IN, LANES), np.float32), sin], axis=0)
    return jnp.asarray(cos), jnp.asarray(sin)


def kernel(x, c, ctx, c_ctx, ada_w, ada_b, ffn_w_in, ffn_w_out, even_w_in, even_w_out, mlstm_conv,
           mlstm_gate_b, mlstm_norm, sgu_norm, sgu_ws, sgu_b, odd_w_qkv, odd_w_out, attn_sink, final_norm):
    cs = jnp.concatenate([c_ctx[None, :], c, jnp.zeros((16 - 1 - B, D), f32)], axis=0)
    mods = _modulation(cs, ada_w, ada_b)[:, :1 + B, :].reshape(2, 1 + B, N_MOD, D)

    fw_in, fw_out = ffn_w_in, ffn_w_out

    m0 = mods[0]
    h = _ffn((ctx.reshape(R_CTX, D), x.reshape(R_LAT, D)), m0, fw_in, fw_out, sel=(0, 0), mi=0, tm=TM_FFN_WIDE)
    qk, vo, uv, gates = _even_in(h, m0, even_w_in, mlstm_conv[0])
    gate_b = jnp.pad(mlstm_gate_b[0].reshape(1, 4 * HEADS_A), ((0, 0), (0, LANES - 4 * HEADS_A)))
    mnorm_t = jnp.broadcast_to(mlstm_norm[0][:, :, None], (HEADS_A, CHUNK, LANES))
    ha_ctx, ha_lat = _mlstm(qk, vo, gates, gate_b, mnorm_t)
    sbx = jnp.repeat(sgu_b[0].T, LANES, axis=1)
    h = _ffn(h, m0, fw_in, fw_out, sel=(0, 1), mi=6,
             even=(ha_ctx, ha_lat, uv, sgu_norm[0].reshape(1, W_A), sgu_ws[0].astype(bf16), sbx,
                   even_w_out[0].astype(bf16)))

    m1 = mods[1]
    h = _ffn(h, m1, fw_in, fw_out, sel=(1, 0), mi=0, tm=TM_FFN_WIDE)
    cos_t, sin_t = _rope_tables()
    qdim = HEADS_C * DH
    kdim = KV_HEADS * DH
    w_q = odd_w_qkv[0][:, :qdim].reshape(D, KV_HEADS, GROUP, DH // 2, 2).transpose(0, 2, 4, 1, 3).reshape(D, qdim)
    w_k = odd_w_qkv[0][:, qdim:qdim + kdim].reshape(D, KV_HEADS, DH // 2, 2).transpose(0, 3, 1, 2).reshape(D, kdim)
    w_qkv = jnp.concatenate([w_q, w_k, odd_w_qkv[0][:, qdim + kdim:]], axis=1).astype(bf16)
    w_o = odd_w_out[0].reshape(KV_HEADS, GROUP, DH, D).transpose(1, 0, 2, 3).reshape(qdim, D).astype(bf16)
    qkv = _odd_in(h, m1, w_qkv, cos_t, sin_t)
    attn = _attention(qkv, attn_sink[0])
    out = _ffn(h, m1, fw_in, fw_out, sel=(1, 1), mi=6, last=(attn, w_o, final_norm), tm=TM_FFN_WIDE)
    return out.reshape(B, SEQ, D)
```

```python
import functools

import numpy as np
import jax
import jax.numpy as jnp
from jax import lax
from jax.experimental import pallas as pl
from jax.experimental.pallas import tpu as pltpu

f32 = jnp.float32
bf16 = jnp.bfloat16

D = 1024
B = 8
SEQ = 2048
CTX = 256
TOK = CTX + SEQ
GRID_W = 64
N_MOD = 9
D_FF = 2816
EPS = 1e-6
HEADS_A = 4
CHUNK = 128
N_CHUNK = TOK // CHUNK
N_CTX_CHUNK = CTX // CHUNK
W_A = 512
EVEN_COLS = 3200
HEADS_C = 16
KV_HEADS = 4
GROUP = HEADS_C // KV_HEADS
DH = 64
QKV = (HEADS_C + 2 * KV_HEADS) * DH
N_BLK = SEQ // CHUNK
ROPE_BASE = 10000.0
LOG2E = 1.4426950408889634

R_CTX = B * CTX
R_LAT = B * SEQ
R_ALL = R_CTX + R_LAT

LANES = 128
TM_FFN = 512
TM_FFN_WIDE = 1024
TM_PROJ = 512
TM_ODD_IN = 1024
FC = 256
N_FC = D_FF // FC
MIB = 1024 * 1024


def _dot(a, b):
    return jnp.dot(a, b, preferred_element_type=f32)


def _dot_nt(a, b):
    return lax.dot_general(a, b, (((1,), (1,)), ((), ())), preferred_element_type=f32)


def _sigmoid(x):
    return 1.0 / (1.0 + jnp.exp(-x))


def _split3(x):
    hi = x.astype(bf16)
    r1 = x - hi.astype(f32)
    mid = r1.astype(bf16)
    lo = (r1 - mid.astype(f32)).astype(bf16)
    return hi, mid, lo


def _modulated(h, shift, scale):
    ms = jnp.mean(h * h, axis=-1, keepdims=True)
    return h * lax.rsqrt(ms + EPS) * (1.0 + scale) + shift


def _mod_kernel(c_ref, w_ref, b_ref, o_ref):
    x = c_ref[...]
    s = x * _sigmoid(x)
    w = w_ref[...]
    s_hi = s.astype(bf16)
    s_lo = (s - s_hi.astype(f32)).astype(bf16)
    w_hi = w.astype(bf16)
    w_lo = (w - w_hi.astype(f32)).astype(bf16)
    o_ref[...] = _dot(s_hi, w_hi) + _dot(s_hi, w_lo) + _dot(s_lo, w_hi) + b_ref[...]


def _modulation(cs, ada_w, ada_b):
    depth = ada_w.shape[0]
    rows = cs.shape[0]
    n_col = N_MOD * D
    tn = 1024
    return pl.pallas_call(
        _mod_kernel,
        grid=(depth, n_col // tn),
        in_specs=[
            pl.BlockSpec((rows, D), lambda l, j: (0, 0)),
            pl.BlockSpec((None, D, tn), lambda l, j: (l, 0, j)),
            pl.BlockSpec((None, 1, tn), lambda l, j: (l, 0, j)),
        ],
        out_specs=pl.BlockSpec((None, rows, tn), lambda l, j: (l, 0, j)),
        out_shape=jax.ShapeDtypeStruct((depth, rows, n_col), f32),
        compiler_params=pltpu.CompilerParams(
            dimension_semantics=("parallel", "parallel"), vmem_limit_bytes=32 * MIB),
        name="modulation",
    )(cs, ada_w, ada_b.reshape(depth, 1, n_col))


def _who_flat(tile, tm):
    n_ctx = R_CTX // tm
    per_b = SEQ // tm
    return jnp.where(tile < n_ctx, 0, 1 + jnp.maximum(tile - n_ctx, 0) // per_b)


W_CHUNKS = 16
W_SLOTS = 4


def _fetch_cast(src, dst, stage, sem, place=None):
    rows = dst.shape[0] // W_CHUNKS

    def piece(c):
        slot = c % W_SLOTS
        return pltpu.make_async_copy(src.at[pl.ds(c * rows, rows), :], stage.at[slot], sem.at[slot])

    for c in range(W_SLOTS - 1):
        piece(c).start()
    for c in range(W_CHUNKS):
        if c + W_SLOTS - 1 < W_CHUNKS:
            piece(c + W_SLOTS - 1).start()
        piece(c).wait()
        if place is None:
            dst[c * rows:(c + 1) * rows, :] = stage[c % W_SLOTS].astype(bf16)
        else:
            place(dst, slice(c * rows, (c + 1) * rows), stage[c % W_SLOTS])


def _gelu_tanh(x):
    return x * (0.5 * (1.0 + jnp.tanh(0.7978845608028654 * (x + 0.044715 * (x * x * x)))))


def _even_mix(ha, uv_ref, sg_ref, ws_ref, sb_ref, wm_ref, hb_scr):
    u = uv_ref[:, 0:W_A].astype(f32)
    v = uv_ref[:, W_A:2 * W_A].astype(f32)
    vn = (v * lax.rsqrt(jnp.mean(v * v, axis=-1, keepdims=True) + EPS) * sg_ref[...]).astype(bf16)
    n_chunk = hb_scr.shape[0] // CHUNK
    for g in range(W_A // LANES):
        cs = slice(g * LANES, (g + 1) * LANES)
        rhs = jnp.concatenate([vn[n * CHUNK:(n + 1) * CHUNK, cs] for n in range(n_chunk)], axis=1)
        mixed = _dot(ws_ref[g], rhs)
        for n in range(n_chunk):
            r = slice(n * CHUNK, (n + 1) * CHUNK)
            hb_scr[r, cs] = (u[r, cs] * (mixed[:, n * LANES:(n + 1) * LANES] + sb_ref[:, cs])).astype(bf16)
    return _dot(ha, wm_ref[0:W_A, :]) + _dot(hb_scr[...], wm_ref[W_A:2 * W_A, :])


def _ffn_kernel(*refs, tm, mi, mixer, final, split, sel):
    refs = list(refs)
    is_ctx = pl.program_id(0) < R_CTX // tm
    if split:
        c_ref, x_ref = refs[0:2]
        refs = refs[2:]
        read_h = lambda: jnp.where(is_ctx, c_ref[...], x_ref[...])
    else:
        h_ref = refs.pop(0)
        read_h = lambda: h_ref[...]
    wi_ref, wo_ref, wi_stage, wo_stage, wi_sem, wo_sem = refs[-6:]
    refs = refs[:-6]
    if mixer == "attn":
        a_ref, wm_ref = refs[0:2]
        refs = refs[2:]
    elif mixer == "even":
        hac_ref, hax_ref, uv_ref, sg_ref, ws_ref, sb_ref, wm_ref = refs[0:7]
        refs = refs[7:]
    mod_ref, wi_hbm, wo_hbm = refs[0:3]
    refs = refs[3:]
    if final:
        fn_ref = refs.pop(0)
    o_ref, n_scr, acc_scr = refs[0:3]
    refs = refs[3:]

    @pl.when(pl.program_id(0) == 0)
    def _():
        _fetch_cast(wi_hbm.at[sel[0], sel[1]], wi_ref, wi_stage, wi_sem)
        _fetch_cast(wo_hbm.at[sel[0], sel[1]], wo_ref, wo_stage, wo_sem)

    if mixer is not None:
        h_scr = refs.pop(0)
        if mixer == "attn":
            y = _dot(a_ref[...], wm_ref[...])
        else:
            ha = jnp.where(is_ctx, hac_ref[...], hax_ref[...])
            y = _even_mix(ha, uv_ref, sg_ref, ws_ref, sb_ref, wm_ref, refs.pop(0))
        h_scr[...] = read_h() + mod_ref[5:6, :] * y
        read_h = lambda: h_scr[...]
    n_scr[...] = _modulated(read_h(), mod_ref[mi:mi + 1, :], mod_ref[mi + 1:mi + 2, :]).astype(bf16)

    for j in range(N_FC):
        n = n_scr[...]
        g = _dot(n, wi_ref[:, j * FC:(j + 1) * FC])
        u = _dot(n, wi_ref[:, D_FF + j * FC:D_FF + (j + 1) * FC])
        a = (g * _sigmoid(g) * u).astype(bf16)
        y = _dot(a, wo_ref[j * FC:(j + 1) * FC, :])
        if j == 0:
            acc_scr[...] = y
        else:
            acc_scr[...] += y
    out = read_h() + (0.5 * mod_ref[mi + 2:mi + 3, :]) * acc_scr[...]
    if final:
        ms = jnp.mean(out * out, axis=-1, keepdims=True)
        out = out * lax.rsqrt(ms + EPS) * fn_ref[...]
    o_ref[...] = out


def _ffn(h, mods, w_in, w_out, *, sel, mi, even=None, last=None, tm=TM_FFN):
    n_ctx = R_CTX // tm
    tile0 = n_ctx if last is not None else 0
    split = isinstance(h, tuple)
    const2 = lambda i: (0, 0)
    ctx_map = lambda i: (jnp.minimum(i, n_ctx - 1), 0)
    lat_map = lambda i: (jnp.maximum(i - n_ctx, 0), 0)
    if split:
        rows_out = R_ALL
        in_specs = [pl.BlockSpec((tm, D), ctx_map), pl.BlockSpec((tm, D), lat_map)]
        args = list(h)
    else:
        rows_out = h.shape[0] - tile0 * tm
        in_specs = [pl.BlockSpec((tm, D), lambda i: (i + tile0, 0))]
        args = [h]
    scratch = [pltpu.VMEM((tm, D), bf16), pltpu.VMEM((tm, D), f32)]
    mixer = None
    if last is not None:
        mixer = "attn"
        attn, w_attn, final_norm = last
        in_specs += [
            pl.BlockSpec((tm, D), lambda i: (i, 0)),
            pl.BlockSpec((D, D), const2, pipeline_mode=pl.Buffered(1)),
        ]
        args += [attn, w_attn]
        scratch.append(pltpu.VMEM((tm, D), f32))
    elif even is not None:
        mixer = "even"
        in_specs += [
            pl.BlockSpec((tm, W_A), ctx_map),
            pl.BlockSpec((tm, W_A), lat_map),
            pl.BlockSpec((tm, 2 * W_A), lambda i: (i, 0)),
            pl.BlockSpec((1, W_A), const2),
            pl.BlockSpec((W_A // LANES, CHUNK, CHUNK), lambda i: (0, 0, 0)),
            pl.BlockSpec((CHUNK, W_A), const2),
            pl.BlockSpec((2 * W_A, D), const2, pipeline_mode=pl.Buffered(1)),
        ]
        args += list(even)
        scratch += [pltpu.VMEM((tm, D), f32), pltpu.VMEM((tm, W_A), bf16)]
    in_specs += [
        pl.BlockSpec((None, N_MOD, D), lambda i: (_who_flat(i + tile0, tm), 0, 0)),
        pl.BlockSpec(memory_space=pl.ANY),
        pl.BlockSpec(memory_space=pl.ANY),
    ]
    args += [mods, w_in, w_out]
    if last is not None:
        in_specs.append(pl.BlockSpec((1, D), const2))
        args.append(final_norm.reshape(1, D))
    scratch += [
        pltpu.VMEM((D, 2 * D_FF), bf16),
        pltpu.VMEM((D_FF, D), bf16),
        pltpu.VMEM((W_SLOTS, D // W_CHUNKS, 2 * D_FF), f32),
        pltpu.VMEM((W_SLOTS, D_FF // W_CHUNKS, D), f32),
        pltpu.SemaphoreType.DMA((W_SLOTS,)),
        pltpu.SemaphoreType.DMA((W_SLOTS,)),
    ]
    return pl.pallas_call(
        functools.partial(_ffn_kernel, tm=tm, mi=mi, mixer=mixer, final=last is not None, split=split, sel=sel),
        grid=(rows_out // tm,),
        in_specs=in_specs,
        out_specs=pl.BlockSpec((tm, D), lambda i: (i, 0)),
        out_shape=jax.ShapeDtypeStruct((rows_out, D), f32),
        scratch_shapes=scratch,
        compiler_params=pltpu.CompilerParams(
            dimension_semantics=("arbitrary",), vmem_limit_bytes=56 * MIB),
        name={None: "ffn", "even": "ffn_even", "attn": "ffn_final"}[mixer],
    )(*args)


HALO = 8


def _fetch_even_w(src, dst, stage, sem):
    g0 = 4 * W_A
    g1 = g0 + 4 * HEADS_A
    pieces = ([(i * LANES, LANES, i * LANES) for i in range(g0 // LANES)]
              + [(g1 + i * LANES, LANES, g0 + i * LANES) for i in range(2 * W_A // LANES)]
              + [(g0, g1 - g0, g0 + 2 * W_A)])

    def copy(i):
        row0, n, _ = pieces[i]
        slot = i % W_SLOTS
        return pltpu.make_async_copy(src.at[pl.ds(row0, n), :], stage.at[slot, pl.ds(0, n), :], sem.at[slot])

    lane = lax.broadcasted_iota(jnp.int32, (D, LANES), 1)
    for i in range(W_SLOTS - 1):
        copy(i).start()
    for i, (_, n, col0) in enumerate(pieces):
        if i + W_SLOTS - 1 < len(pieces):
            copy(i + W_SLOTS - 1).start()
        copy(i).wait()
        x_t = stage[i % W_SLOTS].T
        if n < LANES:
            x_t = jnp.where(lane < n, x_t, 0.0)
        dst[:, col0:col0 + LANES] = x_t.astype(bf16)


def _even_in_kernel(h_ref, hp_ref, hn_ref, mod_ref, w_hbm, cw_ref, qk_ref, vo_ref, uv_ref, g_ref,
                    w_ref, w_stage, w_sem):
    tile = pl.program_id(0)

    @pl.when(tile == 0)
    def _():
        _fetch_even_w(w_hbm.at[0], w_ref, w_stage, w_sem)

    tm = TM_PROJ
    n_chunk = tm // CHUNK
    shift, scale = mod_ref[3:4, :], mod_ref[4:5, :]
    n = _modulated(h_ref[...], shift, scale)
    halo = _modulated(jnp.concatenate([hp_ref[...], hn_ref[...]], axis=0), shift, scale)
    n_ext = jnp.concatenate([halo[0:HALO], n, halo[HALO:2 * HALO]], axis=0).astype(bf16)
    n = n.astype(bf16)

    row = lax.broadcasted_iota(jnp.int32, (tm, LANES), 0)
    is_ctx = tile < R_CTX // tm
    pos = jnp.where(is_ctx, row % CTX, row + (jnp.maximum(tile - R_CTX // tm, 0) % (SEQ // tm)) * tm)
    seq_start = pos == 0
    seq_end = pos == jnp.where(is_ctx, CTX - 1, SEQ - 1)

    p = _dot(n_ext, w_ref[:, 0:2 * W_A])
    cur = p[HALO:HALO + tm]
    prv = pltpu.roll(p, 1, 0)[HALO:HALO + tm]
    nxt = pltpu.roll(p, tm + 2 * HALO - 1, 0)[HALO:HALO + tm]
    for cb in range(2 * HEADS_A):
        cs = slice(cb * LANES, (cb + 1) * LANES)
        y = (cw_ref[0:1, cs] * jnp.where(seq_start, 0.0, prv[:, cs]) + cw_ref[1:2, cs] * cur[:, cs]
             + cw_ref[2:3, cs] * jnp.where(seq_end, 0.0, nxt[:, cs]))
        y = y * _sigmoid(y)
        if cb < HEADS_A:
            for c in range(n_chunk):
                r = slice(c * CHUNK, (c + 1) * CHUNK)
                qk_ref[r, cs] = y[r, :].T.astype(bf16)
        else:
            qk_ref[:, cs] = (y * CHUNK ** -0.5).astype(bf16)

    v = _dot(n, w_ref[:, 2 * W_A:3 * W_A])
    for hd in range(HEADS_A):
        cs = slice(hd * LANES, (hd + 1) * LANES)
        for c in range(n_chunk):
            r = slice(c * CHUNK, (c + 1) * CHUNK)
            vo_ref[r, cs] = v[r, cs].T.astype(bf16)
    vo_ref[:, W_A:2 * W_A] = _sigmoid(_dot(n, w_ref[:, 3 * W_A:4 * W_A])).astype(bf16)
    uv_ref[...] = _gelu_tanh(_dot(n, w_ref[:, 4 * W_A:6 * W_A])).astype(bf16)
    g_ref[...] = _dot(n, w_ref[:, 6 * W_A:EVEN_COLS])


def _even_in(h, mods, w, conv_w):
    tm = TM_PROJ
    out_map = lambda i: (i, 0)
    halo_blocks = tm // HALO
    last_halo = R_ALL // HALO - 1
    return pl.pallas_call(
        _even_in_kernel,
        grid=(R_ALL // tm,),
        in_specs=[
            pl.BlockSpec((tm, D), lambda i: (i, 0)),
            pl.BlockSpec((HALO, D), lambda i: (jnp.maximum(i * halo_blocks - 1, 0), 0)),
            pl.BlockSpec((HALO, D), lambda i: (jnp.minimum((i + 1) * halo_blocks, last_halo), 0)),
            pl.BlockSpec((None, N_MOD, D), lambda i: (_who_flat(i, tm), 0, 0)),
            pl.BlockSpec(memory_space=pl.ANY),
            pl.BlockSpec((3, 2 * W_A), lambda i: (0, 0)),
        ],
        out_specs=[
            pl.BlockSpec((tm, 1024), out_map),
            pl.BlockSpec((tm, 1024), out_map),
            pl.BlockSpec((tm, 1024), out_map),
            pl.BlockSpec((tm, LANES), out_map),
        ],
        out_shape=[
            jax.ShapeDtypeStruct((R_ALL, 1024), bf16),
            jax.ShapeDtypeStruct((R_ALL, 1024), bf16),
            jax.ShapeDtypeStruct((R_ALL, 1024), bf16),
            jax.ShapeDtypeStruct((R_ALL, LANES), f32),
        ],
        scratch_shapes=[
            pltpu.VMEM((D, EVEN_COLS), bf16),
            pltpu.VMEM((W_SLOTS, LANES, D), f32),
            pltpu.SemaphoreType.DMA((W_SLOTS,)),
        ],
        compiler_params=pltpu.CompilerParams(
            dimension_semantics=("arbitrary",), vmem_limit_bytes=40 * MIB),
        name="even_in",
    )(h, h, h, mods, w, conv_w)


N_AUG = CHUNK + 16
CHUNKS_PER_ITER = 18


def _chunk_loop(body):
    def group(i, carry):
        for u in range(CHUNKS_PER_ITER):
            carry = body(i * CHUNKS_PER_ITER + u, carry)
        return carry
    lax.fori_loop(0, N_CHUNK // CHUNKS_PER_ITER, group, 0)


def _mlstm_kernel(qc_ref, ql_ref, kc_ref, kl_ref, vc_ref, vl_ref, oc_ref, ol_ref, gc_ref, gl_ref,
                  gb_ref, mn_ref, outc_ref, outl_ref,
                  ks, qts, vts, og, hg, xc, xr, cl, st, cst, mst):
    head = pl.program_id(1)
    rowi = lax.broadcasted_iota(jnp.int32, (CHUNK, CHUNK), 0)
    coli = lax.broadcasted_iota(jnp.int32, (CHUNK, CHUNK), 1)
    lower = coli <= rowi
    upper = coli >= rowi
    tri = jnp.where(lower, 1.0, 0.0).astype(bf16)

    def part(ctx_ref, lat_ref, c):
        return (ctx_ref, c * CHUNK) if c < N_CTX_CHUNK else (lat_ref, (c - N_CTX_CHUNK) * CHUNK)

    for c in range(N_CHUNK):
        lo = c * CHUNK
        for dst, refs in ((qts, (qc_ref, ql_ref)), (ks, (kc_ref, kl_ref)), (vts, (vc_ref, vl_ref)),
                          (og, (oc_ref, ol_ref))):
            src, at = part(*refs, c)
            dst[lo:lo + CHUNK, :] = src[at:at + CHUNK, :]

    @pl.when(head == 0)
    def _():
        kind = (coli // HEADS_A) % 4
        for c in range(N_CHUNK):
            lo = c * CHUNK
            src, at = part(gc_ref, gl_ref, c)
            gt = src[at:at + CHUNK, :] + gb_ref[...]
            lf = jnp.minimum(gt, 0.0) - jnp.log1p(jnp.exp(-jnp.abs(gt)))
            hi, mid, lw = _split3(lf)
            pre = _dot(tri, hi) + _dot(tri, mid) + _dot(tri, lw)
            suf = pre[CHUNK - 1:CHUNK, :] - pre + lf
            x = jnp.where(kind == 1, pre, jnp.where(kind == 3, suf, gt))
            xr[16 * c:16 * c + 16, :] = x.T[0:16, :]
            xc[lo:lo + CHUNK, :] = x

    lane_shift = (LANES - head) % LANES

    def gate_rows(c, d):
        row = c * 16 + 8 * d + head
        return xr[pl.ds(row, 1), :], xr[pl.ds(row + HEADS_A, 1), :]

    def local_state(c, carry):
        lo = pl.multiple_of(c * CHUNK, CHUNK)
        k = ks[pl.ds(lo, CHUNK), :]
        v_t = vts[pl.ds(lo, CHUNK), :].astype(f32)
        lhs = []
        for d in (0, 1):
            ig_r, b_r = gate_rows(c, d)
            g = b_r[:, CHUNK - 1:CHUNK] if d == 0 else b_r[:, 0:1]
            a_r = g - b_r + ig_r
            m_loc = jnp.max(a_r, axis=1, keepdims=True)
            w_r = jnp.exp(a_r - m_loc)
            lhs += [v_t * w_r, jnp.broadcast_to(w_r, (16, LANES))]
            st[d * N_CHUNK + c, 0:1, :] = jnp.broadcast_to(m_loc, (1, LANES))
            st[d * N_CHUNK + c, 1:2, :] = jnp.broadcast_to(g, (1, LANES))
        both = _dot(jnp.concatenate(lhs, axis=0).astype(bf16), k)
        cl[c] = both[0:N_AUG, :]
        cl[N_CHUNK + c] = both[N_AUG:2 * N_AUG, :]
        return carry

    _chunk_loop(local_state)

    cst[...] = jnp.zeros(cst.shape, f32)
    mst[...] = jnp.zeros(mst.shape, f32)

    def scan_step(i, carry):
        for d in (0, 1):
            c = i if d == 0 else jnp.where(i < N_CTX_CHUNK, N_CTX_CHUNK - 1 - i, N_CHUNK + N_CTX_CHUNK - 1 - i)
            idx = d * N_CHUNK + c
            c_loc = cl[idx]
            m_loc = st[idx, 0:1, :]
            g = st[idx, 1:2, :]
            c_prev = cst[d]
            m_prev = mst[d, 0:1, :]
            m_new = jnp.maximum(g + m_prev, m_loc)
            dec = jnp.exp(g + m_prev - m_new)
            add = jnp.exp(m_loc - m_new)
            cl[idx] = c_prev
            st[idx, 2:3, :] = m_prev
            cst[d] = dec * c_prev + add * c_loc
            mst[d, 0:1, :] = m_new
        return carry

    lax.fori_loop(0, N_CHUNK, scan_step, 0)

    def outputs(c, carry):
        lo = pl.multiple_of(c * CHUNK, CHUNK)
        k = ks[pl.ds(lo, CHUNK), :]
        q_t = qts[pl.ds(lo, CHUNK), :]
        v_aug = jnp.concatenate([vts[pl.ds(lo, CHUNK), :], jnp.ones((16, LANES), bf16)], axis=0)
        q_f = q_t.astype(f32)
        s_t = _dot(k, q_t)
        x_c = pltpu.roll(xc[pl.ds(lo, CHUNK), :], lane_shift, 1)
        hs = None
        for d in (0, 1):
            _, b_r = gate_rows(c, d)
            idx = d * N_CHUNK + c
            r_c = x_c[:, 8 * d:8 * d + 1] - x_c[:, 8 * d + 4:8 * d + 5]
            dm = jnp.where(upper if d == 0 else lower, b_r + r_c, -jnp.inf)
            e_r = b_r + st[idx, 2:3, :]
            m_t = jnp.maximum(e_r, jnp.max(dm, axis=0, keepdims=True))
            p_t = s_t * jnp.exp(dm - m_t)
            inter = jnp.exp(e_r - m_t)
            lhs = jnp.concatenate([v_aug, cl[idx].astype(bf16)], axis=1)
            rhs = jnp.concatenate([p_t, q_f * inter], axis=0).astype(bf16)
            nd = _dot(lhs, rhs)
            den = nd[CHUNK:CHUNK + 1, :]
            h_d = nd[0:CHUNK, :] * (1.0 / jnp.maximum(jnp.abs(den), jnp.exp(-m_t)))
            hs = h_d if hs is None else hs + h_d
        hn = hs * lax.rsqrt(jnp.mean(hs * hs, axis=0, keepdims=True) + EPS) * mn_ref[...]
        hg[pl.ds(lo, CHUNK), :] = (og[pl.ds(lo, CHUNK), :].astype(f32) * hn.T).astype(bf16)
        return carry

    _chunk_loop(outputs)
    outc_ref[...] = hg[0:CTX, :]
    outl_ref[...] = hg[CTX:TOK, :]


def _mlstm(qk, vo, gates, gate_b, mnorm):
    lat0 = R_CTX // SEQ
    ctx = lambda col0: pl.BlockSpec((CTX, LANES), lambda b, h: (b, col0 + h))
    lat = lambda col0: pl.BlockSpec((SEQ, LANES), lambda b, h: (lat0 + b, col0 + h))
    return pl.pallas_call(
        _mlstm_kernel,
        grid=(B, HEADS_A),
        in_specs=[
            ctx(0), lat(0), ctx(HEADS_A), lat(HEADS_A),
            ctx(0), lat(0), ctx(HEADS_A), lat(HEADS_A),
            pl.BlockSpec((CTX, LANES), lambda b, h: (b, 0)),
            pl.BlockSpec((SEQ, LANES), lambda b, h: (lat0 + b, 0)),
            pl.BlockSpec((1, LANES), lambda b, h: (0, 0)),
            pl.BlockSpec((None, CHUNK, LANES), lambda b, h: (h, 0, 0)),
        ],
        out_specs=[
            pl.BlockSpec((CTX, LANES), lambda b, h: (b, h)),
            pl.BlockSpec((SEQ, LANES), lambda b, h: (b, h)),
        ],
        out_shape=[
            jax.ShapeDtypeStruct((R_CTX, W_A), bf16),
            jax.ShapeDtypeStruct((R_LAT, W_A), bf16),
        ],
        scratch_shapes=[
            pltpu.VMEM((TOK, LANES), bf16),
            pltpu.VMEM((TOK, LANES), bf16),
            pltpu.VMEM((TOK, LANES), bf16),
            pltpu.VMEM((TOK, LANES), bf16),
            pltpu.VMEM((TOK, LANES), bf16),
            pltpu.VMEM((TOK, LANES), f32),
            pltpu.VMEM((16 * N_CHUNK, LANES), f32),
            pltpu.VMEM((2 * N_CHUNK, CHUNK + 16, LANES), f32),
            pltpu.VMEM((2 * N_CHUNK, 8, LANES), f32),
            pltpu.VMEM((2, CHUNK + 16, LANES), f32),
            pltpu.VMEM((2, 8, LANES), f32),
        ],
        compiler_params=pltpu.CompilerParams(
            dimension_semantics=("parallel", "arbitrary"), vmem_limit_bytes=40 * MIB),
        name="mlstm",
    )(qk, qk, qk, qk, vo, vo, vo, vo, gates, gates, gate_b, mnorm)


def _odd_in_kernel(h_ref, mod_ref, w_ref, cos_ref, sin_ref, o_ref):
    n = _modulated(h_ref[...], mod_ref[3:4, :], mod_ref[4:5, :]).astype(bf16)
    cos = cos_ref[...]
    sin = sin_ref[...]
    n_rot = (HEADS_C + KV_HEADS) * DH // LANES
    y = _dot(n, w_ref[:, 0:n_rot * LANES])
    for c in range(0, n_rot, 2):
        x1 = y[:, c * LANES:(c + 1) * LANES]
        x2 = y[:, (c + 1) * LANES:(c + 2) * LANES]
        r1 = x1 * cos - x2 * sin
        r2 = x1 * sin + x2 * cos
        if c < HEADS_C * DH // LANES:
            r1 = r1 * (DH ** -0.5 * LOG2E)
            r2 = r2 * (DH ** -0.5 * LOG2E)
        o_ref[:, c * LANES:(c + 1) * LANES] = r1.astype(bf16)
        o_ref[:, (c + 1) * LANES:(c + 2) * LANES] = r2.astype(bf16)
    v0 = n_rot * LANES
    o_ref[:, v0:QKV] = _dot(n, w_ref[:, v0:QKV]).astype(bf16)


def _odd_in(h, mods, w, cos_t, sin_t):
    tm = TM_ODD_IN
    n_ctx = R_CTX // tm
    per_b = SEQ // tm
    rope_map = lambda i: (jnp.where(i < n_ctx, 0, 1 + jnp.maximum(i - n_ctx, 0) % per_b), 0)
    return pl.pallas_call(
        _odd_in_kernel,
        grid=(R_ALL // tm,),
        in_specs=[
            pl.BlockSpec((tm, D), lambda i: (i, 0)),
            pl.BlockSpec((None, N_MOD, D), lambda i: (_who_flat(i, tm), 0, 0)),
            pl.BlockSpec((D, QKV), lambda i: (0, 0), pipeline_mode=pl.Buffered(1)),
            pl.BlockSpec((tm, LANES), rope_map),
            pl.BlockSpec((tm, LANES), rope_map),
        ],
        out_specs=pl.BlockSpec((tm, QKV), lambda i: (i, 0)),
        out_shape=jax.ShapeDtypeStruct((R_ALL, QKV), bf16),
        compiler_params=pltpu.CompilerParams(
            dimension_semantics=("parallel",), vmem_limit_bytes=32 * MIB),
        name="odd_in",
    )(h, mods, w, cos_t, sin_t)


BLOCKS_PER_ITER = 4


def _attn_kernel(sink_ref, q_ref, kc_ref, kl_ref, vc_ref, vl_ref, o_ref,
                 k_all, vt_all, q_all, s_all, p_all, ot_all):
    kv_w = KV_HEADS * DH
    key = lax.broadcasted_iota(jnp.int32, (CHUNK, CHUNK), 0)
    qry = lax.broadcasted_iota(jnp.int32, (CHUNK, CHUNK), 1)
    far = 1 << 20
    lane_head = lax.broadcasted_iota(jnp.int32, (CHUNK, kv_w), 1) % LANES // (DH // 2)
    neg = -1e30
    n_key = CTX + 3 * CHUNK
    n_slab = n_key // CHUNK

    def stage_keys(k_scr, vt_scr, k_rows, v_rows, row0):
        rows = k_rows.shape[0]
        k_scr[row0:row0 + rows, :] = k_rows
        v_t = v_rows.astype(f32).T.astype(bf16)
        for j in range(KV_HEADS):
            vt_scr[j, 0:DH, row0:row0 + rows] = v_t[j * DH:(j + 1) * DH, :]

    for u in range(BLOCKS_PER_ITER):
        stage_keys(k_all.at[u], vt_all.at[u], kc_ref[...], vc_ref[...], 0)
        for j in range(KV_HEADS):
            vt_all[u, j, DH:DH + 16, :] = jnp.ones((16, n_key), bf16)

    def block(blk, u):
        k_scr, vt_scr, q_scr, s_scr, p_scr, ot_scr = (r.at[u] for r in (k_all, vt_all, q_all, s_all, p_all, ot_all))
        q0 = pl.multiple_of(blk * CHUNK, CHUNK)
        prev_ok = key >= qry + jnp.where(blk > 0, 0, far)
        next_ok = key <= qry - jnp.where(blk < N_BLK - 1, 0, far)
        for n, off in enumerate((-1, 0, 1)):
            src = pl.multiple_of(jnp.clip(blk + off, 0, N_BLK - 1) * CHUNK, CHUNK)
            stage_keys(k_scr, vt_scr, kl_ref[pl.ds(src, CHUNK), :], vl_ref[pl.ds(src, CHUNK), :], CTX + n * CHUNK)

        for j in range(KV_HEADS):
            keep = jnp.where(lane_head == j, 1.0, 0.0).astype(bf16)
            for g in range(GROUP):
                q_scr[j, g * CHUNK:(g + 1) * CHUNK, :] = q_ref[pl.ds(q0, CHUNK), g * kv_w:(g + 1) * kv_w] * keep
            s_scr[j] = _dot_nt(k_scr[...], q_scr[j])

        def scores(j, g, slab):
            s = s_scr[j, slab * CHUNK:(slab + 1) * CHUNK, g * CHUNK:(g + 1) * CHUNK]
            if slab == 2:
                s = jnp.where(prev_ok, s, neg)
            if slab == 4:
                s = jnp.where(next_ok, s, neg)
            return s

        for j in range(KV_HEADS):
            sink_terms = []
            for g in range(GROUP):
                sink = jnp.full((1, CHUNK), sink_ref[j * GROUP + g] * LOG2E, f32)
                m8 = None
                for slab in range(n_slab):
                    part = jnp.max(scores(j, g, slab).reshape(CHUNK // 8, 8, CHUNK), axis=0)
                    m8 = part if m8 is None else jnp.maximum(m8, part)
                m = jnp.maximum(sink, jnp.max(m8, axis=0, keepdims=True))
                for slab in range(n_slab):
                    p = jnp.exp2(scores(j, g, slab) - m)
                    p_scr[j, slab * CHUNK:(slab + 1) * CHUNK, g * CHUNK:(g + 1) * CHUNK] = p.astype(bf16)
                sink_terms.append(jnp.exp2(sink - m))
            acc = _dot(vt_scr[j], p_scr[j])
            den = acc[DH:DH + 1, :] + jnp.concatenate(sink_terms, axis=1)
            ot_scr[j * DH:(j + 1) * DH, :] = acc[0:DH, :] * (1.0 / den)
        for g in range(GROUP):
            o_ref[pl.ds(q0, CHUNK), g * kv_w:(g + 1) * kv_w] = (
                ot_scr[:, g * CHUNK:(g + 1) * CHUNK].T.astype(bf16))

    def blocks(i, carry):
        for u in range(BLOCKS_PER_ITER):
            block(i * BLOCKS_PER_ITER + u, u)
        return carry

    lax.fori_loop(0, N_BLK // BLOCKS_PER_ITER, blocks, 0)


def _attention(qkv, sink):
    kv_w = KV_HEADS * DH
    n_key = CTX + 3 * CHUNK
    k_col = HEADS_C * DH // kv_w
    v_col = k_col + 1
    lat0 = R_CTX // SEQ
    ctx = lambda col: pl.BlockSpec((CTX, kv_w), lambda b: (b, col))
    lat = lambda col: pl.BlockSpec((SEQ, kv_w), lambda b: (lat0 + b, col))
    return pl.pallas_call(
        _attn_kernel,
        grid=(B,),
        in_specs=[
            pl.BlockSpec(memory_space=pltpu.SMEM),
            pl.BlockSpec((SEQ, HEADS_C * DH), lambda b: (lat0 + b, 0)),
            ctx(k_col), lat(k_col), ctx(v_col), lat(v_col),
        ],
        out_specs=pl.BlockSpec((SEQ, HEADS_C * DH), lambda b: (b, 0)),
        out_shape=jax.ShapeDtypeStruct((R_LAT, HEADS_C * DH), bf16),
        scratch_shapes=[
            pltpu.VMEM((BLOCKS_PER_ITER, n_key, kv_w), bf16),
            pltpu.VMEM((BLOCKS_PER_ITER, KV_HEADS, DH + 16, n_key), bf16),
            pltpu.VMEM((BLOCKS_PER_ITER, KV_HEADS, GROUP * CHUNK, kv_w), bf16),
            pltpu.VMEM((BLOCKS_PER_ITER, KV_HEADS, n_key, GROUP * CHUNK), f32),
            pltpu.VMEM((BLOCKS_PER_ITER, KV_HEADS, n_key, GROUP * CHUNK), bf16),
            pltpu.VMEM((BLOCKS_PER_ITER, kv_w, GROUP * CHUNK), f32),
        ],
        compiler_params=pltpu.CompilerParams(
            dimension_semantics=("parallel",), vmem_limit_bytes=56 * MIB),
        name="window_attention",
    )(sink, qkv, qkv, qkv, qkv, qkv)


def _rope_tables():
    rows = SEQ // GRID_W
    row, col = np.meshgrid(np.arange(rows), np.arange(GRID_W), indexing='ij')
    n_freq = DH // 4
    inv = (np.float32(ROPE_BASE) ** (-np.arange(n_freq, dtype=np.float32) / np.float32(n_freq))).astype(np.float32)
    ang = np.concatenate([row.reshape(-1, 1).astype(np.float32) * inv,
                          col.reshape(-1, 1).astype(np.float32) * inv], axis=-1)
    reps = 2 * LANES // DH
    cos = np.tile(np.cos(ang).astype(np.float32), (1, reps))
    sin = np.tile(np.sin(ang).astype(np.float32), (1, reps))
    cos = np.concatenate([np.ones((TM_ODD_IN, LANES), np.float32), cos], axis=0)
    sin = np.concatenate([np.zeros((TM_ODD_IN, LANES), np.float32), sin], axis=0)
    return jnp.asarray(cos), jnp.asarray(sin)


def kernel(x, c, ctx, c_ctx, ada_w, ada_b, ffn_w_in, ffn_w_out, even_w_in, even_w_out, mlstm_conv,
           mlstm_gate_b, mlstm_norm, sgu_norm, sgu_ws, sgu_b, odd_w_qkv, odd_w_out, attn_sink, final_norm):
    cs = jnp.concatenate([c_ctx[None, :], c, jnp.zeros((16 - 1 - B, D), f32)], axis=0)
    mods = _modulation(cs, ada_w, ada_b)[:, :1 + B, :].reshape(2, 1 + B, N_MOD, D)

    fw_in, fw_out = ffn_w_in, ffn_w_out

    m0 = mods[0]
    h = _ffn((ctx.reshape(R_CTX, D), x.reshape(R_LAT, D)), m0, fw_in, fw_out, sel=(0, 0), mi=0)
    qk, vo, uv, gates = _even_in(h, m0, jnp.swapaxes(even_w_in, 1, 2), mlstm_conv[0])
    gate_b = jnp.pad(mlstm_gate_b[0].reshape(1, 4 * HEADS_A), ((0, 0), (0, LANES - 4 * HEADS_A)))
    mnorm_t = jnp.broadcast_to(mlstm_norm[0][:, :, None], (HEADS_A, CHUNK, LANES))
    ha_ctx, ha_lat = _mlstm(qk, vo, gates, gate_b, mnorm_t)
    sbx = jnp.repeat(sgu_b[0].T, LANES, axis=1)
    h = _ffn(h, m0, fw_in, fw_out, sel=(0, 1), mi=6,
             even=(ha_ctx, ha_lat, uv, sgu_norm[0].reshape(1, W_A), sgu_ws[0].astype(bf16), sbx,
                   even_w_out[0].astype(bf16)))

    m1 = mods[1]
    h = _ffn(h, m1, fw_in, fw_out, sel=(1, 0), mi=0, tm=TM_FFN_WIDE)
    cos_t, sin_t = _rope_tables()
    qdim = HEADS_C * DH
    kdim = KV_HEADS * DH
    w_q = odd_w_qkv[0][:, :qdim].reshape(D, KV_HEADS, GROUP, DH // 2, 2).transpose(0, 2, 4, 1, 3).reshape(D, qdim)
    w_k = odd_w_qkv[0][:, qdim:qdim + kdim].reshape(D, KV_HEADS, DH // 2, 2).transpose(0, 3, 1, 2).reshape(D, kdim)
    w_qkv = jnp.concatenate([w_q, w_k, odd_w_qkv[0][:, qdim + kdim:]], axis=1).astype(bf16)
    w_o = odd_w_out[0].reshape(KV_HEADS, GROUP, DH, D).transpose(1, 0, 2, 3).reshape(qdim, D).astype(bf16)
    qkv = _odd_in(h, m1, w_qkv, cos_t, sin_t)
    attn = _attention(qkv, attn_sink[0])
    out = _ffn(h, m1, fw_in, fw_out, sel=(1, 1), mi=6, last=(attn, w_o, final_norm))
    return out.reshape(B, SEQ, D)
```

```python
import functools

import numpy as np
import jax
import jax.numpy as jnp
from jax import lax
from jax.experimental import pallas as pl
from jax.experimental.pallas import tpu as pltpu

f32 = jnp.float32
bf16 = jnp.bfloat16

D = 1024
B = 8
SEQ = 2048
CTX = 256
TOK = CTX + SEQ
GRID_W = 64
N_MOD = 9
D_FF = 2816
EPS = 1e-6
HEADS_A = 4
CHUNK = 128
N_CHUNK = TOK // CHUNK
N_CTX_CHUNK = CTX // CHUNK
W_A = 512
EVEN_COLS = 3200
HEADS_C = 16
KV_HEADS = 4
GROUP = HEADS_C // KV_HEADS
DH = 64
QKV = (HEADS_C + 2 * KV_HEADS) * DH
N_BLK = SEQ // CHUNK
ROPE_BASE = 10000.0
LOG2E = 1.4426950408889634

R_CTX = B * CTX
R_LAT = B * SEQ
R_ALL = R_CTX + R_LAT

LANES = 128
TM_FFN = 512
TM_FFN_WIDE = 1024
TM_PROJ = 512
TM_ODD_IN = 1024
FC = 256
N_FC = D_FF // FC
MIB = 1024 * 1024


def _dot(a, b):
    return jnp.dot(a, b, preferred_element_type=f32)


def _dot_nt(a, b):
    return lax.dot_general(a, b, (((1,), (1,)), ((), ())), preferred_element_type=f32)


def _sigmoid(x):
    return 1.0 / (1.0 + jnp.exp(-x))


def _split3(x):
    hi = x.astype(bf16)
    r1 = x - hi.astype(f32)
    mid = r1.astype(bf16)
    lo = (r1 - mid.astype(f32)).astype(bf16)
    return hi, mid, lo


def _modulated(h, shift, scale):
    ms = jnp.mean(h * h, axis=-1, keepdims=True)
    return h * lax.rsqrt(ms + EPS) * (1.0 + scale) + shift


def _mod_kernel(c_ref, w_ref, b_ref, o_ref):
    x = c_ref[...]
    s = x * _sigmoid(x)
    w = w_ref[...]
    s_hi = s.astype(bf16)
    s_lo = (s - s_hi.astype(f32)).astype(bf16)
    w_hi = w.astype(bf16)
    w_lo = (w - w_hi.astype(f32)).astype(bf16)
    o_ref[...] = _dot(s_hi, w_hi) + _dot(s_hi, w_lo) + _dot(s_lo, w_hi) + b_ref[...]


def _modulation(cs, ada_w, ada_b):
    depth = ada_w.shape[0]
    rows = cs.shape[0]
    n_col = N_MOD * D
    tn = n_col // 4
    return pl.pallas_call(
        _mod_kernel,
        grid=(depth, n_col // tn),
        in_specs=[
            pl.BlockSpec((rows, D), lambda l, j: (0, 0)),
            pl.BlockSpec((None, D, tn), lambda l, j: (l, 0, j)),
            pl.BlockSpec((None, 1, tn), lambda l, j: (l, 0, j)),
        ],
        out_specs=pl.BlockSpec((None, rows, tn), lambda l, j: (l, 0, j)),
        out_shape=jax.ShapeDtypeStruct((depth, rows, n_col), f32),
        compiler_params=pltpu.CompilerParams(
            dimension_semantics=("parallel", "parallel"), vmem_limit_bytes=48 * MIB),
        name="modulation",
    )(cs, ada_w, ada_b.reshape(depth, 1, n_col))


def _who_flat(tile, tm):
    n_ctx = R_CTX // tm
    per_b = SEQ // tm
    return jnp.where(tile < n_ctx, 0, 1 + jnp.maximum(tile - n_ctx, 0) // per_b)


W_CHUNKS = 16
W_SLOTS = 4


def _fetch_cast(src, dst, stage, sem, place=None):
    rows = dst.shape[0] // W_CHUNKS

    def piece(c):
        slot = c % W_SLOTS
        return pltpu.make_async_copy(src.at[pl.ds(c * rows, rows), :], stage.at[slot], sem.at[slot])

    for c in range(W_SLOTS - 1):
        piece(c).start()
    for c in range(W_CHUNKS):
        if c + W_SLOTS - 1 < W_CHUNKS:
            piece(c + W_SLOTS - 1).start()
        piece(c).wait()
        if place is None:
            dst[c * rows:(c + 1) * rows, :] = stage[c % W_SLOTS].astype(bf16)
        else:
            place(dst, slice(c * rows, (c + 1) * rows), stage[c % W_SLOTS])


def _gelu_tanh(x):
    return x * (0.5 * (1.0 + jnp.tanh(0.7978845608028654 * (x + 0.044715 * (x * x * x)))))


def _even_mix(ha, uv_ref, sg_ref, ws_ref, sb_ref, wm_ref, hb_scr):
    u = uv_ref[:, 0:W_A].astype(f32)
    v = uv_ref[:, W_A:2 * W_A].astype(f32)
    vn = (v * lax.rsqrt(jnp.mean(v * v, axis=-1, keepdims=True) + EPS) * sg_ref[...]).astype(bf16)
    n_chunk = hb_scr.shape[0] // CHUNK
    for g in range(W_A // LANES):
        cs = slice(g * LANES, (g + 1) * LANES)
        rhs = jnp.concatenate([vn[n * CHUNK:(n + 1) * CHUNK, cs] for n in range(n_chunk)], axis=1)
        mixed = _dot(ws_ref[g], rhs)
        for n in range(n_chunk):
            r = slice(n * CHUNK, (n + 1) * CHUNK)
            hb_scr[r, cs] = (u[r, cs] * (mixed[:, n * LANES:(n + 1) * LANES] + sb_ref[:, cs])).astype(bf16)
    return _dot(ha, wm_ref[0:W_A, :]) + _dot(hb_scr[...], wm_ref[W_A:2 * W_A, :])


def _ffn_kernel(*refs, tm, mi, mixer, final, split, sel):
    refs = list(refs)
    is_ctx = pl.program_id(0) < R_CTX // tm
    if split:
        c_ref, x_ref = refs[0:2]
        refs = refs[2:]
        read_h = lambda: jnp.where(is_ctx, c_ref[...], x_ref[...])
    else:
        h_ref = refs.pop(0)
        read_h = lambda: h_ref[...]
    wi_ref, wo_ref, wi_stage, wo_stage, wi_sem, wo_sem = refs[-6:]
    refs = refs[:-6]
    if mixer == "attn":
        a_ref, wm_ref = refs[0:2]
        refs = refs[2:]
    elif mixer == "even":
        hac_ref, hax_ref, uv_ref, sg_ref, ws_ref, sb_ref, wm_ref = refs[0:7]
        refs = refs[7:]
    mod_ref, wi_hbm, wo_hbm = refs[0:3]
    refs = refs[3:]
    if final:
        fn_ref = refs.pop(0)
    o_ref, n_scr, acc_scr = refs[0:3]
    refs = refs[3:]

    @pl.when(pl.program_id(0) == 0)
    def _():
        _fetch_cast(wi_hbm.at[sel[0], sel[1]], wi_ref, wi_stage, wi_sem)
        _fetch_cast(wo_hbm.at[sel[0], sel[1]], wo_ref, wo_stage, wo_sem)

    if mixer is not None:
        h_scr = refs.pop(0)
        if mixer == "attn":
            y = _dot(a_ref[...], wm_ref[...])
        else:
            ha = jnp.where(is_ctx, hac_ref[...], hax_ref[...])
            y = _even_mix(ha, uv_ref, sg_ref, ws_ref, sb_ref, wm_ref, refs.pop(0))
        h_scr[...] = read_h() + mod_ref[5:6, :] * y
        read_h = lambda: h_scr[...]
    n_scr[...] = _modulated(read_h(), mod_ref[mi:mi + 1, :], mod_ref[mi + 1:mi + 2, :]).astype(bf16)

    for j in range(N_FC):
        n = n_scr[...]
        g = _dot(n, wi_ref[:, j * FC:(j + 1) * FC])
        u = _dot(n, wi_ref[:, D_FF + j * FC:D_FF + (j + 1) * FC])
        a = (g * _sigmoid(g) * u).astype(bf16)
        y = _dot(a, wo_ref[j * FC:(j + 1) * FC, :])
        if j == 0:
            acc_scr[...] = y
        else:
            acc_scr[...] += y
    out = read_h() + (0.5 * mod_ref[mi + 2:mi + 3, :]) * acc_scr[...]
    if final:
        ms = jnp.mean(out * out, axis=-1, keepdims=True)
        out = out * lax.rsqrt(ms + EPS) * fn_ref[...]
    o_ref[...] = out


def _ffn(h, mods, w_in, w_out, *, sel, mi, even=None, last=None, tm=TM_FFN):
    n_ctx = R_CTX // tm
    tile0 = n_ctx if last is not None else 0
    split = isinstance(h, tuple)
    const2 = lambda i: (0, 0)
    ctx_map = lambda i: (jnp.minimum(i, n_ctx - 1), 0)
    lat_map = lambda i: (jnp.maximum(i - n_ctx, 0), 0)
    if split:
        rows_out = R_ALL
        in_specs = [pl.BlockSpec((tm, D), ctx_map), pl.BlockSpec((tm, D), lat_map)]
        args = list(h)
    else:
        rows_out = h.shape[0] - tile0 * tm
        in_specs = [pl.BlockSpec((tm, D), lambda i: (i + tile0, 0))]
        args = [h]
    scratch = [pltpu.VMEM((tm, D), bf16), pltpu.VMEM((tm, D), f32)]
    mixer = None
    if last is not None:
        mixer = "attn"
        attn, w_attn, final_norm = last
        in_specs += [
            pl.BlockSpec((tm, D), lambda i: (i, 0)),
            pl.BlockSpec((D, D), const2, pipeline_mode=pl.Buffered(1)),
        ]
        args += [attn, w_attn]
        scratch.append(pltpu.VMEM((tm, D), f32))
    elif even is not None:
        mixer = "even"
        in_specs += [
            pl.BlockSpec((tm, W_A), ctx_map),
            pl.BlockSpec((tm, W_A), lat_map),
            pl.BlockSpec((tm, 2 * W_A), lambda i: (i, 0)),
            pl.BlockSpec((1, W_A), const2),
            pl.BlockSpec((W_A // LANES, CHUNK, CHUNK), lambda i: (0, 0, 0)),
            pl.BlockSpec((CHUNK, W_A), const2),
            pl.BlockSpec((2 * W_A, D), const2, pipeline_mode=pl.Buffered(1)),
        ]
        args += list(even)
        scratch += [pltpu.VMEM((tm, D), f32), pltpu.VMEM((tm, W_A), bf16)]
    in_specs += [
        pl.BlockSpec((None, N_MOD, D), lambda i: (_who_flat(i + tile0, tm), 0, 0)),
        pl.BlockSpec(memory_space=pl.ANY),
        pl.BlockSpec(memory_space=pl.ANY),
    ]
    args += [mods, w_in, w_out]
    if last is not None:
        in_specs.append(pl.BlockSpec((1, D), const2))
        args.append(final_norm.reshape(1, D))
    scratch += [
        pltpu.VMEM((D, 2 * D_FF), bf16),
        pltpu.VMEM((D_FF, D), bf16),
        pltpu.VMEM((W_SLOTS, D // W_CHUNKS, 2 * D_FF), f32),
        pltpu.VMEM((W_SLOTS, D_FF // W_CHUNKS, D), f32),
        pltpu.SemaphoreType.DMA((W_SLOTS,)),
        pltpu.SemaphoreType.DMA((W_SLOTS,)),
    ]
    return pl.pallas_call(
        functools.partial(_ffn_kernel, tm=tm, mi=mi, mixer=mixer, final=last is not None, split=split, sel=sel),
        grid=(rows_out // tm,),
        in_specs=in_specs,
        out_specs=pl.BlockSpec((tm, D), lambda i: (i, 0)),
        out_shape=jax.ShapeDtypeStruct((rows_out, D), f32),
        scratch_shapes=scratch,
        compiler_params=pltpu.CompilerParams(
            dimension_semantics=("arbitrary",), vmem_limit_bytes=56 * MIB),
        name={None: "ffn", "even": "ffn_even", "attn": "ffn_final"}[mixer],
    )(*args)


HALO = 8


def _fetch_even_w(src, dst, stage, sem):
    g0 = 4 * W_A
    g1 = g0 + 4 * HEADS_A
    pieces = ([(i * LANES, LANES, i * LANES) for i in range(g0 // LANES)]
              + [(g1 + i * LANES, LANES, g0 + i * LANES) for i in range(2 * W_A // LANES)]
              + [(g0, g1 - g0, g0 + 2 * W_A)])

    def copy(i):
        row0, n, _ = pieces[i]
        slot = i % W_SLOTS
        return pltpu.make_async_copy(src.at[pl.ds(row0, n), :], stage.at[slot, pl.ds(0, n), :], sem.at[slot])

    lane = lax.broadcasted_iota(jnp.int32, (D, LANES), 1)
    for i in range(W_SLOTS - 1):
        copy(i).start()
    for i, (_, n, col0) in enumerate(pieces):
        if i + W_SLOTS - 1 < len(pieces):
            copy(i + W_SLOTS - 1).start()
        copy(i).wait()
        x_t = stage[i % W_SLOTS].T
        if n < LANES:
            x_t = jnp.where(lane < n, x_t, 0.0)
        dst[:, col0:col0 + LANES] = x_t.astype(bf16)


def _even_in_kernel(h_ref, hp_ref, hn_ref, mod_ref, w_hbm, cw_ref, qk_ref, vo_ref, uv_ref, g_ref,
                    w_ref, w_stage, w_sem):
    tile = pl.program_id(0)

    @pl.when(tile == 0)
    def _():
        _fetch_even_w(w_hbm.at[0], w_ref, w_stage, w_sem)

    tm = TM_PROJ
    n_chunk = tm // CHUNK
    shift, scale = mod_ref[3:4, :], mod_ref[4:5, :]
    n = _modulated(h_ref[...], shift, scale)
    halo = _modulated(jnp.concatenate([hp_ref[...], hn_ref[...]], axis=0), shift, scale)
    n_ext = jnp.concatenate([halo[0:HALO], n, halo[HALO:2 * HALO]], axis=0).astype(bf16)
    n = n.astype(bf16)

    row = lax.broadcasted_iota(jnp.int32, (tm, LANES), 0)
    is_ctx = tile < R_CTX // tm
    pos = jnp.where(is_ctx, row % CTX, row + (jnp.maximum(tile - R_CTX // tm, 0) % (SEQ // tm)) * tm)
    seq_start = pos == 0
    seq_end = pos == jnp.where(is_ctx, CTX - 1, SEQ - 1)

    p = _dot(n_ext, w_ref[:, 0:2 * W_A])
    cur = p[HALO:HALO + tm]
    prv = pltpu.roll(p, 1, 0)[HALO:HALO + tm]
    nxt = pltpu.roll(p, tm + 2 * HALO - 1, 0)[HALO:HALO + tm]
    for cb in range(2 * HEADS_A):
        cs = slice(cb * LANES, (cb + 1) * LANES)
        y = (cw_ref[0:1, cs] * jnp.where(seq_start, 0.0, prv[:, cs]) + cw_ref[1:2, cs] * cur[:, cs]
             + cw_ref[2:3, cs] * jnp.where(seq_end, 0.0, nxt[:, cs]))
        y = y * _sigmoid(y)
        if cb < HEADS_A:
            for c in range(n_chunk):
                r = slice(c * CHUNK, (c + 1) * CHUNK)
                qk_ref[r, cs] = y[r, :].T.astype(bf16)
        else:
            qk_ref[:, cs] = (y * CHUNK ** -0.5).astype(bf16)

    v = _dot(n, w_ref[:, 2 * W_A:3 * W_A])
    for hd in range(HEADS_A):
        cs = slice(hd * LANES, (hd + 1) * LANES)
        for c in range(n_chunk):
            r = slice(c * CHUNK, (c + 1) * CHUNK)
            vo_ref[r, cs] = v[r, cs].T.astype(bf16)
    vo_ref[:, W_A:2 * W_A] = _sigmoid(_dot(n, w_ref[:, 3 * W_A:4 * W_A])).astype(bf16)
    uv_ref[...] = _gelu_tanh(_dot(n, w_ref[:, 4 * W_A:6 * W_A])).astype(bf16)
    g_ref[...] = _dot(n, w_ref[:, 6 * W_A:EVEN_COLS])


def _even_in(h, mods, w, conv_w):
    tm = TM_PROJ
    out_map = lambda i: (i, 0)
    halo_blocks = tm // HALO
    last_halo = R_ALL // HALO - 1
    return pl.pallas_call(
        _even_in_kernel,
        grid=(R_ALL // tm,),
        in_specs=[
            pl.BlockSpec((tm, D), lambda i: (i, 0)),
            pl.BlockSpec((HALO, D), lambda i: (jnp.maximum(i * halo_blocks - 1, 0), 0)),
            pl.BlockSpec((HALO, D), lambda i: (jnp.minimum((i + 1) * halo_blocks, last_halo), 0)),
            pl.BlockSpec((None, N_MOD, D), lambda i: (_who_flat(i, tm), 0, 0)),
            pl.BlockSpec(memory_space=pl.ANY),
            pl.BlockSpec((3, 2 * W_A), lambda i: (0, 0)),
        ],
        out_specs=[
            pl.BlockSpec((tm, 1024), out_map),
            pl.BlockSpec((tm, 1024), out_map),
            pl.BlockSpec((tm, 1024), out_map),
            pl.BlockSpec((tm, LANES), out_map),
        ],
        out_shape=[
            jax.ShapeDtypeStruct((R_ALL, 1024), bf16),
            jax.ShapeDtypeStruct((R_ALL, 1024), bf16),
            jax.ShapeDtypeStruct((R_ALL, 1024), bf16),
            jax.ShapeDtypeStruct((R_ALL, LANES), f32),
        ],
        scratch_shapes=[
            pltpu.VMEM((D, EVEN_COLS), bf16),
            pltpu.VMEM((W_SLOTS, LANES, D), f32),
            pltpu.SemaphoreType.DMA((W_SLOTS,)),
        ],
        compiler_params=pltpu.CompilerParams(
            dimension_semantics=("arbitrary",), vmem_limit_bytes=40 * MIB),
        name="even_in",
    )(h, h, h, mods, w, conv_w)


N_AUG = CHUNK + 16
CHUNKS_PER_ITER = 18


def _chunk_loop(body):
    def group(i, carry):
        for u in range(CHUNKS_PER_ITER):
            carry = body(i * CHUNKS_PER_ITER + u, carry)
        return carry
    lax.fori_loop(0, N_CHUNK // CHUNKS_PER_ITER, group, 0)


def _mlstm_kernel(qc_ref, ql_ref, kc_ref, kl_ref, vc_ref, vl_ref, oc_ref, ol_ref, gc_ref, gl_ref,
                  gb_ref, mn_ref, outc_ref, outl_ref,
                  ks, qts, vts, og, hg, xc, xr, cl, st, cst, mst):
    head = pl.program_id(1)
    rowi = lax.broadcasted_iota(jnp.int32, (CHUNK, CHUNK), 0)
    coli = lax.broadcasted_iota(jnp.int32, (CHUNK, CHUNK), 1)
    lower = coli <= rowi
    upper = coli >= rowi
    tri = jnp.where(lower, 1.0, 0.0).astype(bf16)

    def part(ctx_ref, lat_ref, c):
        return (ctx_ref, c * CHUNK) if c < N_CTX_CHUNK else (lat_ref, (c - N_CTX_CHUNK) * CHUNK)

    for c in range(N_CHUNK):
        lo = c * CHUNK
        for dst, refs in ((qts, (qc_ref, ql_ref)), (ks, (kc_ref, kl_ref)), (vts, (vc_ref, vl_ref)),
                          (og, (oc_ref, ol_ref))):
            src, at = part(*refs, c)
            dst[lo:lo + CHUNK, :] = src[at:at + CHUNK, :]

    @pl.when(head == 0)
    def _():
        kind = (coli // HEADS_A) % 4
        for c in range(N_CHUNK):
            lo = c * CHUNK
            src, at = part(gc_ref, gl_ref, c)
            gt = src[at:at + CHUNK, :] + gb_ref[...]
            lf = jnp.minimum(gt, 0.0) - jnp.log1p(jnp.exp(-jnp.abs(gt)))
            hi, mid, lw = _split3(lf)
            pre = _dot(tri, hi) + _dot(tri, mid) + _dot(tri, lw)
            suf = pre[CHUNK - 1:CHUNK, :] - pre + lf
            x = jnp.where(kind == 1, pre, jnp.where(kind == 3, suf, gt))
            xr[16 * c:16 * c + 16, :] = x.T[0:16, :]
            xc[lo:lo + CHUNK, :] = x

    lane_shift = (LANES - head) % LANES

    def gate_rows(c, d):
        row = c * 16 + 8 * d + head
        return xr[pl.ds(row, 1), :], xr[pl.ds(row + HEADS_A, 1), :]

    def local_state(c, carry):
        lo = pl.multiple_of(c * CHUNK, CHUNK)
        k = ks[pl.ds(lo, CHUNK), :]
        v_t = vts[pl.ds(lo, CHUNK), :].astype(f32)
        lhs = []
        for d in (0, 1):
            ig_r, b_r = gate_rows(c, d)
            g = b_r[:, CHUNK - 1:CHUNK] if d == 0 else b_r[:, 0:1]
            a_r = g - b_r + ig_r
            m_loc = jnp.max(a_r, axis=1, keepdims=True)
            w_r = jnp.exp(a_r - m_loc)
            lhs += [v_t * w_r, jnp.broadcast_to(w_r, (16, LANES))]
            st[d * N_CHUNK + c, 0:1, :] = jnp.broadcast_to(m_loc, (1, LANES))
            st[d * N_CHUNK + c, 1:2, :] = jnp.broadcast_to(g, (1, LANES))
        both = _dot(jnp.concatenate(lhs, axis=0).astype(bf16), k)
        cl[c] = both[0:N_AUG, :]
        cl[N_CHUNK + c] = both[N_AUG:2 * N_AUG, :]
        return carry

    _chunk_loop(local_state)

    cst[...] = jnp.zeros(cst.shape, f32)
    mst[...] = jnp.zeros(mst.shape, f32)

    def scan_step(i, carry):
        for d in (0, 1):
            c = i if d == 0 else jnp.where(i < N_CTX_CHUNK, N_CTX_CHUNK - 1 - i, N_CHUNK + N_CTX_CHUNK - 1 - i)
            idx = d * N_CHUNK + c
            c_loc = cl[idx]
            m_loc = st[idx, 0:1, :]
            g = st[idx, 1:2, :]
            c_prev = cst[d]
            m_prev = mst[d, 0:1, :]
            m_new = jnp.maximum(g + m_prev, m_loc)
            dec = jnp.exp(g + m_prev - m_new)
            add = jnp.exp(m_loc - m_new)
            cl[idx] = c_prev
            st[idx, 2:3, :] = m_prev
            cst[d] = dec * c_prev + add * c_loc
            mst[d, 0:1, :] = m_new
        return carry

    lax.fori_loop(0, N_CHUNK, scan_step, 0)

    def outputs(c, carry):
        lo = pl.multiple_of(c * CHUNK, CHUNK)
        k = ks[pl.ds(lo, CHUNK), :]
        q_t = qts[pl.ds(lo, CHUNK), :]
        v_aug = jnp.concatenate([vts[pl.ds(lo, CHUNK), :], jnp.ones((16, LANES), bf16)], axis=0)
        q_f = q_t.astype(f32)
        s_t = _dot(k, q_t)
        x_c = pltpu.roll(xc[pl.ds(lo, CHUNK), :], lane_shift, 1)
        hs = None
        for d in (0, 1):
            _, b_r = gate_rows(c, d)
            idx = d * N_CHUNK + c
            r_c = x_c[:, 8 * d:8 * d + 1] - x_c[:, 8 * d + 4:8 * d + 5]
            dm = jnp.where(upper if d == 0 else lower, b_r + r_c, -jnp.inf)
            e_r = b_r + st[idx, 2:3, :]
            m_t = jnp.maximum(e_r, jnp.max(dm, axis=0, keepdims=True))
            p_t = s_t * jnp.exp(dm - m_t)
            inter = jnp.exp(e_r - m_t)
            lhs = jnp.concatenate([v_aug, cl[idx].astype(bf16)], axis=1)
            rhs = jnp.concatenate([p_t, q_f * inter], axis=0).astype(bf16)
            nd = _dot(lhs, rhs)
            den = nd[CHUNK:CHUNK + 1, :]
            h_d = nd[0:CHUNK, :] * (1.0 / jnp.maximum(jnp.abs(den), jnp.exp(-m_t)))
            hs = h_d if hs is None else hs + h_d
        hn = hs * lax.rsqrt(jnp.mean(hs * hs, axis=0, keepdims=True) + EPS) * mn_ref[...]
        hg[pl.ds(lo, CHUNK), :] = (og[pl.ds(lo, CHUNK), :].astype(f32) * hn.T).astype(bf16)
        return carry

    _chunk_loop(outputs)
    outc_ref[...] = hg[0:CTX, :]
    outl_ref[...] = hg[CTX:TOK, :]


def _mlstm(qk, vo, gates, gate_b, mnorm):
    lat0 = R_CTX // SEQ
    ctx = lambda col0: pl.BlockSpec((CTX, LANES), lambda b, h: (b, col0 + h))
    lat = lambda col0: pl.BlockSpec((SEQ, LANES), lambda b, h: (lat0 + b, col0 + h))
    return pl.pallas_call(
        _mlstm_kernel,
        grid=(B, HEADS_A),
        in_specs=[
            ctx(0), lat(0), ctx(HEADS_A), lat(HEADS_A),
            ctx(0), lat(0), ctx(HEADS_A), lat(HEADS_A),
            pl.BlockSpec((CTX, LANES), lambda b, h: (b, 0)),
            pl.BlockSpec((SEQ, LANES), lambda b, h: (lat0 + b, 0)),
            pl.BlockSpec((1, LANES), lambda b, h: (0, 0)),
            pl.BlockSpec((None, CHUNK, LANES), lambda b, h: (h, 0, 0)),
        ],
        out_specs=[
            pl.BlockSpec((CTX, LANES), lambda b, h: (b, h)),
            pl.BlockSpec((SEQ, LANES), lambda b, h: (b, h)),
        ],
        out_shape=[
            jax.ShapeDtypeStruct((R_CTX, W_A), bf16),
            jax.ShapeDtypeStruct((R_LAT, W_A), bf16),
        ],
        scratch_shapes=[
            pltpu.VMEM((TOK, LANES), bf16),
            pltpu.VMEM((TOK, LANES), bf16),
            pltpu.VMEM((TOK, LANES), bf16),
            pltpu.VMEM((TOK, LANES), bf16),
            pltpu.VMEM((TOK, LANES), bf16),
            pltpu.VMEM((TOK, LANES), f32),
            pltpu.VMEM((16 * N_CHUNK, LANES), f32),
            pltpu.VMEM((2 * N_CHUNK, CHUNK + 16, LANES), f32),
            pltpu.VMEM((2 * N_CHUNK, 8, LANES), f32),
            pltpu.VMEM((2, CHUNK + 16, LANES), f32),
            pltpu.VMEM((2, 8, LANES), f32),
        ],
        compiler_params=pltpu.CompilerParams(
            dimension_semantics=("parallel", "arbitrary"), vmem_limit_bytes=40 * MIB),
        name="mlstm",
    )(qk, qk, qk, qk, vo, vo, vo, vo, gates, gates, gate_b, mnorm)


def _odd_in_kernel(h_ref, mod_ref, w_ref, cos_ref, sin_ref, o_ref):
    n = _modulated(h_ref[...], mod_ref[3:4, :], mod_ref[4:5, :]).astype(bf16)
    cos = cos_ref[...]
    sin = sin_ref[...]
    n_rot = (HEADS_C + KV_HEADS) * DH // LANES
    y = _dot(n, w_ref[:, 0:n_rot * LANES])
    for c in range(0, n_rot, 2):
        x1 = y[:, c * LANES:(c + 1) * LANES]
        x2 = y[:, (c + 1) * LANES:(c + 2) * LANES]
        r1 = x1 * cos - x2 * sin
        r2 = x1 * sin + x2 * cos
        if c < HEADS_C * DH // LANES:
            r1 = r1 * (DH ** -0.5 * LOG2E)
            r2 = r2 * (DH ** -0.5 * LOG2E)
        o_ref[:, c * LANES:(c + 1) * LANES] = r1.astype(bf16)
        o_ref[:, (c + 1) * LANES:(c + 2) * LANES] = r2.astype(bf16)
    v0 = n_rot * LANES
    o_ref[:, v0:QKV] = _dot(n, w_ref[:, v0:QKV]).astype(bf16)


def _odd_in(h, mods, w, cos_t, sin_t):
    tm = TM_ODD_IN
    n_ctx = R_CTX // tm
    per_b = SEQ // tm
    rope_map = lambda i: (jnp.where(i < n_ctx, 0, 1 + jnp.maximum(i - n_ctx, 0) % per_b), 0)
    return pl.pallas_call(
        _odd_in_kernel,
        grid=(R_ALL // tm,),
        in_specs=[
            pl.BlockSpec((tm, D), lambda i: (i, 0)),
            pl.BlockSpec((None, N_MOD, D), lambda i: (_who_flat(i, tm), 0, 0)),
            pl.BlockSpec((D, QKV), lambda i: (0, 0), pipeline_mode=pl.Buffered(1)),
            pl.BlockSpec((tm, LANES), rope_map),
            pl.BlockSpec((tm, LANES), rope_map),
        ],
        out_specs=pl.BlockSpec((tm, QKV), lambda i: (i, 0)),
        out_shape=jax.ShapeDtypeStruct((R_ALL, QKV), bf16),
        compiler_params=pltpu.CompilerParams(
            dimension_semantics=("parallel",), vmem_limit_bytes=32 * MIB),
        name="odd_in",
    )(h, mods, w, cos_t, sin_t)


BLOCKS_PER_ITER = 4


def _attn_kernel(sink_ref, q_ref, kc_ref, kl_ref, vc_ref, vl_ref, o_ref,
                 k_all, vt_all, q_all, s_all, p_all, ot_all):
    kv_w = KV_HEADS * DH
    key = lax.broadcasted_iota(jnp.int32, (CHUNK, CHUNK), 0)
    qry = lax.broadcasted_iota(jnp.int32, (CHUNK, CHUNK), 1)
    far = 1 << 20
    lane_head = lax.broadcasted_iota(jnp.int32, (CHUNK, kv_w), 1) % LANES // (DH // 2)
    neg = -1e30
    n_key = CTX + 3 * CHUNK
    n_slab = n_key // CHUNK

    def stage_keys(k_scr, vt_scr, k_rows, v_rows, row0):
        rows = k_rows.shape[0]
        k_scr[row0:row0 + rows, :] = k_rows
        v_t = v_rows.astype(f32).T.astype(bf16)
        for j in range(KV_HEADS):
            vt_scr[j, 0:DH, row0:row0 + rows] = v_t[j * DH:(j + 1) * DH, :]

    for u in range(BLOCKS_PER_ITER):
        stage_keys(k_all.at[u], vt_all.at[u], kc_ref[...], vc_ref[...], 0)
        for j in range(KV_HEADS):
            vt_all[u, j, DH:DH + 16, :] = jnp.ones((16, n_key), bf16)

    def block(blk, u):
        k_scr, vt_scr, q_scr, s_scr, p_scr, ot_scr = (r.at[u] for r in (k_all, vt_all, q_all, s_all, p_all, ot_all))
        q0 = pl.multiple_of(blk * CHUNK, CHUNK)
        prev_ok = key >= qry + jnp.where(blk > 0, 0, far)
        next_ok = key <= qry - jnp.where(blk < N_BLK - 1, 0, far)
        for n, off in enumerate((-1, 0, 1)):
            src = pl.multiple_of(jnp.clip(blk + off, 0, N_BLK - 1) * CHUNK, CHUNK)
            stage_keys(k_scr, vt_scr, kl_ref[pl.ds(src, CHUNK), :], vl_ref[pl.ds(src, CHUNK), :], CTX + n * CHUNK)

        for j in range(KV_HEADS):
            keep = jnp.where(lane_head == j, 1.0, 0.0).astype(bf16)
            for g in range(GROUP):
                q_scr[j, g * CHUNK:(g + 1) * CHUNK, :] = q_ref[pl.ds(q0, CHUNK), g * kv_w:(g + 1) * kv_w] * keep
            s_scr[j] = _dot_nt(k_scr[...], q_scr[j])

        def scores(j, g, slab):
            s = s_scr[j, slab * CHUNK:(slab + 1) * CHUNK, g * CHUNK:(g + 1) * CHUNK]
            if slab == 2:
                s = jnp.where(prev_ok, s, neg)
            if slab == 4:
                s = jnp.where(next_ok, s, neg)
            return s

        for j in range(KV_HEADS):
            sink_terms = []
            for g in range(GROUP):
                sink = jnp.full((1, CHUNK), sink_ref[j * GROUP + g] * LOG2E, f32)
                m8 = None
                for slab in range(n_slab):
                    part = jnp.max(scores(j, g, slab).reshape(CHUNK // 8, 8, CHUNK), axis=0)
                    m8 = part if m8 is None else jnp.maximum(m8, part)
                m = jnp.maximum(sink, jnp.max(m8, axis=0, keepdims=True))
                for slab in range(n_slab):
                    p = jnp.exp2(scores(j, g, slab) - m)
                    p_scr[j, slab * CHUNK:(slab + 1) * CHUNK, g * CHUNK:(g + 1) * CHUNK] = p.astype(bf16)
                sink_terms.append(jnp.exp2(sink - m))
            acc = _dot(vt_scr[j], p_scr[j])
            den = acc[DH:DH + 1, :] + jnp.concatenate(sink_terms, axis=1)
            ot_scr[j * DH:(j + 1) * DH, :] = acc[0:DH, :] * (1.0 / den)
        for g in range(GROUP):
            o_ref[pl.ds(q0, CHUNK), g * kv_w:(g + 1) * kv_w] = (
                ot_scr[:, g * CHUNK:(g + 1) * CHUNK].T.astype(bf16))

    def blocks(i, carry):
        for u in range(BLOCKS_PER_ITER):
            block(i * BLOCKS_PER_ITER + u, u)
        return carry

    lax.fori_loop(0, N_BLK // BLOCKS_PER_ITER, blocks, 0)


def _attention(qkv, sink):
    kv_w = KV_HEADS * DH
    n_key = CTX + 3 * CHUNK
    k_col = HEADS_C * DH // kv_w
    v_col = k_col + 1
    lat0 = R_CTX // SEQ
    ctx = lambda col: pl.BlockSpec((CTX, kv_w), lambda b: (b, col))
    lat = lambda col: pl.BlockSpec((SEQ, kv_w), lambda b: (lat0 + b, col))
    return pl.pallas_call(
        _attn_kernel,
        grid=(B,),
        in_specs=[
            pl.BlockSpec(memory_space=pltpu.SMEM),
            pl.BlockSpec((SEQ, HEADS_C * DH), lambda b: (lat0 + b, 0)),
            ctx(k_col), lat(k_col), ctx(v_col), lat(v_col),
        ],
        out_specs=pl.BlockSpec((SEQ, HEADS_C * DH), lambda b: (b, 0)),
        out_shape=jax.ShapeDtypeStruct((R_LAT, HEADS_C * DH), bf16),
        scratch_shapes=[
            pltpu.VMEM((BLOCKS_PER_ITER, n_key, kv_w), bf16),
            pltpu.VMEM((BLOCKS_PER_ITER, KV_HEADS, DH + 16, n_key), bf16),
            pltpu.VMEM((BLOCKS_PER_ITER, KV_HEADS, GROUP * CHUNK, kv_w), bf16),
            pltpu.VMEM((BLOCKS_PER_ITER, KV_HEADS, n_key, GROUP * CHUNK), f32),
            pltpu.VMEM((BLOCKS_PER_ITER, KV_HEADS, n_key, GROUP * CHUNK), bf16),
            pltpu.VMEM((BLOCKS_PER_ITER, kv_w, GROUP * CHUNK), f32),
        ],
        compiler_params=pltpu.CompilerParams(
            dimension_semantics=("parallel",), vmem_limit_bytes=56 * MIB),
        name="window_attention",
    )(sink, qkv, qkv, qkv, qkv, qkv)


def _rope_tables():
    rows = SEQ // GRID_W
    row, col = np.meshgrid(np.arange(rows), np.arange(GRID_W), indexing='ij')
    n_freq = DH // 4
    inv = (np.float32(ROPE_BASE) ** (-np.arange(n_freq, dtype=np.float32) / np.float32(n_freq))).astype(np.float32)
    ang = np.concatenate([row.reshape(-1, 1).astype(np.float32) * inv,
                          col.reshape(-1, 1).astype(np.float32) * inv], axis=-1)
    reps = 2 * LANES // DH
    cos = np.tile(np.cos(ang).astype(np.float32), (1, reps))
    sin = np.tile(np.sin(ang).astype(np.float32), (1, reps))
    cos = np.concatenate([np.ones((TM_ODD_IN, LANES), np.float32), cos], axis=0)
    sin = np.concatenate([np.zeros((TM_ODD_IN, LANES), np.float32), sin], axis=0)
    return jnp.asarray(cos), jnp.asarray(sin)


def kernel(x, c, ctx, c_ctx, ada_w, ada_b, ffn_w_in, ffn_w_out, even_w_in, even_w_out, mlstm_conv,
           mlstm_gate_b, mlstm_norm, sgu_norm, sgu_ws, sgu_b, odd_w_qkv, odd_w_out, attn_sink, final_norm):
    cs = jnp.concatenate([c_ctx[None, :], c, jnp.zeros((16 - 1 - B, D), f32)], axis=0)
    mods = _modulation(cs, ada_w, ada_b)[:, :1 + B, :].reshape(2, 1 + B, N_MOD, D)

    fw_in, fw_out = ffn_w_in, ffn_w_out

    m0 = mods[0]
    h = _ffn((ctx.reshape(R_CTX, D), x.reshape(R_LAT, D)), m0, fw_in, fw_out, sel=(0, 0), mi=0)
    qk, vo, uv, gates = _even_in(h, m0, jnp.swapaxes(even_w_in, 1, 2), mlstm_conv[0])
    gate_b = jnp.pad(mlstm_gate_b[0].reshape(1, 4 * HEADS_A), ((0, 0), (0, LANES - 4 * HEADS_A)))
    mnorm_t = jnp.broadcast_to(mlstm_norm[0][:, :, None], (HEADS_A, CHUNK, LANES))
    ha_ctx, ha_lat = _mlstm(qk, vo, gates, gate_b, mnorm_t)
    sbx = jnp.repeat(sgu_b[0].T, LANES, axis=1)
    h = _ffn(h, m0, fw_in, fw_out, sel=(0, 1), mi=6,
             even=(ha_ctx, ha_lat, uv, sgu_norm[0].reshape(1, W_A), sgu_ws[0].astype(bf16), sbx,
                   even_w_out[0].astype(bf16)))

    m1 = mods[1]
    h = _ffn(h, m1, fw_in, fw_out, sel=(1, 0), mi=0, tm=TM_FFN_WIDE)
    cos_t, sin_t = _rope_tables()
    qdim = HEADS_C * DH
    kdim = KV_HEADS * DH
    w_q = odd_w_qkv[0][:, :qdim].reshape(D, KV_HEADS, GROUP, DH // 2, 2).transpose(0, 2, 4, 1, 3).reshape(D, qdim)
    w_k = odd_w_qkv[0][:, qdim:qdim + kdim].reshape(D, KV_HEADS, DH // 2, 2).transpose(0, 3, 1, 2).reshape(D, kdim)
    w_qkv = jnp.concatenate([w_q, w_k, odd_w_qkv[0][:, qdim + kdim:]], axis=1).astype(bf16)
    w_o = odd_w_out[0].reshape(KV_HEADS, GROUP, DH, D).transpose(1, 0, 2, 3).reshape(qdim, D).astype(bf16)
    qkv = _odd_in(h, m1, w_qkv, cos_t, sin_t)
    attn = _attention(qkv, attn_sink[0])
    out = _ffn(h, m1, fw_in, fw_out, sel=(1, 1), mi=6, last=(attn, w_o, final_norm))
    return out.reshape(B, SEQ, D)
```
